```python
import math
import jax
import jax.numpy as jnp
from jax import lax
import numpy as np

D_MODEL = 1024
BATCH = 8
SEQ = 2048
DEPTH = 1
DEC_BATCH = 8
DEC_SEQ = 16
PAST_LEN = 4096

CHUNK = 64
Q_BLOCK = 128
POOL_WINDOWS = (2, 4, 8, 16)
POOL_GROUPS = 4
POOL_GROUP_DIM = 128
POOL_DIM = POOL_GROUPS * POOL_GROUP_DIM
POOL_HIST = max(POOL_WINDOWS) - 1
N_HEADS = 8
HEAD_DIM = 64
V_DIM = 2 * HEAD_DIM
QK_DIM = N_HEADS * 2 * HEAD_DIM
ATTN_V_WIDTH = N_HEADS * V_DIM
ATTN_SCALE = HEAD_DIM ** -0.5
ROPE_THETA = 10000.0
SUBLN_EPS = 1e-5
N_BRANCHES = 2
IN_WIDTH = POOL_DIM + 2 * QK_DIM + ATTN_V_WIDTH + N_BRANCHES * D_MODEL
N_EXPERTS = 32
TOP_K = 4
D_EXPERT = 1024
SWIGLU_LIMIT = 7.0
SWIGLU_ALPHA = 1.702
MOE_BLOCK = 128
DN_ALPHA = (2.0 * DEPTH) ** 0.25
DN_BETA = (8.0 * DEPTH) ** -0.25
LN_EPS = 1e-5
NEG_INF = -1e30

kernel_name = 'streaming_pool_diffattn_moe_deepnorm'


def _layer_norm(x, g, b):
    xf = x.astype(jnp.float32)
    mu = jnp.mean(xf, -1, keepdims=True)
    var = jnp.mean(jnp.square(xf - mu), -1, keepdims=True)
    return ((xf - mu) * lax.rsqrt(var + LN_EPS) * g.astype(jnp.float32) + b.astype(jnp.float32)).astype(x.dtype)


def _rope(x, pos):
    half = HEAD_DIM // 2
    inv = ROPE_THETA ** (-jnp.arange(half, dtype=jnp.float32) / half)
    ang = pos.astype(jnp.float32)[:, None] * inv[None, :]
    cos = jnp.cos(ang)[None, :, None, None, :]
    sin = jnp.sin(ang)[None, :, None, None, :]
    xf = x.astype(jnp.float32)
    x1, x2 = xf[..., :half], xf[..., half:]
    return jnp.concatenate([x1 * cos - x2 * sin, x2 * cos + x1 * sin], -1).astype(x.dtype)


def _pool_branch(u_hist, u_new, pos0, w_mix, scale):
    B, T, _ = u_new.shape
    ext_raw = jnp.concatenate([u_hist.astype(u_new.dtype), u_new], 1)
    ext = ext_raw.astype(jnp.float32)
    cs = jnp.concatenate([jnp.zeros_like(ext[:, :1]), jnp.cumsum(ext, 1)], 1)
    pos = pos0 + jnp.arange(T, dtype=jnp.int32)
    means = []
    for g, w in enumerate(POOL_WINDOWS):
        sl = slice(g * POOL_GROUP_DIM, (g + 1) * POOL_GROUP_DIM)
        end = cs[:, POOL_HIST + 1:POOL_HIST + 1 + T, sl]
        beg = cs[:, POOL_HIST + 1 - w:POOL_HIST + 1 - w + T, sl]
        cnt = jnp.minimum(pos + 1, w).astype(jnp.float32)[None, :, None]
        means.append((end - beg) / cnt)
    d = (jnp.concatenate(means, -1) - ext[:, POOL_HIST:]).astype(u_new.dtype)
    d = d.reshape(B, T, POOL_GROUPS, POOL_GROUP_DIM)
    y = jnp.einsum('btgc,gcd->btgd', d, w_mix).reshape(B, T, POOL_DIM) * scale
    return y, ext_raw[:, -POOL_HIST:]


def _diff_attn(q, k, v, q_pos, k_pos, lam):
    s = jnp.einsum('bqhcd,bkhcd->bhcqk', q, k, preferred_element_type=jnp.float32) * ATTN_SCALE
    mask = (k_pos // CHUNK)[None, :] <= (q_pos // CHUNK)[:, None]
    s = jnp.where(mask, s, NEG_INF)
    p = jax.nn.softmax(s, axis=-1)
    w = p[:, :, 0] - lam * p[:, :, 1]
    return jnp.einsum('bhqk,bkhe->bqhe', w.astype(v.dtype), v)


def _attn_prompt(q, k, v, lam):
    B, T = q.shape[:2]
    nb = T // Q_BLOCK
    qb = q.reshape(B, nb, Q_BLOCK, N_HEADS, 2, HEAD_DIM).transpose(1, 0, 2, 3, 4, 5)
    k_pos = jnp.arange(T, dtype=jnp.int32)

    def one_block(args):
        qi, i = args
        q_pos = i * Q_BLOCK + jnp.arange(Q_BLOCK, dtype=jnp.int32)
        return _diff_attn(qi, k, v, q_pos, k_pos, lam)

    o = lax.map(one_block, (qb, jnp.arange(nb, dtype=jnp.int32)))
    return o.transpose(1, 0, 2, 3, 4).reshape(B, T, N_HEADS, V_DIM)


def _head_norm(o, g, lam_init):
    B, T = o.shape[:2]
    of = o.astype(jnp.float32)
    of = of * lax.rsqrt(jnp.mean(jnp.square(of), -1, keepdims=True) + SUBLN_EPS) * g.astype(jnp.float32)
    return (of * (1.0 - lam_init)).astype(o.dtype).reshape(B, T, ATTN_V_WIDTH)


def _moe(x, w_router, b_router, w_e_in, b_e_in, w_e_out, b_e_out):
    B, T, D = x.shape
    xt = x.reshape(-1, D)
    N = xt.shape[0]
    logits = (xt @ w_router).astype(jnp.float32) + b_router.astype(jnp.float32)
    top_val, top_idx = lax.top_k(logits, TOP_K)
    gate = jax.nn.softmax(top_val, axis=-1)
    A = N * TOP_K
    flat_e = top_idx.reshape(A)
    order = jnp.argsort(flat_e)
    sorted_e = flat_e[order]
    sorted_tok = (order // TOP_K).astype(jnp.int32)
    counts = jnp.bincount(flat_e, length=N_EXPERTS)
    padded = (counts + MOE_BLOCK - 1) // MOE_BLOCK * MOE_BLOCK
    pad_end = jnp.cumsum(padded)
    pad_start = pad_end - padded
    raw_start = jnp.cumsum(counts) - counts
    dest = pad_start[sorted_e] + jnp.arange(A, dtype=jnp.int32) - raw_start[sorted_e]
    n_blocks = -(-A // MOE_BLOCK) + N_EXPERTS
    rows = n_blocks * MOE_BLOCK
    row_tok = jnp.full((rows,), N, jnp.int32).at[dest].set(sorted_tok)
    blk_start = jnp.arange(n_blocks, dtype=jnp.int32) * MOE_BLOCK
    blk_expert = jnp.minimum(jnp.searchsorted(pad_end, blk_start, side='right'), N_EXPERTS - 1)
    x_pad = jnp.concatenate([xt, jnp.zeros((1, D), xt.dtype)], 0)

    def run_block(args):
        toks, e = args
        hb = x_pad[toks] @ w_e_in[e] + b_e_in[e]
        glu = jnp.minimum(hb[:, ::2], SWIGLU_LIMIT)
        lin = jnp.clip(hb[:, 1::2], -SWIGLU_LIMIT, SWIGLU_LIMIT)
        act = glu * jax.nn.sigmoid(SWIGLU_ALPHA * glu) * (lin + 1.0)
        return act @ w_e_out[e] + b_e_out[e]

    yb = lax.map(run_block, (row_tok.reshape(n_blocks, MOE_BLOCK), blk_expert))
    y_sorted = yb.reshape(rows, D)[dest].astype(jnp.float32)
    w_sorted = gate.reshape(A)[order]
    y = jax.ops.segment_sum(y_sorted * w_sorted[:, None], sorted_tok, num_segments=N)
    return y.reshape(B, T, D).astype(x.dtype)


def _layer(x, pos0, k_cache, v_cache, pool_hist, lam_init,
           w_in, w_pool_mix, pool_scale, w_pool_out, lambda_q1, lambda_k1, lambda_q2, lambda_k2,
           attn_norm_g, w_attn_out, w_out, ln1_g, ln1_b,
           w_router, b_router, w_expert_in, b_expert_in, w_expert_out, b_expert_out, ln2_g, ln2_b):
    B, T, _ = x.shape
    pos = pos0 + jnp.arange(T, dtype=jnp.int32)
    h = x @ w_in
    c1 = POOL_DIM
    c2 = c1 + QK_DIM
    c3 = c2 + QK_DIM
    c4 = c3 + ATTN_V_WIDTH
    q = _rope(h[..., c1:c2].reshape(B, T, N_HEADS, 2, HEAD_DIM), pos)
    k = _rope(h[..., c2:c3].reshape(B, T, N_HEADS, 2, HEAD_DIM), pos)
    v = h[..., c3:c4].reshape(B, T, N_HEADS, V_DIM)
    gates = jax.nn.sigmoid(h[..., c4:].astype(jnp.float32)).reshape(B, T, N_BRANCHES, D_MODEL)
    pool_y, pool_new = _pool_branch(pool_hist, h[..., :c1], pos0, w_pool_mix, pool_scale)
    lam = (jnp.exp(jnp.sum(lambda_q1.astype(jnp.float32) * lambda_k1.astype(jnp.float32)))
           - jnp.exp(jnp.sum(lambda_q2.astype(jnp.float32) * lambda_k2.astype(jnp.float32))) + lam_init)
    if k_cache is None:
        o = _attn_prompt(q, k, v, lam)
    else:
        k_all = jnp.concatenate([k_cache.astype(k.dtype), k], 1)
        v_all = jnp.concatenate([v_cache.astype(v.dtype), v], 1)
        o = _diff_attn(q, k_all, v_all, pos, jnp.arange(k_all.shape[1], dtype=jnp.int32), lam)
    attn_y = _head_norm(o, attn_norm_g, lam_init)
    mixed = (gates[:, :, 0] * (pool_y @ w_pool_out).astype(jnp.float32)
             + gates[:, :, 1] * (attn_y @ w_attn_out).astype(jnp.float32))
    mix_out = mixed.astype(x.dtype) @ w_out
    x1 = _layer_norm(DN_ALPHA * x + mix_out, ln1_g, ln1_b)
    ffn = _moe(x1, w_router, b_router, w_expert_in, b_expert_in, w_expert_out, b_expert_out)
    x2 = _layer_norm(DN_ALPHA * x1 + ffn, ln2_g, ln2_b)
    return x2, k, v, pool_new


def setup_inputs(seed: int = 0) -> dict:
    key = jax.random.key(seed)
    ks = jax.random.split(key, 32)
    L = DEPTH

    def nrm(k, shape, scale):
        return jax.random.normal(k, shape, jnp.float32) * scale

    return {
        'x_prompt': nrm(ks[0], (BATCH, SEQ, D_MODEL), 1.0),
        'x_sample': nrm(ks[1], (DEC_BATCH, DEC_SEQ, D_MODEL), 1.0),
        'cache_k': nrm(ks[2], (L, DEC_BATCH, PAST_LEN, N_HEADS, 2, HEAD_DIM), 1.0),
        'cache_v': nrm(ks[3], (L, DEC_BATCH, PAST_LEN, N_HEADS, V_DIM), 1.0),
        'state_pool': nrm(ks[4], (L, DEC_BATCH, POOL_HIST, POOL_DIM), 1.0),
        'w_in': nrm(ks[5], (L, D_MODEL, IN_WIDTH), D_MODEL ** -0.5),
        'w_pool_mix': nrm(ks[6], (L, POOL_GROUPS, POOL_GROUP_DIM, POOL_GROUP_DIM), POOL_GROUP_DIM ** -0.5),
        'pool_scale': 1.0 + nrm(ks[7], (L, POOL_DIM), 0.1),
        'w_pool_out': nrm(ks[8], (L, POOL_DIM, D_MODEL), POOL_DIM ** -0.5),
        'lambda_q1': nrm(ks[9], (L, HEAD_DIM), 0.1),
        'lambda_k1': nrm(ks[10], (L, HEAD_DIM), 0.1),
        'lambda_q2': nrm(ks[11], (L, HEAD_DIM), 0.1),
        'lambda_k2': nrm(ks[12], (L, HEAD_DIM), 0.1),
        'attn_norm_g': 1.0 + nrm(ks[13], (L, V_DIM), 0.1),
        'w_attn_out': nrm(ks[14], (L, ATTN_V_WIDTH, D_MODEL), ATTN_V_WIDTH ** -0.5),
        'w_out': nrm(ks[15], (L, D_MODEL, D_MODEL), D_MODEL ** -0.5 * DN_BETA),
        'ln1_g': 1.0 + nrm(ks[16], (L, D_MODEL), 0.1),
        'ln1_b': nrm(ks[17], (L, D_MODEL), 0.02),
        'w_router': nrm(ks[18], (L, D_MODEL, N_EXPERTS), D_MODEL ** -0.5),
        'b_router': nrm(ks[19], (L, N_EXPERTS), 0.01),
        'w_expert_in': nrm(ks[20], (L, N_EXPERTS, D_MODEL, 2 * D_EXPERT), D_MODEL ** -0.5),
        'b_expert_in': nrm(ks[21], (L, N_EXPERTS, 2 * D_EXPERT), 0.02),
        'w_expert_out': nrm(ks[22], (L, N_EXPERTS, D_EXPERT, D_MODEL), D_EXPERT ** -0.5 * DN_BETA),
        'b_expert_out': nrm(ks[23], (L, N_EXPERTS, D_MODEL), 0.02),
        'ln2_g': 1.0 + nrm(ks[24], (L, D_MODEL), 0.1),
        'ln2_b': nrm(ks[25], (L, D_MODEL), 0.02),
    }


def reference(x_prompt, x_sample, cache_k, cache_v, state_pool,
              w_in, w_pool_mix, pool_scale, w_pool_out, lambda_q1, lambda_k1, lambda_q2, lambda_k2,
              attn_norm_g, w_attn_out, w_out, ln1_g, ln1_b,
              w_router, b_router, w_expert_in, b_expert_in, w_expert_out, b_expert_out, ln2_g, ln2_b):
    y_p = x_prompt
    y_s = x_sample
    past_len = cache_k.shape[2]
    kp, vp, pp, ksm, vsm, psm = [], [], [], [], [], []
    for l in range(DEPTH):
        lam_init = 0.8 - 0.6 * math.exp(-0.3 * l)
        params = (w_in[l], w_pool_mix[l], pool_scale[l], w_pool_out[l],
                  lambda_q1[l], lambda_k1[l], lambda_q2[l], lambda_k2[l],
                  attn_norm_g[l], w_attn_out[l], w_out[l], ln1_g[l], ln1_b[l],
                  w_router[l], b_router[l], w_expert_in[l], b_expert_in[l],
                  w_expert_out[l], b_expert_out[l], ln2_g[l], ln2_b[l])
        fresh_hist = jnp.zeros((y_p.shape[0], POOL_HIST, POOL_DIM), y_p.dtype)
        y_p, k1, v1, s1 = _layer(y_p, 0, None, None, fresh_hist, lam_init, *params)
        y_s, k2, v2, s2 = _layer(y_s, past_len, cache_k[l], cache_v[l], state_pool[l], lam_init, *params)
        kp.append(k1)
        vp.append(v1)
        pp.append(s1)
        ksm.append(k2)
        vsm.append(v2)
        psm.append(s2)
    k_prompt = jnp.stack(kp)
    v_prompt = jnp.stack(vp)
    pool_prompt = jnp.stack(pp)
    k_sample = jnp.stack(ksm)
    v_sample = jnp.stack(vsm)
    pool_sample = jnp.stack(psm)
    return (y_p, y_s, k_prompt, v_prompt, pool_prompt, k_sample, v_sample, pool_sample)
```

```python
import functools
import math

import jax
import jax.numpy as jnp
from jax import lax
from jax.experimental import pallas as pl
from jax.experimental.pallas import tpu as pltpu

D_MODEL = 1024
CHUNK = 64
POOL_WINDOWS = (2, 4, 8, 16)
POOL_GROUP_DIM = 128
POOL_DIM = len(POOL_WINDOWS) * POOL_GROUP_DIM
POOL_HIST = max(POOL_WINDOWS) - 1
HIST_ROWS = POOL_HIST + 1
N_HEADS = 8
HEAD_DIM = 64
V_DIM = 2 * HEAD_DIM
QK_DIM = N_HEADS * 2 * HEAD_DIM
ATTN_V_WIDTH = N_HEADS * V_DIM
ATTN_SCALE = HEAD_DIM ** -0.5
ROPE_THETA = 10000.0
SUBLN_EPS = 1e-5
N_EXPERTS = 32
TOP_K = 4
D_EXPERT = 1024
SWIGLU_LIMIT = 7.0
SWIGLU_ALPHA = 1.702
LN_EPS = 1e-5
NEG_INF = -1e30
LANES = 128

F32 = jnp.float32
BF16 = jnp.bfloat16

VMEM_LIMIT = 56 * 1024 * 1024
EXPERT_BLOCK = 256


def _dot(a, b):
    return jnp.dot(a, b, preferred_element_type=F32)


def _dot_nt(a, b):
    return lax.dot_general(a, b, (((1,), (1,)), ((), ())), preferred_element_type=F32)


def _params(semantics):
    return pltpu.CompilerParams(dimension_semantics=semantics, vmem_limit_bytes=VMEM_LIMIT)


def _inproj_kernel(x_ref, w_ref, cos_ref, sin_ref, icnt_ref, hist_ref, wmix_ref, pscale_ref,
                   q_ref, k_ref, v_ref, kb_ref, vb_ref, py_ref, pnew_ref, ext_ref, *, bb, tm):
    t = pl.program_id(1)
    x = x_ref[...].astype(BF16)

    u = _dot(x, w_ref[:, 0:POOL_DIM])

    @pl.when(t == 0)
    def _():
        ext_ref[:, 0:HIST_ROWS, :] = hist_ref[...]

    for b in range(bb):
        ext_ref[b, HIST_ROWS:HIST_ROWS + tm, :] = u[b * tm:(b + 1) * tm]
    for b in range(bb):
        for g, w in enumerate(POOL_WINDOWS):
            cols = slice(g * POOL_GROUP_DIM, (g + 1) * POOL_GROUP_DIM)
            cur = ext_ref[b, HIST_ROWS:HIST_ROWS + tm, cols]
            acc = cur
            for j in range(1, w):
                acc = acc + ext_ref[b, HIST_ROWS - j:HIST_ROWS - j + tm, cols]
            d = acc * icnt_ref[:, cols] - cur
            y = _dot(d.astype(BF16), wmix_ref[g]) * pscale_ref[:, cols]
            py_ref[b * tm:(b + 1) * tm, cols] = y.astype(BF16)
    tail = ext_ref[:, tm:tm + HIST_ROWS, :]
    pnew_ref[...] = tail
    ext_ref[:, 0:HIST_ROWS, :] = tail

    cos = cos_ref[...]
    sin = sin_ref[...]
    if bb > 1:
        cos = jnp.concatenate([cos] * bb, axis=0)
        sin = jnp.concatenate([sin] * bb, axis=0)
    lane = lax.broadcasted_iota(jnp.int32, cos.shape, 1)
    first_half = (lane % HEAD_DIM) < (HEAD_DIM // 2)

    def rope(z):
        partner = jnp.where(first_half, pltpu.roll(z, LANES - HEAD_DIM // 2, 1),
                            pltpu.roll(z, HEAD_DIM // 2, 1))
        return z * cos + partner * sin

    hq = _dot(x, w_ref[:, POOL_DIM:POOL_DIM + QK_DIM])
    for h in range(N_HEADS):
        sl = slice(h * V_DIM, (h + 1) * V_DIM)
        q_ref[:, sl] = (rope(hq[:, sl]) * ATTN_SCALE).astype(BF16)
    hk = _dot(x, w_ref[:, POOL_DIM + QK_DIM:POOL_DIM + 2 * QK_DIM])
    for h in range(N_HEADS):
        sl = slice(h * V_DIM, (h + 1) * V_DIM)
        kr = rope(hk[:, sl])
        k_ref[:, sl] = kr
        kb_ref[:, sl] = kr.astype(BF16)
    hv = _dot(x, w_ref[:, POOL_DIM + 2 * QK_DIM:POOL_DIM + 2 * QK_DIM + ATTN_V_WIDTH])
    v_ref[...] = hv
    vb_ref[...] = hv.astype(BF16)


def _inproj(x2d, w_qkvp, cos, sin, icnt, hist, wmix, pscale, *, n_streams, seq, bb, tm):
    n = n_streams * seq
    nt = seq // tm
    grid = (n_streams // bb, nt)
    rows = bb * tm
    row_map = lambda i, t: (i * nt + t, 0)
    const2 = lambda i, t: (0, 0)
    width = w_qkvp.shape[1]
    out_shape = (
        jax.ShapeDtypeStruct((n, QK_DIM), BF16),
        jax.ShapeDtypeStruct((n, QK_DIM), F32),
        jax.ShapeDtypeStruct((n, ATTN_V_WIDTH), F32),
        jax.ShapeDtypeStruct((n, QK_DIM), BF16),
        jax.ShapeDtypeStruct((n, ATTN_V_WIDTH), BF16),
        jax.ShapeDtypeStruct((n, POOL_DIM), BF16),
        jax.ShapeDtypeStruct((n_streams, HIST_ROWS, POOL_DIM), F32),
    )
    return pl.pallas_call(
        functools.partial(_inproj_kernel, bb=bb, tm=tm),
        out_shape=out_shape,
        grid=grid,
        in_specs=[
            pl.BlockSpec((rows, D_MODEL), row_map),
            pl.BlockSpec((D_MODEL, width), const2),
            pl.BlockSpec((tm, LANES), lambda i, t: (t, 0)),
            pl.BlockSpec((tm, LANES), lambda i, t: (t, 0)),
            pl.BlockSpec((tm, POOL_DIM), lambda i, t: (t, 0)),
            pl.BlockSpec((bb, HIST_ROWS, POOL_DIM), lambda i, t: (i, 0, 0)),
            pl.BlockSpec((len(POOL_WINDOWS), POOL_GROUP_DIM, POOL_GROUP_DIM), lambda i, t: (0, 0, 0)),
            pl.BlockSpec((1, POOL_DIM), const2),
        ],
        out_specs=(
            pl.BlockSpec((rows, QK_DIM), row_map),
            pl.BlockSpec((rows, QK_DIM), row_map),
            pl.BlockSpec((rows, ATTN_V_WIDTH), row_map),
            pl.BlockSpec((rows, QK_DIM), row_map),
            pl.BlockSpec((rows, ATTN_V_WIDTH), row_map),
            pl.BlockSpec((rows, POOL_DIM), row_map),
            pl.BlockSpec((bb, HIST_ROWS, POOL_DIM), lambda i, t: (i, 0, 0)),
        ),
        scratch_shapes=[pltpu.VMEM((bb, HIST_ROWS + tm, POOL_DIM), F32)],
        compiler_params=_params(("arbitrary", "arbitrary")),
        name="inproj",
    )(x2d, w_qkvp, cos, sin, icnt, hist, wmix, pscale)


def _lambda_value(lam_ref, lam_init):
    lv = lam_ref[...]
    s1 = jnp.sum(lv[0:1] * lv[1:2], axis=1, keepdims=True)
    s2 = jnp.sum(lv[2:3] * lv[3:4], axis=1, keepdims=True)
    return jnp.exp(s1) - jnp.exp(s2) + lam_init


def _head_norm(o, g, lam_init):
    ms = jnp.mean(o * o, axis=-1, keepdims=True)
    return o * lax.rsqrt(ms + SUBLN_EPS) * g * (1.0 - lam_init)


def _attn_prompt_kernel(lam_ref, g_ref, q_ref, k_ref, v_ref, o_ref, *, tq, lam_init):
    i = pl.program_id(2)
    lam = _lambda_value(lam_ref, lam_init)
    q = q_ref[...]
    lane = lax.broadcasted_iota(jnp.int32, q.shape, 1)
    zero = jnp.zeros_like(q)
    qc = (jnp.where(lane < HEAD_DIM, q, zero), jnp.where(lane >= HEAD_DIM, q, zero))

    def tile_update(j, carry, masked):
        start = pl.multiple_of(j * tq, tq)
        kt = k_ref[pl.ds(start, tq), :]
        vt = v_ref[pl.ds(start, tq), :]
        out = []
        for c in range(2):
            m, l, a = carry[3 * c:3 * c + 3]
            s = _dot_nt(qc[c], kt)
            if masked:
                qpos = i * tq + lax.broadcasted_iota(jnp.int32, s.shape, 0)
                kpos = j * tq + lax.broadcasted_iota(jnp.int32, s.shape, 1)
                s = jnp.where((kpos // CHUNK) <= (qpos // CHUNK), s, NEG_INF)
            m_new = jnp.maximum(m, jnp.max(s, axis=1, keepdims=True))
            alpha = jnp.exp(m - m_new)
            p = jnp.exp(s - m_new)
            l = alpha * l + jnp.sum(p, axis=1, keepdims=True)
            a = alpha * a + _dot(p.astype(BF16), vt)
            out += [m_new, l, a]
        return tuple(out)

    init = []
    for _ in range(2):
        init += [jnp.full((tq, 1), NEG_INF, F32), jnp.zeros((tq, 1), F32), jnp.zeros((tq, V_DIM), F32)]
    carry = lax.fori_loop(0, i, lambda j, c: tile_update(j, c, False), tuple(init))
    m0, l0, a0, m1, l1, a1 = tile_update(i, carry, True)
    o = a0 / l0 - lam * (a1 / l1)
    o_ref[...] = _head_norm(o, g_ref[...], lam_init).astype(o_ref.dtype)


def _attn_prompt(lam_vecs, norm_g, q, kb, vb, *, n_streams, seq, tq, lam_init):
    nq = seq // tq
    return pl.pallas_call(
        functools.partial(_attn_prompt_kernel, tq=tq, lam_init=lam_init),
        out_shape=jax.ShapeDtypeStruct((n_streams * seq, ATTN_V_WIDTH), BF16),
        grid=(n_streams, N_HEADS, nq),
        in_specs=[
            pl.BlockSpec((4, HEAD_DIM), lambda b, h, i: (0, 0)),
            pl.BlockSpec((1, V_DIM), lambda b, h, i: (0, 0)),
            pl.BlockSpec((tq, V_DIM), lambda b, h, i: (b * nq + i, h)),
            pl.BlockSpec((seq, V_DIM), lambda b, h, i: (b, h)),
            pl.BlockSpec((seq, V_DIM), lambda b, h, i: (b, h)),
        ],
        out_specs=pl.BlockSpec((tq, V_DIM), lambda b, h, i: (b * nq + i, h)),
        compiler_params=_params(("arbitrary", "arbitrary", "arbitrary")),
        name="attn_prompt",
    )(lam_vecs, norm_g, q, kb, vb)


def _attn_sample_kernel(lam_ref, g_ref, q_ref, kc_ref, vc_ref, kn_ref, vn_ref, o_ref,
                        s_scr, w_scr, wn_scr, m_scr, acc_scr, qbd_scr, *, nk, tk, tn, past, lam_init):
    j = pl.program_id(1)
    rows = 2 * N_HEADS * tn
    half = N_HEADS * tn

    @pl.when(j == 0)
    def _():
        q = q_ref[...]
        qt = jnp.concatenate([q] * (2 * N_HEADS), axis=0)
        r = lax.broadcasted_iota(jnp.int32, qt.shape, 0)
        l = lax.broadcasted_iota(jnp.int32, qt.shape, 1)
        keep = ((r // half) == ((l % V_DIM) // HEAD_DIM)) & (((r % half) // tn) == (l // V_DIM))
        qbd_scr[...] = jnp.where(keep, qt, jnp.zeros_like(qt))
        m_scr[...] = jnp.full(m_scr.shape, NEG_INF, F32)
        acc_scr[...] = jnp.zeros(acc_scr.shape, F32)

    @pl.when(j < nk)
    def _():
        s = _dot_nt(qbd_scr[...], kc_ref[...].astype(BF16))
        s_scr[j] = s
        m_scr[...] = jnp.maximum(m_scr[...], jnp.max(s, axis=1, keepdims=True))

    @pl.when(j == nk - 1)
    def _():
        lam = _lambda_value(lam_ref, lam_init)
        sn = _dot_nt(qbd_scr[...], kn_ref[...].astype(BF16))
        qpos = past + (lax.broadcasted_iota(jnp.int32, sn.shape, 0) % tn)
        kpos = past + lax.broadcasted_iota(jnp.int32, sn.shape, 1)
        sn = jnp.where((kpos // CHUNK) <= (qpos // CHUNK), sn, NEG_INF)
        m = jnp.maximum(m_scr[...], jnp.max(sn, axis=1, keepdims=True))
        pn = jnp.exp(sn - m)
        l = jnp.sum(pn, axis=1, keepdims=True)
        for c in range(nk):
            p = jnp.exp(s_scr[c] - m)
            s_scr[c] = p
            l = l + jnp.sum(p, axis=1, keepdims=True)
        r0 = 1.0 / l[:half]
        r1 = lam / l[half:]
        wn_scr[...] = pn[:half] * r0 - pn[half:] * r1
        for c in range(nk):
            p = s_scr[c]
            w_scr[c] = (p[:half] * r0 - p[half:] * r1).astype(BF16)

    @pl.when(j >= nk)
    def _():
        acc_scr[...] += _dot(w_scr[j - nk], vc_ref[...].astype(BF16))

    @pl.when(j == 2 * nk - 1)
    def _():
        acc = acc_scr[...] + _dot(wn_scr[...].astype(BF16), vn_ref[...].astype(BF16))
        g = g_ref[...]
        for h in range(N_HEADS):
            o = acc[h * tn:(h + 1) * tn, h * V_DIM:(h + 1) * V_DIM]
            o_ref[:, h * V_DIM:(h + 1) * V_DIM] = _head_norm(o, g, lam_init).astype(o_ref.dtype)


def _attn_sample(lam_vecs, norm_g, q, kc, vc, kn, vn, *, n_streams, tn, past, tk, lam_init):
    nk = past // tk
    rows = 2 * N_HEADS * tn
    half = N_HEADS * tn
    return pl.pallas_call(
        functools.partial(_attn_sample_kernel, nk=nk, tk=tk, tn=tn, past=past, lam_init=lam_init),
        out_shape=jax.ShapeDtypeStruct((n_streams * tn, ATTN_V_WIDTH), BF16),
        grid=(n_streams, 2 * nk),
        in_specs=[
            pl.BlockSpec((4, HEAD_DIM), lambda b, j: (0, 0)),
            pl.BlockSpec((1, V_DIM), lambda b, j: (0, 0)),
            pl.BlockSpec((tn, QK_DIM), lambda b, j: (b, 0)),
            pl.BlockSpec((None, tk, QK_DIM), lambda b, j: (b, jnp.minimum(j, nk - 1), 0)),
            pl.BlockSpec((None, tk, ATTN_V_WIDTH), lambda b, j: (b, jnp.maximum(j - nk, 0), 0)),
            pl.BlockSpec((tn, QK_DIM), lambda b, j: (b, 0)),
            pl.BlockSpec((tn, ATTN_V_WIDTH), lambda b, j: (b, 0)),
        ],
        out_specs=pl.BlockSpec((tn, ATTN_V_WIDTH), lambda b, j: (b, 0)),
        scratch_shapes=[
            pltpu.VMEM((nk, rows, tk), F32),
            pltpu.VMEM((nk, half, tk), BF16),
            pltpu.VMEM((half, tn), F32),
            pltpu.VMEM((rows, 1), F32),
            pltpu.VMEM((half, ATTN_V_WIDTH), F32),
            pltpu.VMEM((rows, QK_DIM), BF16),
        ],
        compiler_params=_params(("arbitrary", "arbitrary")),
        name="attn_sample",
    )(lam_vecs, norm_g, q, kc, vc, kn, vn)


def _layer_norm(z, g, b):
    mu = jnp.mean(z, axis=-1, keepdims=True)
    zc = z - mu
    var = jnp.mean(zc * zc, axis=-1, keepdims=True)
    return zc * lax.rsqrt(var + LN_EPS) * g + b


def _postmix_kernel(x_ref, py_ref, ay_ref, wg_ref, wpo_ref, wao_ref, wout_ref, g1_ref, b1_ref,
                    wr_ref, br_ref, x1_ref, idx_ref, gate_ref, rank_ref, cnt_ref, carry_ref,
                    *, tm, dn_alpha):
    step = pl.program_id(0)

    @pl.when(step == 0)
    def _():
        carry_ref[...] = jnp.zeros(carry_ref.shape, F32)

    x = x_ref[...]
    xb = x.astype(BF16)
    gates = jax.nn.sigmoid(_dot(xb, wg_ref[...]))
    a = _dot(py_ref[...], wpo_ref[...])
    b = _dot(ay_ref[...], wao_ref[...])
    mixed = gates[:, :D_MODEL] * a + gates[:, D_MODEL:] * b
    mo = _dot(mixed.astype(BF16), wout_ref[...])
    x1 = _layer_norm(dn_alpha * x + mo, g1_ref[...], b1_ref[...])
    x1_ref[...] = x1

    logits = _dot(x1.astype(BF16), wr_ref[...]) + br_ref[...]
    lane = lax.broadcasted_iota(jnp.int32, logits.shape, 1)
    work = logits
    vals, idxs = [], []
    for _ in range(TOP_K):
        mx = jnp.max(work, axis=1, keepdims=True)
        ix = jnp.min(jnp.where(work == mx, lane, N_EXPERTS), axis=1, keepdims=True)
        vals.append(mx)
        idxs.append(ix)
        work = jnp.where(lane == ix, -jnp.inf, work)
    exps = [jnp.exp(v - vals[0]) for v in vals]
    denom = exps[0] + exps[1] + exps[2] + exps[3]

    onehot = jnp.zeros(logits.shape, F32)
    for ix in idxs:
        onehot = onehot + (lane == ix).astype(F32)
    r = lax.broadcasted_iota(jnp.int32, (tm, tm), 0)
    c = lax.broadcasted_iota(jnp.int32, (tm, tm), 1)
    tri = jnp.where(c < r, 1.0, 0.0).astype(BF16)
    before = _dot(tri, onehot.astype(BF16)) + carry_ref[...]
    carry_ref[...] = carry_ref[...] + jnp.sum(onehot, axis=0, keepdims=True)
    cnt_ref[...] = carry_ref[...]

    lane_out = lax.broadcasted_iota(jnp.int32, (tm, LANES), 1)
    idx_out = jnp.zeros((tm, LANES), jnp.int32)
    rank_out = jnp.zeros((tm, LANES), jnp.int32)
    gate_out = jnp.zeros((tm, LANES), F32)
    for k in range(TOP_K):
        rank_k = jnp.sum(jnp.where(lane == idxs[k], before, 0.0), axis=1, keepdims=True).astype(jnp.int32)
        idx_out = jnp.where(lane_out == k, idxs[k], idx_out)
        rank_out = jnp.where(lane_out == k, rank_k, rank_out)
        gate_out = jnp.where(lane_out == k, exps[k] / denom, gate_out)
    idx_ref[...] = idx_out
    rank_ref[...] = rank_out
    gate_ref[...] = gate_out


def _postmix(x2d, py, ay, wg, wpo, wao, wout, g1, b1, wr, br, *, tm, dn_alpha):
    n = x2d.shape[0]
    row = lambda i: (i, 0)
    const = lambda i: (0, 0)
    out_shape = (
        jax.ShapeDtypeStruct((n, D_MODEL), F32),
        jax.ShapeDtypeStruct((n, LANES), jnp.int32),
        jax.ShapeDtypeStruct((n, LANES), F32),
        jax.ShapeDtypeStruct((n, LANES), jnp.int32),
        jax.ShapeDtypeStruct((1, N_EXPERTS), F32),
    )
    return pl.pallas_call(
        functools.partial(_postmix_kernel, tm=tm, dn_alpha=dn_alpha),
        out_shape=out_shape,
        grid=(n // tm,),
        in_specs=[
            pl.BlockSpec((tm, D_MODEL), row),
            pl.BlockSpec((tm, POOL_DIM), row),
            pl.BlockSpec((tm, ATTN_V_WIDTH), row),
            pl.BlockSpec(wg.shape, const),
            pl.BlockSpec(wpo.shape, const),
            pl.BlockSpec(wao.shape, const),
            pl.BlockSpec(wout.shape, const),
            pl.BlockSpec((1, D_MODEL), const),
            pl.BlockSpec((1, D_MODEL), const),
            pl.BlockSpec(wr.shape, const),
            pl.BlockSpec((1, N_EXPERTS), const),
        ],
        out_specs=(
            pl.BlockSpec((tm, D_MODEL), row),
            pl.BlockSpec((tm, LANES), row),
            pl.BlockSpec((tm, LANES), row),
            pl.BlockSpec((tm, LANES), row),
            pl.BlockSpec((1, N_EXPERTS), const),
        ),
        scratch_shapes=[pltpu.VMEM((1, N_EXPERTS), F32)],
        compiler_params=_params(("arbitrary",)),
        name="postmix",
    )(x2d, py, ay, wg, wpo, wao, wout, g1, b1, wr, br)


def _dispatch_kernel(base_ref, idx_ref, rank_ref, x_ref, xs_in_ref, xs_ref, sem, *, tm):
    del xs_in_ref

    def row_copy(r, k):
        a = r * TOP_K + k
        slot = base_ref[idx_ref[a]] + rank_ref[a]
        return pltpu.make_async_copy(x_ref.at[pl.ds(r, 1)], xs_ref.at[pl.ds(slot, 1)], sem)

    def start(r, _):
        for k in range(TOP_K):
            row_copy(r, k).start()
        return 0

    def wait(r, _):
        for k in range(TOP_K):
            row_copy(r, k).wait()
        return 0

    lax.fori_loop(0, tm, start, 0)
    lax.fori_loop(0, tm, wait, 0)


def _dispatch(base, idx_flat, rank_flat, x1, xs, *, tm):
    n = x1.shape[0]
    grid_spec = pltpu.PrefetchScalarGridSpec(
        num_scalar_prefetch=1,
        grid=(n // tm,),
        in_specs=[
            pl.BlockSpec((tm * TOP_K,), lambda i, base: (i,), memory_space=pltpu.SMEM),
            pl.BlockSpec((tm * TOP_K,), lambda i, base: (i,), memory_space=pltpu.SMEM),
            pl.BlockSpec((tm, D_MODEL), lambda i, base: (i, 0)),
            pl.BlockSpec(memory_space=pl.ANY),
        ],
        out_specs=pl.BlockSpec(memory_space=pl.ANY),
        scratch_shapes=[pltpu.SemaphoreType.DMA],
    )
    return pl.pallas_call(
        functools.partial(_dispatch_kernel, tm=tm),
        out_shape=jax.ShapeDtypeStruct(xs.shape, xs.dtype),
        grid_spec=grid_spec,
        input_output_aliases={4: 0},
        compiler_params=_params(("arbitrary",)),
        name="dispatch",
    )(base, idx_flat, rank_flat, x1, xs)


def _experts_kernel(be_ref, nu_ref, xs_ref, wg_ref, wl_ref, bg_ref, bl_ref, wo_ref, bo_ref, y_ref):
    used = pl.program_id(0) < nu_ref[0]

    @pl.when(jnp.logical_not(used))
    def _():
        y_ref[...] = jnp.zeros(y_ref.shape, y_ref.dtype)

    @pl.when(used)
    def _():
        xb = xs_ref[...].astype(BF16)
        glu = jnp.minimum(_dot(xb, wg_ref[...]) + bg_ref[...], SWIGLU_LIMIT)
        lin = jnp.clip(_dot(xb, wl_ref[...]) + bl_ref[...], -SWIGLU_LIMIT, SWIGLU_LIMIT)
        act = glu * jax.nn.sigmoid(SWIGLU_ALPHA * glu) * (lin + 1.0)
        y_ref[...] = _dot(act.astype(BF16), wo_ref[...]) + bo_ref[...]


def _experts(blk_expert, n_used, xs, w_glu, w_lin, b_glu, b_lin, w_out, b_out):
    rows = xs.shape[0]
    n_blocks = rows // EXPERT_BLOCK
    row = lambda i, be, nu: (jnp.minimum(i, nu[0] - 1), 0)
    wsel = lambda i, be, nu: (be[i], 0, 0)
    grid_spec = pltpu.PrefetchScalarGridSpec(
        num_scalar_prefetch=2,
        grid=(n_blocks,),
        in_specs=[
            pl.BlockSpec((EXPERT_BLOCK, D_MODEL), row),
            pl.BlockSpec((None, D_MODEL, D_EXPERT), wsel),
            pl.BlockSpec((None, D_MODEL, D_EXPERT), wsel),
            pl.BlockSpec((None, 1, D_EXPERT), wsel),
            pl.BlockSpec((None, 1, D_EXPERT), wsel),
            pl.BlockSpec((None, D_EXPERT, D_MODEL), wsel),
            pl.BlockSpec((None, 1, D_MODEL), wsel),
        ],
        out_specs=pl.BlockSpec((EXPERT_BLOCK, D_MODEL), lambda i, be, nu: (i, 0)),
    )
    return pl.pallas_call(
        _experts_kernel,
        out_shape=jax.ShapeDtypeStruct((rows, D_MODEL), F32),
        grid_spec=grid_spec,
        compiler_params=_params(("arbitrary",)),
        name="experts",
    )(blk_expert, n_used, xs, w_glu, w_lin, b_glu, b_lin, w_out, b_out)


def _combine_kernel(base_ref, idx_ref, rank_ref, gate_ref, x1_ref, g2_ref, b2_ref, yb_ref, o_ref,
                    buf, sem, *, tm, dn_alpha):
    def row_copy(r, k):
        a = r * TOP_K + k
        slot = base_ref[idx_ref[a]] + rank_ref[a]
        return pltpu.make_async_copy(yb_ref.at[pl.ds(slot, 1)], buf.at[k, pl.ds(r, 1)], sem)

    def start(r, _):
        for k in range(TOP_K):
            row_copy(r, k).start()
        return 0

    def wait(r, _):
        for k in range(TOP_K):
            row_copy(r, k).wait()
        return 0

    lax.fori_loop(0, tm, start, 0)
    lax.fori_loop(0, tm, wait, 0)
    gate = gate_ref[...]
    y = gate[:, 0:1] * buf[0]
    for k in range(1, TOP_K):
        y = y + gate[:, k:k + 1] * buf[k]
    o_ref[...] = _layer_norm(dn_alpha * x1_ref[...] + y, g2_ref[...], b2_ref[...])


def _combine(base, idx_flat, rank_flat, gate, x1, g2, b2, yb, *, tm, dn_alpha):
    n = x1.shape[0]
    grid_spec = pltpu.PrefetchScalarGridSpec(
        num_scalar_prefetch=1,
        grid=(n // tm,),
        in_specs=[
            pl.BlockSpec((tm * TOP_K,), lambda i, base: (i,), memory_space=pltpu.SMEM),
            pl.BlockSpec((tm * TOP_K,), lambda i, base: (i,), memory_space=pltpu.SMEM),
            pl.BlockSpec((tm, LANES), lambda i, base: (i, 0)),
            pl.BlockSpec((tm, D_MODEL), lambda i, base: (i, 0)),
            pl.BlockSpec((1, D_MODEL), lambda i, base: (0, 0)),
            pl.BlockSpec((1, D_MODEL), lambda i, base: (0, 0)),
            pl.BlockSpec(memory_space=pl.ANY),
        ],
        out_specs=pl.BlockSpec((tm, D_MODEL), lambda i, base: (i, 0)),
        scratch_shapes=[pltpu.VMEM((TOP_K, tm, D_MODEL), F32), pltpu.SemaphoreType.DMA],
    )
    return pl.pallas_call(
        functools.partial(_combine_kernel, tm=tm, dn_alpha=dn_alpha),
        out_shape=jax.ShapeDtypeStruct((n, D_MODEL), F32),
        grid_spec=grid_spec,
        compiler_params=_params(("arbitrary",)),
        name="combine",
    )(base, idx_flat, rank_flat, gate, x1, g2, b2, yb)


def _position_tables(pos0, seq):
    half = HEAD_DIM // 2
    pos = pos0 + jnp.arange(seq, dtype=jnp.int32)
    inv = ROPE_THETA ** (-jnp.arange(half, dtype=F32) / half)
    ang = pos.astype(F32)[:, None] * inv[None, :]
    cos, sin = jnp.cos(ang), jnp.sin(ang)
    cos_t = jnp.concatenate([cos, cos, cos, cos], axis=-1)
    sin_t = jnp.concatenate([-sin, sin, -sin, sin], axis=-1)
    icnt = jnp.concatenate(
        [jnp.broadcast_to((1.0 / jnp.minimum(pos + 1, w).astype(F32))[:, None], (seq, POOL_GROUP_DIM))
         for w in POOL_WINDOWS], axis=-1)
    return cos_t, sin_t, icnt


def _tile(n, pref):
    t = min(n, pref)
    while n % t:
        t //= 2
    return t


def kernel(x_prompt, x_sample, cache_k, cache_v, state_pool, w_in, w_pool_mix, pool_scale, w_pool_out,
           lambda_q1, lambda_k1, lambda_q2, lambda_k2, attn_norm_g, w_attn_out, w_out, ln1_g, ln1_b,
           w_router, b_router, w_expert_in, b_expert_in, w_expert_out, b_expert_out, ln2_g, ln2_b):
    depth = w_in.shape[0]
    assert depth == 1, "single-layer step"
    dn_alpha = (2.0 * depth) ** 0.25
    lam_init = 0.8 - 0.6 * math.exp(-0.3 * 0)
    bp, sp, _ = x_prompt.shape
    bs, ss, _ = x_sample.shape
    past = cache_k.shape[2]
    np_, ns = bp * sp, bs * ss

    c_gate = POOL_DIM + 2 * QK_DIM + ATTN_V_WIDTH
    w_qkvp = w_in[0, :, :c_gate].astype(BF16)
    w_gate = w_in[0, :, c_gate:].astype(BF16)
    wmix = w_pool_mix[0].astype(BF16)
    pscale = pool_scale[0].reshape(1, POOL_DIM)
    wpo = w_pool_out[0].astype(BF16)
    wao = w_attn_out[0].astype(BF16)
    wout = w_out[0].astype(BF16)
    wr = w_router[0].astype(BF16)
    br = b_router[0].reshape(1, N_EXPERTS)
    lam_vecs = jnp.stack([lambda_q1[0], lambda_k1[0], lambda_q2[0], lambda_k2[0]])
    norm_g = attn_norm_g[0].reshape(1, V_DIM)
    g1, b1 = ln1_g[0].reshape(1, D_MODEL), ln1_b[0].reshape(1, D_MODEL)
    g2, b2 = ln2_g[0].reshape(1, D_MODEL), ln2_b[0].reshape(1, D_MODEL)
    wei = w_expert_in[0].reshape(N_EXPERTS, D_MODEL, D_EXPERT, 2)
    w_glu = wei[..., 0].astype(BF16)
    w_lin = wei[..., 1].astype(BF16)
    bei = b_expert_in[0].reshape(N_EXPERTS, 1, D_EXPERT, 2)
    b_glu, b_lin = bei[..., 0], bei[..., 1]
    w_eo = w_expert_out[0].astype(BF16)
    b_eo = b_expert_out[0].reshape(N_EXPERTS, 1, D_MODEL)

    xp = x_prompt.reshape(np_, D_MODEL)
    tm_p = _tile(sp, 256)
    cos_p, sin_p, icnt_p = _position_tables(0, sp)
    hist_p = jnp.zeros((bp, HIST_ROWS, POOL_DIM), F32)
    q_p, k_p, v_p, kb_p, vb_p, py_p, pnew_p = _inproj(
        xp, w_qkvp, cos_p, sin_p, icnt_p, hist_p, wmix, pscale, n_streams=bp, seq=sp, bb=1, tm=tm_p)
    ay_p = _attn_prompt(lam_vecs, norm_g, q_p, kb_p, vb_p, n_streams=bp, seq=sp,
                        tq=_tile(sp, 256), lam_init=lam_init)
    x1_p, idx_p, gate_p, rank_p, cnt_p = _postmix(
        xp, py_p, ay_p, w_gate, wpo, wao, wout, g1, b1, wr, br, tm=_tile(np_, 256), dn_alpha=dn_alpha)

    xs_ = x_sample.reshape(ns, D_MODEL)
    cos_s, sin_s, icnt_s = _position_tables(past, ss)
    hist_s = jnp.concatenate([jnp.zeros((bs, 1, POOL_DIM), F32), state_pool[0]], axis=1)
    q_s, k_s, v_s, _, _, py_s, pnew_s = _inproj(
        xs_, w_qkvp, cos_s, sin_s, icnt_s, hist_s, wmix, pscale, n_streams=bs, seq=ss, bb=bs, tm=ss)
    kc = cache_k[0].reshape(bs, past, QK_DIM)
    vc = cache_v[0].reshape(bs, past, ATTN_V_WIDTH)
    ay_s = _attn_sample(lam_vecs, norm_g, q_s, kc, vc, k_s, v_s, n_streams=bs, tn=ss, past=past,
                        tk=_tile(past, 512), lam_init=lam_init)
    x1_s, idx_s, gate_s, rank_s, cnt_s = _postmix(
        xs_, py_s, ay_s, w_gate, wpo, wao, wout, g1, b1, wr, br, tm=_tile(ns, 256), dn_alpha=dn_alpha)

    cnt_p_i = cnt_p[0].astype(jnp.int32)
    counts = cnt_p_i + cnt_s[0].astype(jnp.int32)
    padded = (counts + EXPERT_BLOCK - 1) // EXPERT_BLOCK * EXPERT_BLOCK
    pad_end = jnp.cumsum(padded)
    base_p = (pad_end - padded).astype(jnp.int32)
    base_s = base_p + cnt_p_i
    n_blocks = -(-(np_ + ns) * TOP_K // EXPERT_BLOCK) + N_EXPERTS
    n_used = (pad_end[-1:] // EXPERT_BLOCK).astype(jnp.int32)
    blk_start = jnp.arange(n_blocks, dtype=jnp.int32) * EXPERT_BLOCK
    blk_expert = jnp.minimum(jnp.searchsorted(pad_end, blk_start, side='right'), N_EXPERTS - 1).astype(jnp.int32)

    flat = lambda a: a[:, :TOP_K].reshape(-1)
    idx_pf, rank_pf, idx_sf, rank_sf = flat(idx_p), flat(rank_p), flat(idx_s), flat(rank_s)
    xsorted = jnp.zeros((n_blocks * EXPERT_BLOCK, D_MODEL), F32)
    xsorted = _dispatch(base_p, idx_pf, rank_pf, x1_p, xsorted, tm=_tile(np_, 256))
    xsorted = _dispatch(base_s, idx_sf, rank_sf, x1_s, xsorted, tm=_tile(ns, 256))
    yb = _experts(blk_expert, n_used, xsorted, w_glu, w_lin, b_glu, b_lin, w_eo, b_eo)
    y_p = _combine(base_p, idx_pf, rank_pf, gate_p, x1_p, g2, b2, yb, tm=_tile(np_, 256), dn_alpha=dn_alpha)
    y_s = _combine(base_s, idx_sf, rank_sf, gate_s, x1_s, g2, b2, yb, tm=_tile(ns, 256), dn_alpha=dn_alpha)

    return (
        y_p.reshape(bp, sp, D_MODEL),
        y_s.reshape(bs, ss, D_MODEL),
        k_p.reshape(1, bp, sp, N_HEADS, 2, HEAD_DIM),
        v_p.reshape(1, bp, sp, N_HEADS, V_DIM),
        pnew_p[:, 1:].reshape(1, bp, POOL_HIST, POOL_DIM),
        k_s.reshape(1, bs, ss, N_HEADS, 2, HEAD_DIM),
        v_s.reshape(1, bs, ss, N_HEADS, V_DIM),
        pnew_s[:, 1:].reshape(1, bs, POOL_HIST, POOL_DIM),
    )
```

```python
import functools
import math

import jax
import jax.numpy as jnp
from jax import lax
from jax.experimental import pallas as pl
from jax.experimental.pallas import tpu as pltpu

D_MODEL = 1024
CHUNK = 64
POOL_WINDOWS = (2, 4, 8, 16)
POOL_GROUP_DIM = 128
POOL_DIM = len(POOL_WINDOWS) * POOL_GROUP_DIM
POOL_HIST = max(POOL_WINDOWS) - 1
HIST_ROWS = POOL_HIST + 1
N_HEADS = 8
HEAD_DIM = 64
HALF_DIM = HEAD_DIM // 2
V_DIM = 2 * HEAD_DIM
QK_DIM = N_HEADS * 2 * HEAD_DIM
ATTN_V_WIDTH = N_HEADS * V_DIM
ATTN_SCALE = HEAD_DIM ** -0.5
ROPE_THETA = 10000.0
SUBLN_EPS = 1e-5
N_EXPERTS = 32
TOP_K = 4
D_EXPERT = 1024
SWIGLU_LIMIT = 7.0
SWIGLU_ALPHA = 1.702
LN_EPS = 1e-5
NEG_INF = -1e30
LANES = 128
MXU_DIM = 256

F32 = jnp.float32
BF16 = jnp.bfloat16

VMEM_LIMIT = 56 * 1024 * 1024
EXPERT_BLOCK = 256


def _dot(a, b):
    return jnp.dot(a, b, preferred_element_type=F32)


def _dot_nt(a, b):
    return lax.dot_general(a, b, (((1,), (1,)), ((), ())), preferred_element_type=F32)


def _params(semantics):
    return pltpu.CompilerParams(dimension_semantics=semantics, vmem_limit_bytes=VMEM_LIMIT)


def _pool_branch(x, w_ref, icnt_ref, hist_ref, wmix_ref, pscale_ref, py_ref, pnew_ref, ext_ref, *, bb, tm):
    u = _dot(x, w_ref[:, 0:POOL_DIM])

    @pl.when(pl.program_id(1) == 0)
    def _():
        ext_ref[:, 0:HIST_ROWS, :] = hist_ref[...]

    for b in range(bb):
        ext_ref[b, HIST_ROWS:HIST_ROWS + tm, :] = u[b * tm:(b + 1) * tm]
    for b in range(bb):
        for g, w in enumerate(POOL_WINDOWS):
            cols = slice(g * POOL_GROUP_DIM, (g + 1) * POOL_GROUP_DIM)
            cur = ext_ref[b, HIST_ROWS:HIST_ROWS + tm, cols]
            acc = cur
            for j in range(1, w):
                acc = acc + ext_ref[b, HIST_ROWS - j:HIST_ROWS - j + tm, cols]
            d = acc * icnt_ref[:, cols] - cur
            y = _dot(d.astype(BF16), wmix_ref[g]) * pscale_ref[:, cols]
            py_ref[b * tm:(b + 1) * tm, cols] = y.astype(BF16)
    tail = ext_ref[:, tm:tm + HIST_ROWS, :]
    pnew_ref[...] = tail
    ext_ref[:, 0:HIST_ROWS, :] = tail


def _rope_rows(z, cos, sin):
    lane = lax.broadcasted_iota(jnp.int32, z.shape, 1)
    first_half = (lane % HEAD_DIM) < HALF_DIM
    partner = jnp.where(first_half, pltpu.roll(z, LANES - HALF_DIM, 1), pltpu.roll(z, HALF_DIM, 1))
    return z * cos + partner * sin


def _inproj_prompt_kernel(x_ref, w_ref, wkt_ref, cos_ref, sin_ref, cost_ref, sint_ref, icnt_ref, hist_ref,
                          wmix_ref, pscale_ref, q_ref, kt_ref, ktb_ref, v_ref, vb_ref, py_ref, pnew_ref,
                          ext_ref, *, tm):
    x = x_ref[...].astype(BF16)
    _pool_branch(x, w_ref, icnt_ref, hist_ref, wmix_ref, pscale_ref, py_ref, pnew_ref, ext_ref, bb=1, tm=tm)

    cos, sin = cos_ref[...], sin_ref[...]
    hq = _dot(x, w_ref[:, POOL_DIM:POOL_DIM + QK_DIM])
    for h in range(N_HEADS):
        sl = slice(h * V_DIM, (h + 1) * V_DIM)
        q_ref[:, sl] = (_rope_rows(hq[:, sl], cos, sin) * ATTN_SCALE).astype(BF16)

    hkt = _dot_nt(wkt_ref[...], x)
    cost, sint = cost_ref[...], sint_ref[...]
    for hc in range(2 * N_HEADS):
        r0 = hc * HEAD_DIM
        x1 = hkt[r0:r0 + HALF_DIM]
        x2 = hkt[r0 + HALF_DIM:r0 + HEAD_DIM]
        o1 = x1 * cost - x2 * sint
        o2 = x2 * cost + x1 * sint
        kt_ref[r0:r0 + HALF_DIM, :] = o1
        kt_ref[r0 + HALF_DIM:r0 + HEAD_DIM, :] = o2
        ktb_ref[r0:r0 + HALF_DIM, :] = o1.astype(BF16)
        ktb_ref[r0 + HALF_DIM:r0 + HEAD_DIM, :] = o2.astype(BF16)

    hv = _dot(x, w_ref[:, POOL_DIM + QK_DIM:POOL_DIM + QK_DIM + ATTN_V_WIDTH])
    vb_ref[...] = hv.astype(BF16)
    for h in range(N_HEADS):
        v_ref[:, h, :] = hv[:, h * V_DIM:(h + 1) * V_DIM]


def _inproj_prompt(x2d, w_pqv, wkt, cos, sin, cost, sint, icnt, hist, wmix, pscale, *, n_streams, seq, tm):
    n = n_streams * seq
    nt = seq // tm
    row_map = lambda b, t: (b * nt + t, 0)
    const2 = lambda b, t: (0, 0)
    out_shape = (
        jax.ShapeDtypeStruct((n, QK_DIM), BF16),
        jax.ShapeDtypeStruct((n_streams, QK_DIM, seq), F32),
        jax.ShapeDtypeStruct((n_streams, QK_DIM, seq), BF16),
        jax.ShapeDtypeStruct((n_streams, seq, N_HEADS, V_DIM), F32),
        jax.ShapeDtypeStruct((n, ATTN_V_WIDTH), BF16),
        jax.ShapeDtypeStruct((n, POOL_DIM), BF16),
        jax.ShapeDtypeStruct((n_streams, HIST_ROWS, POOL_DIM), F32),
    )
    return pl.pallas_call(
        functools.partial(_inproj_prompt_kernel, tm=tm),
        out_shape=out_shape,
        grid=(n_streams, nt),
        in_specs=[
            pl.BlockSpec((tm, D_MODEL), row_map),
            pl.BlockSpec(w_pqv.shape, const2),
            pl.BlockSpec(wkt.shape, const2),
            pl.BlockSpec((tm, LANES), lambda b, t: (t, 0)),
            pl.BlockSpec((tm, LANES), lambda b, t: (t, 0)),
            pl.BlockSpec((HALF_DIM, tm), lambda b, t: (0, t)),
            pl.BlockSpec((HALF_DIM, tm), lambda b, t: (0, t)),
            pl.BlockSpec((tm, POOL_DIM), lambda b, t: (t, 0)),
            pl.BlockSpec((1, HIST_ROWS, POOL_DIM), lambda b, t: (b, 0, 0)),
            pl.BlockSpec((len(POOL_WINDOWS), POOL_GROUP_DIM, POOL_GROUP_DIM), lambda b, t: (0, 0, 0)),
            pl.BlockSpec((1, POOL_DIM), const2),
        ],
        out_specs=(
            pl.BlockSpec((tm, QK_DIM), row_map),
            pl.BlockSpec((None, QK_DIM, tm), lambda b, t: (b, 0, t)),
            pl.BlockSpec((None, QK_DIM, tm), lambda b, t: (b, 0, t)),
            pl.BlockSpec((None, tm, N_HEADS, V_DIM), lambda b, t: (b, t, 0, 0)),
            pl.BlockSpec((tm, ATTN_V_WIDTH), row_map),
            pl.BlockSpec((tm, POOL_DIM), row_map),
            pl.BlockSpec((1, HIST_ROWS, POOL_DIM), lambda b, t: (b, 0, 0)),
        ),
        scratch_shapes=[pltpu.VMEM((1, HIST_ROWS + tm, POOL_DIM), F32)],
        compiler_params=_params(("arbitrary", "arbitrary")),
        name="inproj_prompt",
    )(x2d, w_pqv, wkt, cos, sin, cost, sint, icnt, hist, wmix, pscale)


def _inproj_sample_kernel(x_ref, w_ref, wk_ref, cos_ref, sin_ref, icnt_ref, hist_ref, wmix_ref, pscale_ref,
                          q_ref, k_ref, v_ref, py_ref, pnew_ref, ext_ref, *, bb, tm):
    x = x_ref[...].astype(BF16)
    _pool_branch(x, w_ref, icnt_ref, hist_ref, wmix_ref, pscale_ref, py_ref, pnew_ref, ext_ref, bb=bb, tm=tm)
    cos = jnp.concatenate([cos_ref[...]] * bb, axis=0)
    sin = jnp.concatenate([sin_ref[...]] * bb, axis=0)
    hq = _dot(x, w_ref[:, POOL_DIM:POOL_DIM + QK_DIM])
    hk = _dot(x, wk_ref[...])
    for h in range(N_HEADS):
        sl = slice(h * V_DIM, (h + 1) * V_DIM)
        q_ref[:, sl] = (_rope_rows(hq[:, sl], cos, sin) * ATTN_SCALE).astype(BF16)
        k_ref[:, sl] = _rope_rows(hk[:, sl], cos, sin)
    v_ref[...] = _dot(x, w_ref[:, POOL_DIM + QK_DIM:POOL_DIM + QK_DIM + ATTN_V_WIDTH])


def _inproj_sample(x2d, w_pqv, wk, cos, sin, icnt, hist, wmix, pscale, *, n_streams, seq):
    n = n_streams * seq
    const2 = lambda i, t: (0, 0)
    const3 = lambda i, t: (0, 0, 0)
    out_shape = (
        jax.ShapeDtypeStruct((n, QK_DIM), BF16),
        jax.ShapeDtypeStruct((n, QK_DIM), F32),
        jax.ShapeDtypeStruct((n, ATTN_V_WIDTH), F32),
        jax.ShapeDtypeStruct((n, POOL_DIM), BF16),
        jax.ShapeDtypeStruct((n_streams, HIST_ROWS, POOL_DIM), F32),
    )
    return pl.pallas_call(
        functools.partial(_inproj_sample_kernel, bb=n_streams, tm=seq),
        out_shape=out_shape,
        grid=(1, 1),
        in_specs=[
            pl.BlockSpec((n, D_MODEL), const2),
            pl.BlockSpec(w_pqv.shape, const2),
            pl.BlockSpec(wk.shape, const2),
            pl.BlockSpec((seq, LANES), const2),
            pl.BlockSpec((seq, LANES), const2),
            pl.BlockSpec((seq, POOL_DIM), const2),
            pl.BlockSpec((n_streams, HIST_ROWS, POOL_DIM), const3),
            pl.BlockSpec((len(POOL_WINDOWS), POOL_GROUP_DIM, POOL_GROUP_DIM), const3),
            pl.BlockSpec((1, POOL_DIM), const2),
        ],
        out_specs=(
            pl.BlockSpec((n, QK_DIM), const2),
            pl.BlockSpec((n, QK_DIM), const2),
            pl.BlockSpec((n, ATTN_V_WIDTH), const2),
            pl.BlockSpec((n, POOL_DIM), const2),
            pl.BlockSpec((n_streams, HIST_ROWS, POOL_DIM), const3),
        ),
        scratch_shapes=[pltpu.VMEM((n_streams, HIST_ROWS + seq, POOL_DIM), F32)],
        compiler_params=_params(("arbitrary", "arbitrary")),
        name="inproj_sample",
    )(x2d, w_pqv, wk, cos, sin, icnt, hist, wmix, pscale)


def _lambda_value(lam_ref, lam_init):
    lv = lam_ref[...]
    s1 = jnp.sum(lv[0:1] * lv[1:2], axis=1, keepdims=True)
    s2 = jnp.sum(lv[2:3] * lv[3:4], axis=1, keepdims=True)
    return jnp.exp(s1) - jnp.exp(s2) + lam_init


def _head_norm(o, g, lam_init):
    ms = jnp.mean(o * o, axis=-1, keepdims=True)
    return o * lax.rsqrt(ms + SUBLN_EPS) * g * (1.0 - lam_init)


def _attn_prompt_kernel(lam_ref, g_ref, q_ref, kt_ref, v_ref, o_ref, *, seq, tq, lam_init):
    lam = _lambda_value(lam_ref, lam_init)
    g = g_ref[...]
    r = lax.broadcasted_iota(jnp.int32, (tq, tq), 0)
    c = lax.broadcasted_iota(jnp.int32, (tq, tq), 1)
    diag_visible = (c // CHUNK) <= (r // CHUNK)
    lane = lax.broadcasted_iota(jnp.int32, (tq, V_DIM), 1)

    for i in range(seq // tq):
        lo = i * tq
        q = q_ref[lo:lo + tq, :]
        zero = jnp.zeros_like(q)
        qc = (jnp.where(lane < HEAD_DIM, q, zero), jnp.where(lane >= HEAD_DIM, q, zero))
        sd = [jnp.where(diag_visible, _dot(qc[k], kt_ref[:, lo:lo + tq]), NEG_INF) for k in range(2)]
        m = [jnp.max(s, axis=1, keepdims=True) for s in sd]
        if i > 0:
            sp = [_dot(qc[k], kt_ref[:, 0:lo]) for k in range(2)]
            m = [jnp.maximum(m[k], jnp.max(sp[k], axis=1, keepdims=True)) for k in range(2)]
        pd = [jnp.exp(sd[k] - m[k]) for k in range(2)]
        l = [jnp.sum(p, axis=1, keepdims=True) for p in pd]
        if i > 0:
            pp = [jnp.exp(sp[k] - m[k]) for k in range(2)]
            l = [l[k] + jnp.sum(pp[k], axis=1, keepdims=True) for k in range(2)]
        r0 = 1.0 / l[0]
        r1 = lam / l[1]
        o = _dot((pd[0] * r0 - pd[1] * r1).astype(BF16), v_ref[lo:lo + tq, :])
        if i > 0:
            o = o + _dot((pp[0] * r0 - pp[1] * r1).astype(BF16), v_ref[0:lo, :])
        o_ref[lo:lo + tq, :] = _head_norm(o, g, lam_init).astype(o_ref.dtype)


def _attn_prompt(lam_vecs, norm_g, q, ktb, vb, *, n_streams, seq, tq, lam_init):
    return pl.pallas_call(
        functools.partial(_attn_prompt_kernel, seq=seq, tq=tq, lam_init=lam_init),
        out_shape=jax.ShapeDtypeStruct((n_streams * seq, ATTN_V_WIDTH), BF16),
        grid=(n_streams, N_HEADS),
        in_specs=[
            pl.BlockSpec((4, HEAD_DIM), lambda b, h: (0, 0)),
            pl.BlockSpec((1, V_DIM), lambda b, h: (0, 0)),
            pl.BlockSpec((seq, V_DIM), lambda b, h: (b, h)),
            pl.BlockSpec((None, V_DIM, seq), lambda b, h: (b, h, 0)),
            pl.BlockSpec((seq, V_DIM), lambda b, h: (b, h)),
        ],
        out_specs=pl.BlockSpec((seq, V_DIM), lambda b, h: (b, h)),
        compiler_params=_params(("arbitrary", "arbitrary")),
        name="attn_prompt",
    )(lam_vecs, norm_g, q, ktb, vb)


def _attn_sample_kernel(lam_ref, g_ref, q_ref, kc_ref, vc_ref, kn_ref, vn_ref, o_ref,
                        s_scr, w_scr, wn_scr, m_scr, acc_scr, qbd_scr, *, nk, tn, past, lam_init):
    j = pl.program_id(1)
    half = N_HEADS * tn

    @pl.when(j == 0)
    def _():
        q = q_ref[...]
        qt = jnp.concatenate([q] * (2 * N_HEADS), axis=0)
        r = lax.broadcasted_iota(jnp.int32, qt.shape, 0)
        l = lax.broadcasted_iota(jnp.int32, qt.shape, 1)
        keep = ((r // half) == ((l % V_DIM) // HEAD_DIM)) & (((r % half) // tn) == (l // V_DIM))
        qbd_scr[...] = jnp.where(keep, qt, jnp.zeros_like(qt))
        m_scr[...] = jnp.full(m_scr.shape, NEG_INF, F32)
        acc_scr[...] = jnp.zeros(acc_scr.shape, F32)

    @pl.when(j < nk)
    def _():
        s = _dot(qbd_scr[...], kc_ref[...].astype(BF16))
        s_scr[j] = s
        m_scr[...] = jnp.maximum(m_scr[...], jnp.max(s, axis=1, keepdims=True))

    @pl.when(j == nk - 1)
    def _():
        lam = _lambda_value(lam_ref, lam_init)
        sn = _dot_nt(qbd_scr[...], kn_ref[...].astype(BF16))
        qpos = past + (lax.broadcasted_iota(jnp.int32, sn.shape, 0) % tn)
        kpos = past + lax.broadcasted_iota(jnp.int32, sn.shape, 1)
        sn = jnp.where((kpos // CHUNK) <= (qpos // CHUNK), sn, NEG_INF)
        m = jnp.maximum(m_scr[...], jnp.max(sn, axis=1, keepdims=True))
        pn = jnp.exp(sn - m)
        l = jnp.sum(pn, axis=1, keepdims=True)
        for c in range(nk):
            p = jnp.exp(s_scr[c] - m)
            s_scr[c] = p
            l = l + jnp.sum(p, axis=1, keepdims=True)
        r0 = 1.0 / l[:half]
        r1 = lam / l[half:]
        wn_scr[...] = pn[:half] * r0 - pn[half:] * r1
        for c in range(nk):
            p = s_scr[c]
            w_scr[c] = (p[:half] * r0 - p[half:] * r1).astype(BF16)

    def v_rows(ref):
        return jnp.concatenate([ref[:, h, :] for h in range(N_HEADS)], axis=1).astype(BF16)

    @pl.when(j >= nk)
    def _():
        acc_scr[...] += _dot(w_scr[j - nk], v_rows(vc_ref))

    @pl.when(j == 2 * nk - 1)
    def _():
        acc = acc_scr[...] + _dot(wn_scr[...].astype(BF16), vn_ref[...].astype(BF16))
        g = g_ref[...]
        for h in range(N_HEADS):
            o = acc[h * tn:(h + 1) * tn, h * V_DIM:(h + 1) * V_DIM]
            o_ref[:, h * V_DIM:(h + 1) * V_DIM] = _head_norm(o, g, lam_init).astype(o_ref.dtype)


def _attn_sample(lam_vecs, norm_g, q, kct, vc, kn, vn, *, n_streams, tn, past, tk, lam_init):
    nk = past // tk
    rows = 2 * N_HEADS * tn
    half = N_HEADS * tn
    return pl.pallas_call(
        functools.partial(_attn_sample_kernel, nk=nk, tn=tn, past=past, lam_init=lam_init),
        out_shape=jax.ShapeDtypeStruct((n_streams * tn, ATTN_V_WIDTH), BF16),
        grid=(n_streams, 2 * nk),
        in_specs=[
            pl.BlockSpec((4, HEAD_DIM), lambda b, j: (0, 0)),
            pl.BlockSpec((1, V_DIM), lambda b, j: (0, 0)),
            pl.BlockSpec((tn, QK_DIM), lambda b, j: (b, 0)),
            pl.BlockSpec((None, QK_DIM, tk), lambda b, j: (b, 0, jnp.minimum(j, nk - 1))),
            pl.BlockSpec((None, tk, N_HEADS, V_DIM), lambda b, j: (b, jnp.maximum(j - nk, 0), 0, 0)),
            pl.BlockSpec((tn, QK_DIM), lambda b, j: (b, 0)),
            pl.BlockSpec((tn, ATTN_V_WIDTH), lambda b, j: (b, 0)),
        ],
        out_specs=pl.BlockSpec((tn, ATTN_V_WIDTH), lambda b, j: (b, 0)),
        scratch_shapes=[
            pltpu.VMEM((nk, rows, tk), F32),
            pltpu.VMEM((nk, half, tk), BF16),
            pltpu.VMEM((half, tn), F32),
            pltpu.VMEM((rows, 1), F32),
            pltpu.VMEM((half, ATTN_V_WIDTH), F32),
            pltpu.VMEM((rows, QK_DIM), BF16),
        ],
        compiler_params=_params(("arbitrary", "arbitrary")),
        name="attn_sample",
    )(lam_vecs, norm_g, q, kct, vc, kn, vn)


def _layer_norm(z, g, b):
    mu = jnp.mean(z, axis=-1, keepdims=True)
    zc = z - mu
    var = jnp.mean(zc * zc, axis=-1, keepdims=True)
    return zc * lax.rsqrt(var + LN_EPS) * g + b


def _postmix_kernel(x_ref, py_ref, ay_ref, wg_ref, wpo_ref, wao_ref, wout_ref, g1_ref, b1_ref,
                    wr_ref, br_ref, x1_ref, idx_ref, gate_ref, rank_ref, cnt_ref, carry_ref,
                    *, tm, dn_alpha):
    step = pl.program_id(0)

    @pl.when(step == 0)
    def _():
        carry_ref[...] = jnp.zeros(carry_ref.shape, F32)

    x = x_ref[...]
    xb = x.astype(BF16)
    gates = jax.nn.sigmoid(_dot(xb, wg_ref[...]))
    a = _dot(py_ref[...], wpo_ref[...])
    b = _dot(ay_ref[...], wao_ref[...])
    mixed = gates[:, :D_MODEL] * a + gates[:, D_MODEL:] * b
    mo = _dot(mixed.astype(BF16), wout_ref[...])
    x1 = _layer_norm(dn_alpha * x + mo, g1_ref[...], b1_ref[...])
    x1_ref[...] = x1

    logits = _dot(x1.astype(BF16), wr_ref[...]) + br_ref[...]
    lane = lax.broadcasted_iota(jnp.int32, logits.shape, 1)
    work = logits
    vals, idxs = [], []
    for _ in range(TOP_K):
        mx = jnp.max(work, axis=1, keepdims=True)
        ix = jnp.min(jnp.where(work == mx, lane, N_EXPERTS), axis=1, keepdims=True)
        vals.append(mx)
        idxs.append(ix)
        work = jnp.where(lane == ix, -jnp.inf, work)
    exps = [jnp.exp(v - vals[0]) for v in vals]
    denom = exps[0] + exps[1] + exps[2] + exps[3]

    onehot = jnp.zeros(logits.shape, F32)
    for ix in idxs:
        onehot = onehot + (lane == ix).astype(F32)
    r = lax.broadcasted_iota(jnp.int32, (tm, tm), 0)
    c = lax.broadcasted_iota(jnp.int32, (tm, tm), 1)
    tri = jnp.where(c < r, 1.0, 0.0).astype(BF16)
    before = _dot(tri, onehot.astype(BF16)) + carry_ref[...]
    carry_ref[...] = carry_ref[...] + jnp.sum(onehot, axis=0, keepdims=True)
    cnt_ref[...] = carry_ref[...]

    lane_out = lax.broadcasted_iota(jnp.int32, (tm, LANES), 1)
    idx_out = jnp.zeros((tm, LANES), jnp.int32)
    rank_out = jnp.zeros((tm, LANES), jnp.int32)
    gate_out = jnp.zeros((tm, LANES), F32)
    for k in range(TOP_K):
        rank_k = jnp.sum(jnp.where(lane == idxs[k], before, 0.0), axis=1, keepdims=True).astype(jnp.int32)
        idx_out = jnp.where(lane_out == k, idxs[k], idx_out)
        rank_out = jnp.where(lane_out == k, rank_k, rank_out)
        gate_out = jnp.where(lane_out == k, exps[k] / denom, gate_out)
    idx_ref[...] = idx_out
    rank_ref[...] = rank_out
    gate_ref[...] = gate_out


def _postmix(x2d, py, ay, wg, wpo, wao, wout, g1, b1, wr, br, *, tm, dn_alpha):
    n = x2d.shape[0]
    row = lambda i: (i, 0)
    const = lambda i: (0, 0)
    out_shape = (
        jax.ShapeDtypeStruct((n, D_MODEL), F32),
        jax.ShapeDtypeStruct((n, LANES), jnp.int32),
        jax.ShapeDtypeStruct((n, LANES), F32),
        jax.ShapeDtypeStruct((n, LANES), jnp.int32),
        jax.ShapeDtypeStruct((1, N_EXPERTS), F32),
    )
    return pl.pallas_call(
        functools.partial(_postmix_kernel, tm=tm, dn_alpha=dn_alpha),
        out_shape=out_shape,
        grid=(n // tm,),
        in_specs=[
            pl.BlockSpec((tm, D_MODEL), row),
            pl.BlockSpec((tm, POOL_DIM), row),
            pl.BlockSpec((tm, ATTN_V_WIDTH), row),
            pl.BlockSpec(wg.shape, const),
            pl.BlockSpec(wpo.shape, const),
            pl.BlockSpec(wao.shape, const),
            pl.BlockSpec(wout.shape, const),
            pl.BlockSpec((1, D_MODEL), const),
            pl.BlockSpec((1, D_MODEL), const),
            pl.BlockSpec(wr.shape, const),
            pl.BlockSpec((1, N_EXPERTS), const),
        ],
        out_specs=(
            pl.BlockSpec((tm, D_MODEL), row),
            pl.BlockSpec((tm, LANES), row),
            pl.BlockSpec((tm, LANES), row),
            pl.BlockSpec((tm, LANES), row),
            pl.BlockSpec((1, N_EXPERTS), const),
        ),
        scratch_shapes=[pltpu.VMEM((1, N_EXPERTS), F32)],
        compiler_params=_params(("arbitrary",)),
        name="postmix",
    )(x2d, py, ay, wg, wpo, wao, wout, g1, b1, wr, br)


def _dispatch_rows(base_ref, idx_ref, rank_ref, x_ref, xs_ref, sem, tm):
    def row_copy(r, k):
        a = r * TOP_K + k
        slot = base_ref[idx_ref[a]] + rank_ref[a]
        return pltpu.make_async_copy(x_ref.at[pl.ds(r, 1)], xs_ref.at[pl.ds(slot, 1)], sem)

    def start(r, _):
        for k in range(TOP_K):
            row_copy(r, k).start()
        return 0

    def wait(r, _):
        for k in range(TOP_K):
            row_copy(r, k).wait()
        return 0

    lax.fori_loop(0, tm, start, 0)
    lax.fori_loop(0, tm, wait, 0)


def _dispatch_first_kernel(base_ref, tail_ref, nu_ref, idx_ref, rank_ref, x_ref, xs_ref, zbuf, sem, zsem,
                           *, tm, n_blocks):
    @pl.when(pl.program_id(0) == 0)
    def _():
        zbuf[...] = jnp.zeros(zbuf.shape, zbuf.dtype)

        def zero_copy(row):
            row = pl.multiple_of(row, EXPERT_BLOCK)
            return pltpu.make_async_copy(zbuf, xs_ref.at[pl.ds(row, EXPERT_BLOCK)], zsem)

        def over_blocks(fn):
            for e in range(N_EXPERTS):
                @pl.when(tail_ref[e] >= 0)
                def _():
                    fn(zero_copy(tail_ref[e]))
            lax.fori_loop(nu_ref[0], n_blocks, lambda b, c: (fn(zero_copy(b * EXPERT_BLOCK)), c)[1], 0)

        over_blocks(lambda cp: cp.start())
        over_blocks(lambda cp: cp.wait())

    _dispatch_rows(base_ref, idx_ref, rank_ref, x_ref, xs_ref, sem, tm)


def _dispatch_next_kernel(base_ref, idx_ref, rank_ref, x_ref, xs_in_ref, xs_ref, sem, *, tm):
    del xs_in_ref
    _dispatch_rows(base_ref, idx_ref, rank_ref, x_ref, xs_ref, sem, tm)


def _dispatch_first(base, tail, n_used, idx_flat, rank_flat, x1, *, tm, n_blocks):
    n = x1.shape[0]
    smem_tile = pl.BlockSpec((tm * TOP_K,), lambda i, *_: (i,), memory_space=pltpu.SMEM)
    grid_spec = pltpu.PrefetchScalarGridSpec(
        num_scalar_prefetch=3,
        grid=(n // tm,),
        in_specs=[smem_tile, smem_tile, pl.BlockSpec((tm, D_MODEL), lambda i, *_: (i, 0))],
        out_specs=pl.BlockSpec(memory_space=pl.ANY),
        scratch_shapes=[pltpu.VMEM((EXPERT_BLOCK, D_MODEL), F32), pltpu.SemaphoreType.DMA,
                        pltpu.SemaphoreType.DMA],
    )
    return pl.pallas_call(
        functools.partial(_dispatch_first_kernel, tm=tm, n_blocks=n_blocks),
        out_shape=jax.ShapeDtypeStruct((n_blocks * EXPERT_BLOCK, D_MODEL), F32),
        grid_spec=grid_spec,
        compiler_params=_params(("arbitrary",)),
        name="dispatch_first",
    )(base, tail, n_used, idx_flat, rank_flat, x1)


def _dispatch_next(base, idx_flat, rank_flat, x1, xs, *, tm):
    n = x1.shape[0]
    smem_tile = pl.BlockSpec((tm * TOP_K,), lambda i, *_: (i,), memory_space=pltpu.SMEM)
    grid_spec = pltpu.PrefetchScalarGridSpec(
        num_scalar_prefetch=1,
        grid=(n // tm,),
        in_specs=[smem_tile, smem_tile, pl.BlockSpec((tm, D_MODEL), lambda i, *_: (i, 0)),
                  pl.BlockSpec(memory_space=pl.ANY)],
        out_specs=pl.BlockSpec(memory_space=pl.ANY),
        scratch_shapes=[pltpu.SemaphoreType.DMA],
    )
    return pl.pallas_call(
        functools.partial(_dispatch_next_kernel, tm=tm),
        out_shape=jax.ShapeDtypeStruct(xs.shape, xs.dtype),
        grid_spec=grid_spec,
        input_output_aliases={4: 0},
        compiler_params=_params(("arbitrary",)),
        name="dispatch_next",
    )(base, idx_flat, rank_flat, x1, xs)


def _experts_kernel(be_ref, nu_ref, xs_ref, win_ref, bg_ref, bl_ref, wo_ref, bo_ref, y_ref,
                    wg_scr, wl_scr, wo_scr):
    i = pl.program_id(0)
    used = i < nu_ref[0]
    first_of_expert = jnp.logical_or(i == 0, be_ref[i] != be_ref[jnp.maximum(i - 1, 0)])

    @pl.when(jnp.logical_not(used))
    def _():
        y_ref[...] = jnp.zeros(y_ref.shape, y_ref.dtype)

    @pl.when(jnp.logical_and(used, first_of_expert))
    def _():
        r = lax.broadcasted_iota(jnp.int32, (MXU_DIM, MXU_DIM), 0)
        c = lax.broadcasted_iota(jnp.int32, (MXU_DIM, MXU_DIM), 1)
        src = jnp.where(c < LANES, 2 * c, 2 * (c - LANES) + 1)
        sel = jnp.where(r == src, 1.0, 0.0).astype(BF16)
        for gq in range(2 * D_EXPERT // MXU_DIM):
            blk = win_ref[:, gq * MXU_DIM:(gq + 1) * MXU_DIM].astype(BF16)
            d = _dot(blk, sel)
            wg_scr[:, gq * LANES:(gq + 1) * LANES] = d[:, :LANES].astype(BF16)
            wl_scr[:, gq * LANES:(gq + 1) * LANES] = d[:, LANES:].astype(BF16)
        wo_scr[...] = wo_ref[...].astype(BF16)

    @pl.when(used)
    def _():
        xb = xs_ref[...].astype(BF16)
        glu = jnp.minimum(_dot(xb, wg_scr[...]) + bg_ref[...], SWIGLU_LIMIT)
        lin = jnp.clip(_dot(xb, wl_scr[...]) + bl_ref[...], -SWIGLU_LIMIT, SWIGLU_LIMIT)
        act = glu * jax.nn.sigmoid(SWIGLU_ALPHA * glu) * (lin + 1.0)
        y_ref[...] = _dot(act.astype(BF16), wo_scr[...]) + bo_ref[...]


def _experts(blk_expert, n_used, xs, w_in, b_glu, b_lin, w_out, b_out):
    rows = xs.shape[0]
    n_blocks = rows // EXPERT_BLOCK
    wsel = lambda i, be, nu: (be[i], 0, 0)
    grid_spec = pltpu.PrefetchScalarGridSpec(
        num_scalar_prefetch=2,
        grid=(n_blocks,),
        in_specs=[
            pl.BlockSpec((EXPERT_BLOCK, D_MODEL), lambda i, be, nu: (jnp.minimum(i, nu[0] - 1), 0)),
            pl.BlockSpec((None, D_MODEL, 2 * D_EXPERT), wsel),
            pl.BlockSpec((None, 1, D_EXPERT), wsel),
            pl.BlockSpec((None, 1, D_EXPERT), wsel),
            pl.BlockSpec((None, D_EXPERT, D_MODEL), wsel),
            pl.BlockSpec((None, 1, D_MODEL), wsel),
        ],
        out_specs=pl.BlockSpec((EXPERT_BLOCK, D_MODEL), lambda i, be, nu: (i, 0)),
        scratch_shapes=[pltpu.VMEM((D_MODEL, D_EXPERT), BF16), pltpu.VMEM((D_MODEL, D_EXPERT), BF16),
                        pltpu.VMEM((D_EXPERT, D_MODEL), BF16)],
    )
    return pl.pallas_call(
        _experts_kernel,
        out_shape=jax.ShapeDtypeStruct((rows, D_MODEL), F32),
        grid_spec=grid_spec,
        compiler_params=_params(("arbitrary",)),
        name="experts",
    )(blk_expert, n_used, xs, w_in, b_glu, b_lin, w_out, b_out)


def _combine_kernel(base_ref, idx_ref, rank_ref, gate_ref, x1_ref, g2_ref, b2_ref, yb_ref, o_ref,
                    buf, sem, *, tm, dn_alpha):
    def row_copy(r, k):
        a = r * TOP_K + k
        slot = base_ref[idx_ref[a]] + rank_ref[a]
        return pltpu.make_async_copy(yb_ref.at[pl.ds(slot, 1)], buf.at[k, pl.ds(r, 1)], sem)

    def start(r, _):
        for k in range(TOP_K):
            row_copy(r, k).start()
        return 0

    def wait(r, _):
        for k in range(TOP_K):
            row_copy(r, k).wait()
        return 0

    lax.fori_loop(0, tm, start, 0)
    lax.fori_loop(0, tm, wait, 0)
    gate = gate_ref[...]
    y = gate[:, 0:1] * buf[0]
    for k in range(1, TOP_K):
        y = y + gate[:, k:k + 1] * buf[k]
    o_ref[...] = _layer_norm(dn_alpha * x1_ref[...] + y, g2_ref[...], b2_ref[...])


def _combine(base, idx_flat, rank_flat, gate, x1, g2, b2, yb, *, tm, dn_alpha):
    n = x1.shape[0]
    smem_tile = pl.BlockSpec((tm * TOP_K,), lambda i, base: (i,), memory_space=pltpu.SMEM)
    grid_spec = pltpu.PrefetchScalarGridSpec(
        num_scalar_prefetch=1,
        grid=(n // tm,),
        in_specs=[
            smem_tile,
            smem_tile,
            pl.BlockSpec((tm, LANES), lambda i, base: (i, 0)),
            pl.BlockSpec((tm, D_MODEL), lambda i, base: (i, 0)),
            pl.BlockSpec((1, D_MODEL), lambda i, base: (0, 0)),
            pl.BlockSpec((1, D_MODEL), lambda i, base: (0, 0)),
            pl.BlockSpec(memory_space=pl.ANY),
        ],
        out_specs=pl.BlockSpec((tm, D_MODEL), lambda i, base: (i, 0)),
        scratch_shapes=[pltpu.VMEM((TOP_K, tm, D_MODEL), F32), pltpu.SemaphoreType.DMA],
    )
    return pl.pallas_call(
        functools.partial(_combine_kernel, tm=tm, dn_alpha=dn_alpha),
        out_shape=jax.ShapeDtypeStruct((n, D_MODEL), F32),
        grid_spec=grid_spec,
        compiler_params=_params(("arbitrary",)),
        name="combine",
    )(base, idx_flat, rank_flat, gate, x1, g2, b2, yb)


def _position_tables(pos0, seq):
    pos = pos0 + jnp.arange(seq, dtype=jnp.int32)
    inv = ROPE_THETA ** (-jnp.arange(HALF_DIM, dtype=F32) / HALF_DIM)
    ang = pos.astype(F32)[:, None] * inv[None, :]
    cos, sin = jnp.cos(ang), jnp.sin(ang)
    cos_rows = jnp.concatenate([cos, cos, cos, cos], axis=-1)
    sin_rows = jnp.concatenate([-sin, sin, -sin, sin], axis=-1)
    icnt = jnp.concatenate(
        [jnp.broadcast_to((1.0 / jnp.minimum(pos + 1, w).astype(F32))[:, None], (seq, POOL_GROUP_DIM))
         for w in POOL_WINDOWS], axis=-1)
    return cos_rows, sin_rows, cos.T, sin.T, icnt


def _tile(n, pref):
    t = min(n, pref)
    while n % t:
        t //= 2
    return t


def kernel(x_prompt, x_sample, cache_k, cache_v, state_pool, w_in, w_pool_mix, pool_scale, w_pool_out,
           lambda_q1, lambda_k1, lambda_q2, lambda_k2, attn_norm_g, w_attn_out, w_out, ln1_g, ln1_b,
           w_router, b_router, w_expert_in, b_expert_in, w_expert_out, b_expert_out, ln2_g, ln2_b):
    depth = w_in.shape[0]
    assert depth == 1, "single-layer step"
    dn_alpha = (2.0 * depth) ** 0.25
    lam_init = 0.8 - 0.6 * math.exp(-0.3 * 0)
    bp, sp, _ = x_prompt.shape
    bs, ss, _ = x_sample.shape
    past = cache_k.shape[2]
    np_, ns = bp * sp, bs * ss

    c_q, c_k, c_v = POOL_DIM, POOL_DIM + QK_DIM, POOL_DIM + 2 * QK_DIM
    c_gate = c_v + ATTN_V_WIDTH
    w0 = w_in[0]
    w_pqv = jnp.concatenate([w0[:, :c_k], w0[:, c_v:c_gate]], axis=1).astype(BF16)
    w_k = w0[:, c_k:c_v].astype(BF16)
    w_gate = w0[:, c_gate:].astype(BF16)
    wmix = w_pool_mix[0].astype(BF16)
    pscale = pool_scale[0].reshape(1, POOL_DIM)
    wpo = w_pool_out[0].astype(BF16)
    wao = w_attn_out[0].astype(BF16)
    wout = w_out[0].astype(BF16)
    wr = w_router[0].astype(BF16)
    br = b_router[0].reshape(1, N_EXPERTS)
    lam_vecs = jnp.stack([lambda_q1[0], lambda_k1[0], lambda_q2[0], lambda_k2[0]])
    norm_g = attn_norm_g[0].reshape(1, V_DIM)
    g1, b1 = ln1_g[0].reshape(1, D_MODEL), ln1_b[0].reshape(1, D_MODEL)
    g2, b2 = ln2_g[0].reshape(1, D_MODEL), ln2_b[0].reshape(1, D_MODEL)
    b_glu = b_expert_in[0][:, 0::2].reshape(N_EXPERTS, 1, D_EXPERT)
    b_lin = b_expert_in[0][:, 1::2].reshape(N_EXPERTS, 1, D_EXPERT)
    b_eo = b_expert_out[0].reshape(N_EXPERTS, 1, D_MODEL)

    xp = x_prompt.reshape(np_, D_MODEL)
    cos_p, sin_p, cost_p, sint_p, icnt_p = _position_tables(0, sp)
    hist_p = jnp.zeros((bp, HIST_ROWS, POOL_DIM), F32)
    q_p, kt_p, ktb_p, v_p, vb_p, py_p, pnew_p = _inproj_prompt(
        xp, w_pqv, w_k.T, cos_p, sin_p, cost_p, sint_p, icnt_p, hist_p, wmix, pscale,
        n_streams=bp, seq=sp, tm=_tile(sp, 256))
    ay_p = _attn_prompt(lam_vecs, norm_g, q_p, ktb_p, vb_p, n_streams=bp, seq=sp,
                        tq=_tile(sp, 256), lam_init=lam_init)
    x1_p, idx_p, gate_p, rank_p, cnt_p = _postmix(
        xp, py_p, ay_p, w_gate, wpo, wao, wout, g1, b1, wr, br, tm=_tile(np_, 256), dn_alpha=dn_alpha)

    xs_ = x_sample.reshape(ns, D_MODEL)
    cos_s, sin_s, _, _, icnt_s = _position_tables(past, ss)
    hist_s = jnp.concatenate([jnp.zeros((bs, 1, POOL_DIM), F32), state_pool[0]], axis=1)
    q_s, k_s, v_s, py_s, pnew_s = _inproj_sample(
        xs_, w_pqv, w_k, cos_s, sin_s, icnt_s, hist_s, wmix, pscale, n_streams=bs, seq=ss)
    kct = jnp.transpose(cache_k[0], (0, 2, 3, 4, 1)).reshape(bs, QK_DIM, past)
    ay_s = _attn_sample(lam_vecs, norm_g, q_s, kct, cache_v[0], k_s, v_s, n_streams=bs, tn=ss, past=past,
                        tk=_tile(past, 512), lam_init=lam_init)
    x1_s, idx_s, gate_s, rank_s, cnt_s = _postmix(
        xs_, py_s, ay_s, w_gate, wpo, wao, wout, g1, b1, wr, br, tm=_tile(ns, 256), dn_alpha=dn_alpha)

    cnt_p_i = cnt_p[0].astype(jnp.int32)
    counts = cnt_p_i + cnt_s[0].astype(jnp.int32)
    padded = (counts + EXPERT_BLOCK - 1) // EXPERT_BLOCK * EXPERT_BLOCK
    pad_end = jnp.cumsum(padded).astype(jnp.int32)
    base_p = pad_end - padded
    base_s = base_p + cnt_p_i
    tail = jnp.where(padded > 0, pad_end - EXPERT_BLOCK, -1).astype(jnp.int32)
    n_blocks = -(-(np_ + ns) * TOP_K // EXPERT_BLOCK) + N_EXPERTS
    n_used = pad_end[-1:] // EXPERT_BLOCK
    blk_start = jnp.arange(n_blocks, dtype=jnp.int32) * EXPERT_BLOCK
    blk_expert = jnp.minimum(jnp.sum((blk_start[:, None] >= pad_end[None, :]).astype(jnp.int32), axis=1),
                             N_EXPERTS - 1)

    flat = lambda a: a[:, :TOP_K].reshape(-1)
    idx_pf, rank_pf, idx_sf, rank_sf = flat(idx_p), flat(rank_p), flat(idx_s), flat(rank_s)
    xsorted = _dispatch_first(base_p, tail, n_used, idx_pf, rank_pf, x1_p, tm=_tile(np_, 256), n_blocks=n_blocks)
    xsorted = _dispatch_next(base_s, idx_sf, rank_sf, x1_s, xsorted, tm=_tile(ns, 256))
    yb = _experts(blk_expert, n_used, xsorted, w_expert_in[0], b_glu, b_lin, w_expert_out[0], b_eo)
    y_p = _combine(base_p, idx_pf, rank_pf, gate_p, x1_p, g2, b2, yb, tm=_tile(np_, 256), dn_alpha=dn_alpha)
    y_s = _combine(base_s, idx_sf, rank_sf, gate_s, x1_s, g2, b2, yb, tm=_tile(ns, 256), dn_alpha=dn_alpha)

    k_prompt = jnp.transpose(kt_p.reshape(bp, N_HEADS, 2, HEAD_DIM, sp), (0, 4, 1, 2, 3))
    return (
        y_p.reshape(bp, sp, D_MODEL),
        y_s.reshape(bs, ss, D_MODEL),
        k_prompt[None],
        v_p[None],
        pnew_p[:, 1:].reshape(1, bp, POOL_HIST, POOL_DIM),
        k_s.reshape(1, bs, ss, N_HEADS, 2, HEAD_DIM),
        v_s.reshape(1, bs, ss, N_HEADS, V_DIM),
        pnew_s[:, 1:].reshape(1, bs, POOL_HIST, POOL_DIM),
    )
```

```python
import functools
import math

import jax
import jax.numpy as jnp
from jax import lax
from jax.experimental import pallas as pl
from jax.experimental.pallas import tpu as pltpu

D_MODEL = 1024
CHUNK = 64
POOL_WINDOWS = (2, 4, 8, 16)
POOL_GROUP_DIM = 128
POOL_DIM = len(POOL_WINDOWS) * POOL_GROUP_DIM
POOL_HIST = max(POOL_WINDOWS) - 1
HIST_ROWS = POOL_HIST + 1
N_HEADS = 8
HEAD_DIM = 64
HALF_DIM = HEAD_DIM // 2
V_DIM = 2 * HEAD_DIM
QK_DIM = N_HEADS * 2 * HEAD_DIM
ATTN_V_WIDTH = N_HEADS * V_DIM
ATTN_SCALE = HEAD_DIM ** -0.5
ROPE_THETA = 10000.0
SUBLN_EPS = 1e-5
N_EXPERTS = 32
TOP_K = 4
D_EXPERT = 1024
SWIGLU_LIMIT = 7.0
SWIGLU_ALPHA = 1.702
LN_EPS = 1e-5
NEG_INF = -1e30
LANES = 128
MXU_DIM = 256

F32 = jnp.float32
BF16 = jnp.bfloat16

VMEM_LIMIT = 56 * 1024 * 1024
EXPERT_BLOCK = 256
RUN_ALIGN = 8
ROUTE_TILE = 256
RUN_BITS = tuple(range(3, 9))


def _dot(a, b):
    return jnp.dot(a, b, preferred_element_type=F32)


def _dot_nt(a, b):
    return lax.dot_general(a, b, (((1,), (1,)), ((), ())), preferred_element_type=F32)


def _params(semantics):
    return pltpu.CompilerParams(dimension_semantics=semantics, vmem_limit_bytes=VMEM_LIMIT)


def _pool_branch(x, w_ref, icnt_ref, hist_ref, wmix_ref, pscale_ref, py_ref, pnew_ref, ext_ref, *, bb, tm):
    u = _dot(x, w_ref[:, 0:POOL_DIM])

    @pl.when(pl.program_id(1) == 0)
    def _():
        ext_ref[:, 0:HIST_ROWS, :] = hist_ref[...]

    for b in range(bb):
        ext_ref[b, HIST_ROWS:HIST_ROWS + tm, :] = u[b * tm:(b + 1) * tm]
    for b in range(bb):
        for g, w in enumerate(POOL_WINDOWS):
            cols = slice(g * POOL_GROUP_DIM, (g + 1) * POOL_GROUP_DIM)
            cur = ext_ref[b, HIST_ROWS:HIST_ROWS + tm, cols]
            acc = cur
            for j in range(1, w):
                acc = acc + ext_ref[b, HIST_ROWS - j:HIST_ROWS - j + tm, cols]
            d = acc * icnt_ref[:, cols] - cur
            y = _dot(d.astype(BF16), wmix_ref[g]) * pscale_ref[:, cols]
            py_ref[b * tm:(b + 1) * tm, cols] = y.astype(BF16)
    tail = ext_ref[:, tm:tm + HIST_ROWS, :]
    pnew_ref[...] = tail
    ext_ref[:, 0:HIST_ROWS, :] = tail


def _rope_rows(z, cos, sin):
    lane = lax.broadcasted_iota(jnp.int32, z.shape, 1)
    first_half = (lane % HEAD_DIM) < HALF_DIM
    partner = jnp.where(first_half, pltpu.roll(z, LANES - HALF_DIM, 1), pltpu.roll(z, HALF_DIM, 1))
    return z * cos + partner * sin


def _inproj_prompt_kernel(x_ref, w_ref, wkt_ref, cos_ref, sin_ref, cost_ref, sint_ref, icnt_ref, hist_ref,
                          wmix_ref, pscale_ref, q_ref, kt_ref, ktb_ref, v_ref, vb_ref, py_ref, pnew_ref,
                          ext_ref, *, tm):
    x = x_ref[...].astype(BF16)
    _pool_branch(x, w_ref, icnt_ref, hist_ref, wmix_ref, pscale_ref, py_ref, pnew_ref, ext_ref, bb=1, tm=tm)

    cos, sin = cos_ref[...], sin_ref[...]
    hq = _dot(x, w_ref[:, POOL_DIM:POOL_DIM + QK_DIM])
    for h in range(N_HEADS):
        sl = slice(h * V_DIM, (h + 1) * V_DIM)
        q_ref[:, sl] = (_rope_rows(hq[:, sl], cos, sin) * ATTN_SCALE).astype(BF16)

    hkt = _dot_nt(wkt_ref[...], x)
    cost, sint = cost_ref[...], sint_ref[...]
    for hc in range(2 * N_HEADS):
        r0 = hc * HEAD_DIM
        x1 = hkt[r0:r0 + HALF_DIM]
        x2 = hkt[r0 + HALF_DIM:r0 + HEAD_DIM]
        o1 = x1 * cost - x2 * sint
        o2 = x2 * cost + x1 * sint
        kt_ref[r0:r0 + HALF_DIM, :] = o1
        kt_ref[r0 + HALF_DIM:r0 + HEAD_DIM, :] = o2
        ktb_ref[r0:r0 + HALF_DIM, :] = o1.astype(BF16)
        ktb_ref[r0 + HALF_DIM:r0 + HEAD_DIM, :] = o2.astype(BF16)

    hv = _dot(x, w_ref[:, POOL_DIM + QK_DIM:POOL_DIM + QK_DIM + ATTN_V_WIDTH])
    vb_ref[...] = hv.astype(BF16)
    for h in range(N_HEADS):
        v_ref[:, h, :] = hv[:, h * V_DIM:(h + 1) * V_DIM]


def _inproj_prompt(x2d, w_pqv, wkt, cos, sin, cost, sint, icnt, hist, wmix, pscale, *, n_streams, seq, tm):
    n = n_streams * seq
    nt = seq // tm
    row_map = lambda b, t: (b * nt + t, 0)
    const2 = lambda b, t: (0, 0)
    out_shape = (
        jax.ShapeDtypeStruct((n, QK_DIM), BF16),
        jax.ShapeDtypeStruct((n_streams, QK_DIM, seq), F32),
        jax.ShapeDtypeStruct((n_streams, QK_DIM, seq), BF16),
        jax.ShapeDtypeStruct((n_streams, seq, N_HEADS, V_DIM), F32),
        jax.ShapeDtypeStruct((n, ATTN_V_WIDTH), BF16),
        jax.ShapeDtypeStruct((n, POOL_DIM), BF16),
        jax.ShapeDtypeStruct((n_streams, HIST_ROWS, POOL_DIM), F32),
    )
    return pl.pallas_call(
        functools.partial(_inproj_prompt_kernel, tm=tm),
        out_shape=out_shape,
        grid=(n_streams, nt),
        in_specs=[
            pl.BlockSpec((tm, D_MODEL), row_map),
            pl.BlockSpec(w_pqv.shape, const2),
            pl.BlockSpec(wkt.shape, const2),
            pl.BlockSpec((tm, LANES), lambda b, t: (t, 0)),
            pl.BlockSpec((tm, LANES), lambda b, t: (t, 0)),
            pl.BlockSpec((HALF_DIM, tm), lambda b, t: (0, t)),
            pl.BlockSpec((HALF_DIM, tm), lambda b, t: (0, t)),
            pl.BlockSpec((tm, POOL_DIM), lambda b, t: (t, 0)),
            pl.BlockSpec((1, HIST_ROWS, POOL_DIM), lambda b, t: (b, 0, 0)),
            pl.BlockSpec((len(POOL_WINDOWS), POOL_GROUP_DIM, POOL_GROUP_DIM), lambda b, t: (0, 0, 0)),
            pl.BlockSpec((1, POOL_DIM), const2),
        ],
        out_specs=(
            pl.BlockSpec((tm, QK_DIM), row_map),
            pl.BlockSpec((None, QK_DIM, tm), lambda b, t: (b, 0, t)),
            pl.BlockSpec((None, QK_DIM, tm), lambda b, t: (b, 0, t)),
            pl.BlockSpec((None, tm, N_HEADS, V_DIM), lambda b, t: (b, t, 0, 0)),
            pl.BlockSpec((tm, ATTN_V_WIDTH), row_map),
            pl.BlockSpec((tm, POOL_DIM), row_map),
            pl.BlockSpec((1, HIST_ROWS, POOL_DIM), lambda b, t: (b, 0, 0)),
        ),
        scratch_shapes=[pltpu.VMEM((1, HIST_ROWS + tm, POOL_DIM), F32)],
        compiler_params=_params(("arbitrary", "arbitrary")),
        name="inproj_prompt",
    )(x2d, w_pqv, wkt, cos, sin, cost, sint, icnt, hist, wmix, pscale)


def _inproj_sample_kernel(x_ref, w_ref, wk_ref, cos_ref, sin_ref, icnt_ref, hist_ref, wmix_ref, pscale_ref,
                          q_ref, k_ref, v_ref, py_ref, pnew_ref, ext_ref, *, bb, tm):
    x = x_ref[...].astype(BF16)
    _pool_branch(x, w_ref, icnt_ref, hist_ref, wmix_ref, pscale_ref, py_ref, pnew_ref, ext_ref, bb=bb, tm=tm)
    cos = jnp.concatenate([cos_ref[...]] * bb, axis=0)
    sin = jnp.concatenate([sin_ref[...]] * bb, axis=0)
    hq = _dot(x, w_ref[:, POOL_DIM:POOL_DIM + QK_DIM])
    hk = _dot(x, wk_ref[...])
    for h in range(N_HEADS):
        sl = slice(h * V_DIM, (h + 1) * V_DIM)
        q_ref[:, sl] = (_rope_rows(hq[:, sl], cos, sin) * ATTN_SCALE).astype(BF16)
        k_ref[:, sl] = _rope_rows(hk[:, sl], cos, sin)
    v_ref[...] = _dot(x, w_ref[:, POOL_DIM + QK_DIM:POOL_DIM + QK_DIM + ATTN_V_WIDTH])


def _inproj_sample(x2d, w_pqv, wk, cos, sin, icnt, hist, wmix, pscale, *, n_streams, seq):
    n = n_streams * seq
    const2 = lambda i, t: (0, 0)
    const3 = lambda i, t: (0, 0, 0)
    out_shape = (
        jax.ShapeDtypeStruct((n, QK_DIM), BF16),
        jax.ShapeDtypeStruct((n, QK_DIM), F32),
        jax.ShapeDtypeStruct((n, ATTN_V_WIDTH), F32),
        jax.ShapeDtypeStruct((n, POOL_DIM), BF16),
        jax.ShapeDtypeStruct((n_streams, HIST_ROWS, POOL_DIM), F32),
    )
    return pl.pallas_call(
        functools.partial(_inproj_sample_kernel, bb=n_streams, tm=seq),
        out_shape=out_shape,
        grid=(1, 1),
        in_specs=[
            pl.BlockSpec((n, D_MODEL), const2),
            pl.BlockSpec(w_pqv.shape, const2),
            pl.BlockSpec(wk.shape, const2),
            pl.BlockSpec((seq, LANES), const2),
            pl.BlockSpec((seq, LANES), const2),
            pl.BlockSpec((seq, POOL_DIM), const2),
            pl.BlockSpec((n_streams, HIST_ROWS, POOL_DIM), const3),
            pl.BlockSpec((len(POOL_WINDOWS), POOL_GROUP_DIM, POOL_GROUP_DIM), const3),
            pl.BlockSpec((1, POOL_DIM), const2),
        ],
        out_specs=(
            pl.BlockSpec((n, QK_DIM), const2),
            pl.BlockSpec((n, QK_DIM), const2),
            pl.BlockSpec((n, ATTN_V_WIDTH), const2),
            pl.BlockSpec((n, POOL_DIM), const2),
            pl.BlockSpec((n_streams, HIST_ROWS, POOL_DIM), const3),
        ),
        scratch_shapes=[pltpu.VMEM((n_streams, HIST_ROWS + seq, POOL_DIM), F32)],
        compiler_params=_params(("arbitrary", "arbitrary")),
        name="inproj_sample",
    )(x2d, w_pqv, wk, cos, sin, icnt, hist, wmix, pscale)


def _lambda_value(lam_ref, lam_init):
    lv = lam_ref[...]
    s1 = jnp.sum(lv[0:1] * lv[1:2], axis=1, keepdims=True)
    s2 = jnp.sum(lv[2:3] * lv[3:4], axis=1, keepdims=True)
    return jnp.exp(s1) - jnp.exp(s2) + lam_init


def _head_norm(o, g, lam_init):
    ms = jnp.mean(o * o, axis=-1, keepdims=True)
    return o * lax.rsqrt(ms + SUBLN_EPS) * g * (1.0 - lam_init)


def _attn_prompt_kernel(lam_ref, g_ref, q_ref, kt_ref, v_ref, o_ref, *, seq, tq, lam_init):
    lam = _lambda_value(lam_ref, lam_init)
    g = g_ref[...]
    r = lax.broadcasted_iota(jnp.int32, (tq, tq), 0)
    c = lax.broadcasted_iota(jnp.int32, (tq, tq), 1)
    diag_visible = (c // CHUNK) <= (r // CHUNK)
    lane = lax.broadcasted_iota(jnp.int32, (tq, V_DIM), 1)

    for i in range(seq // tq):
        lo = i * tq
        q = q_ref[lo:lo + tq, :]
        zero = jnp.zeros_like(q)
        qc = (jnp.where(lane < HEAD_DIM, q, zero), jnp.where(lane >= HEAD_DIM, q, zero))
        sd = [jnp.where(diag_visible, _dot(qc[k], kt_ref[:, lo:lo + tq]), NEG_INF) for k in range(2)]
        m = [jnp.max(s, axis=1, keepdims=True) for s in sd]
        if i > 0:
            sp = [_dot(qc[k], kt_ref[:, 0:lo]) for k in range(2)]
            m = [jnp.maximum(m[k], jnp.max(sp[k], axis=1, keepdims=True)) for k in range(2)]
        pd = [jnp.exp(sd[k] - m[k]) for k in range(2)]
        l = [jnp.sum(p, axis=1, keepdims=True) for p in pd]
        if i > 0:
            pp = [jnp.exp(sp[k] - m[k]) for k in range(2)]
            l = [l[k] + jnp.sum(pp[k], axis=1, keepdims=True) for k in range(2)]
        r0 = 1.0 / l[0]
        r1 = lam / l[1]
        o = _dot((pd[0] * r0 - pd[1] * r1).astype(BF16), v_ref[lo:lo + tq, :])
        if i > 0:
            o = o + _dot((pp[0] * r0 - pp[1] * r1).astype(BF16), v_ref[0:lo, :])
        o_ref[lo:lo + tq, :] = _head_norm(o, g, lam_init).astype(o_ref.dtype)


def _attn_prompt(lam_vecs, norm_g, q, ktb, vb, *, n_streams, seq, tq, lam_init):
    return pl.pallas_call(
        functools.partial(_attn_prompt_kernel, seq=seq, tq=tq, lam_init=lam_init),
        out_shape=jax.ShapeDtypeStruct((n_streams * seq, ATTN_V_WIDTH), BF16),
        grid=(n_streams, N_HEADS),
        in_specs=[
            pl.BlockSpec((4, HEAD_DIM), lambda b, h: (0, 0)),
            pl.BlockSpec((1, V_DIM), lambda b, h: (0, 0)),
            pl.BlockSpec((seq, V_DIM), lambda b, h: (b, h)),
            pl.BlockSpec((None, V_DIM, seq), lambda b, h: (b, h, 0)),
            pl.BlockSpec((seq, V_DIM), lambda b, h: (b, h)),
        ],
        out_specs=pl.BlockSpec((seq, V_DIM), lambda b, h: (b, h)),
        compiler_params=_params(("arbitrary", "arbitrary")),
        name="attn_prompt",
    )(lam_vecs, norm_g, q, ktb, vb)


def _attn_sample_kernel(lam_ref, g_ref, q_ref, kc_ref, vc_ref, kn_ref, vn_ref, o_ref,
                        s_scr, w_scr, wn_scr, m_scr, acc_scr, qbd_scr, *, nk, tn, past, lam_init):
    j = pl.program_id(1)
    half = N_HEADS * tn

    @pl.when(j == 0)
    def _():
        q = q_ref[...]
        qt = jnp.concatenate([q] * (2 * N_HEADS), axis=0)
        r = lax.broadcasted_iota(jnp.int32, qt.shape, 0)
        l = lax.broadcasted_iota(jnp.int32, qt.shape, 1)
        keep = ((r // half) == ((l % V_DIM) // HEAD_DIM)) & (((r % half) // tn) == (l // V_DIM))
        qbd_scr[...] = jnp.where(keep, qt, jnp.zeros_like(qt))
        m_scr[...] = jnp.full(m_scr.shape, NEG_INF, F32)
        acc_scr[...] = jnp.zeros(acc_scr.shape, F32)

    @pl.when(j < nk)
    def _():
        s = _dot(qbd_scr[...], kc_ref[...].astype(BF16))
        s_scr[j] = s
        m_scr[...] = jnp.maximum(m_scr[...], jnp.max(s, axis=1, keepdims=True))

    @pl.when(j == nk - 1)
    def _():
        lam = _lambda_value(lam_ref, lam_init)
        sn = _dot_nt(qbd_scr[...], kn_ref[...].astype(BF16))
        qpos = past + (lax.broadcasted_iota(jnp.int32, sn.shape, 0) % tn)
        kpos = past + lax.broadcasted_iota(jnp.int32, sn.shape, 1)
        sn = jnp.where((kpos // CHUNK) <= (qpos // CHUNK), sn, NEG_INF)
        m = jnp.maximum(m_scr[...], jnp.max(sn, axis=1, keepdims=True))
        pn = jnp.exp(sn - m)
        l = jnp.sum(pn, axis=1, keepdims=True)
        for c in range(nk):
            p = jnp.exp(s_scr[c] - m)
            s_scr[c] = p
            l = l + jnp.sum(p, axis=1, keepdims=True)
        r0 = 1.0 / l[:half]
        r1 = lam / l[half:]
        wn_scr[...] = pn[:half] * r0 - pn[half:] * r1
        for c in range(nk):
            p = s_scr[c]
            w_scr[c] = (p[:half] * r0 - p[half:] * r1).astype(BF16)

    def v_rows(ref):
        return jnp.concatenate([ref[:, h, :] for h in range(N_HEADS)], axis=1).astype(BF16)

    @pl.when(j >= nk)
    def _():
        acc_scr[...] += _dot(w_scr[j - nk], v_rows(vc_ref))

    @pl.when(j == 2 * nk - 1)
    def _():
        acc = acc_scr[...] + _dot(wn_scr[...].astype(BF16), vn_ref[...].astype(BF16))
        g = g_ref[...]
        for h in range(N_HEADS):
            o = acc[h * tn:(h + 1) * tn, h * V_DIM:(h + 1) * V_DIM]
            o_ref[:, h * V_DIM:(h + 1) * V_DIM] = _head_norm(o, g, lam_init).astype(o_ref.dtype)


def _attn_sample(lam_vecs, norm_g, q, kct, vc, kn, vn, *, n_streams, tn, past, tk, lam_init):
    nk = past // tk
    rows = 2 * N_HEADS * tn
    half = N_HEADS * tn
    return pl.pallas_call(
        functools.partial(_attn_sample_kernel, nk=nk, tn=tn, past=past, lam_init=lam_init),
        out_shape=jax.ShapeDtypeStruct((n_streams * tn, ATTN_V_WIDTH), BF16),
        grid=(n_streams, 2 * nk),
        in_specs=[
            pl.BlockSpec((4, HEAD_DIM), lambda b, j: (0, 0)),
            pl.BlockSpec((1, V_DIM), lambda b, j: (0, 0)),
            pl.BlockSpec((tn, QK_DIM), lambda b, j: (b, 0)),
            pl.BlockSpec((None, QK_DIM, tk), lambda b, j: (b, 0, jnp.minimum(j, nk - 1))),
            pl.BlockSpec((None, tk, N_HEADS, V_DIM), lambda b, j: (b, jnp.maximum(j - nk, 0), 0, 0)),
            pl.BlockSpec((tn, QK_DIM), lambda b, j: (b, 0)),
            pl.BlockSpec((tn, ATTN_V_WIDTH), lambda b, j: (b, 0)),
        ],
        out_specs=pl.BlockSpec((tn, ATTN_V_WIDTH), lambda b, j: (b, 0)),
        scratch_shapes=[
            pltpu.VMEM((nk, rows, tk), F32),
            pltpu.VMEM((nk, half, tk), BF16),
            pltpu.VMEM((half, tn), F32),
            pltpu.VMEM((rows, 1), F32),
            pltpu.VMEM((half, ATTN_V_WIDTH), F32),
            pltpu.VMEM((rows, QK_DIM), BF16),
        ],
        compiler_params=_params(("arbitrary", "arbitrary")),
        name="attn_sample",
    )(lam_vecs, norm_g, q, kct, vc, kn, vn)


def _layer_norm(z, g, b):
    mu = jnp.mean(z, axis=-1, keepdims=True)
    zc = z - mu
    var = jnp.mean(zc * zc, axis=-1, keepdims=True)
    return zc * lax.rsqrt(var + LN_EPS) * g + b


def _postmix_kernel(x_ref, py_ref, ay_ref, wg_ref, wpo_ref, wao_ref, wout_ref, g1_ref, b1_ref,
                    wr_ref, br_ref, x1_ref, gate_ref, lpos_ref, cnt_ref, *, tm, dn_alpha):
    x = x_ref[...]
    xb = x.astype(BF16)
    gates = jax.nn.sigmoid(_dot(xb, wg_ref[...]))
    a = _dot(py_ref[...], wpo_ref[...])
    b = _dot(ay_ref[...], wao_ref[...])
    mixed = gates[:, :D_MODEL] * a + gates[:, D_MODEL:] * b
    mo = _dot(mixed.astype(BF16), wout_ref[...])
    x1 = _layer_norm(dn_alpha * x + mo, g1_ref[...], b1_ref[...])
    x1_ref[...] = x1

    logits = _dot(x1.astype(BF16), wr_ref[...]) + br_ref[...]
    lane = lax.broadcasted_iota(jnp.int32, logits.shape, 1)
    work = logits
    vals, idxs = [], []
    for _ in range(TOP_K):
        mx = jnp.max(work, axis=1, keepdims=True)
        ix = jnp.min(jnp.where(work == mx, lane, N_EXPERTS), axis=1, keepdims=True)
        vals.append(mx)
        idxs.append(ix)
        work = jnp.where(lane == ix, -jnp.inf, work)
    exps = [jnp.exp(v - vals[0]) for v in vals]
    denom = exps[0] + exps[1] + exps[2] + exps[3]

    onehot = jnp.zeros(logits.shape, F32)
    for ix in idxs:
        onehot = onehot + (lane == ix).astype(F32)
    r = lax.broadcasted_iota(jnp.int32, (tm, tm), 0)
    c = lax.broadcasted_iota(jnp.int32, (tm, tm), 1)
    tri = jnp.where(c < r, 1.0, 0.0).astype(BF16)
    earlier = _dot(tri, onehot.astype(BF16))
    cnt = jnp.sum(onehot, axis=0, keepdims=True)
    units = jnp.floor((cnt + (RUN_ALIGN - 1.0)) * (1.0 / RUN_ALIGN))
    er = lax.broadcasted_iota(jnp.int32, (N_EXPERTS, N_EXPERTS), 0)
    ec = lax.broadcasted_iota(jnp.int32, (N_EXPERTS, N_EXPERTS), 1)
    upper = jnp.where(er < ec, 1.0, 0.0).astype(BF16)
    run_off = _dot(jnp.broadcast_to(units, (8, N_EXPERTS)).astype(BF16), upper)[0:1] * float(RUN_ALIGN)
    cnt_ref[...] = (units * float(RUN_ALIGN)).reshape(cnt_ref.shape)
    pos = earlier + run_off

    lane_out = lax.broadcasted_iota(jnp.int32, (tm, LANES), 1)
    lpos_out = jnp.zeros((tm, LANES), jnp.int32)
    gate_out = jnp.zeros((tm, LANES), F32)
    for k in range(TOP_K):
        lpos_k = jnp.sum(jnp.where(lane == idxs[k], pos, 0.0), axis=1, keepdims=True).astype(jnp.int32)
        lpos_out = jnp.where(lane_out == k, lpos_k, lpos_out)
        gate_out = jnp.where(lane_out == k, exps[k] / denom, gate_out)
    lpos_ref[...] = lpos_out
    gate_ref[...] = gate_out


def _postmix(x2d, py, ay, wg, wpo, wao, wout, g1, b1, wr, br, *, tm, dn_alpha):
    n = x2d.shape[0]
    row = lambda i: (i, 0)
    const = lambda i: (0, 0)
    out_shape = (
        jax.ShapeDtypeStruct((n, D_MODEL), F32),
        jax.ShapeDtypeStruct((n, LANES), F32),
        jax.ShapeDtypeStruct((n, LANES), jnp.int32),
        jax.ShapeDtypeStruct((n // tm, 1, N_EXPERTS), F32),
    )
    return pl.pallas_call(
        functools.partial(_postmix_kernel, tm=tm, dn_alpha=dn_alpha),
        out_shape=out_shape,
        grid=(n // tm,),
        in_specs=[
            pl.BlockSpec((tm, D_MODEL), row),
            pl.BlockSpec((tm, POOL_DIM), row),
            pl.BlockSpec((tm, ATTN_V_WIDTH), row),
            pl.BlockSpec(wg.shape, const),
            pl.BlockSpec(wpo.shape, const),
            pl.BlockSpec(wao.shape, const),
            pl.BlockSpec(wout.shape, const),
            pl.BlockSpec((1, D_MODEL), const),
            pl.BlockSpec((1, D_MODEL), const),
            pl.BlockSpec(wr.shape, const),
            pl.BlockSpec((1, N_EXPERTS), const),
        ],
        out_specs=(
            pl.BlockSpec((tm, D_MODEL), row),
            pl.BlockSpec((tm, LANES), row),
            pl.BlockSpec((tm, LANES), row),
            pl.BlockSpec((1, 1, N_EXPERTS), lambda i: (i, 0, 0)),
        ),
        compiler_params=_params(("arbitrary",)),
        name="postmix",
    )(x2d, py, ay, wg, wpo, wao, wout, g1, b1, wr, br)


def _local_rows(tm):
    return TOP_K * tm + N_EXPERTS * RUN_ALIGN


def _for_each_run_chunk(tile, tab_refs, local_buf, sorted_ref, sem, to_sorted, fn):
    gs_ref, off_ref, cnt_ref = tab_refs

    def per_expert(e, carry):
        t = tile * N_EXPERTS + e
        cnt, off, gs = cnt_ref[t], off_ref[t], gs_ref[t]
        for b in RUN_BITS:
            size = 1 << b

            @pl.when((cnt & size) != 0)
            def _():
                lower = cnt & (size - 1)
                loc = local_buf.at[pl.ds(pl.multiple_of(off + lower, RUN_ALIGN), size)]
                srt = sorted_ref.at[pl.ds(pl.multiple_of(gs + lower, RUN_ALIGN), size)]
                fn(pltpu.make_async_copy(loc, srt, sem) if to_sorted else pltpu.make_async_copy(srt, loc, sem))
        return carry

    lax.fori_loop(0, N_EXPERTS, per_expert, 0)


def _dispatch_tile(tab_refs, lpos_ref, x_ref, xs_ref, xloc, sem, tm):
    lpos = lpos_ref[...]
    col = lax.broadcasted_iota(jnp.int32, (tm, _local_rows(tm)), 1)
    hit = col == lpos[:, 0:1]
    for k in range(1, TOP_K):
        hit = jnp.logical_or(hit, col == lpos[:, k:k + 1])
    perm_t = jnp.where(hit, 1.0, 0.0).astype(BF16)
    xloc[...] = lax.dot_general(perm_t, x_ref[...].astype(BF16), (((0,), (0,)), ((), ())),
                                preferred_element_type=F32)
    tile = pl.program_id(0)
    _for_each_run_chunk(tile, tab_refs, xloc, xs_ref, sem, True, lambda cp: cp.start())
    _for_each_run_chunk(tile, tab_refs, xloc, xs_ref, sem, True, lambda cp: cp.wait())


def _dispatch_first_kernel(tail_ref, nu_ref, gs_ref, off_ref, cnt_ref, lpos_ref, x_ref, xs_ref,
                           xloc, zbuf, sem, zsem, *, tm, n_blocks):
    @pl.when(pl.program_id(0) == 0)
    def _():
        zbuf[...] = jnp.zeros(zbuf.shape, zbuf.dtype)

        def zero_copy(row):
            row = pl.multiple_of(row, EXPERT_BLOCK)
            return pltpu.make_async_copy(zbuf, xs_ref.at[pl.ds(row, EXPERT_BLOCK)], zsem)

        def over_blocks(fn):
            for e in range(N_EXPERTS):
                @pl.when(tail_ref[e] >= 0)
                def _():
                    fn(zero_copy(tail_ref[e]))
            lax.fori_loop(nu_ref[0], n_blocks, lambda b, c: (fn(zero_copy(b * EXPERT_BLOCK)), c)[1], 0)

        over_blocks(lambda cp: cp.start())
        over_blocks(lambda cp: cp.wait())

    _dispatch_tile((gs_ref, off_ref, cnt_ref), lpos_ref, x_ref, xs_ref, xloc, sem, tm)


def _dispatch_next_kernel(gs_ref, off_ref, cnt_ref, lpos_ref, x_ref, xs_in_ref, xs_ref, xloc, sem, *, tm):
    del xs_in_ref
    _dispatch_tile((gs_ref, off_ref, cnt_ref), lpos_ref, x_ref, xs_ref, xloc, sem, tm)


def _dispatch_first(tail, n_used, tabs, lpos, x1, *, tm, n_blocks):
    n = x1.shape[0]
    grid_spec = pltpu.PrefetchScalarGridSpec(
        num_scalar_prefetch=5,
        grid=(n // tm,),
        in_specs=[pl.BlockSpec((tm, LANES), lambda i, *_: (i, 0)),
                  pl.BlockSpec((tm, D_MODEL), lambda i, *_: (i, 0))],
        out_specs=pl.BlockSpec(memory_space=pl.ANY),
        scratch_shapes=[pltpu.VMEM((_local_rows(tm), D_MODEL), F32),
                        pltpu.VMEM((EXPERT_BLOCK, D_MODEL), F32),
                        pltpu.SemaphoreType.DMA, pltpu.SemaphoreType.DMA],
    )
    return pl.pallas_call(
        functools.partial(_dispatch_first_kernel, tm=tm, n_blocks=n_blocks),
        out_shape=jax.ShapeDtypeStruct((n_blocks * EXPERT_BLOCK, D_MODEL), F32),
        grid_spec=grid_spec,
        compiler_params=_params(("arbitrary",)),
        name="dispatch_first",
    )(tail, n_used, *tabs, lpos, x1)


def _dispatch_next(tabs, lpos, x1, xs, *, tm):
    n = x1.shape[0]
    grid_spec = pltpu.PrefetchScalarGridSpec(
        num_scalar_prefetch=3,
        grid=(n // tm,),
        in_specs=[pl.BlockSpec((tm, LANES), lambda i, *_: (i, 0)),
                  pl.BlockSpec((tm, D_MODEL), lambda i, *_: (i, 0)),
                  pl.BlockSpec(memory_space=pl.ANY)],
        out_specs=pl.BlockSpec(memory_space=pl.ANY),
        scratch_shapes=[pltpu.VMEM((_local_rows(tm), D_MODEL), F32), pltpu.SemaphoreType.DMA],
    )
    return pl.pallas_call(
        functools.partial(_dispatch_next_kernel, tm=tm),
        out_shape=jax.ShapeDtypeStruct(xs.shape, xs.dtype),
        grid_spec=grid_spec,
        input_output_aliases={5: 0},
        compiler_params=_params(("arbitrary",)),
        name="dispatch_next",
    )(*tabs, lpos, x1, xs)


def _experts_kernel(be_ref, nu_ref, xs_ref, win_ref, bg_ref, bl_ref, wo_ref, bo_ref, y_ref,
                    wg_scr, wl_scr, wo_scr):
    i = pl.program_id(0)
    used = i < nu_ref[0]
    first_of_expert = jnp.logical_or(i == 0, be_ref[i] != be_ref[jnp.maximum(i - 1, 0)])

    @pl.when(jnp.logical_not(used))
    def _():
        y_ref[...] = jnp.zeros(y_ref.shape, y_ref.dtype)

    @pl.when(jnp.logical_and(used, first_of_expert))
    def _():
        r = lax.broadcasted_iota(jnp.int32, (MXU_DIM, MXU_DIM), 0)
        c = lax.broadcasted_iota(jnp.int32, (MXU_DIM, MXU_DIM), 1)
        src = jnp.where(c < LANES, 2 * c, 2 * (c - LANES) + 1)
        sel = jnp.where(r == src, 1.0, 0.0).astype(BF16)
        for gq in range(2 * D_EXPERT // MXU_DIM):
            blk = win_ref[:, gq * MXU_DIM:(gq + 1) * MXU_DIM].astype(BF16)
            d = _dot(blk, sel)
            wg_scr[:, gq * LANES:(gq + 1) * LANES] = d[:, :LANES].astype(BF16)
            wl_scr[:, gq * LANES:(gq + 1) * LANES] = d[:, LANES:].astype(BF16)
        wo_scr[...] = wo_ref[...].astype(BF16)

    @pl.when(used)
    def _():
        xb = xs_ref[...].astype(BF16)
        glu = jnp.minimum(_dot(xb, wg_scr[...]) + bg_ref[...], SWIGLU_LIMIT)
        lin = jnp.clip(_dot(xb, wl_scr[...]) + bl_ref[...], -SWIGLU_LIMIT, SWIGLU_LIMIT)
        act = glu * jax.nn.sigmoid(SWIGLU_ALPHA * glu) * (lin + 1.0)
        y_ref[...] = _dot(act.astype(BF16), wo_scr[...]) + bo_ref[...]


def _experts(blk_expert, n_used, xs, w_in, b_glu, b_lin, w_out, b_out):
    rows = xs.shape[0]
    n_blocks = rows // EXPERT_BLOCK
    wsel = lambda i, be, nu: (be[i], 0, 0)
    grid_spec = pltpu.PrefetchScalarGridSpec(
        num_scalar_prefetch=2,
        grid=(n_blocks,),
        in_specs=[
            pl.BlockSpec((EXPERT_BLOCK, D_MODEL), lambda i, be, nu: (jnp.minimum(i, nu[0] - 1), 0)),
            pl.BlockSpec((None, D_MODEL, 2 * D_EXPERT), wsel),
            pl.BlockSpec((None, 1, D_EXPERT), wsel),
            pl.BlockSpec((None, 1, D_EXPERT), wsel),
            pl.BlockSpec((None, D_EXPERT, D_MODEL), wsel),
            pl.BlockSpec((None, 1, D_MODEL), wsel),
        ],
        out_specs=pl.BlockSpec((EXPERT_BLOCK, D_MODEL), lambda i, be, nu: (i, 0)),
        scratch_shapes=[pltpu.VMEM((D_MODEL, D_EXPERT), BF16), pltpu.VMEM((D_MODEL, D_EXPERT), BF16),
                        pltpu.VMEM((D_EXPERT, D_MODEL), BF16)],
    )
    return pl.pallas_call(
        _experts_kernel,
        out_shape=jax.ShapeDtypeStruct((rows, D_MODEL), F32),
        grid_spec=grid_spec,
        compiler_params=_params(("arbitrary",)),
        name="experts",
    )(blk_expert, n_used, xs, w_in, b_glu, b_lin, w_out, b_out)


def _split_bf16(a):
    hi = a.astype(BF16)
    return hi, (a - hi.astype(F32)).astype(BF16)


def _combine_kernel(gs_ref, off_ref, cnt_ref, gate_ref, lpos_ref, x1_ref, g2_ref, b2_ref, yb_ref, o_ref,
                    yloc, sem, *, tm, dn_alpha):
    tile = pl.program_id(0)

    @pl.when(tile == 0)
    def _():
        yloc[...] = jnp.zeros(yloc.shape, yloc.dtype)

    tabs = (gs_ref, off_ref, cnt_ref)
    _for_each_run_chunk(tile, tabs, yloc, yb_ref, sem, False, lambda cp: cp.start())
    _for_each_run_chunk(tile, tabs, yloc, yb_ref, sem, False, lambda cp: cp.wait())

    gate, lpos = gate_ref[...], lpos_ref[...]
    col = lax.broadcasted_iota(jnp.int32, (tm, _local_rows(tm)), 1)
    weights = jnp.zeros(col.shape, F32)
    for k in range(TOP_K):
        weights = jnp.where(col == lpos[:, k:k + 1], gate[:, k:k + 1], weights)
    w_hi, w_lo = _split_bf16(weights)
    y_hi, y_lo = _split_bf16(yloc[...])
    y = _dot(w_hi, y_hi) + (_dot(w_hi, y_lo) + _dot(w_lo, y_hi))
    o_ref[...] = _layer_norm(dn_alpha * x1_ref[...] + y, g2_ref[...], b2_ref[...])


def _combine(tabs, gate, lpos, x1, g2, b2, yb, *, tm, dn_alpha):
    n = x1.shape[0]
    grid_spec = pltpu.PrefetchScalarGridSpec(
        num_scalar_prefetch=3,
        grid=(n // tm,),
        in_specs=[
            pl.BlockSpec((tm, LANES), lambda i, *_: (i, 0)),
            pl.BlockSpec((tm, LANES), lambda i, *_: (i, 0)),
            pl.BlockSpec((tm, D_MODEL), lambda i, *_: (i, 0)),
            pl.BlockSpec((1, D_MODEL), lambda i, *_: (0, 0)),
            pl.BlockSpec((1, D_MODEL), lambda i, *_: (0, 0)),
            pl.BlockSpec(memory_space=pl.ANY),
        ],
        out_specs=pl.BlockSpec((tm, D_MODEL), lambda i, *_: (i, 0)),
        scratch_shapes=[pltpu.VMEM((_local_rows(tm), D_MODEL), F32), pltpu.SemaphoreType.DMA],
    )
    return pl.pallas_call(
        functools.partial(_combine_kernel, tm=tm, dn_alpha=dn_alpha),
        out_shape=jax.ShapeDtypeStruct((n, D_MODEL), F32),
        grid_spec=grid_spec,
        compiler_params=_params(("arbitrary",)),
        name="combine",
    )(*tabs, gate, lpos, x1, g2, b2, yb)


def _position_tables(pos0, seq):
    pos = pos0 + jnp.arange(seq, dtype=jnp.int32)
    inv = ROPE_THETA ** (-jnp.arange(HALF_DIM, dtype=F32) / HALF_DIM)
    ang = pos.astype(F32)[:, None] * inv[None, :]
    cos, sin = jnp.cos(ang), jnp.sin(ang)
    cos_rows = jnp.concatenate([cos, cos, cos, cos], axis=-1)
    sin_rows = jnp.concatenate([-sin, sin, -sin, sin], axis=-1)
    icnt = jnp.concatenate(
        [jnp.broadcast_to((1.0 / jnp.minimum(pos + 1, w).astype(F32))[:, None], (seq, POOL_GROUP_DIM))
         for w in POOL_WINDOWS], axis=-1)
    return cos_rows, sin_rows, cos.T, sin.T, icnt


def _tile(n, pref):
    t = min(n, pref)
    while n % t:
        t //= 2
    return t


def kernel(x_prompt, x_sample, cache_k, cache_v, state_pool, w_in, w_pool_mix, pool_scale, w_pool_out,
           lambda_q1, lambda_k1, lambda_q2, lambda_k2, attn_norm_g, w_attn_out, w_out, ln1_g, ln1_b,
           w_router, b_router, w_expert_in, b_expert_in, w_expert_out, b_expert_out, ln2_g, ln2_b):
    depth = w_in.shape[0]
    assert depth == 1, "single-layer step"
    dn_alpha = (2.0 * depth) ** 0.25
    lam_init = 0.8 - 0.6 * math.exp(-0.3 * 0)
    bp, sp, _ = x_prompt.shape
    bs, ss, _ = x_sample.shape
    past = cache_k.shape[2]
    np_, ns = bp * sp, bs * ss

    c_q, c_k, c_v = POOL_DIM, POOL_DIM + QK_DIM, POOL_DIM + 2 * QK_DIM
    c_gate = c_v + ATTN_V_WIDTH
    w0 = w_in[0]
    w_pqv = jnp.concatenate([w0[:, :c_k], w0[:, c_v:c_gate]], axis=1).astype(BF16)
    w_k = w0[:, c_k:c_v].astype(BF16)
    w_gate = w0[:, c_gate:].astype(BF16)
    wmix = w_pool_mix[0].astype(BF16)
    pscale = pool_scale[0].reshape(1, POOL_DIM)
    wpo = w_pool_out[0].astype(BF16)
    wao = w_attn_out[0].astype(BF16)
    wout = w_out[0].astype(BF16)
    wr = w_router[0].astype(BF16)
    br = b_router[0].reshape(1, N_EXPERTS)
    lam_vecs = jnp.stack([lambda_q1[0], lambda_k1[0], lambda_q2[0], lambda_k2[0]])
    norm_g = attn_norm_g[0].reshape(1, V_DIM)
    g1, b1 = ln1_g[0].reshape(1, D_MODEL), ln1_b[0].reshape(1, D_MODEL)
    g2, b2 = ln2_g[0].reshape(1, D_MODEL), ln2_b[0].reshape(1, D_MODEL)
    b_glu = b_expert_in[0][:, 0::2].reshape(N_EXPERTS, 1, D_EXPERT)
    b_lin = b_expert_in[0][:, 1::2].reshape(N_EXPERTS, 1, D_EXPERT)
    b_eo = b_expert_out[0].reshape(N_EXPERTS, 1, D_MODEL)

    xp = x_prompt.reshape(np_, D_MODEL)
    cos_p, sin_p, cost_p, sint_p, icnt_p = _position_tables(0, sp)
    hist_p = jnp.zeros((bp, HIST_ROWS, POOL_DIM), F32)
    q_p, kt_p, ktb_p, v_p, vb_p, py_p, pnew_p = _inproj_prompt(
        xp, w_pqv, w_k.T, cos_p, sin_p, cost_p, sint_p, icnt_p, hist_p, wmix, pscale,
        n_streams=bp, seq=sp, tm=_tile(sp, 256))
    ay_p = _attn_prompt(lam_vecs, norm_g, q_p, ktb_p, vb_p, n_streams=bp, seq=sp,
                        tq=_tile(sp, 256), lam_init=lam_init)
    tm_p, tm_s = _tile(np_, ROUTE_TILE), _tile(ns, ROUTE_TILE)
    x1_p, gate_p, lpos_p, cnt_p = _postmix(
        xp, py_p, ay_p, w_gate, wpo, wao, wout, g1, b1, wr, br, tm=tm_p, dn_alpha=dn_alpha)

    xs_ = x_sample.reshape(ns, D_MODEL)
    cos_s, sin_s, _, _, icnt_s = _position_tables(past, ss)
    hist_s = jnp.concatenate([jnp.zeros((bs, 1, POOL_DIM), F32), state_pool[0]], axis=1)
    q_s, k_s, v_s, py_s, pnew_s = _inproj_sample(
        xs_, w_pqv, w_k, cos_s, sin_s, icnt_s, hist_s, wmix, pscale, n_streams=bs, seq=ss)
    kct = jnp.transpose(cache_k[0], (0, 2, 3, 4, 1)).reshape(bs, QK_DIM, past)
    ay_s = _attn_sample(lam_vecs, norm_g, q_s, kct, cache_v[0], k_s, v_s, n_streams=bs, tn=ss, past=past,
                        tk=_tile(past, 512), lam_init=lam_init)
    x1_s, gate_s, lpos_s, cnt_s = _postmix(
        xs_, py_s, ay_s, w_gate, wpo, wao, wout, g1, b1, wr, br, tm=tm_s, dn_alpha=dn_alpha)

    ntp = np_ // tm_p
    cnt = jnp.concatenate([cnt_p[:, 0, :], cnt_s[:, 0, :]], axis=0).astype(jnp.int32)
    n_tiles = cnt.shape[0]
    group = jnp.sum(cnt, axis=0)
    padded = (group + EXPERT_BLOCK - 1) // EXPERT_BLOCK * EXPERT_BLOCK
    pad_end = jnp.cumsum(padded).astype(jnp.int32)
    run_start = (pad_end - padded)[None, :] + jnp.cumsum(cnt, axis=0) - cnt
    run_off = jnp.cumsum(cnt, axis=1) - cnt
    tail = jnp.where(padded > 0, pad_end - EXPERT_BLOCK, -1).astype(jnp.int32)
    max_rows = (np_ + ns) * TOP_K + n_tiles * N_EXPERTS * (RUN_ALIGN - 1) + N_EXPERTS * (EXPERT_BLOCK - 1)
    n_blocks = -(-max_rows // EXPERT_BLOCK)
    n_used = pad_end[-1:] // EXPERT_BLOCK
    blk_start = jnp.arange(n_blocks, dtype=jnp.int32) * EXPERT_BLOCK
    blk_expert = jnp.minimum(jnp.sum((blk_start[:, None] >= pad_end[None, :]).astype(jnp.int32), axis=1),
                             N_EXPERTS - 1)
    tabs_p = tuple(a[:ntp].reshape(-1).astype(jnp.int32) for a in (run_start, run_off, cnt))
    tabs_s = tuple(a[ntp:].reshape(-1).astype(jnp.int32) for a in (run_start, run_off, cnt))

    xsorted = _dispatch_first(tail, n_used, tabs_p, lpos_p, x1_p, tm=tm_p, n_blocks=n_blocks)
    xsorted = _dispatch_next(tabs_s, lpos_s, x1_s, xsorted, tm=tm_s)
    yb = _experts(blk_expert, n_used, xsorted, w_expert_in[0], b_glu, b_lin, w_expert_out[0], b_eo)
    y_p = _combine(tabs_p, gate_p, lpos_p, x1_p, g2, b2, yb, tm=tm_p, dn_alpha=dn_alpha)
    y_s = _combine(tabs_s, gate_s, lpos_s, x1_s, g2, b2, yb, tm=tm_s, dn_alpha=dn_alpha)

    k_prompt = jnp.transpose(kt_p.reshape(bp, N_HEADS, 2, HEAD_DIM, sp), (0, 4, 1, 2, 3))
    return (
        y_p.reshape(bp, sp, D_MODEL),
        y_s.reshape(bs, ss, D_MODEL),
        k_prompt[None],
        v_p[None],
        pnew_p[:, 1:].reshape(1, bp, POOL_HIST, POOL_DIM),
        k_s.reshape(1, bs, ss, N_HEADS, 2, HEAD_DIM),
        v_s.reshape(1, bs, ss, N_HEADS, V_DIM),
        pnew_s[:, 1:].reshape(1, bs, POOL_HIST, POOL_DIM),
    )
```

```python
import functools
import math

import jax
import jax.numpy as jnp
from jax import lax
from jax.experimental import pallas as pl
from jax.experimental.pallas import tpu as pltpu

D_MODEL = 1024
CHUNK = 64
POOL_WINDOWS = (2, 4, 8, 16)
POOL_GROUP_DIM = 128
POOL_DIM = len(POOL_WINDOWS) * POOL_GROUP_DIM
POOL_HIST = max(POOL_WINDOWS) - 1
HIST_ROWS = POOL_HIST + 1
N_HEADS = 8
HEAD_DIM = 64
HALF_DIM = HEAD_DIM // 2
V_DIM = 2 * HEAD_DIM
QK_DIM = N_HEADS * 2 * HEAD_DIM
ATTN_V_WIDTH = N_HEADS * V_DIM
ATTN_SCALE = HEAD_DIM ** -0.5
ROPE_THETA = 10000.0
SUBLN_EPS = 1e-5
N_EXPERTS = 32
TOP_K = 4
D_EXPERT = 1024
SWIGLU_LIMIT = 7.0
SWIGLU_ALPHA = 1.702
LN_EPS = 1e-5
NEG_INF = -1e30
LANES = 128
MXU_DIM = 256

F32 = jnp.float32
BF16 = jnp.bfloat16

VMEM_LIMIT = 56 * 1024 * 1024
EXPERT_BLOCK = 256
RUN_ALIGN = 8
ROUTE_TILE = 256
DENSE_TILE = 512
RUN_BITS = tuple(range(3, 9))


def _dot(a, b):
    return jnp.dot(a, b, preferred_element_type=F32)


def _dot_nt(a, b):
    return lax.dot_general(a, b, (((1,), (1,)), ((), ())), preferred_element_type=F32)


def _params(semantics):
    return pltpu.CompilerParams(dimension_semantics=semantics, vmem_limit_bytes=VMEM_LIMIT)


def _pool_branch(x, w_ref, icnt_ref, hist_ref, wmix_ref, pscale_ref, py_ref, pnew_ref, ext_ref, *, bb, tm):
    u = _dot(x, w_ref[:, 0:POOL_DIM])

    @pl.when(pl.program_id(1) == 0)
    def _():
        ext_ref[:, 0:HIST_ROWS, :] = hist_ref[...]

    for b in range(bb):
        ext_ref[b, HIST_ROWS:HIST_ROWS + tm, :] = u[b * tm:(b + 1) * tm]
    for b in range(bb):
        for g, w in enumerate(POOL_WINDOWS):
            cols = slice(g * POOL_GROUP_DIM, (g + 1) * POOL_GROUP_DIM)
            cur = ext_ref[b, HIST_ROWS:HIST_ROWS + tm, cols]
            acc = cur
            for j in range(1, w):
                acc = acc + ext_ref[b, HIST_ROWS - j:HIST_ROWS - j + tm, cols]
            d = acc * icnt_ref[:, cols] - cur
            y = _dot(d.astype(BF16), wmix_ref[g]) * pscale_ref[:, cols]
            py_ref[b * tm:(b + 1) * tm, cols] = y.astype(BF16)
    tail = ext_ref[:, tm:tm + HIST_ROWS, :]
    pnew_ref[...] = tail
    ext_ref[:, 0:HIST_ROWS, :] = tail


def _rope_rows(z, cos, sin):
    lane = lax.broadcasted_iota(jnp.int32, z.shape, 1)
    first_half = (lane % HEAD_DIM) < HALF_DIM
    partner = jnp.where(first_half, pltpu.roll(z, LANES - HALF_DIM, 1), pltpu.roll(z, HALF_DIM, 1))
    return z * cos + partner * sin


def _inproj_prompt_kernel(x_ref, w_ref, wkt_ref, cos_ref, sin_ref, cost_ref, sint_ref, icnt_ref, hist_ref,
                          wmix_ref, pscale_ref, q_ref, kt_ref, ktb_ref, v_ref, vb_ref, py_ref, pnew_ref,
                          ext_ref, *, tm):
    x = x_ref[...].astype(BF16)
    _pool_branch(x, w_ref, icnt_ref, hist_ref, wmix_ref, pscale_ref, py_ref, pnew_ref, ext_ref, bb=1, tm=tm)

    cos, sin = cos_ref[...], sin_ref[...]
    hq = _dot(x, w_ref[:, POOL_DIM:POOL_DIM + QK_DIM])
    for h in range(N_HEADS):
        sl = slice(h * V_DIM, (h + 1) * V_DIM)
        q_ref[:, sl] = (_rope_rows(hq[:, sl], cos, sin) * ATTN_SCALE).astype(BF16)

    hkt = _dot_nt(wkt_ref[...], x)
    cost, sint = cost_ref[...], sint_ref[...]
    for hc in range(2 * N_HEADS):
        r0 = hc * HEAD_DIM
        x1 = hkt[r0:r0 + HALF_DIM]
        x2 = hkt[r0 + HALF_DIM:r0 + HEAD_DIM]
        o1 = x1 * cost - x2 * sint
        o2 = x2 * cost + x1 * sint
        kt_ref[r0:r0 + HALF_DIM, :] = o1
        kt_ref[r0 + HALF_DIM:r0 + HEAD_DIM, :] = o2
        ktb_ref[r0:r0 + HALF_DIM, :] = o1.astype(BF16)
        ktb_ref[r0 + HALF_DIM:r0 + HEAD_DIM, :] = o2.astype(BF16)

    hv = _dot(x, w_ref[:, POOL_DIM + QK_DIM:POOL_DIM + QK_DIM + ATTN_V_WIDTH])
    vb_ref[...] = hv.astype(BF16)
    for h in range(N_HEADS):
        v_ref[:, h, :] = hv[:, h * V_DIM:(h + 1) * V_DIM]


def _inproj_prompt(x2d, w_pqv, wkt, cos, sin, cost, sint, icnt, hist, wmix, pscale, *, n_streams, seq, tm):
    n = n_streams * seq
    nt = seq // tm
    row_map = lambda b, t: (b * nt + t, 0)
    const2 = lambda b, t: (0, 0)
    out_shape = (
        jax.ShapeDtypeStruct((n, QK_DIM), BF16),
        jax.ShapeDtypeStruct((n_streams, QK_DIM, seq), F32),
        jax.ShapeDtypeStruct((n_streams, QK_DIM, seq), BF16),
        jax.ShapeDtypeStruct((n_streams, seq, N_HEADS, V_DIM), F32),
        jax.ShapeDtypeStruct((n, ATTN_V_WIDTH), BF16),
        jax.ShapeDtypeStruct((n, POOL_DIM), BF16),
        jax.ShapeDtypeStruct((n_streams, HIST_ROWS, POOL_DIM), F32),
    )
    return pl.pallas_call(
        functools.partial(_inproj_prompt_kernel, tm=tm),
        out_shape=out_shape,
        grid=(n_streams, nt),
        in_specs=[
            pl.BlockSpec((tm, D_MODEL), row_map),
            pl.BlockSpec(w_pqv.shape, const2),
            pl.BlockSpec(wkt.shape, const2),
            pl.BlockSpec((tm, LANES), lambda b, t: (t, 0)),
            pl.BlockSpec((tm, LANES), lambda b, t: (t, 0)),
            pl.BlockSpec((HALF_DIM, tm), lambda b, t: (0, t)),
            pl.BlockSpec((HALF_DIM, tm), lambda b, t: (0, t)),
            pl.BlockSpec((tm, POOL_DIM), lambda b, t: (t, 0)),
            pl.BlockSpec((1, HIST_ROWS, POOL_DIM), lambda b, t: (b, 0, 0)),
            pl.BlockSpec((len(POOL_WINDOWS), POOL_GROUP_DIM, POOL_GROUP_DIM), lambda b, t: (0, 0, 0)),
            pl.BlockSpec((1, POOL_DIM), const2),
        ],
        out_specs=(
            pl.BlockSpec((tm, QK_DIM), row_map),
            pl.BlockSpec((None, QK_DIM, tm), lambda b, t: (b, 0, t)),
            pl.BlockSpec((None, QK_DIM, tm), lambda b, t: (b, 0, t)),
            pl.BlockSpec((None, tm, N_HEADS, V_DIM), lambda b, t: (b, t, 0, 0)),
            pl.BlockSpec((tm, ATTN_V_WIDTH), row_map),
            pl.BlockSpec((tm, POOL_DIM), row_map),
            pl.BlockSpec((1, HIST_ROWS, POOL_DIM), lambda b, t: (b, 0, 0)),
        ),
        scratch_shapes=[pltpu.VMEM((1, HIST_ROWS + tm, POOL_DIM), F32)],
        compiler_params=_params(("arbitrary", "arbitrary")),
        name="inproj_prompt",
    )(x2d, w_pqv, wkt, cos, sin, cost, sint, icnt, hist, wmix, pscale)


def _inproj_sample_kernel(x_ref, w_ref, wk_ref, cos_ref, sin_ref, icnt_ref, hist_ref, wmix_ref, pscale_ref,
                          q_ref, k_ref, v_ref, py_ref, pnew_ref, ext_ref, *, bb, tm):
    x = x_ref[...].astype(BF16)
    _pool_branch(x, w_ref, icnt_ref, hist_ref, wmix_ref, pscale_ref, py_ref, pnew_ref, ext_ref, bb=bb, tm=tm)
    cos = jnp.concatenate([cos_ref[...]] * bb, axis=0)
    sin = jnp.concatenate([sin_ref[...]] * bb, axis=0)
    hq = _dot(x, w_ref[:, POOL_DIM:POOL_DIM + QK_DIM])
    hk = _dot(x, wk_ref[...])
    for h in range(N_HEADS):
        sl = slice(h * V_DIM, (h + 1) * V_DIM)
        q_ref[:, sl] = (_rope_rows(hq[:, sl], cos, sin) * ATTN_SCALE).astype(BF16)
        k_ref[:, sl] = _rope_rows(hk[:, sl], cos, sin)
    v_ref[...] = _dot(x, w_ref[:, POOL_DIM + QK_DIM:POOL_DIM + QK_DIM + ATTN_V_WIDTH])


def _inproj_sample(x2d, w_pqv, wk, cos, sin, icnt, hist, wmix, pscale, *, n_streams, seq):
    n = n_streams * seq
    const2 = lambda i, t: (0, 0)
    const3 = lambda i, t: (0, 0, 0)
    out_shape = (
        jax.ShapeDtypeStruct((n, QK_DIM), BF16),
        jax.ShapeDtypeStruct((n, QK_DIM), F32),
        jax.ShapeDtypeStruct((n, ATTN_V_WIDTH), F32),
        jax.ShapeDtypeStruct((n, POOL_DIM), BF16),
        jax.ShapeDtypeStruct((n_streams, HIST_ROWS, POOL_DIM), F32),
    )
    return pl.pallas_call(
        functools.partial(_inproj_sample_kernel, bb=n_streams, tm=seq),
        out_shape=out_shape,
        grid=(1, 1),
        in_specs=[
            pl.BlockSpec((n, D_MODEL), const2),
            pl.BlockSpec(w_pqv.shape, const2),
            pl.BlockSpec(wk.shape, const2),
            pl.BlockSpec((seq, LANES), const2),
            pl.BlockSpec((seq, LANES), const2),
            pl.BlockSpec((seq, POOL_DIM), const2),
            pl.BlockSpec((n_streams, HIST_ROWS, POOL_DIM), const3),
            pl.BlockSpec((len(POOL_WINDOWS), POOL_GROUP_DIM, POOL_GROUP_DIM), const3),
            pl.BlockSpec((1, POOL_DIM), const2),
        ],
        out_specs=(
            pl.BlockSpec((n, QK_DIM), const2),
            pl.BlockSpec((n, QK_DIM), const2),
            pl.BlockSpec((n, ATTN_V_WIDTH), const2),
            pl.BlockSpec((n, POOL_DIM), const2),
            pl.BlockSpec((n_streams, HIST_ROWS, POOL_DIM), const3),
        ),
        scratch_shapes=[pltpu.VMEM((n_streams, HIST_ROWS + seq, POOL_DIM), F32)],
        compiler_params=_params(("arbitrary", "arbitrary")),
        name="inproj_sample",
    )(x2d, w_pqv, wk, cos, sin, icnt, hist, wmix, pscale)


def _lambda_value(lam_ref, lam_init):
    lv = lam_ref[...]
    s1 = jnp.sum(lv[0:1] * lv[1:2], axis=1, keepdims=True)
    s2 = jnp.sum(lv[2:3] * lv[3:4], axis=1, keepdims=True)
    return jnp.exp(s1) - jnp.exp(s2) + lam_init


def _head_norm(o, g, lam_init):
    ms = jnp.mean(o * o, axis=-1, keepdims=True)
    return o * lax.rsqrt(ms + SUBLN_EPS) * g * (1.0 - lam_init)


def _attn_prompt_kernel(lam_ref, g_ref, q_ref, kt_ref, v_ref, o_ref, *, seq, tq, lam_init):
    lam = _lambda_value(lam_ref, lam_init)
    g = g_ref[...]
    r = lax.broadcasted_iota(jnp.int32, (tq, tq), 0)
    c = lax.broadcasted_iota(jnp.int32, (tq, tq), 1)
    diag_visible = (c // CHUNK) <= (r // CHUNK)
    lane = lax.broadcasted_iota(jnp.int32, (tq, V_DIM), 1)

    for i in range(seq // tq):
        lo = i * tq
        q = q_ref[lo:lo + tq, :]
        zero = jnp.zeros_like(q)
        qc = (jnp.where(lane < HEAD_DIM, q, zero), jnp.where(lane >= HEAD_DIM, q, zero))
        sd = [jnp.where(diag_visible, _dot(qc[k], kt_ref[:, lo:lo + tq]), NEG_INF) for k in range(2)]
        m = [jnp.max(s, axis=1, keepdims=True) for s in sd]
        if i > 0:
            sp = [_dot(qc[k], kt_ref[:, 0:lo]) for k in range(2)]
            m = [jnp.maximum(m[k], jnp.max(sp[k], axis=1, keepdims=True)) for k in range(2)]
        pd = [jnp.exp(sd[k] - m[k]) for k in range(2)]
        l = [jnp.sum(p, axis=1, keepdims=True) for p in pd]
        if i > 0:
            pp = [jnp.exp(sp[k] - m[k]) for k in range(2)]
            l = [l[k] + jnp.sum(pp[k], axis=1, keepdims=True) for k in range(2)]
        r0 = 1.0 / l[0]
        r1 = lam / l[1]
        o = _dot((pd[0] * r0 - pd[1] * r1).astype(BF16), v_ref[lo:lo + tq, :])
        if i > 0:
            o = o + _dot((pp[0] * r0 - pp[1] * r1).astype(BF16), v_ref[0:lo, :])
        o_ref[lo:lo + tq, :] = _head_norm(o, g, lam_init).astype(o_ref.dtype)


def _attn_prompt(lam_vecs, norm_g, q, ktb, vb, *, n_streams, seq, tq, lam_init):
    return pl.pallas_call(
        functools.partial(_attn_prompt_kernel, seq=seq, tq=tq, lam_init=lam_init),
        out_shape=jax.ShapeDtypeStruct((n_streams * seq, ATTN_V_WIDTH), BF16),
        grid=(n_streams, N_HEADS),
        in_specs=[
            pl.BlockSpec((4, HEAD_DIM), lambda b, h: (0, 0)),
            pl.BlockSpec((1, V_DIM), lambda b, h: (0, 0)),
            pl.BlockSpec((seq, V_DIM), lambda b, h: (b, h)),
            pl.BlockSpec((None, V_DIM, seq), lambda b, h: (b, h, 0)),
            pl.BlockSpec((seq, V_DIM), lambda b, h: (b, h)),
        ],
        out_specs=pl.BlockSpec((seq, V_DIM), lambda b, h: (b, h)),
        compiler_params=_params(("arbitrary", "arbitrary")),
        name="attn_prompt",
    )(lam_vecs, norm_g, q, ktb, vb)


def _attn_sample_kernel(lam_ref, g_ref, q_ref, kc_ref, vc_ref, kn_ref, vn_ref, o_ref,
                        s_scr, w_scr, wn_scr, m_scr, acc_scr, qbd_scr, *, nk, tn, past, lam_init):
    j = pl.program_id(1)
    half = N_HEADS * tn

    @pl.when(j == 0)
    def _():
        q = q_ref[...]
        qt = jnp.concatenate([q] * (2 * N_HEADS), axis=0)
        r = lax.broadcasted_iota(jnp.int32, qt.shape, 0)
        l = lax.broadcasted_iota(jnp.int32, qt.shape, 1)
        keep = ((r // half) == ((l % V_DIM) // HEAD_DIM)) & (((r % half) // tn) == (l // V_DIM))
        qbd_scr[...] = jnp.where(keep, qt, jnp.zeros_like(qt))
        m_scr[...] = jnp.full(m_scr.shape, NEG_INF, F32)
        acc_scr[...] = jnp.zeros(acc_scr.shape, F32)

    @pl.when(j < nk)
    def _():
        s = _dot(qbd_scr[...], kc_ref[...].astype(BF16))
        s_scr[j] = s
        m_scr[...] = jnp.maximum(m_scr[...], jnp.max(s, axis=1, keepdims=True))

    @pl.when(j == nk - 1)
    def _():
        lam = _lambda_value(lam_ref, lam_init)
        sn = _dot_nt(qbd_scr[...], kn_ref[...].astype(BF16))
        qpos = past + (lax.broadcasted_iota(jnp.int32, sn.shape, 0) % tn)
        kpos = past + lax.broadcasted_iota(jnp.int32, sn.shape, 1)
        sn = jnp.where((kpos // CHUNK) <= (qpos // CHUNK), sn, NEG_INF)
        m = jnp.maximum(m_scr[...], jnp.max(sn, axis=1, keepdims=True))
        pn = jnp.exp(sn - m)
        l = jnp.sum(pn, axis=1, keepdims=True)
        for c in range(nk):
            p = jnp.exp(s_scr[c] - m)
            s_scr[c] = p
            l = l + jnp.sum(p, axis=1, keepdims=True)
        r0 = 1.0 / l[:half]
        r1 = lam / l[half:]
        wn_scr[...] = pn[:half] * r0 - pn[half:] * r1
        for c in range(nk):
            p = s_scr[c]
            w_scr[c] = (p[:half] * r0 - p[half:] * r1).astype(BF16)

    def v_rows(ref):
        return jnp.concatenate([ref[:, h, :] for h in range(N_HEADS)], axis=1).astype(BF16)

    @pl.when(j >= nk)
    def _():
        acc_scr[...] += _dot(w_scr[j - nk], v_rows(vc_ref))

    @pl.when(j == 2 * nk - 1)
    def _():
        acc = acc_scr[...] + _dot(wn_scr[...].astype(BF16), vn_ref[...].astype(BF16))
        g = g_ref[...]
        for h in range(N_HEADS):
            o = acc[h * tn:(h + 1) * tn, h * V_DIM:(h + 1) * V_DIM]
            o_ref[:, h * V_DIM:(h + 1) * V_DIM] = _head_norm(o, g, lam_init).astype(o_ref.dtype)


def _attn_sample(lam_vecs, norm_g, q, kct, vc, kn, vn, *, n_streams, tn, past, tk, lam_init):
    nk = past // tk
    rows = 2 * N_HEADS * tn
    half = N_HEADS * tn
    return pl.pallas_call(
        functools.partial(_attn_sample_kernel, nk=nk, tn=tn, past=past, lam_init=lam_init),
        out_shape=jax.ShapeDtypeStruct((n_streams * tn, ATTN_V_WIDTH), BF16),
        grid=(n_streams, 2 * nk),
        in_specs=[
            pl.BlockSpec((4, HEAD_DIM), lambda b, j: (0, 0)),
            pl.BlockSpec((1, V_DIM), lambda b, j: (0, 0)),
            pl.BlockSpec((tn, QK_DIM), lambda b, j: (b, 0)),
            pl.BlockSpec((None, QK_DIM, tk), lambda b, j: (b, 0, jnp.minimum(j, nk - 1))),
            pl.BlockSpec((None, tk, N_HEADS, V_DIM), lambda b, j: (b, jnp.maximum(j - nk, 0), 0, 0)),
            pl.BlockSpec((tn, QK_DIM), lambda b, j: (b, 0)),
            pl.BlockSpec((tn, ATTN_V_WIDTH), lambda b, j: (b, 0)),
        ],
        out_specs=pl.BlockSpec((tn, ATTN_V_WIDTH), lambda b, j: (b, 0)),
        scratch_shapes=[
            pltpu.VMEM((nk, rows, tk), F32),
            pltpu.VMEM((nk, half, tk), BF16),
            pltpu.VMEM((half, tn), F32),
            pltpu.VMEM((rows, 1), F32),
            pltpu.VMEM((half, ATTN_V_WIDTH), F32),
            pltpu.VMEM((rows, QK_DIM), BF16),
        ],
        compiler_params=_params(("arbitrary", "arbitrary")),
        name="attn_sample",
    )(lam_vecs, norm_g, q, kct, vc, kn, vn)


def _layer_norm(z, g, b):
    mu = jnp.mean(z, axis=-1, keepdims=True)
    zc = z - mu
    var = jnp.mean(zc * zc, axis=-1, keepdims=True)
    return zc * lax.rsqrt(var + LN_EPS) * g + b


def _postmix_kernel(x_ref, py_ref, ay_ref, wg_ref, wpo_ref, wao_ref, wout_ref, g1_ref, b1_ref,
                    wr_ref, br_ref, x1_ref, gate_ref, lpos_ref, cnt_ref, *, tm, rt, dn_alpha):
    x = x_ref[...]
    xb = x.astype(BF16)
    gates = jax.nn.sigmoid(_dot(xb, wg_ref[...]))
    a = _dot(py_ref[...], wpo_ref[...])
    b = _dot(ay_ref[...], wao_ref[...])
    mixed = gates[:, :D_MODEL] * a + gates[:, D_MODEL:] * b
    mo = _dot(mixed.astype(BF16), wout_ref[...])
    x1 = _layer_norm(dn_alpha * x + mo, g1_ref[...], b1_ref[...])
    x1_ref[...] = x1

    logits = _dot(x1.astype(BF16), wr_ref[...]) + br_ref[...]
    for sub in range(tm // rt):
        rows = slice(sub * rt, (sub + 1) * rt)
        gate_ref[rows, :], lpos_ref[rows, :], cnt_ref[sub] = _route_tile(logits[rows], rt)


def _route_tile(logits, tm):
    lane = lax.broadcasted_iota(jnp.int32, logits.shape, 1)
    work = logits
    vals, idxs = [], []
    for _ in range(TOP_K):
        mx = jnp.max(work, axis=1, keepdims=True)
        ix = jnp.min(jnp.where(work == mx, lane, N_EXPERTS), axis=1, keepdims=True)
        vals.append(mx)
        idxs.append(ix)
        work = jnp.where(lane == ix, -jnp.inf, work)
    exps = [jnp.exp(v - vals[0]) for v in vals]
    denom = exps[0] + exps[1] + exps[2] + exps[3]

    onehot = jnp.zeros(logits.shape, F32)
    for ix in idxs:
        onehot = onehot + (lane == ix).astype(F32)
    r = lax.broadcasted_iota(jnp.int32, (tm, tm), 0)
    c = lax.broadcasted_iota(jnp.int32, (tm, tm), 1)
    tri = jnp.where(c < r, 1.0, 0.0).astype(BF16)
    earlier = _dot(tri, onehot.astype(BF16))
    cnt = jnp.sum(onehot, axis=0, keepdims=True)
    units = jnp.floor((cnt + (RUN_ALIGN - 1.0)) * (1.0 / RUN_ALIGN))
    er = lax.broadcasted_iota(jnp.int32, (N_EXPERTS, N_EXPERTS), 0)
    ec = lax.broadcasted_iota(jnp.int32, (N_EXPERTS, N_EXPERTS), 1)
    upper = jnp.where(er < ec, 1.0, 0.0).astype(BF16)
    run_off = _dot(jnp.broadcast_to(units, (8, N_EXPERTS)).astype(BF16), upper)[0:1] * float(RUN_ALIGN)
    pos = earlier + run_off

    lane_out = lax.broadcasted_iota(jnp.int32, (tm, LANES), 1)
    lpos_out = jnp.zeros((tm, LANES), jnp.int32)
    gate_out = jnp.zeros((tm, LANES), F32)
    for k in range(TOP_K):
        lpos_k = jnp.sum(jnp.where(lane == idxs[k], pos, 0.0), axis=1, keepdims=True).astype(jnp.int32)
        lpos_out = jnp.where(lane_out == k, lpos_k, lpos_out)
        gate_out = jnp.where(lane_out == k, exps[k] / denom, gate_out)
    return gate_out, lpos_out, units * float(RUN_ALIGN)


def _postmix(x2d, py, ay, wg, wpo, wao, wout, g1, b1, wr, br, *, tm, rt, dn_alpha):
    n = x2d.shape[0]
    row = lambda i: (i, 0)
    const = lambda i: (0, 0)
    out_shape = (
        jax.ShapeDtypeStruct((n, D_MODEL), F32),
        jax.ShapeDtypeStruct((n, LANES), F32),
        jax.ShapeDtypeStruct((n, LANES), jnp.int32),
        jax.ShapeDtypeStruct((n // rt, 1, N_EXPERTS), F32),
    )
    return pl.pallas_call(
        functools.partial(_postmix_kernel, tm=tm, rt=rt, dn_alpha=dn_alpha),
        out_shape=out_shape,
        grid=(n // tm,),
        in_specs=[
            pl.BlockSpec((tm, D_MODEL), row),
            pl.BlockSpec((tm, POOL_DIM), row),
            pl.BlockSpec((tm, ATTN_V_WIDTH), row),
            pl.BlockSpec(wg.shape, const),
            pl.BlockSpec(wpo.shape, const),
            pl.BlockSpec(wao.shape, const),
            pl.BlockSpec(wout.shape, const),
            pl.BlockSpec((1, D_MODEL), const),
            pl.BlockSpec((1, D_MODEL), const),
            pl.BlockSpec(wr.shape, const),
            pl.BlockSpec((1, N_EXPERTS), const),
        ],
        out_specs=(
            pl.BlockSpec((tm, D_MODEL), row),
            pl.BlockSpec((tm, LANES), row),
            pl.BlockSpec((tm, LANES), row),
            pl.BlockSpec((tm // rt, 1, N_EXPERTS), lambda i: (i, 0, 0)),
        ),
        compiler_params=_params(("arbitrary",)),
        name="postmix",
    )(x2d, py, ay, wg, wpo, wao, wout, g1, b1, wr, br)


def _local_rows(tm):
    return TOP_K * tm + N_EXPERTS * RUN_ALIGN


def _for_each_run_chunk(tile, tab_refs, local_buf, sorted_ref, sem, to_sorted, fn):
    gs_ref, off_ref, cnt_ref = tab_refs

    def per_expert(e, carry):
        t = tile * N_EXPERTS + e
        cnt, off, gs = cnt_ref[t], off_ref[t], gs_ref[t]
        for b in RUN_BITS:
            size = 1 << b

            @pl.when((cnt & size) != 0)
            def _():
                lower = cnt & (size - 1)
                loc = local_buf.at[pl.ds(pl.multiple_of(off + lower, RUN_ALIGN), size)]
                srt = sorted_ref.at[pl.ds(pl.multiple_of(gs + lower, RUN_ALIGN), size)]
                fn(pltpu.make_async_copy(loc, srt, sem) if to_sorted else pltpu.make_async_copy(srt, loc, sem))
        return carry

    lax.fori_loop(0, N_EXPERTS, per_expert, 0)


def _dispatch_tile(tab_refs, lpos_ref, x_ref, xs_ref, xloc, sem, tm):
    lpos = lpos_ref[...]
    col = lax.broadcasted_iota(jnp.int32, (tm, _local_rows(tm)), 1)
    hit = col == lpos[:, 0:1]
    for k in range(1, TOP_K):
        hit = jnp.logical_or(hit, col == lpos[:, k:k + 1])
    perm_t = jnp.where(hit, 1.0, 0.0).astype(BF16)
    xloc[...] = lax.dot_general(perm_t, x_ref[...].astype(BF16), (((0,), (0,)), ((), ())),
                                preferred_element_type=F32)
    tile = pl.program_id(0)
    _for_each_run_chunk(tile, tab_refs, xloc, xs_ref, sem, True, lambda cp: cp.start())
    _for_each_run_chunk(tile, tab_refs, xloc, xs_ref, sem, True, lambda cp: cp.wait())


def _dispatch_first_kernel(tail_ref, nu_ref, gs_ref, off_ref, cnt_ref, lpos_ref, x_ref, xs_ref,
                           xloc, zbuf, sem, zsem, *, tm, n_blocks):
    @pl.when(pl.program_id(0) == 0)
    def _():
        zbuf[...] = jnp.zeros(zbuf.shape, zbuf.dtype)

        def zero_copy(row):
            row = pl.multiple_of(row, EXPERT_BLOCK)
            return pltpu.make_async_copy(zbuf, xs_ref.at[pl.ds(row, EXPERT_BLOCK)], zsem)

        def over_blocks(fn):
            for e in range(N_EXPERTS):
                @pl.when(tail_ref[e] >= 0)
                def _():
                    fn(zero_copy(tail_ref[e]))
            lax.fori_loop(nu_ref[0], n_blocks, lambda b, c: (fn(zero_copy(b * EXPERT_BLOCK)), c)[1], 0)

        over_blocks(lambda cp: cp.start())
        over_blocks(lambda cp: cp.wait())

    _dispatch_tile((gs_ref, off_ref, cnt_ref), lpos_ref, x_ref, xs_ref, xloc, sem, tm)


def _dispatch_next_kernel(gs_ref, off_ref, cnt_ref, lpos_ref, x_ref, xs_in_ref, xs_ref, xloc, sem, *, tm):
    del xs_in_ref
    _dispatch_tile((gs_ref, off_ref, cnt_ref), lpos_ref, x_ref, xs_ref, xloc, sem, tm)


def _dispatch_first(tail, n_used, tabs, lpos, x1, *, tm, n_blocks):
    n = x1.shape[0]
    grid_spec = pltpu.PrefetchScalarGridSpec(
        num_scalar_prefetch=5,
        grid=(n // tm,),
        in_specs=[pl.BlockSpec((tm, LANES), lambda i, *_: (i, 0)),
                  pl.BlockSpec((tm, D_MODEL), lambda i, *_: (i, 0))],
        out_specs=pl.BlockSpec(memory_space=pl.ANY),
        scratch_shapes=[pltpu.VMEM((_local_rows(tm), D_MODEL), F32),
                        pltpu.VMEM((EXPERT_BLOCK, D_MODEL), F32),
                        pltpu.SemaphoreType.DMA, pltpu.SemaphoreType.DMA],
    )
    return pl.pallas_call(
        functools.partial(_dispatch_first_kernel, tm=tm, n_blocks=n_blocks),
        out_shape=jax.ShapeDtypeStruct((n_blocks * EXPERT_BLOCK, D_MODEL), F32),
        grid_spec=grid_spec,
        compiler_params=_params(("arbitrary",)),
        name="dispatch_first",
    )(tail, n_used, *tabs, lpos, x1)


def _dispatch_next(tabs, lpos, x1, xs, *, tm):
    n = x1.shape[0]
    grid_spec = pltpu.PrefetchScalarGridSpec(
        num_scalar_prefetch=3,
        grid=(n // tm,),
        in_specs=[pl.BlockSpec((tm, LANES), lambda i, *_: (i, 0)),
                  pl.BlockSpec((tm, D_MODEL), lambda i, *_: (i, 0)),
                  pl.BlockSpec(memory_space=pl.ANY)],
        out_specs=pl.BlockSpec(memory_space=pl.ANY),
        scratch_shapes=[pltpu.VMEM((_local_rows(tm), D_MODEL), F32), pltpu.SemaphoreType.DMA],
    )
    return pl.pallas_call(
        functools.partial(_dispatch_next_kernel, tm=tm),
        out_shape=jax.ShapeDtypeStruct(xs.shape, xs.dtype),
        grid_spec=grid_spec,
        input_output_aliases={5: 0},
        compiler_params=_params(("arbitrary",)),
        name="dispatch_next",
    )(*tabs, lpos, x1, xs)


def _experts_kernel(be_ref, nu_ref, xs_ref, win_ref, bg_ref, bl_ref, wo_ref, bo_ref, y_ref,
                    wint_scr, wot_scr, bin_scr, bo_scr):
    i = pl.program_id(0)
    used = i < nu_ref[0]
    first_of_expert = jnp.logical_or(i == 0, be_ref[i] != be_ref[jnp.maximum(i - 1, 0)])

    @pl.when(jnp.logical_not(used))
    def _():
        y_ref[...] = jnp.zeros(y_ref.shape, y_ref.dtype)

    @pl.when(jnp.logical_and(used, first_of_expert))
    def _():
        r = lax.broadcasted_iota(jnp.int32, (MXU_DIM, MXU_DIM), 0)
        c = lax.broadcasted_iota(jnp.int32, (MXU_DIM, MXU_DIM), 1)
        src = jnp.where(r < LANES, 2 * r, 2 * (r - LANES) + 1)
        pick = jnp.where(c == src, 1.0, 0.0).astype(BF16)
        ident = jnp.where(c == r, 1.0, 0.0).astype(BF16)
        for gq in range(2 * D_EXPERT // MXU_DIM):
            blk = win_ref[:, gq * MXU_DIM:(gq + 1) * MXU_DIM].astype(BF16)
            d = _dot_nt(pick, blk).astype(BF16)
            wint_scr[gq * LANES:(gq + 1) * LANES, :] = d[:LANES]
            wint_scr[D_EXPERT + gq * LANES:D_EXPERT + (gq + 1) * LANES, :] = d[LANES:]
        for gq in range(D_MODEL // MXU_DIM):
            blk = wo_ref[:, gq * MXU_DIM:(gq + 1) * MXU_DIM].astype(BF16)
            wot_scr[gq * MXU_DIM:(gq + 1) * MXU_DIM, :] = _dot_nt(ident, blk).astype(BF16)
        bin_scr[0:D_EXPERT, :] = jnp.broadcast_to(bg_ref[...], (EXPERT_BLOCK, D_EXPERT)).T
        bin_scr[D_EXPERT:, :] = jnp.broadcast_to(bl_ref[...], (EXPERT_BLOCK, D_EXPERT)).T
        bo_scr[...] = jnp.broadcast_to(bo_ref[...], (EXPERT_BLOCK, D_MODEL)).T

    @pl.when(used)
    def _():
        xb = xs_ref[...].astype(BF16)
        h = _dot_nt(wint_scr[...], xb) + bin_scr[...]
        glu = jnp.minimum(h[:D_EXPERT], SWIGLU_LIMIT)
        lin = jnp.clip(h[D_EXPERT:], -SWIGLU_LIMIT, SWIGLU_LIMIT)
        act = glu * jax.nn.sigmoid(SWIGLU_ALPHA * glu) * (lin + 1.0)
        yt = _dot(wot_scr[...], act.astype(BF16)) + bo_scr[...]
        y_ref[...] = yt.T


def _experts(blk_expert, n_used, xs, w_in, b_glu, b_lin, w_out, b_out):
    rows = xs.shape[0]
    n_blocks = rows // EXPERT_BLOCK
    wsel = lambda i, be, nu: (be[i], 0, 0)
    grid_spec = pltpu.PrefetchScalarGridSpec(
        num_scalar_prefetch=2,
        grid=(n_blocks,),
        in_specs=[
            pl.BlockSpec((EXPERT_BLOCK, D_MODEL), lambda i, be, nu: (jnp.minimum(i, nu[0] - 1), 0)),
            pl.BlockSpec((None, D_MODEL, 2 * D_EXPERT), wsel),
            pl.BlockSpec((None, 1, D_EXPERT), wsel),
            pl.BlockSpec((None, 1, D_EXPERT), wsel),
            pl.BlockSpec((None, D_EXPERT, D_MODEL), wsel),
            pl.BlockSpec((None, 1, D_MODEL), wsel),
        ],
        out_specs=pl.BlockSpec((EXPERT_BLOCK, D_MODEL), lambda i, be, nu: (i, 0)),
        scratch_shapes=[pltpu.VMEM((2 * D_EXPERT, D_MODEL), BF16),
                        pltpu.VMEM((D_MODEL, D_EXPERT), BF16),
                        pltpu.VMEM((2 * D_EXPERT, EXPERT_BLOCK), F32),
                        pltpu.VMEM((D_MODEL, EXPERT_BLOCK), F32)],
    )
    return pl.pallas_call(
        _experts_kernel,
        out_shape=jax.ShapeDtypeStruct((rows, D_MODEL), F32),
        grid_spec=grid_spec,
        compiler_params=_params(("arbitrary",)),
        name="experts",
    )(blk_expert, n_used, xs, w_in, b_glu, b_lin, w_out, b_out)


def _split_bf16(a):
    hi = a.astype(BF16)
    return hi, (a - hi.astype(F32)).astype(BF16)


def _combine_kernel(gs_ref, off_ref, cnt_ref, gate_ref, lpos_ref, x1_ref, g2_ref, b2_ref, yb_ref, o_ref,
                    yloc, sem, *, tm, dn_alpha):
    tile = pl.program_id(0)

    @pl.when(tile == 0)
    def _():
        yloc[...] = jnp.zeros(yloc.shape, yloc.dtype)

    tabs = (gs_ref, off_ref, cnt_ref)
    _for_each_run_chunk(tile, tabs, yloc, yb_ref, sem, False, lambda cp: cp.start())
    _for_each_run_chunk(tile, tabs, yloc, yb_ref, sem, False, lambda cp: cp.wait())

    gate, lpos = gate_ref[...], lpos_ref[...]
    col = lax.broadcasted_iota(jnp.int32, (tm, _local_rows(tm)), 1)
    weights = jnp.zeros(col.shape, F32)
    for k in range(TOP_K):
        weights = jnp.where(col == lpos[:, k:k + 1], gate[:, k:k + 1], weights)
    w_hi, w_lo = _split_bf16(weights)
    y_hi, y_lo = _split_bf16(yloc[...])
    y = _dot(w_hi, y_hi) + (_dot(w_hi, y_lo) + _dot(w_lo, y_hi))
    o_ref[...] = _layer_norm(dn_alpha * x1_ref[...] + y, g2_ref[...], b2_ref[...])


def _combine(tabs, gate, lpos, x1, g2, b2, yb, *, tm, dn_alpha):
    n = x1.shape[0]
    grid_spec = pltpu.PrefetchScalarGridSpec(
        num_scalar_prefetch=3,
        grid=(n // tm,),
        in_specs=[
            pl.BlockSpec((tm, LANES), lambda i, *_: (i, 0)),
            pl.BlockSpec((tm, LANES), lambda i, *_: (i, 0)),
            pl.BlockSpec((tm, D_MODEL), lambda i, *_: (i, 0)),
            pl.BlockSpec((1, D_MODEL), lambda i, *_: (0, 0)),
            pl.BlockSpec((1, D_MODEL), lambda i, *_: (0, 0)),
            pl.BlockSpec(memory_space=pl.ANY),
        ],
        out_specs=pl.BlockSpec((tm, D_MODEL), lambda i, *_: (i, 0)),
        scratch_shapes=[pltpu.VMEM((_local_rows(tm), D_MODEL), F32), pltpu.SemaphoreType.DMA],
    )
    return pl.pallas_call(
        functools.partial(_combine_kernel, tm=tm, dn_alpha=dn_alpha),
        out_shape=jax.ShapeDtypeStruct((n, D_MODEL), F32),
        grid_spec=grid_spec,
        compiler_params=_params(("arbitrary",)),
        name="combine",
    )(*tabs, gate, lpos, x1, g2, b2, yb)


def _position_tables(pos0, seq):
    pos = pos0 + jnp.arange(seq, dtype=jnp.int32)
    inv = ROPE_THETA ** (-jnp.arange(HALF_DIM, dtype=F32) / HALF_DIM)
    ang = pos.astype(F32)[:, None] * inv[None, :]
    cos, sin = jnp.cos(ang), jnp.sin(ang)
    cos_rows = jnp.concatenate([cos, cos, cos, cos], axis=-1)
    sin_rows = jnp.concatenate([-sin, sin, -sin, sin], axis=-1)
    icnt = jnp.concatenate(
        [jnp.broadcast_to((1.0 / jnp.minimum(pos + 1, w).astype(F32))[:, None], (seq, POOL_GROUP_DIM))
         for w in POOL_WINDOWS], axis=-1)
    return cos_rows, sin_rows, cos.T, sin.T, icnt


def _tile(n, pref):
    t = min(n, pref)
    while n % t:
        t //= 2
    return t


def kernel(x_prompt, x_sample, cache_k, cache_v, state_pool, w_in, w_pool_mix, pool_scale, w_pool_out,
           lambda_q1, lambda_k1, lambda_q2, lambda_k2, attn_norm_g, w_attn_out, w_out, ln1_g, ln1_b,
           w_router, b_router, w_expert_in, b_expert_in, w_expert_out, b_expert_out, ln2_g, ln2_b):
    depth = w_in.shape[0]
    assert depth == 1, "single-layer step"
    dn_alpha = (2.0 * depth) ** 0.25
    lam_init = 0.8 - 0.6 * math.exp(-0.3 * 0)
    bp, sp, _ = x_prompt.shape
    bs, ss, _ = x_sample.shape
    past = cache_k.shape[2]
    np_, ns = bp * sp, bs * ss

    c_q, c_k, c_v = POOL_DIM, POOL_DIM + QK_DIM, POOL_DIM + 2 * QK_DIM
    c_gate = c_v + ATTN_V_WIDTH
    w0 = w_in[0]
    w_pqv = jnp.concatenate([w0[:, :c_k], w0[:, c_v:c_gate]], axis=1).astype(BF16)
    w_k = w0[:, c_k:c_v].astype(BF16)
    w_gate = w0[:, c_gate:].astype(BF16)
    wmix = w_pool_mix[0].astype(BF16)
    pscale = pool_scale[0].reshape(1, POOL_DIM)
    wpo = w_pool_out[0].astype(BF16)
    wao = w_attn_out[0].astype(BF16)
    wout = w_out[0].astype(BF16)
    wr = w_router[0].astype(BF16)
    br = b_router[0].reshape(1, N_EXPERTS)
    lam_vecs = jnp.stack([lambda_q1[0], lambda_k1[0], lambda_q2[0], lambda_k2[0]])
    norm_g = attn_norm_g[0].reshape(1, V_DIM)
    g1, b1 = ln1_g[0].reshape(1, D_MODEL), ln1_b[0].reshape(1, D_MODEL)
    g2, b2 = ln2_g[0].reshape(1, D_MODEL), ln2_b[0].reshape(1, D_MODEL)
    b_glu = b_expert_in[0][:, 0::2].reshape(N_EXPERTS, 1, D_EXPERT)
    b_lin = b_expert_in[0][:, 1::2].reshape(N_EXPERTS, 1, D_EXPERT)
    b_eo = b_expert_out[0].reshape(N_EXPERTS, 1, D_MODEL)

    xp = x_prompt.reshape(np_, D_MODEL)
    cos_p, sin_p, cost_p, sint_p, icnt_p = _position_tables(0, sp)
    hist_p = jnp.zeros((bp, HIST_ROWS, POOL_DIM), F32)
    q_p, kt_p, ktb_p, v_p, vb_p, py_p, pnew_p = _inproj_prompt(
        xp, w_pqv, w_k.T, cos_p, sin_p, cost_p, sint_p, icnt_p, hist_p, wmix, pscale,
        n_streams=bp, seq=sp, tm=_tile(sp, DENSE_TILE))
    ay_p = _attn_prompt(lam_vecs, norm_g, q_p, ktb_p, vb_p, n_streams=bp, seq=sp,
                        tq=_tile(sp, 256), lam_init=lam_init)
    tm_p, tm_s = _tile(np_, ROUTE_TILE), _tile(ns, ROUTE_TILE)
    x1_p, gate_p, lpos_p, cnt_p = _postmix(
        xp, py_p, ay_p, w_gate, wpo, wao, wout, g1, b1, wr, br,
        tm=max(tm_p, _tile(np_, DENSE_TILE)), rt=tm_p, dn_alpha=dn_alpha)

    xs_ = x_sample.reshape(ns, D_MODEL)
    cos_s, sin_s, _, _, icnt_s = _position_tables(past, ss)
    hist_s = jnp.concatenate([jnp.zeros((bs, 1, POOL_DIM), F32), state_pool[0]], axis=1)
    q_s, k_s, v_s, py_s, pnew_s = _inproj_sample(
        xs_, w_pqv, w_k, cos_s, sin_s, icnt_s, hist_s, wmix, pscale, n_streams=bs, seq=ss)
    kct = jnp.transpose(cache_k[0], (0, 2, 3, 4, 1)).reshape(bs, QK_DIM, past)
    ay_s = _attn_sample(lam_vecs, norm_g, q_s, kct, cache_v[0], k_s, v_s, n_streams=bs, tn=ss, past=past,
                        tk=_tile(past, 512), lam_init=lam_init)
    x1_s, gate_s, lpos_s, cnt_s = _postmix(
        xs_, py_s, ay_s, w_gate, wpo, wao, wout, g1, b1, wr, br, tm=tm_s, rt=tm_s, dn_alpha=dn_alpha)

    ntp = np_ // tm_p
    cnt = jnp.concatenate([cnt_p[:, 0, :], cnt_s[:, 0, :]], axis=0).astype(jnp.int32)
    n_tiles = cnt.shape[0]
    group = jnp.sum(cnt, axis=0)
    padded = (group + EXPERT_BLOCK - 1) // EXPERT_BLOCK * EXPERT_BLOCK
    pad_end = jnp.cumsum(padded).astype(jnp.int32)
    run_start = (pad_end - padded)[None, :] + jnp.cumsum(cnt, axis=0) - cnt
    run_off = jnp.cumsum(cnt, axis=1) - cnt
    tail = jnp.where(padded > 0, pad_end - EXPERT_BLOCK, -1).astype(jnp.int32)
    max_rows = (np_ + ns) * TOP_K + n_tiles * N_EXPERTS * (RUN_ALIGN - 1) + N_EXPERTS * (EXPERT_BLOCK - 1)
    n_blocks = -(-max_rows // EXPERT_BLOCK)
    n_used = pad_end[-1:] // EXPERT_BLOCK
    blk_start = jnp.arange(n_blocks, dtype=jnp.int32) * EXPERT_BLOCK
    blk_expert = jnp.minimum(jnp.sum((blk_start[:, None] >= pad_end[None, :]).astype(jnp.int32), axis=1),
                             N_EXPERTS - 1)
    tabs_p = tuple(a[:ntp].reshape(-1).astype(jnp.int32) for a in (run_start, run_off, cnt))
    tabs_s = tuple(a[ntp:].reshape(-1).astype(jnp.int32) for a in (run_start, run_off, cnt))

    xsorted = _dispatch_first(tail, n_used, tabs_p, lpos_p, x1_p, tm=tm_p, n_blocks=n_blocks)
    xsorted = _dispatch_next(tabs_s, lpos_s, x1_s, xsorted, tm=tm_s)
    yb = _experts(blk_expert, n_used, xsorted, w_expert_in[0], b_glu, b_lin, w_expert_out[0], b_eo)
    y_p = _combine(tabs_p, gate_p, lpos_p, x1_p, g2, b2, yb, tm=tm_p, dn_alpha=dn_alpha)
    y_s = _combine(tabs_s, gate_s, lpos_s, x1_s, g2, b2, yb, tm=tm_s, dn_alpha=dn_alpha)

    k_prompt = jnp.transpose(kt_p.reshape(bp, N_HEADS, 2, HEAD_DIM, sp), (0, 4, 1, 2, 3))
    return (
        y_p.reshape(bp, sp, D_MODEL),
        y_s.reshape(bs, ss, D_MODEL),
        k_prompt[None],
        v_p[None],
        pnew_p[:, 1:].reshape(1, bp, POOL_HIST, POOL_DIM),
        k_s.reshape(1, bs, ss, N_HEADS, 2, HEAD_DIM),
        v_s.reshape(1, bs, ss, N_HEADS, V_DIM),
        pnew_s[:, 1:].reshape(1, bs, POOL_HIST, POOL_DIM),
    )
```

```python
import functools
import math

import jax
import jax.numpy as jnp
from jax import lax
from jax.experimental import pallas as pl
from jax.experimental.pallas import tpu as pltpu

D_MODEL = 1024
CHUNK = 64
POOL_WINDOWS = (2, 4, 8, 16)
POOL_GROUP_DIM = 128
POOL_DIM = len(POOL_WINDOWS) * POOL_GROUP_DIM
POOL_HIST = max(POOL_WINDOWS) - 1
HIST_ROWS = POOL_HIST + 1
N_HEADS = 8
HEAD_DIM = 64
HALF_DIM = HEAD_DIM // 2
V_DIM = 2 * HEAD_DIM
QK_DIM = N_HEADS * 2 * HEAD_DIM
ATTN_V_WIDTH = N_HEADS * V_DIM
ATTN_SCALE = HEAD_DIM ** -0.5
ROPE_THETA = 10000.0
SUBLN_EPS = 1e-5
N_EXPERTS = 32
TOP_K = 4
D_EXPERT = 1024
SWIGLU_LIMIT = 7.0
SWIGLU_ALPHA = 1.702
LN_EPS = 1e-5
NEG_INF = -1e30
LANES = 128
MXU_DIM = 256

F32 = jnp.float32
BF16 = jnp.bfloat16

VMEM_LIMIT = 56 * 1024 * 1024
EXPERT_BLOCK = 256
RUN_ALIGN = 8
ROUTE_TILE = 256
DENSE_TILE = 512
RUN_BITS = tuple(range(3, 9))


def _dot(a, b):
    return jnp.dot(a, b, preferred_element_type=F32)


def _dot_nt(a, b):
    return lax.dot_general(a, b, (((1,), (1,)), ((), ())), preferred_element_type=F32)


def _params(semantics):
    return pltpu.CompilerParams(dimension_semantics=semantics, vmem_limit_bytes=VMEM_LIMIT)


def _pool_branch(x, w_ref, icnt_ref, hist_ref, wmix_ref, pscale_ref, py_ref, pnew_ref, ext_ref, *, bb, tm):
    u = _dot(x, w_ref[:, 0:POOL_DIM])

    @pl.when(pl.program_id(1) == 0)
    def _():
        ext_ref[:, 0:HIST_ROWS, :] = hist_ref[...]

    for b in range(bb):
        ext_ref[b, HIST_ROWS:HIST_ROWS + tm, :] = u[b * tm:(b + 1) * tm]
    for b in range(bb):
        for g, w in enumerate(POOL_WINDOWS):
            cols = slice(g * POOL_GROUP_DIM, (g + 1) * POOL_GROUP_DIM)
            cur = ext_ref[b, HIST_ROWS:HIST_ROWS + tm, cols]
            acc = cur
            for j in range(1, w):
                acc = acc + ext_ref[b, HIST_ROWS - j:HIST_ROWS - j + tm, cols]
            d = acc * icnt_ref[:, cols] - cur
            y = _dot(d.astype(BF16), wmix_ref[g]) * pscale_ref[:, cols]
            py_ref[b * tm:(b + 1) * tm, cols] = y.astype(BF16)
    tail = ext_ref[:, tm:tm + HIST_ROWS, :]
    pnew_ref[...] = tail
    ext_ref[:, 0:HIST_ROWS, :] = tail


def _rope_rows(z, cos, sin):
    lane = lax.broadcasted_iota(jnp.int32, z.shape, 1)
    first_half = (lane % HEAD_DIM) < HALF_DIM
    partner = jnp.where(first_half, pltpu.roll(z, LANES - HALF_DIM, 1), pltpu.roll(z, HALF_DIM, 1))
    return z * cos + partner * sin


def _inproj_prompt_kernel(x_ref, w_ref, wkt_ref, cos_ref, sin_ref, cost_ref, sint_ref, icnt_ref, hist_ref,
                          wmix_ref, pscale_ref, q_ref, kt_ref, ktb_ref, v_ref, vb_ref, py_ref, pnew_ref,
                          ext_ref, *, tm):
    x = x_ref[...].astype(BF16)
    _pool_branch(x, w_ref, icnt_ref, hist_ref, wmix_ref, pscale_ref, py_ref, pnew_ref, ext_ref, bb=1, tm=tm)

    cos, sin = cos_ref[...], sin_ref[...]
    hq = _dot(x, w_ref[:, POOL_DIM:POOL_DIM + QK_DIM])
    for h in range(N_HEADS):
        sl = slice(h * V_DIM, (h + 1) * V_DIM)
        q_ref[:, sl] = (_rope_rows(hq[:, sl], cos, sin) * ATTN_SCALE).astype(BF16)

    hkt = _dot_nt(wkt_ref[...], x)
    cost, sint = cost_ref[...], sint_ref[...]
    for hc in range(2 * N_HEADS):
        r0 = hc * HEAD_DIM
        x1 = hkt[r0:r0 + HALF_DIM]
        x2 = hkt[r0 + HALF_DIM:r0 + HEAD_DIM]
        o1 = x1 * cost - x2 * sint
        o2 = x2 * cost + x1 * sint
        kt_ref[r0:r0 + HALF_DIM, :] = o1
        kt_ref[r0 + HALF_DIM:r0 + HEAD_DIM, :] = o2
        ktb_ref[r0:r0 + HALF_DIM, :] = o1.astype(BF16)
        ktb_ref[r0 + HALF_DIM:r0 + HEAD_DIM, :] = o2.astype(BF16)

    hv = _dot(x, w_ref[:, POOL_DIM + QK_DIM:POOL_DIM + QK_DIM + ATTN_V_WIDTH])
    vb_ref[...] = hv.astype(BF16)
    for h in range(N_HEADS):
        v_ref[:, h, :] = hv[:, h * V_DIM:(h + 1) * V_DIM]


def _inproj_prompt(x2d, w_pqv, wkt, cos, sin, cost, sint, icnt, hist, wmix, pscale, *, n_streams, seq, tm):
    n = n_streams * seq
    nt = seq // tm
    row_map = lambda b, t: (b * nt + t, 0)
    const2 = lambda b, t: (0, 0)
    out_shape = (
        jax.ShapeDtypeStruct((n, QK_DIM), BF16),
        jax.ShapeDtypeStruct((n_streams, QK_DIM, seq), F32),
        jax.ShapeDtypeStruct((n_streams, QK_DIM, seq), BF16),
        jax.ShapeDtypeStruct((n_streams, seq, N_HEADS, V_DIM), F32),
        jax.ShapeDtypeStruct((n, ATTN_V_WIDTH), BF16),
        jax.ShapeDtypeStruct((n, POOL_DIM), BF16),
        jax.ShapeDtypeStruct((n_streams, HIST_ROWS, POOL_DIM), F32),
    )
    return pl.pallas_call(
        functools.partial(_inproj_prompt_kernel, tm=tm),
        out_shape=out_shape,
        grid=(n_streams, nt),
        in_specs=[
            pl.BlockSpec((tm, D_MODEL), row_map),
            pl.BlockSpec(w_pqv.shape, const2),
            pl.BlockSpec(wkt.shape, const2),
            pl.BlockSpec((tm, LANES), lambda b, t: (t, 0)),
            pl.BlockSpec((tm, LANES), lambda b, t: (t, 0)),
            pl.BlockSpec((HALF_DIM, tm), lambda b, t: (0, t)),
            pl.BlockSpec((HALF_DIM, tm), lambda b, t: (0, t)),
            pl.BlockSpec((tm, POOL_DIM), lambda b, t: (t, 0)),
            pl.BlockSpec((1, HIST_ROWS, POOL_DIM), lambda b, t: (b, 0, 0)),
            pl.BlockSpec((len(POOL_WINDOWS), POOL_GROUP_DIM, POOL_GROUP_DIM), lambda b, t: (0, 0, 0)),
            pl.BlockSpec((1, POOL_DIM), const2),
        ],
        out_specs=(
            pl.BlockSpec((tm, QK_DIM), row_map),
            pl.BlockSpec((None, QK_DIM, tm), lambda b, t: (b, 0, t)),
            pl.BlockSpec((None, QK_DIM, tm), lambda b, t: (b, 0, t)),
            pl.BlockSpec((None, tm, N_HEADS, V_DIM), lambda b, t: (b, t, 0, 0)),
            pl.BlockSpec((tm, ATTN_V_WIDTH), row_map),
            pl.BlockSpec((tm, POOL_DIM), row_map),
            pl.BlockSpec((1, HIST_ROWS, POOL_DIM), lambda b, t: (b, 0, 0)),
        ),
        scratch_shapes=[pltpu.VMEM((1, HIST_ROWS + tm, POOL_DIM), F32)],
        compiler_params=_params(("arbitrary", "arbitrary")),
        name="inproj_prompt",
    )(x2d, w_pqv, wkt, cos, sin, cost, sint, icnt, hist, wmix, pscale)


def _inproj_sample_kernel(x_ref, w_ref, wk_ref, cos_ref, sin_ref, icnt_ref, hist_ref, wmix_ref, pscale_ref,
                          q_ref, k_ref, v_ref, py_ref, pnew_ref, ext_ref, *, bb, tm):
    x = x_ref[...].astype(BF16)
    _pool_branch(x, w_ref, icnt_ref, hist_ref, wmix_ref, pscale_ref, py_ref, pnew_ref, ext_ref, bb=bb, tm=tm)
    cos = jnp.concatenate([cos_ref[...]] * bb, axis=0)
    sin = jnp.concatenate([sin_ref[...]] * bb, axis=0)
    hq = _dot(x, w_ref[:, POOL_DIM:POOL_DIM + QK_DIM])
    hk = _dot(x, wk_ref[...])
    for h in range(N_HEADS):
        sl = slice(h * V_DIM, (h + 1) * V_DIM)
        q_ref[:, sl] = (_rope_rows(hq[:, sl], cos, sin) * ATTN_SCALE).astype(BF16)
        k_ref[:, sl] = _rope_rows(hk[:, sl], cos, sin)
    v_ref[...] = _dot(x, w_ref[:, POOL_DIM + QK_DIM:POOL_DIM + QK_DIM + ATTN_V_WIDTH])


def _inproj_sample(x2d, w_pqv, wk, cos, sin, icnt, hist, wmix, pscale, *, n_streams, seq):
    n = n_streams * seq
    const2 = lambda i, t: (0, 0)
    const3 = lambda i, t: (0, 0, 0)
    out_shape = (
        jax.ShapeDtypeStruct((n, QK_DIM), BF16),
        jax.ShapeDtypeStruct((n, QK_DIM), F32),
        jax.ShapeDtypeStruct((n, ATTN_V_WIDTH), F32),
        jax.ShapeDtypeStruct((n, POOL_DIM), BF16),
        jax.ShapeDtypeStruct((n_streams, HIST_ROWS, POOL_DIM), F32),
    )
    return pl.pallas_call(
        functools.partial(_inproj_sample_kernel, bb=n_streams, tm=seq),
        out_shape=out_shape,
        grid=(1, 1),
        in_specs=[
            pl.BlockSpec((n, D_MODEL), const2),
            pl.BlockSpec(w_pqv.shape, const2),
            pl.BlockSpec(wk.shape, const2),
            pl.BlockSpec((seq, LANES), const2),
            pl.BlockSpec((seq, LANES), const2),
            pl.BlockSpec((seq, POOL_DIM), const2),
            pl.BlockSpec((n_streams, HIST_ROWS, POOL_DIM), const3),
            pl.BlockSpec((len(POOL_WINDOWS), POOL_GROUP_DIM, POOL_GROUP_DIM), const3),
            pl.BlockSpec((1, POOL_DIM), const2),
        ],
        out_specs=(
            pl.BlockSpec((n, QK_DIM), const2),
            pl.BlockSpec((n, QK_DIM), const2),
            pl.BlockSpec((n, ATTN_V_WIDTH), const2),
            pl.BlockSpec((n, POOL_DIM), const2),
            pl.BlockSpec((n_streams, HIST_ROWS, POOL_DIM), const3),
        ),
        scratch_shapes=[pltpu.VMEM((n_streams, HIST_ROWS + seq, POOL_DIM), F32)],
        compiler_params=_params(("arbitrary", "arbitrary")),
        name="inproj_sample",
    )(x2d, w_pqv, wk, cos, sin, icnt, hist, wmix, pscale)


def _lambda_value(lam_ref, lam_init):
    lv = lam_ref[...]
    s1 = jnp.sum(lv[0:1] * lv[1:2], axis=1, keepdims=True)
    s2 = jnp.sum(lv[2:3] * lv[3:4], axis=1, keepdims=True)
    return jnp.exp(s1) - jnp.exp(s2) + lam_init


def _head_norm(o, g, lam_init):
    ms = jnp.mean(o * o, axis=-1, keepdims=True)
    return o * lax.rsqrt(ms + SUBLN_EPS) * g * (1.0 - lam_init)


def _attn_prompt_kernel(lam_ref, g_ref, q_ref, kt_ref, v_ref, o_ref, *, seq, tq, lam_init):
    lam = _lambda_value(lam_ref, lam_init)
    g = g_ref[...]
    r = lax.broadcasted_iota(jnp.int32, (tq, tq), 0)
    c = lax.broadcasted_iota(jnp.int32, (tq, tq), 1)
    diag_visible = (c // CHUNK) <= (r // CHUNK)
    lane = lax.broadcasted_iota(jnp.int32, (tq, V_DIM), 1)

    for i in range(seq // tq):
        lo = i * tq
        q = q_ref[lo:lo + tq, :]
        zero = jnp.zeros_like(q)
        qc = (jnp.where(lane < HEAD_DIM, q, zero), jnp.where(lane >= HEAD_DIM, q, zero))
        sd = [jnp.where(diag_visible, _dot(qc[k], kt_ref[:, lo:lo + tq]), NEG_INF) for k in range(2)]
        m = [jnp.max(s, axis=1, keepdims=True) for s in sd]
        if i > 0:
            sp = [_dot(qc[k], kt_ref[:, 0:lo]) for k in range(2)]
            m = [jnp.maximum(m[k], jnp.max(sp[k], axis=1, keepdims=True)) for k in range(2)]
        pd = [jnp.exp(sd[k] - m[k]) for k in range(2)]
        l = [jnp.sum(p, axis=1, keepdims=True) for p in pd]
        if i > 0:
            pp = [jnp.exp(sp[k] - m[k]) for k in range(2)]
            l = [l[k] + jnp.sum(pp[k], axis=1, keepdims=True) for k in range(2)]
        r0 = 1.0 / l[0]
        r1 = lam / l[1]
        o = _dot((pd[0] * r0 - pd[1] * r1).astype(BF16), v_ref[lo:lo + tq, :])
        if i > 0:
            o = o + _dot((pp[0] * r0 - pp[1] * r1).astype(BF16), v_ref[0:lo, :])
        o_ref[lo:lo + tq, :] = _head_norm(o, g, lam_init).astype(o_ref.dtype)


def _attn_prompt(lam_vecs, norm_g, q, ktb, vb, *, n_streams, seq, tq, lam_init):
    return pl.pallas_call(
        functools.partial(_attn_prompt_kernel, seq=seq, tq=tq, lam_init=lam_init),
        out_shape=jax.ShapeDtypeStruct((n_streams * seq, ATTN_V_WIDTH), BF16),
        grid=(n_streams, N_HEADS),
        in_specs=[
            pl.BlockSpec((4, HEAD_DIM), lambda b, h: (0, 0)),
            pl.BlockSpec((1, V_DIM), lambda b, h: (0, 0)),
            pl.BlockSpec((seq, V_DIM), lambda b, h: (b, h)),
            pl.BlockSpec((None, V_DIM, seq), lambda b, h: (b, h, 0)),
            pl.BlockSpec((seq, V_DIM), lambda b, h: (b, h)),
        ],
        out_specs=pl.BlockSpec((seq, V_DIM), lambda b, h: (b, h)),
        compiler_params=_params(("arbitrary", "arbitrary")),
        name="attn_prompt",
    )(lam_vecs, norm_g, q, ktb, vb)


def _attn_sample_kernel(lam_ref, g_ref, q_ref, kc_ref, vc_ref, kn_ref, vn_ref, o_ref,
                        s_scr, w_scr, wn_scr, m_scr, acc_scr, qbd_scr, *, nk, tn, past, lam_init):
    j = pl.program_id(1)
    half = N_HEADS * tn

    @pl.when(j == 0)
    def _():
        q = q_ref[...]
        qt = jnp.concatenate([q] * (2 * N_HEADS), axis=0)
        r = lax.broadcasted_iota(jnp.int32, qt.shape, 0)
        l = lax.broadcasted_iota(jnp.int32, qt.shape, 1)
        keep = ((r // half) == ((l % V_DIM) // HEAD_DIM)) & (((r % half) // tn) == (l // V_DIM))
        qbd_scr[...] = jnp.where(keep, qt, jnp.zeros_like(qt))
        m_scr[...] = jnp.full(m_scr.shape, NEG_INF, F32)
        acc_scr[...] = jnp.zeros(acc_scr.shape, F32)

    @pl.when(j < nk)
    def _():
        s = _dot(qbd_scr[...], kc_ref[...].astype(BF16))
        s_scr[j] = s
        m_scr[...] = jnp.maximum(m_scr[...], jnp.max(s, axis=1, keepdims=True))

    @pl.when(j == nk - 1)
    def _():
        lam = _lambda_value(lam_ref, lam_init)
        sn = _dot_nt(qbd_scr[...], kn_ref[...].astype(BF16))
        qpos = past + (lax.broadcasted_iota(jnp.int32, sn.shape, 0) % tn)
        kpos = past + lax.broadcasted_iota(jnp.int32, sn.shape, 1)
        sn = jnp.where((kpos // CHUNK) <= (qpos // CHUNK), sn, NEG_INF)
        m = jnp.maximum(m_scr[...], jnp.max(sn, axis=1, keepdims=True))
        pn = jnp.exp(sn - m)
        l = jnp.sum(pn, axis=1, keepdims=True)
        for c in range(nk):
            p = jnp.exp(s_scr[c] - m)
            s_scr[c] = p
            l = l + jnp.sum(p, axis=1, keepdims=True)
        r0 = 1.0 / l[:half]
        r1 = lam / l[half:]
        wn_scr[...] = pn[:half] * r0 - pn[half:] * r1
        for c in range(nk):
            p = s_scr[c]
            w_scr[c] = (p[:half] * r0 - p[half:] * r1).astype(BF16)

    def v_rows(ref):
        return jnp.concatenate([ref[:, h, :] for h in range(N_HEADS)], axis=1).astype(BF16)

    @pl.when(j >= nk)
    def _():
        acc_scr[...] += _dot(w_scr[j - nk], v_rows(vc_ref))

    @pl.when(j == 2 * nk - 1)
    def _():
        acc = acc_scr[...] + _dot(wn_scr[...].astype(BF16), vn_ref[...].astype(BF16))
        g = g_ref[...]
        for h in range(N_HEADS):
            o = acc[h * tn:(h + 1) * tn, h * V_DIM:(h + 1) * V_DIM]
            o_ref[:, h * V_DIM:(h + 1) * V_DIM] = _head_norm(o, g, lam_init).astype(o_ref.dtype)


def _attn_sample(lam_vecs, norm_g, q, kct, vc, kn, vn, *, n_streams, tn, past, tk, lam_init):
    nk = past // tk
    rows = 2 * N_HEADS * tn
    half = N_HEADS * tn
    return pl.pallas_call(
        functools.partial(_attn_sample_kernel, nk=nk, tn=tn, past=past, lam_init=lam_init),
        out_shape=jax.ShapeDtypeStruct((n_streams * tn, ATTN_V_WIDTH), BF16),
        grid=(n_streams, 2 * nk),
        in_specs=[
            pl.BlockSpec((4, HEAD_DIM), lambda b, j: (0, 0)),
            pl.BlockSpec((1, V_DIM), lambda b, j: (0, 0)),
            pl.BlockSpec((tn, QK_DIM), lambda b, j: (b, 0)),
            pl.BlockSpec((None, QK_DIM, tk), lambda b, j: (b, 0, jnp.minimum(j, nk - 1))),
            pl.BlockSpec((None, tk, N_HEADS, V_DIM), lambda b, j: (b, jnp.maximum(j - nk, 0), 0, 0)),
            pl.BlockSpec((tn, QK_DIM), lambda b, j: (b, 0)),
            pl.BlockSpec((tn, ATTN_V_WIDTH), lambda b, j: (b, 0)),
        ],
        out_specs=pl.BlockSpec((tn, ATTN_V_WIDTH), lambda b, j: (b, 0)),
        scratch_shapes=[
            pltpu.VMEM((nk, rows, tk), F32),
            pltpu.VMEM((nk, half, tk), BF16),
            pltpu.VMEM((half, tn), F32),
            pltpu.VMEM((rows, 1), F32),
            pltpu.VMEM((half, ATTN_V_WIDTH), F32),
            pltpu.VMEM((rows, QK_DIM), BF16),
        ],
        compiler_params=_params(("arbitrary", "arbitrary")),
        name="attn_sample",
    )(lam_vecs, norm_g, q, kct, vc, kn, vn)


def _layer_norm(z, g, b):
    mu = jnp.mean(z, axis=-1, keepdims=True)
    zc = z - mu
    var = jnp.mean(zc * zc, axis=-1, keepdims=True)
    return zc * lax.rsqrt(var + LN_EPS) * g + b


def _postmix_kernel(x_ref, py_ref, ay_ref, wg_ref, wpo_ref, wao_ref, wout_ref, g1_ref, b1_ref,
                    wr_ref, br_ref, x1_ref, gate_ref, lpos_ref, cnt_ref, *, tm, rt, dn_alpha):
    x = x_ref[...]
    xb = x.astype(BF16)
    gates = jax.nn.sigmoid(_dot(xb, wg_ref[...]))
    a = _dot(py_ref[...], wpo_ref[...])
    b = _dot(ay_ref[...], wao_ref[...])
    mixed = gates[:, :D_MODEL] * a + gates[:, D_MODEL:] * b
    mo = _dot(mixed.astype(BF16), wout_ref[...])
    x1 = _layer_norm(dn_alpha * x + mo, g1_ref[...], b1_ref[...])
    x1_ref[...] = x1

    logits = _dot(x1.astype(BF16), wr_ref[...]) + br_ref[...]
    for sub in range(tm // rt):
        rows = slice(sub * rt, (sub + 1) * rt)
        gate_ref[rows, :], lpos_ref[rows, :], cnt_ref[sub] = _route_tile(logits[rows], rt)


def _route_tile(logits, tm):
    lane = lax.broadcasted_iota(jnp.int32, logits.shape, 1)
    work = logits
    vals, idxs = [], []
    for _ in range(TOP_K):
        mx = jnp.max(work, axis=1, keepdims=True)
        ix = jnp.min(jnp.where(work == mx, lane, N_EXPERTS), axis=1, keepdims=True)
        vals.append(mx)
        idxs.append(ix)
        work = jnp.where(lane == ix, -jnp.inf, work)
    exps = [jnp.exp(v - vals[0]) for v in vals]
    denom = exps[0] + exps[1] + exps[2] + exps[3]

    onehot = jnp.zeros(logits.shape, F32)
    for ix in idxs:
        onehot = onehot + (lane == ix).astype(F32)
    r = lax.broadcasted_iota(jnp.int32, (tm, tm), 0)
    c = lax.broadcasted_iota(jnp.int32, (tm, tm), 1)
    tri = jnp.where(c < r, 1.0, 0.0).astype(BF16)
    earlier = _dot(tri, onehot.astype(BF16))
    cnt = jnp.sum(onehot, axis=0, keepdims=True)
    units = jnp.floor((cnt + (RUN_ALIGN - 1.0)) * (1.0 / RUN_ALIGN))
    er = lax.broadcasted_iota(jnp.int32, (N_EXPERTS, N_EXPERTS), 0)
    ec = lax.broadcasted_iota(jnp.int32, (N_EXPERTS, N_EXPERTS), 1)
    upper = jnp.where(er < ec, 1.0, 0.0).astype(BF16)
    run_off = _dot(jnp.broadcast_to(units, (8, N_EXPERTS)).astype(BF16), upper)[0:1] * float(RUN_ALIGN)
    pos = earlier + run_off

    lane_out = lax.broadcasted_iota(jnp.int32, (tm, LANES), 1)
    lpos_out = jnp.zeros((tm, LANES), jnp.int32)
    gate_out = jnp.zeros((tm, LANES), F32)
    for k in range(TOP_K):
        lpos_k = jnp.sum(jnp.where(lane == idxs[k], pos, 0.0), axis=1, keepdims=True).astype(jnp.int32)
        lpos_out = jnp.where(lane_out == k, lpos_k, lpos_out)
        gate_out = jnp.where(lane_out == k, exps[k] / denom, gate_out)
    return gate_out, lpos_out, units * float(RUN_ALIGN)


def _postmix(x2d, py, ay, wg, wpo, wao, wout, g1, b1, wr, br, *, tm, rt, dn_alpha):
    n = x2d.shape[0]
    row = lambda i: (i, 0)
    const = lambda i: (0, 0)
    out_shape = (
        jax.ShapeDtypeStruct((n, D_MODEL), F32),
        jax.ShapeDtypeStruct((n, LANES), F32),
        jax.ShapeDtypeStruct((n, LANES), jnp.int32),
        jax.ShapeDtypeStruct((n // rt, 1, N_EXPERTS), F32),
    )
    return pl.pallas_call(
        functools.partial(_postmix_kernel, tm=tm, rt=rt, dn_alpha=dn_alpha),
        out_shape=out_shape,
        grid=(n // tm,),
        in_specs=[
            pl.BlockSpec((tm, D_MODEL), row),
            pl.BlockSpec((tm, POOL_DIM), row),
            pl.BlockSpec((tm, ATTN_V_WIDTH), row),
            pl.BlockSpec(wg.shape, const),
            pl.BlockSpec(wpo.shape, const),
            pl.BlockSpec(wao.shape, const),
            pl.BlockSpec(wout.shape, const),
            pl.BlockSpec((1, D_MODEL), const),
            pl.BlockSpec((1, D_MODEL), const),
            pl.BlockSpec(wr.shape, const),
            pl.BlockSpec((1, N_EXPERTS), const),
        ],
        out_specs=(
            pl.BlockSpec((tm, D_MODEL), row),
            pl.BlockSpec((tm, LANES), row),
            pl.BlockSpec((tm, LANES), row),
            pl.BlockSpec((tm // rt, 1, N_EXPERTS), lambda i: (i, 0, 0)),
        ),
        compiler_params=_params(("arbitrary",)),
        name="postmix",
    )(x2d, py, ay, wg, wpo, wao, wout, g1, b1, wr, br)


def _local_rows(tm):
    return TOP_K * tm + N_EXPERTS * RUN_ALIGN


def _for_each_run_chunk(tile, tab_refs, local_buf, sorted_ref, sem, to_sorted, fn):
    gs_ref, off_ref, cnt_ref = tab_refs

    def per_expert(e, carry):
        t = tile * N_EXPERTS + e
        cnt, off, gs = cnt_ref[t], off_ref[t], gs_ref[t]
        for b in RUN_BITS:
            size = 1 << b

            @pl.when((cnt & size) != 0)
            def _():
                lower = cnt & (size - 1)
                loc = local_buf.at[pl.ds(pl.multiple_of(off + lower, RUN_ALIGN), size)]
                srt = sorted_ref.at[pl.ds(pl.multiple_of(gs + lower, RUN_ALIGN), size)]
                fn(pltpu.make_async_copy(loc, srt, sem) if to_sorted else pltpu.make_async_copy(srt, loc, sem))
        return carry

    lax.fori_loop(0, N_EXPERTS, per_expert, 0)


def _dispatch_tile(tab_refs, lpos_ref, x_ref, xs_ref, xloc, sem, tm):
    lpos = lpos_ref[...]
    col = lax.broadcasted_iota(jnp.int32, (tm, _local_rows(tm)), 1)
    hit = col == lpos[:, 0:1]
    for k in range(1, TOP_K):
        hit = jnp.logical_or(hit, col == lpos[:, k:k + 1])
    perm_t = jnp.where(hit, 1.0, 0.0).astype(BF16)
    tile = pl.program_id(0)
    slot = tile % 2
    xloc[slot] = lax.dot_general(perm_t, x_ref[...].astype(BF16), (((0,), (0,)), ((), ())),
                                 preferred_element_type=F32)

    def copies(t, s, fn):
        _for_each_run_chunk(t, tab_refs, xloc.at[s], xs_ref, sem.at[s], True, fn)

    copies(tile, slot, lambda cp: cp.start())

    @pl.when(tile > 0)
    def _():
        copies(tile - 1, 1 - slot, lambda cp: cp.wait())

    @pl.when(tile == pl.num_programs(0) - 1)
    def _():
        copies(tile, slot, lambda cp: cp.wait())


def _dispatch_first_kernel(tail_ref, nu_ref, gs_ref, off_ref, cnt_ref, lpos_ref, x_ref, xs_ref,
                           xloc, zbuf, sem, zsem, *, tm, n_blocks):
    @pl.when(pl.program_id(0) == 0)
    def _():
        zbuf[...] = jnp.zeros(zbuf.shape, zbuf.dtype)

        def zero_copy(row):
            row = pl.multiple_of(row, EXPERT_BLOCK)
            return pltpu.make_async_copy(zbuf, xs_ref.at[pl.ds(row, EXPERT_BLOCK)], zsem)

        def over_blocks(fn):
            for e in range(N_EXPERTS):
                @pl.when(tail_ref[e] >= 0)
                def _():
                    fn(zero_copy(tail_ref[e]))
            lax.fori_loop(nu_ref[0], n_blocks, lambda b, c: (fn(zero_copy(b * EXPERT_BLOCK)), c)[1], 0)

        over_blocks(lambda cp: cp.start())
        over_blocks(lambda cp: cp.wait())

    _dispatch_tile((gs_ref, off_ref, cnt_ref), lpos_ref, x_ref, xs_ref, xloc, sem, tm)


def _dispatch_next_kernel(gs_ref, off_ref, cnt_ref, lpos_ref, x_ref, xs_in_ref, xs_ref, xloc, sem, *, tm):
    del xs_in_ref
    _dispatch_tile((gs_ref, off_ref, cnt_ref), lpos_ref, x_ref, xs_ref, xloc, sem, tm)


def _dispatch_first(tail, n_used, tabs, lpos, x1, *, tm, n_blocks):
    n = x1.shape[0]
    grid_spec = pltpu.PrefetchScalarGridSpec(
        num_scalar_prefetch=5,
        grid=(n // tm,),
        in_specs=[pl.BlockSpec((tm, LANES), lambda i, *_: (i, 0)),
                  pl.BlockSpec((tm, D_MODEL), lambda i, *_: (i, 0))],
        out_specs=pl.BlockSpec(memory_space=pl.ANY),
        scratch_shapes=[pltpu.VMEM((2, _local_rows(tm), D_MODEL), F32),
                        pltpu.VMEM((EXPERT_BLOCK, D_MODEL), F32),
                        pltpu.SemaphoreType.DMA((2,)), pltpu.SemaphoreType.DMA],
    )
    return pl.pallas_call(
        functools.partial(_dispatch_first_kernel, tm=tm, n_blocks=n_blocks),
        out_shape=jax.ShapeDtypeStruct((n_blocks * EXPERT_BLOCK, D_MODEL), F32),
        grid_spec=grid_spec,
        compiler_params=_params(("arbitrary",)),
        name="dispatch_first",
    )(tail, n_used, *tabs, lpos, x1)


def _dispatch_next(tabs, lpos, x1, xs, *, tm):
    n = x1.shape[0]
    grid_spec = pltpu.PrefetchScalarGridSpec(
        num_scalar_prefetch=3,
        grid=(n // tm,),
        in_specs=[pl.BlockSpec((tm, LANES), lambda i, *_: (i, 0)),
                  pl.BlockSpec((tm, D_MODEL), lambda i, *_: (i, 0)),
                  pl.BlockSpec(memory_space=pl.ANY)],
        out_specs=pl.BlockSpec(memory_space=pl.ANY),
        scratch_shapes=[pltpu.VMEM((2, _local_rows(tm), D_MODEL), F32), pltpu.SemaphoreType.DMA((2,))],
    )
    return pl.pallas_call(
        functools.partial(_dispatch_next_kernel, tm=tm),
        out_shape=jax.ShapeDtypeStruct(xs.shape, xs.dtype),
        grid_spec=grid_spec,
        input_output_aliases={5: 0},
        compiler_params=_params(("arbitrary",)),
        name="dispatch_next",
    )(*tabs, lpos, x1, xs)


def _experts_kernel(be_ref, nxt_ref, nu_ref, xs_ref, bg_ref, bl_ref, bo_ref, win_hbm, wo_hbm, y_ref,
                    win_buf, wo_buf, wg_scr, wl_scr, wo_scr, sem_in, sem_out):
    i = pl.program_id(0)
    used = i < nu_ref[0]
    first_of_expert = jnp.logical_or(i == 0, be_ref[i] != be_ref[jnp.maximum(i - 1, 0)])

    def weight_copies(e):
        return (pltpu.make_async_copy(win_hbm.at[e], win_buf, sem_in),
                pltpu.make_async_copy(wo_hbm.at[e], wo_buf, sem_out))

    @pl.when(jnp.logical_not(used))
    def _():
        y_ref[...] = jnp.zeros(y_ref.shape, y_ref.dtype)

    @pl.when(i == 0)
    def _():
        for cp in weight_copies(be_ref[0]):
            cp.start()

    @pl.when(jnp.logical_and(used, first_of_expert))
    def _():
        for cp in weight_copies(be_ref[i]):
            cp.wait()
        r = lax.broadcasted_iota(jnp.int32, (MXU_DIM, MXU_DIM), 0)
        c = lax.broadcasted_iota(jnp.int32, (MXU_DIM, MXU_DIM), 1)
        src = jnp.where(c < LANES, 2 * c, 2 * (c - LANES) + 1)
        sel = jnp.where(r == src, 1.0, 0.0).astype(BF16)
        for gq in range(2 * D_EXPERT // MXU_DIM):
            blk = win_buf[:, gq * MXU_DIM:(gq + 1) * MXU_DIM].astype(BF16)
            d = _dot(blk, sel)
            wg_scr[:, gq * LANES:(gq + 1) * LANES] = d[:, :LANES].astype(BF16)
            wl_scr[:, gq * LANES:(gq + 1) * LANES] = d[:, LANES:].astype(BF16)
        wo_scr[...] = wo_buf[...].astype(BF16)

        @pl.when(nxt_ref[i] >= 0)
        def _():
            for cp in weight_copies(nxt_ref[i]):
                cp.start()

    @pl.when(used)
    def _():
        xb = xs_ref[...].astype(BF16)
        glu = jnp.minimum(_dot(xb, wg_scr[...]) + bg_ref[...], SWIGLU_LIMIT)
        lin = jnp.clip(_dot(xb, wl_scr[...]) + bl_ref[...], -SWIGLU_LIMIT, SWIGLU_LIMIT)
        act = glu * jax.nn.sigmoid(SWIGLU_ALPHA * glu) * (lin + 1.0)
        y_ref[...] = _dot(act.astype(BF16), wo_scr[...]) + bo_ref[...]


def _experts(blk_expert, next_expert, n_used, xs, w_in, b_glu, b_lin, w_out, b_out):
    rows = xs.shape[0]
    n_blocks = rows // EXPERT_BLOCK
    wsel = lambda i, be, nxt, nu: (be[i], 0, 0)
    grid_spec = pltpu.PrefetchScalarGridSpec(
        num_scalar_prefetch=3,
        grid=(n_blocks,),
        in_specs=[
            pl.BlockSpec((EXPERT_BLOCK, D_MODEL), lambda i, be, nxt, nu: (jnp.minimum(i, nu[0] - 1), 0)),
            pl.BlockSpec((None, 1, D_EXPERT), wsel),
            pl.BlockSpec((None, 1, D_EXPERT), wsel),
            pl.BlockSpec((None, 1, D_MODEL), wsel),
            pl.BlockSpec(memory_space=pl.ANY),
            pl.BlockSpec(memory_space=pl.ANY),
        ],
        out_specs=pl.BlockSpec((EXPERT_BLOCK, D_MODEL), lambda i, be, nxt, nu: (i, 0)),
        scratch_shapes=[pltpu.VMEM((D_MODEL, 2 * D_EXPERT), F32),
                        pltpu.VMEM((D_EXPERT, D_MODEL), F32),
                        pltpu.VMEM((D_MODEL, D_EXPERT), BF16),
                        pltpu.VMEM((D_MODEL, D_EXPERT), BF16),
                        pltpu.VMEM((D_EXPERT, D_MODEL), BF16),
                        pltpu.SemaphoreType.DMA, pltpu.SemaphoreType.DMA],
    )
    return pl.pallas_call(
        _experts_kernel,
        out_shape=jax.ShapeDtypeStruct((rows, D_MODEL), F32),
        grid_spec=grid_spec,
        compiler_params=_params(("arbitrary",)),
        name="experts",
    )(blk_expert, next_expert, n_used, xs, b_glu, b_lin, b_out, w_in, w_out)


def _split_bf16(a):
    hi = a.astype(BF16)
    return hi, (a - hi.astype(F32)).astype(BF16)


def _combine_kernel(gs_ref, off_ref, cnt_ref, gate_ref, lpos_ref, x1_ref, g2_ref, b2_ref, yb_ref, o_ref,
                    yloc, sem, *, tm, dn_alpha):
    tile = pl.program_id(0)
    slot = tile % 2
    tabs = (gs_ref, off_ref, cnt_ref)

    def copies(t, s, fn):
        _for_each_run_chunk(t, tabs, yloc.at[s], yb_ref, sem.at[s], False, fn)

    @pl.when(tile == 0)
    def _():
        yloc[...] = jnp.zeros(yloc.shape, yloc.dtype)
        copies(tile, slot, lambda cp: cp.start())

    @pl.when(tile + 1 < pl.num_programs(0))
    def _():
        copies(tile + 1, 1 - slot, lambda cp: cp.start())

    copies(tile, slot, lambda cp: cp.wait())

    gate, lpos = gate_ref[...], lpos_ref[...]
    col = lax.broadcasted_iota(jnp.int32, (tm, _local_rows(tm)), 1)
    weights = jnp.zeros(col.shape, F32)
    for k in range(TOP_K):
        weights = jnp.where(col == lpos[:, k:k + 1], gate[:, k:k + 1], weights)
    w_hi, w_lo = _split_bf16(weights)
    y_hi, y_lo = _split_bf16(yloc[slot])
    y = _dot(w_hi, y_hi) + (_dot(w_hi, y_lo) + _dot(w_lo, y_hi))
    o_ref[...] = _layer_norm(dn_alpha * x1_ref[...] + y, g2_ref[...], b2_ref[...])


def _combine(tabs, gate, lpos, x1, g2, b2, yb, *, tm, dn_alpha):
    n = x1.shape[0]
    grid_spec = pltpu.PrefetchScalarGridSpec(
        num_scalar_prefetch=3,
        grid=(n // tm,),
        in_specs=[
            pl.BlockSpec((tm, LANES), lambda i, *_: (i, 0)),
            pl.BlockSpec((tm, LANES), lambda i, *_: (i, 0)),
            pl.BlockSpec((tm, D_MODEL), lambda i, *_: (i, 0)),
            pl.BlockSpec((1, D_MODEL), lambda i, *_: (0, 0)),
            pl.BlockSpec((1, D_MODEL), lambda i, *_: (0, 0)),
            pl.BlockSpec(memory_space=pl.ANY),
        ],
        out_specs=pl.BlockSpec((tm, D_MODEL), lambda i, *_: (i, 0)),
        scratch_shapes=[pltpu.VMEM((2, _local_rows(tm), D_MODEL), F32), pltpu.SemaphoreType.DMA((2,))],
    )
    return pl.pallas_call(
        functools.partial(_combine_kernel, tm=tm, dn_alpha=dn_alpha),
        out_shape=jax.ShapeDtypeStruct((n, D_MODEL), F32),
        grid_spec=grid_spec,
        compiler_params=_params(("arbitrary",)),
        name="combine",
    )(*tabs, gate, lpos, x1, g2, b2, yb)


def _position_tables(pos0, seq):
    pos = pos0 + jnp.arange(seq, dtype=jnp.int32)
    inv = ROPE_THETA ** (-jnp.arange(HALF_DIM, dtype=F32) / HALF_DIM)
    ang = pos.astype(F32)[:, None] * inv[None, :]
    cos, sin = jnp.cos(ang), jnp.sin(ang)
    cos_rows = jnp.concatenate([cos, cos, cos, cos], axis=-1)
    sin_rows = jnp.concatenate([-sin, sin, -sin, sin], axis=-1)
    icnt = jnp.concatenate(
        [jnp.broadcast_to((1.0 / jnp.minimum(pos + 1, w).astype(F32))[:, None], (seq, POOL_GROUP_DIM))
         for w in POOL_WINDOWS], axis=-1)
    return cos_rows, sin_rows, cos.T, sin.T, icnt


def _tile(n, pref):
    t = min(n, pref)
    while n % t:
        t //= 2
    return t


def kernel(x_prompt, x_sample, cache_k, cache_v, state_pool, w_in, w_pool_mix, pool_scale, w_pool_out,
           lambda_q1, lambda_k1, lambda_q2, lambda_k2, attn_norm_g, w_attn_out, w_out, ln1_g, ln1_b,
           w_router, b_router, w_expert_in, b_expert_in, w_expert_out, b_expert_out, ln2_g, ln2_b):
    depth = w_in.shape[0]
    assert depth == 1, "single-layer step"
    dn_alpha = (2.0 * depth) ** 0.25
    lam_init = 0.8 - 0.6 * math.exp(-0.3 * 0)
    bp, sp, _ = x_prompt.shape
    bs, ss, _ = x_sample.shape
    past = cache_k.shape[2]
    np_, ns = bp * sp, bs * ss

    c_q, c_k, c_v = POOL_DIM, POOL_DIM + QK_DIM, POOL_DIM + 2 * QK_DIM
    c_gate = c_v + ATTN_V_WIDTH
    w0 = w_in[0]
    w_pqv = jnp.concatenate([w0[:, :c_k], w0[:, c_v:c_gate]], axis=1).astype(BF16)
    w_k = w0[:, c_k:c_v].astype(BF16)
    w_gate = w0[:, c_gate:].astype(BF16)
    wmix = w_pool_mix[0].astype(BF16)
    pscale = pool_scale[0].reshape(1, POOL_DIM)
    wpo = w_pool_out[0].astype(BF16)
    wao = w_attn_out[0].astype(BF16)
    wout = w_out[0].astype(BF16)
    wr = w_router[0].astype(BF16)
    br = b_router[0].reshape(1, N_EXPERTS)
    lam_vecs = jnp.stack([lambda_q1[0], lambda_k1[0], lambda_q2[0], lambda_k2[0]])
    norm_g = attn_norm_g[0].reshape(1, V_DIM)
    g1, b1 = ln1_g[0].reshape(1, D_MODEL), ln1_b[0].reshape(1, D_MODEL)
    g2, b2 = ln2_g[0].reshape(1, D_MODEL), ln2_b[0].reshape(1, D_MODEL)
    b_glu = b_expert_in[0][:, 0::2].reshape(N_EXPERTS, 1, D_EXPERT)
    b_lin = b_expert_in[0][:, 1::2].reshape(N_EXPERTS, 1, D_EXPERT)
    b_eo = b_expert_out[0].reshape(N_EXPERTS, 1, D_MODEL)

    xp = x_prompt.reshape(np_, D_MODEL)
    cos_p, sin_p, cost_p, sint_p, icnt_p = _position_tables(0, sp)
    hist_p = jnp.zeros((bp, HIST_ROWS, POOL_DIM), F32)
    q_p, kt_p, ktb_p, v_p, vb_p, py_p, pnew_p = _inproj_prompt(
        xp, w_pqv, w_k.T, cos_p, sin_p, cost_p, sint_p, icnt_p, hist_p, wmix, pscale,
        n_streams=bp, seq=sp, tm=_tile(sp, DENSE_TILE))
    ay_p = _attn_prompt(lam_vecs, norm_g, q_p, ktb_p, vb_p, n_streams=bp, seq=sp,
                        tq=_tile(sp, 256), lam_init=lam_init)
    tm_p, tm_s = _tile(np_, ROUTE_TILE), _tile(ns, ROUTE_TILE)
    x1_p, gate_p, lpos_p, cnt_p = _postmix(
        xp, py_p, ay_p, w_gate, wpo, wao, wout, g1, b1, wr, br,
        tm=max(tm_p, _tile(np_, DENSE_TILE)), rt=tm_p, dn_alpha=dn_alpha)

    xs_ = x_sample.reshape(ns, D_MODEL)
    cos_s, sin_s, _, _, icnt_s = _position_tables(past, ss)
    hist_s = jnp.concatenate([jnp.zeros((bs, 1, POOL_DIM), F32), state_pool[0]], axis=1)
    q_s, k_s, v_s, py_s, pnew_s = _inproj_sample(
        xs_, w_pqv, w_k, cos_s, sin_s, icnt_s, hist_s, wmix, pscale, n_streams=bs, seq=ss)
    kct = jnp.transpose(cache_k[0], (0, 2, 3, 4, 1)).reshape(bs, QK_DIM, past)
    ay_s = _attn_sample(lam_vecs, norm_g, q_s, kct, cache_v[0], k_s, v_s, n_streams=bs, tn=ss, past=past,
                        tk=_tile(past, 512), lam_init=lam_init)
    x1_s, gate_s, lpos_s, cnt_s = _postmix(
        xs_, py_s, ay_s, w_gate, wpo, wao, wout, g1, b1, wr, br, tm=tm_s, rt=tm_s, dn_alpha=dn_alpha)

    ntp = np_ // tm_p
    cnt = jnp.concatenate([cnt_p[:, 0, :], cnt_s[:, 0, :]], axis=0).astype(jnp.int32)
    n_tiles = cnt.shape[0]
    group = jnp.sum(cnt, axis=0)
    padded = (group + EXPERT_BLOCK - 1) // EXPERT_BLOCK * EXPERT_BLOCK
    pad_end = jnp.cumsum(padded).astype(jnp.int32)
    run_start = (pad_end - padded)[None, :] + jnp.cumsum(cnt, axis=0) - cnt
    run_off = jnp.cumsum(cnt, axis=1) - cnt
    tail = jnp.where(padded > 0, pad_end - EXPERT_BLOCK, -1).astype(jnp.int32)
    max_rows = (np_ + ns) * TOP_K + n_tiles * N_EXPERTS * (RUN_ALIGN - 1) + N_EXPERTS * (EXPERT_BLOCK - 1)
    n_blocks = -(-max_rows // EXPERT_BLOCK)
    n_used = pad_end[-1:] // EXPERT_BLOCK
    blk_start = jnp.arange(n_blocks, dtype=jnp.int32) * EXPERT_BLOCK
    blk_expert = jnp.minimum(jnp.sum((blk_start[:, None] >= pad_end[None, :]).astype(jnp.int32), axis=1),
                             N_EXPERTS - 1)
    tabs_p = tuple(a[:ntp].reshape(-1).astype(jnp.int32) for a in (run_start, run_off, cnt))
    tabs_s = tuple(a[ntp:].reshape(-1).astype(jnp.int32) for a in (run_start, run_off, cnt))

    xsorted = _dispatch_first(tail, n_used, tabs_p, lpos_p, x1_p, tm=tm_p, n_blocks=n_blocks)
    xsorted = _dispatch_next(tabs_s, lpos_s, x1_s, xsorted, tm=tm_s)
    blk = jnp.arange(n_blocks, dtype=jnp.int32)
    later = (blk[None, :] > blk[:, None]) & (blk_expert[None, :] != blk_expert[:, None]) & (blk[None, :] < n_used)
    next_expert = jnp.where(jnp.any(later, axis=1), blk_expert[jnp.argmax(later, axis=1)], -1).astype(jnp.int32)
    yb = _experts(blk_expert, next_expert, n_used, xsorted, w_expert_in[0], b_glu, b_lin, w_expert_out[0], b_eo)
    y_p = _combine(tabs_p, gate_p, lpos_p, x1_p, g2, b2, yb, tm=tm_p, dn_alpha=dn_alpha)
    y_s = _combine(tabs_s, gate_s, lpos_s, x1_s, g2, b2, yb, tm=tm_s, dn_alpha=dn_alpha)

    k_prompt = jnp.transpose(kt_p.reshape(bp, N_HEADS, 2, HEAD_DIM, sp), (0, 4, 1, 2, 3))
    return (
        y_p.reshape(bp, sp, D_MODEL),
        y_s.reshape(bs, ss, D_MODEL),
        k_prompt[None],
        v_p[None],
        pnew_p[:, 1:].reshape(1, bp, POOL_HIST, POOL_DIM),
        k_s.reshape(1, bs, ss, N_HEADS, 2, HEAD_DIM),
        v_s.reshape(1, bs, ss, N_HEADS, V_DIM),
        pnew_s[:, 1:].reshape(1, bs, POOL_HIST, POOL_DIM),
    )
```

```python
import functools
import math

import jax
import jax.numpy as jnp
from jax import lax
from jax.experimental import pallas as pl
from jax.experimental.pallas import tpu as pltpu

D_MODEL = 1024
CHUNK = 64
POOL_WINDOWS = (2, 4, 8, 16)
POOL_GROUP_DIM = 128
POOL_DIM = len(POOL_WINDOWS) * POOL_GROUP_DIM
POOL_HIST = max(POOL_WINDOWS) - 1
HIST_ROWS = POOL_HIST + 1
N_HEADS = 8
HEAD_DIM = 64
HALF_DIM = HEAD_DIM // 2
V_DIM = 2 * HEAD_DIM
QK_DIM = N_HEADS * 2 * HEAD_DIM
ATTN_V_WIDTH = N_HEADS * V_DIM
ATTN_SCALE = HEAD_DIM ** -0.5
LOG2_E = math.log2(math.e)
ROPE_THETA = 10000.0
SUBLN_EPS = 1e-5
N_EXPERTS = 32
TOP_K = 4
D_EXPERT = 1024
SWIGLU_LIMIT = 7.0
SWIGLU_ALPHA = 1.702
LN_EPS = 1e-5
NEG_INF = -1e30
LANES = 128
MXU_DIM = 256

F32 = jnp.float32
BF16 = jnp.bfloat16

VMEM_LIMIT = 56 * 1024 * 1024
EXPERT_BLOCK = 256
RUN_ALIGN = 8
ROUTE_TILE = 256
DENSE_TILE = 512
ATTN_Q_TILE = 256
RUN_BITS = tuple(range(3, 9))


def _dot(a, b):
    return jnp.dot(a, b, preferred_element_type=F32)


def _dot_nt(a, b):
    return lax.dot_general(a, b, (((1,), (1,)), ((), ())), preferred_element_type=F32)


def _params(semantics):
    return pltpu.CompilerParams(dimension_semantics=semantics, vmem_limit_bytes=VMEM_LIMIT)


def _pool_branch(x, w_ref, icnt_ref, hist_ref, wmix_ref, pscale_ref, py_ref, pnew_ref, ext_ref, *, bb, tm):
    u = _dot(x, w_ref[:, 0:POOL_DIM])

    @pl.when(pl.program_id(1) == 0)
    def _():
        ext_ref[:, 0:HIST_ROWS, :] = hist_ref[...]

    for b in range(bb):
        ext_ref[b, HIST_ROWS:HIST_ROWS + tm, :] = u[b * tm:(b + 1) * tm]
    for b in range(bb):
        for g, w in enumerate(POOL_WINDOWS):
            cols = slice(g * POOL_GROUP_DIM, (g + 1) * POOL_GROUP_DIM)
            cur = ext_ref[b, HIST_ROWS:HIST_ROWS + tm, cols]
            acc = cur
            for j in range(1, w):
                acc = acc + ext_ref[b, HIST_ROWS - j:HIST_ROWS - j + tm, cols]
            d = acc * icnt_ref[:, cols] - cur
            y = _dot(d.astype(BF16), wmix_ref[g]) * pscale_ref[:, cols]
            py_ref[b * tm:(b + 1) * tm, cols] = y.astype(BF16)
    tail = ext_ref[:, tm:tm + HIST_ROWS, :]
    pnew_ref[...] = tail
    ext_ref[:, 0:HIST_ROWS, :] = tail


def _rope_rows(z, cos, sin):
    lane = lax.broadcasted_iota(jnp.int32, z.shape, 1)
    first_half = (lane % HEAD_DIM) < HALF_DIM
    partner = jnp.where(first_half, pltpu.roll(z, LANES - HALF_DIM, 1), pltpu.roll(z, HALF_DIM, 1))
    return z * cos + partner * sin


def _inproj_prompt_kernel(x_ref, w_ref, wkt_ref, cos_ref, sin_ref, cost_ref, sint_ref, icnt_ref, hist_ref,
                          wmix_ref, pscale_ref, q_ref, kt_ref, ktb_ref, v_ref, vb_ref, py_ref, pnew_ref,
                          ext_ref, *, tm):
    x = x_ref[...].astype(BF16)
    _pool_branch(x, w_ref, icnt_ref, hist_ref, wmix_ref, pscale_ref, py_ref, pnew_ref, ext_ref, bb=1, tm=tm)

    cos, sin = cos_ref[...], sin_ref[...]
    hq = _dot(x, w_ref[:, POOL_DIM:POOL_DIM + QK_DIM])
    for h in range(N_HEADS):
        sl = slice(h * V_DIM, (h + 1) * V_DIM)
        q_ref[:, sl] = (_rope_rows(hq[:, sl], cos, sin) * (ATTN_SCALE * LOG2_E)).astype(BF16)

    hkt = _dot_nt(wkt_ref[...], x)
    cost, sint = cost_ref[...], sint_ref[...]
    for hc in range(2 * N_HEADS):
        r0 = hc * HEAD_DIM
        x1 = hkt[r0:r0 + HALF_DIM]
        x2 = hkt[r0 + HALF_DIM:r0 + HEAD_DIM]
        o1 = x1 * cost - x2 * sint
        o2 = x2 * cost + x1 * sint
        kt_ref[r0:r0 + HALF_DIM, :] = o1
        kt_ref[r0 + HALF_DIM:r0 + HEAD_DIM, :] = o2
        ktb_ref[r0:r0 + HALF_DIM, :] = o1.astype(BF16)
        ktb_ref[r0 + HALF_DIM:r0 + HEAD_DIM, :] = o2.astype(BF16)

    hv = _dot(x, w_ref[:, POOL_DIM + QK_DIM:POOL_DIM + QK_DIM + ATTN_V_WIDTH])
    vb_ref[...] = hv.astype(BF16)
    for h in range(N_HEADS):
        v_ref[pl.ds(h, tm, stride=N_HEADS), :] = hv[:, h * V_DIM:(h + 1) * V_DIM]


def _inproj_prompt(x2d, w_pqv, wkt, cos, sin, cost, sint, icnt, hist, wmix, pscale, *, n_streams, seq, tm):
    n = n_streams * seq
    nt = seq // tm
    row_map = lambda b, t: (b * nt + t, 0)
    const2 = lambda b, t: (0, 0)
    out_shape = (
        jax.ShapeDtypeStruct((n, QK_DIM), BF16),
        jax.ShapeDtypeStruct((n_streams, QK_DIM, seq), F32),
        jax.ShapeDtypeStruct((n_streams, QK_DIM, seq), BF16),
        jax.ShapeDtypeStruct((n_streams, seq * N_HEADS, V_DIM), F32),
        jax.ShapeDtypeStruct((n, ATTN_V_WIDTH), BF16),
        jax.ShapeDtypeStruct((n, POOL_DIM), BF16),
        jax.ShapeDtypeStruct((n_streams, HIST_ROWS, POOL_DIM), F32),
    )
    return pl.pallas_call(
        functools.partial(_inproj_prompt_kernel, tm=tm),
        out_shape=out_shape,
        grid=(n_streams, nt),
        in_specs=[
            pl.BlockSpec((tm, D_MODEL), row_map),
            pl.BlockSpec(w_pqv.shape, const2),
            pl.BlockSpec(wkt.shape, const2),
            pl.BlockSpec((tm, LANES), lambda b, t: (t, 0)),
            pl.BlockSpec((tm, LANES), lambda b, t: (t, 0)),
            pl.BlockSpec((HALF_DIM, tm), lambda b, t: (0, t)),
            pl.BlockSpec((HALF_DIM, tm), lambda b, t: (0, t)),
            pl.BlockSpec((tm, POOL_DIM), lambda b, t: (t, 0)),
            pl.BlockSpec((1, HIST_ROWS, POOL_DIM), lambda b, t: (b, 0, 0)),
            pl.BlockSpec((len(POOL_WINDOWS), POOL_GROUP_DIM, POOL_GROUP_DIM), lambda b, t: (0, 0, 0)),
            pl.BlockSpec((1, POOL_DIM), const2),
        ],
        out_specs=(
            pl.BlockSpec((tm, QK_DIM), row_map),
            pl.BlockSpec((None, QK_DIM, tm), lambda b, t: (b, 0, t)),
            pl.BlockSpec((None, QK_DIM, tm), lambda b, t: (b, 0, t)),
            pl.BlockSpec((None, tm * N_HEADS, V_DIM), lambda b, t: (b, t, 0)),
            pl.BlockSpec((tm, ATTN_V_WIDTH), row_map),
            pl.BlockSpec((tm, POOL_DIM), row_map),
            pl.BlockSpec((1, HIST_ROWS, POOL_DIM), lambda b, t: (b, 0, 0)),
        ),
        scratch_shapes=[pltpu.VMEM((1, HIST_ROWS + tm, POOL_DIM), F32)],
        compiler_params=_params(("arbitrary", "arbitrary")),
        name="inproj_prompt",
    )(x2d, w_pqv, wkt, cos, sin, cost, sint, icnt, hist, wmix, pscale)


def _inproj_sample_kernel(x_ref, w_ref, wk_ref, cos_ref, sin_ref, icnt_ref, hist_ref, wmix_ref, pscale_ref,
                          q_ref, k_ref, v_ref, py_ref, pnew_ref, ext_ref, *, bb, tm):
    x = x_ref[...].astype(BF16)
    _pool_branch(x, w_ref, icnt_ref, hist_ref, wmix_ref, pscale_ref, py_ref, pnew_ref, ext_ref, bb=bb, tm=tm)
    cos = jnp.concatenate([cos_ref[...]] * bb, axis=0)
    sin = jnp.concatenate([sin_ref[...]] * bb, axis=0)
    hq = _dot(x, w_ref[:, POOL_DIM:POOL_DIM + QK_DIM])
    hk = _dot(x, wk_ref[...])
    for h in range(N_HEADS):
        sl = slice(h * V_DIM, (h + 1) * V_DIM)
        q_ref[:, sl] = (_rope_rows(hq[:, sl], cos, sin) * ATTN_SCALE).astype(BF16)
        k_ref[:, sl] = _rope_rows(hk[:, sl], cos, sin)
    v_ref[...] = _dot(x, w_ref[:, POOL_DIM + QK_DIM:POOL_DIM + QK_DIM + ATTN_V_WIDTH])


def _inproj_sample(x2d, w_pqv, wk, cos, sin, icnt, hist, wmix, pscale, *, n_streams, seq):
    n = n_streams * seq
    const2 = lambda i, t: (0, 0)
    const3 = lambda i, t: (0, 0, 0)
    out_shape = (
        jax.ShapeDtypeStruct((n, QK_DIM), BF16),
        jax.ShapeDtypeStruct((n, QK_DIM), F32),
        jax.ShapeDtypeStruct((n, ATTN_V_WIDTH), F32),
        jax.ShapeDtypeStruct((n, POOL_DIM), BF16),
        jax.ShapeDtypeStruct((n_streams, HIST_ROWS, POOL_DIM), F32),
    )
    return pl.pallas_call(
        functools.partial(_inproj_sample_kernel, bb=n_streams, tm=seq),
        out_shape=out_shape,
        grid=(1, 1),
        in_specs=[
            pl.BlockSpec((n, D_MODEL), const2),
            pl.BlockSpec(w_pqv.shape, const2),
            pl.BlockSpec(wk.shape, const2),
            pl.BlockSpec((seq, LANES), const2),
            pl.BlockSpec((seq, LANES), const2),
            pl.BlockSpec((seq, POOL_DIM), const2),
            pl.BlockSpec((n_streams, HIST_ROWS, POOL_DIM), const3),
            pl.BlockSpec((len(POOL_WINDOWS), POOL_GROUP_DIM, POOL_GROUP_DIM), const3),
            pl.BlockSpec((1, POOL_DIM), const2),
        ],
        out_specs=(
            pl.BlockSpec((n, QK_DIM), const2),
            pl.BlockSpec((n, QK_DIM), const2),
            pl.BlockSpec((n, ATTN_V_WIDTH), const2),
            pl.BlockSpec((n, POOL_DIM), const2),
            pl.BlockSpec((n_streams, HIST_ROWS, POOL_DIM), const3),
        ),
        scratch_shapes=[pltpu.VMEM((n_streams, HIST_ROWS + seq, POOL_DIM), F32)],
        compiler_params=_params(("arbitrary", "arbitrary")),
        name="inproj_sample",
    )(x2d, w_pqv, wk, cos, sin, icnt, hist, wmix, pscale)


def _lambda_value(lam_ref, lam_init):
    lv = lam_ref[...]
    s1 = jnp.sum(lv[0:1] * lv[1:2], axis=1, keepdims=True)
    s2 = jnp.sum(lv[2:3] * lv[3:4], axis=1, keepdims=True)
    return jnp.exp(s1) - jnp.exp(s2) + lam_init


def _head_norm(o, g, lam_init):
    ms = jnp.mean(o * o, axis=-1, keepdims=True)
    return o * lax.rsqrt(ms + SUBLN_EPS) * g * (1.0 - lam_init)


def _attn_prompt_kernel(lam_ref, g_ref, q_ref, kt_ref, v_ref, o_ref, vext, *, seq, tq, lam_init):
    lam = _lambda_value(lam_ref, lam_init)
    g = g_ref[...]
    r = lax.broadcasted_iota(jnp.int32, (tq, tq), 0)
    c = lax.broadcasted_iota(jnp.int32, (tq, tq), 1)
    diag_visible = (c // CHUNK) <= (r // CHUNK)
    lane = lax.broadcasted_iota(jnp.int32, (tq, V_DIM), 1)
    vext[:, 0:V_DIM] = v_ref[...]
    vlane = lax.broadcasted_iota(jnp.int32, (seq, V_DIM), 1)
    vext[:, V_DIM:] = jnp.where(vlane == 0, 1.0, 0.0).astype(vext.dtype)

    for i in range(seq // tq):
        lo = i * tq
        q = q_ref[lo:lo + tq, :]
        zero = jnp.zeros_like(q)
        qc = (jnp.where(lane < HEAD_DIM, q, zero), jnp.where(lane >= HEAD_DIM, q, zero))
        normed = []
        for k in range(2):
            sd = jnp.where(diag_visible, _dot(qc[k], kt_ref[:, lo:lo + tq]), NEG_INF)
            m = jnp.max(sd, axis=1, keepdims=True)
            if i > 0:
                sp = _dot(qc[k], kt_ref[:, 0:lo])
                m = jnp.maximum(m, jnp.max(sp, axis=1, keepdims=True))
            acc = _dot(jnp.exp2(sd - m).astype(BF16), vext[lo:lo + tq, :])
            if i > 0:
                acc = acc + _dot(jnp.exp2(sp - m).astype(BF16), vext[0:lo, :])
            normed.append(acc[:, 0:V_DIM] / acc[:, V_DIM:V_DIM + 1])
        o = normed[0] - lam * normed[1]
        o_ref[lo:lo + tq, :] = _head_norm(o, g, lam_init).astype(o_ref.dtype)


def _attn_prompt(lam_vecs, norm_g, q, ktb, vb, *, n_streams, seq, tq, lam_init):
    return pl.pallas_call(
        functools.partial(_attn_prompt_kernel, seq=seq, tq=tq, lam_init=lam_init),
        out_shape=jax.ShapeDtypeStruct((n_streams * seq, ATTN_V_WIDTH), BF16),
        grid=(n_streams, N_HEADS),
        in_specs=[
            pl.BlockSpec((4, HEAD_DIM), lambda b, h: (0, 0)),
            pl.BlockSpec((1, V_DIM), lambda b, h: (0, 0)),
            pl.BlockSpec((seq, V_DIM), lambda b, h: (b, h)),
            pl.BlockSpec((None, V_DIM, seq), lambda b, h: (b, h, 0)),
            pl.BlockSpec((seq, V_DIM), lambda b, h: (b, h)),
        ],
        out_specs=pl.BlockSpec((seq, V_DIM), lambda b, h: (b, h)),
        scratch_shapes=[pltpu.VMEM((seq, MXU_DIM), BF16)],
        compiler_params=_params(("arbitrary", "arbitrary")),
        name="attn_prompt",
    )(lam_vecs, norm_g, q, ktb, vb)


def _attn_sample_kernel(lam_ref, g_ref, q_ref, kc_ref, vc_ref, kn_ref, vn_ref, o_ref,
                        s_scr, w_scr, wn_scr, m_scr, acc_scr, qbd_scr, *, nk, tn, past, lam_init):
    j = pl.program_id(1)
    half = N_HEADS * tn

    @pl.when(j == 0)
    def _():
        q = q_ref[...]
        qt = jnp.concatenate([q] * (2 * N_HEADS), axis=0)
        r = lax.broadcasted_iota(jnp.int32, qt.shape, 0)
        l = lax.broadcasted_iota(jnp.int32, qt.shape, 1)
        keep = ((r // half) == ((l % V_DIM) // HEAD_DIM)) & (((r % half) // tn) == (l // V_DIM))
        qbd_scr[...] = jnp.where(keep, qt, jnp.zeros_like(qt))
        m_scr[...] = jnp.full(m_scr.shape, NEG_INF, F32)
        acc_scr[...] = jnp.zeros(acc_scr.shape, F32)

    @pl.when(j < nk)
    def _():
        s = _dot(qbd_scr[...], kc_ref[...].astype(BF16))
        s_scr[j] = s
        m_scr[...] = jnp.maximum(m_scr[...], jnp.max(s, axis=1, keepdims=True))

    @pl.when(j == nk - 1)
    def _():
        lam = _lambda_value(lam_ref, lam_init)
        sn = _dot_nt(qbd_scr[...], kn_ref[...].astype(BF16))
        qpos = past + (lax.broadcasted_iota(jnp.int32, sn.shape, 0) % tn)
        kpos = past + lax.broadcasted_iota(jnp.int32, sn.shape, 1)
        sn = jnp.where((kpos // CHUNK) <= (qpos // CHUNK), sn, NEG_INF)
        m = jnp.maximum(m_scr[...], jnp.max(sn, axis=1, keepdims=True))
        pn = jnp.exp(sn - m)
        l = jnp.sum(pn, axis=1, keepdims=True)
        for c in range(nk):
            p = jnp.exp(s_scr[c] - m)
            s_scr[c] = p
            l = l + jnp.sum(p, axis=1, keepdims=True)
        r0 = 1.0 / l[:half]
        r1 = lam / l[half:]
        wn_scr[...] = pn[:half] * r0 - pn[half:] * r1
        for c in range(nk):
            p = s_scr[c]
            w_scr[c] = (p[:half] * r0 - p[half:] * r1).astype(BF16)

    def v_rows(ref):
        tk = ref.shape[0] // N_HEADS
        return jnp.concatenate([ref[pl.ds(h, tk, stride=N_HEADS), :] for h in range(N_HEADS)],
                               axis=1).astype(BF16)

    @pl.when(j >= nk)
    def _():
        acc_scr[...] += _dot(w_scr[j - nk], v_rows(vc_ref))

    @pl.when(j == 2 * nk - 1)
    def _():
        acc = acc_scr[...] + _dot(wn_scr[...].astype(BF16), vn_ref[...].astype(BF16))
        g = g_ref[...]
        for h in range(N_HEADS):
            o = acc[h * tn:(h + 1) * tn, h * V_DIM:(h + 1) * V_DIM]
            o_ref[:, h * V_DIM:(h + 1) * V_DIM] = _head_norm(o, g, lam_init).astype(o_ref.dtype)


def _attn_sample(lam_vecs, norm_g, q, kct, vc, kn, vn, *, n_streams, tn, past, tk, lam_init):
    nk = past // tk
    rows = 2 * N_HEADS * tn
    half = N_HEADS * tn
    return pl.pallas_call(
        functools.partial(_attn_sample_kernel, nk=nk, tn=tn, past=past, lam_init=lam_init),
        out_shape=jax.ShapeDtypeStruct((n_streams * tn, ATTN_V_WIDTH), BF16),
        grid=(n_streams, 2 * nk),
        in_specs=[
            pl.BlockSpec((4, HEAD_DIM), lambda b, j: (0, 0)),
            pl.BlockSpec((1, V_DIM), lambda b, j: (0, 0)),
            pl.BlockSpec((tn, QK_DIM), lambda b, j: (b, 0)),
            pl.BlockSpec((None, QK_DIM, tk), lambda b, j: (b, 0, jnp.minimum(j, nk - 1))),
            pl.BlockSpec((None, tk * N_HEADS, V_DIM), lambda b, j: (b, jnp.maximum(j - nk, 0), 0)),
            pl.BlockSpec((tn, QK_DIM), lambda b, j: (b, 0)),
            pl.BlockSpec((tn, ATTN_V_WIDTH), lambda b, j: (b, 0)),
        ],
        out_specs=pl.BlockSpec((tn, ATTN_V_WIDTH), lambda b, j: (b, 0)),
        scratch_shapes=[
            pltpu.VMEM((nk, rows, tk), F32),
            pltpu.VMEM((nk, half, tk), BF16),
            pltpu.VMEM((half, tn), F32),
            pltpu.VMEM((rows, 1), F32),
            pltpu.VMEM((half, ATTN_V_WIDTH), F32),
            pltpu.VMEM((rows, QK_DIM), BF16),
        ],
        compiler_params=_params(("arbitrary", "arbitrary")),
        name="attn_sample",
    )(lam_vecs, norm_g, q, kct, vc, kn, vn)


def _layer_norm(z, g, b):
    mu = jnp.mean(z, axis=-1, keepdims=True)
    zc = z - mu
    var = jnp.mean(zc * zc, axis=-1, keepdims=True)
    return zc * lax.rsqrt(var + LN_EPS) * g + b


def _postmix_kernel(x_ref, py_ref, ay_ref, wg_ref, wpo_ref, wao_ref, wout_ref, g1_ref, b1_ref,
                    wr_ref, br_ref, x1_ref, gate_ref, lpos_ref, cnt_ref, *, tm, rt, dn_alpha):
    x = x_ref[...]
    xb = x.astype(BF16)
    gates = jax.nn.sigmoid(_dot(xb, wg_ref[...]))
    a = _dot(py_ref[...], wpo_ref[...])
    b = _dot(ay_ref[...], wao_ref[...])
    mixed = gates[:, :D_MODEL] * a + gates[:, D_MODEL:] * b
    mo = _dot(mixed.astype(BF16), wout_ref[...])
    x1 = _layer_norm(dn_alpha * x + mo, g1_ref[...], b1_ref[...])
    x1_ref[...] = x1

    logits = _dot(x1.astype(BF16), wr_ref[...]) + br_ref[...]
    for sub in range(tm // rt):
        rows = slice(sub * rt, (sub + 1) * rt)
        gate_ref[rows, :], lpos_ref[rows, :], cnt_ref[sub] = _route_tile(logits[rows], rt)


def _route_tile(logits, tm):
    lane = lax.broadcasted_iota(jnp.int32, logits.shape, 1)
    work = logits
    vals, idxs = [], []
    for _ in range(TOP_K):
        mx = jnp.max(work, axis=1, keepdims=True)
        ix = jnp.min(jnp.where(work == mx, lane, N_EXPERTS), axis=1, keepdims=True)
        vals.append(mx)
        idxs.append(ix)
        work = jnp.where(lane == ix, -jnp.inf, work)
    exps = [jnp.exp(v - vals[0]) for v in vals]
    denom = exps[0] + exps[1] + exps[2] + exps[3]

    onehot = jnp.zeros(logits.shape, F32)
    for ix in idxs:
        onehot = onehot + (lane == ix).astype(F32)
    r = lax.broadcasted_iota(jnp.int32, (tm, tm), 0)
    c = lax.broadcasted_iota(jnp.int32, (tm, tm), 1)
    tri = jnp.where(c < r, 1.0, 0.0).astype(BF16)
    earlier = _dot(tri, onehot.astype(BF16))
    cnt = jnp.sum(onehot, axis=0, keepdims=True)
    units = jnp.floor((cnt + (RUN_ALIGN - 1.0)) * (1.0 / RUN_ALIGN))
    er = lax.broadcasted_iota(jnp.int32, (N_EXPERTS, N_EXPERTS), 0)
    ec = lax.broadcasted_iota(jnp.int32, (N_EXPERTS, N_EXPERTS), 1)
    upper = jnp.where(er < ec, 1.0, 0.0).astype(BF16)
    run_off = _dot(jnp.broadcast_to(units, (8, N_EXPERTS)).astype(BF16), upper)[0:1] * float(RUN_ALIGN)
    pos = earlier + run_off

    lane_out = lax.broadcasted_iota(jnp.int32, (tm, LANES), 1)
    lpos_out = jnp.zeros((tm, LANES), jnp.int32)
    gate_out = jnp.zeros((tm, LANES), F32)
    for k in range(TOP_K):
        lpos_k = jnp.sum(jnp.where(lane == idxs[k], pos, 0.0), axis=1, keepdims=True).astype(jnp.int32)
        lpos_out = jnp.where(lane_out == k, lpos_k, lpos_out)
        gate_out = jnp.where(lane_out == k, exps[k] / denom, gate_out)
    return gate_out, lpos_out, units * float(RUN_ALIGN)


def _postmix(x2d, py, ay, wg, wpo, wao, wout, g1, b1, wr, br, *, tm, rt, dn_alpha):
    n = x2d.shape[0]
    row = lambda i: (i, 0)
    const = lambda i: (0, 0)
    out_shape = (
        jax.ShapeDtypeStruct((n, D_MODEL), F32),
        jax.ShapeDtypeStruct((n, LANES), F32),
        jax.ShapeDtypeStruct((n, LANES), jnp.int32),
        jax.ShapeDtypeStruct((n // rt, 1, N_EXPERTS), F32),
    )
    return pl.pallas_call(
        functools.partial(_postmix_kernel, tm=tm, rt=rt, dn_alpha=dn_alpha),
        out_shape=out_shape,
        grid=(n // tm,),
        in_specs=[
            pl.BlockSpec((tm, D_MODEL), row),
            pl.BlockSpec((tm, POOL_DIM), row),
            pl.BlockSpec((tm, ATTN_V_WIDTH), row),
            pl.BlockSpec(wg.shape, const),
            pl.BlockSpec(wpo.shape, const),
            pl.BlockSpec(wao.shape, const),
            pl.BlockSpec(wout.shape, const),
            pl.BlockSpec((1, D_MODEL), const),
            pl.BlockSpec((1, D_MODEL), const),
            pl.BlockSpec(wr.shape, const),
            pl.BlockSpec((1, N_EXPERTS), const),
        ],
        out_specs=(
            pl.BlockSpec((tm, D_MODEL), row),
            pl.BlockSpec((tm, LANES), row),
            pl.BlockSpec((tm, LANES), row),
            pl.BlockSpec((tm // rt, 1, N_EXPERTS), lambda i: (i, 0, 0)),
        ),
        compiler_params=_params(("arbitrary",)),
        name="postmix",
    )(x2d, py, ay, wg, wpo, wao, wout, g1, b1, wr, br)


def _local_rows(tm):
    return TOP_K * tm + N_EXPERTS * RUN_ALIGN


def _for_each_run_chunk(tile, tab_refs, local_buf, sorted_ref, sem, to_sorted, fn):
    gs_ref, off_ref, cnt_ref = tab_refs

    def per_expert(e, carry):
        t = tile * N_EXPERTS + e
        cnt, off, gs = cnt_ref[t], off_ref[t], gs_ref[t]
        for b in RUN_BITS:
            size = 1 << b

            @pl.when((cnt & size) != 0)
            def _():
                lower = cnt & (size - 1)
                loc = local_buf.at[pl.ds(pl.multiple_of(off + lower, RUN_ALIGN), size)]
                srt = sorted_ref.at[pl.ds(pl.multiple_of(gs + lower, RUN_ALIGN), size)]
                fn(pltpu.make_async_copy(loc, srt, sem) if to_sorted else pltpu.make_async_copy(srt, loc, sem))
        return carry

    lax.fori_loop(0, N_EXPERTS, per_expert, 0)


def _dispatch_tile(tab_refs, lpos_ref, x_ref, xs_ref, xloc, sem, tm):
    lpos = lpos_ref[...]
    col = lax.broadcasted_iota(jnp.int32, (tm, _local_rows(tm)), 1)
    hit = col == lpos[:, 0:1]
    for k in range(1, TOP_K):
        hit = jnp.logical_or(hit, col == lpos[:, k:k + 1])
    perm_t = jnp.where(hit, 1.0, 0.0).astype(BF16)
    tile = pl.program_id(0)
    slot = tile % 2
    xloc[slot] = lax.dot_general(perm_t, x_ref[...].astype(BF16), (((0,), (0,)), ((), ())),
                                 preferred_element_type=F32)

    def copies(t, s, fn):
        _for_each_run_chunk(t, tab_refs, xloc.at[s], xs_ref, sem.at[s], True, fn)

    copies(tile, slot, lambda cp: cp.start())

    @pl.when(tile > 0)
    def _():
        copies(tile - 1, 1 - slot, lambda cp: cp.wait())

    @pl.when(tile == pl.num_programs(0) - 1)
    def _():
        copies(tile, slot, lambda cp: cp.wait())


def _dispatch_first_kernel(tail_ref, nu_ref, gs_ref, off_ref, cnt_ref, lpos_ref, x_ref, xs_ref,
                           xloc, zbuf, sem, zsem, *, tm, n_blocks):
    @pl.when(pl.program_id(0) == 0)
    def _():
        zbuf[...] = jnp.zeros(zbuf.shape, zbuf.dtype)

        def zero_copy(row):
            row = pl.multiple_of(row, EXPERT_BLOCK)
            return pltpu.make_async_copy(zbuf, xs_ref.at[pl.ds(row, EXPERT_BLOCK)], zsem)

        def over_blocks(fn):
            for e in range(N_EXPERTS):
                @pl.when(tail_ref[e] >= 0)
                def _():
                    fn(zero_copy(tail_ref[e]))
            lax.fori_loop(nu_ref[0], n_blocks, lambda b, c: (fn(zero_copy(b * EXPERT_BLOCK)), c)[1], 0)

        over_blocks(lambda cp: cp.start())
        over_blocks(lambda cp: cp.wait())

    _dispatch_tile((gs_ref, off_ref, cnt_ref), lpos_ref, x_ref, xs_ref, xloc, sem, tm)


def _dispatch_next_kernel(gs_ref, off_ref, cnt_ref, lpos_ref, x_ref, xs_in_ref, xs_ref, xloc, sem, *, tm):
    del xs_in_ref
    _dispatch_tile((gs_ref, off_ref, cnt_ref), lpos_ref, x_ref, xs_ref, xloc, sem, tm)


def _dispatch_first(tail, n_used, tabs, lpos, x1, *, tm, n_blocks):
    n = x1.shape[0]
    grid_spec = pltpu.PrefetchScalarGridSpec(
        num_scalar_prefetch=5,
        grid=(n // tm,),
        in_specs=[pl.BlockSpec((tm, LANES), lambda i, *_: (i, 0)),
                  pl.BlockSpec((tm, D_MODEL), lambda i, *_: (i, 0))],
        out_specs=pl.BlockSpec(memory_space=pl.ANY),
        scratch_shapes=[pltpu.VMEM((2, _local_rows(tm), D_MODEL), F32),
                        pltpu.VMEM((EXPERT_BLOCK, D_MODEL), F32),
                        pltpu.SemaphoreType.DMA((2,)), pltpu.SemaphoreType.DMA],
    )
    return pl.pallas_call(
        functools.partial(_dispatch_first_kernel, tm=tm, n_blocks=n_blocks),
        out_shape=jax.ShapeDtypeStruct((n_blocks * EXPERT_BLOCK, D_MODEL), F32),
        grid_spec=grid_spec,
        compiler_params=_params(("arbitrary",)),
        name="dispatch_first",
    )(tail, n_used, *tabs, lpos, x1)


def _dispatch_next(tabs, lpos, x1, xs, *, tm):
    n = x1.shape[0]
    grid_spec = pltpu.PrefetchScalarGridSpec(
        num_scalar_prefetch=3,
        grid=(n // tm,),
        in_specs=[pl.BlockSpec((tm, LANES), lambda i, *_: (i, 0)),
                  pl.BlockSpec((tm, D_MODEL), lambda i, *_: (i, 0)),
                  pl.BlockSpec(memory_space=pl.ANY)],
        out_specs=pl.BlockSpec(memory_space=pl.ANY),
        scratch_shapes=[pltpu.VMEM((2, _local_rows(tm), D_MODEL), F32), pltpu.SemaphoreType.DMA((2,))],
    )
    return pl.pallas_call(
        functools.partial(_dispatch_next_kernel, tm=tm),
        out_shape=jax.ShapeDtypeStruct(xs.shape, xs.dtype),
        grid_spec=grid_spec,
        input_output_aliases={5: 0},
        compiler_params=_params(("arbitrary",)),
        name="dispatch_next",
    )(*tabs, lpos, x1, xs)


def _experts_kernel(be_ref, nxt_ref, nu_ref, xs_ref, bg_ref, bl_ref, bo_ref, win_hbm, wo_hbm, y_ref,
                    win_buf, wo_buf, wg_scr, wl_scr, wo_scr, sem_in, sem_out):
    i = pl.program_id(0)
    used = i < nu_ref[0]
    first_of_expert = jnp.logical_or(i == 0, be_ref[i] != be_ref[jnp.maximum(i - 1, 0)])

    def weight_copies(e):
        return (pltpu.make_async_copy(win_hbm.at[e], win_buf, sem_in),
                pltpu.make_async_copy(wo_hbm.at[e], wo_buf, sem_out))

    @pl.when(jnp.logical_not(used))
    def _():
        y_ref[...] = jnp.zeros(y_ref.shape, y_ref.dtype)

    @pl.when(i == 0)
    def _():
        for cp in weight_copies(be_ref[0]):
            cp.start()

    @pl.when(jnp.logical_and(used, first_of_expert))
    def _():
        for cp in weight_copies(be_ref[i]):
            cp.wait()
        r = lax.broadcasted_iota(jnp.int32, (MXU_DIM, MXU_DIM), 0)
        c = lax.broadcasted_iota(jnp.int32, (MXU_DIM, MXU_DIM), 1)
        src = jnp.where(c < LANES, 2 * c, 2 * (c - LANES) + 1)
        sel = jnp.where(r == src, 1.0, 0.0).astype(BF16)
        for gq in range(2 * D_EXPERT // MXU_DIM):
            blk = win_buf[:, gq * MXU_DIM:(gq + 1) * MXU_DIM].astype(BF16)
            d = _dot(blk, sel)
            wg_scr[:, gq * LANES:(gq + 1) * LANES] = d[:, :LANES].astype(BF16)
            wl_scr[:, gq * LANES:(gq + 1) * LANES] = d[:, LANES:].astype(BF16)
        wo_scr[...] = wo_buf[...].astype(BF16)

        @pl.when(nxt_ref[i] >= 0)
        def _():
            for cp in weight_copies(nxt_ref[i]):
                cp.start()

    @pl.when(used)
    def _():
        xb = xs_ref[...].astype(BF16)
        glu = jnp.minimum(_dot(xb, wg_scr[...]) + bg_ref[...], SWIGLU_LIMIT)
        lin = jnp.clip(_dot(xb, wl_scr[...]) + bl_ref[...], -SWIGLU_LIMIT, SWIGLU_LIMIT)
        act = glu * jax.nn.sigmoid(SWIGLU_ALPHA * glu) * (lin + 1.0)
        y_ref[...] = _dot(act.astype(BF16), wo_scr[...]) + bo_ref[...]


def _experts(blk_expert, next_expert, n_used, xs, w_in, b_glu, b_lin, w_out, b_out):
    rows = xs.shape[0]
    n_blocks = rows // EXPERT_BLOCK
    wsel = lambda i, be, nxt, nu: (be[i], 0, 0)
    grid_spec = pltpu.PrefetchScalarGridSpec(
        num_scalar_prefetch=3,
        grid=(n_blocks,),
        in_specs=[
            pl.BlockSpec((EXPERT_BLOCK, D_MODEL), lambda i, be, nxt, nu: (jnp.minimum(i, nu[0] - 1), 0)),
            pl.BlockSpec((None, 1, D_EXPERT), wsel),
            pl.BlockSpec((None, 1, D_EXPERT), wsel),
            pl.BlockSpec((None, 1, D_MODEL), wsel),
            pl.BlockSpec(memory_space=pl.ANY),
            pl.BlockSpec(memory_space=pl.ANY),
        ],
        out_specs=pl.BlockSpec((EXPERT_BLOCK, D_MODEL), lambda i, be, nxt, nu: (i, 0)),
        scratch_shapes=[pltpu.VMEM((D_MODEL, 2 * D_EXPERT), F32),
                        pltpu.VMEM((D_EXPERT, D_MODEL), F32),
                        pltpu.VMEM((D_MODEL, D_EXPERT), BF16),
                        pltpu.VMEM((D_MODEL, D_EXPERT), BF16),
                        pltpu.VMEM((D_EXPERT, D_MODEL), BF16),
                        pltpu.SemaphoreType.DMA, pltpu.SemaphoreType.DMA],
    )
    return pl.pallas_call(
        _experts_kernel,
        out_shape=jax.ShapeDtypeStruct((rows, D_MODEL), F32),
        grid_spec=grid_spec,
        compiler_params=_params(("arbitrary",)),
        name="experts",
    )(blk_expert, next_expert, n_used, xs, b_glu, b_lin, b_out, w_in, w_out)


def _split_bf16(a):
    hi = a.astype(BF16)
    return hi, (a - hi.astype(F32)).astype(BF16)


def _combine_kernel(gs_ref, off_ref, cnt_ref, gate_ref, lpos_ref, x1_ref, g2_ref, b2_ref, yb_ref, o_ref,
                    yloc, sem, *, tm, dn_alpha):
    tile = pl.program_id(0)
    slot = tile % 2
    tabs = (gs_ref, off_ref, cnt_ref)

    def copies(t, s, fn):
        _for_each_run_chunk(t, tabs, yloc.at[s], yb_ref, sem.at[s], False, fn)

    @pl.when(tile == 0)
    def _():
        yloc[...] = jnp.zeros(yloc.shape, yloc.dtype)
        copies(tile, slot, lambda cp: cp.start())

    @pl.when(tile + 1 < pl.num_programs(0))
    def _():
        copies(tile + 1, 1 - slot, lambda cp: cp.start())

    copies(tile, slot, lambda cp: cp.wait())

    gate, lpos = gate_ref[...], lpos_ref[...]
    col = lax.broadcasted_iota(jnp.int32, (tm, _local_rows(tm)), 1)
    weights = jnp.zeros(col.shape, F32)
    for k in range(TOP_K):
        weights = jnp.where(col == lpos[:, k:k + 1], gate[:, k:k + 1], weights)
    w_hi, w_lo = _split_bf16(weights)
    y_hi, y_lo = _split_bf16(yloc[slot])
    y = _dot(w_hi, y_hi) + (_dot(w_hi, y_lo) + _dot(w_lo, y_hi))
    o_ref[...] = _layer_norm(dn_alpha * x1_ref[...] + y, g2_ref[...], b2_ref[...])


def _combine(tabs, gate, lpos, x1, g2, b2, yb, *, tm, dn_alpha):
    n = x1.shape[0]
    grid_spec = pltpu.PrefetchScalarGridSpec(
        num_scalar_prefetch=3,
        grid=(n // tm,),
        in_specs=[
            pl.BlockSpec((tm, LANES), lambda i, *_: (i, 0)),
            pl.BlockSpec((tm, LANES), lambda i, *_: (i, 0)),
            pl.BlockSpec((tm, D_MODEL), lambda i, *_: (i, 0)),
            pl.BlockSpec((1, D_MODEL), lambda i, *_: (0, 0)),
            pl.BlockSpec((1, D_MODEL), lambda i, *_: (0, 0)),
            pl.BlockSpec(memory_space=pl.ANY),
        ],
        out_specs=pl.BlockSpec((tm, D_MODEL), lambda i, *_: (i, 0)),
        scratch_shapes=[pltpu.VMEM((2, _local_rows(tm), D_MODEL), F32), pltpu.SemaphoreType.DMA((2,))],
    )
    return pl.pallas_call(
        functools.partial(_combine_kernel, tm=tm, dn_alpha=dn_alpha),
        out_shape=jax.ShapeDtypeStruct((n, D_MODEL), F32),
        grid_spec=grid_spec,
        compiler_params=_params(("arbitrary",)),
        name="combine",
    )(*tabs, gate, lpos, x1, g2, b2, yb)


def _position_tables(pos0, seq):
    pos = pos0 + jnp.arange(seq, dtype=jnp.int32)
    inv = ROPE_THETA ** (-jnp.arange(HALF_DIM, dtype=F32) / HALF_DIM)
    ang = pos.astype(F32)[:, None] * inv[None, :]
    cos, sin = jnp.cos(ang), jnp.sin(ang)
    cos_rows = jnp.concatenate([cos, cos, cos, cos], axis=-1)
    sin_rows = jnp.concatenate([-sin, sin, -sin, sin], axis=-1)
    icnt = jnp.concatenate(
        [jnp.broadcast_to((1.0 / jnp.minimum(pos + 1, w).astype(F32))[:, None], (seq, POOL_GROUP_DIM))
         for w in POOL_WINDOWS], axis=-1)
    return cos_rows, sin_rows, cos.T, sin.T, icnt


def _tile(n, pref):
    t = min(n, pref)
    while n % t:
        t //= 2
    return t


def kernel(x_prompt, x_sample, cache_k, cache_v, state_pool, w_in, w_pool_mix, pool_scale, w_pool_out,
           lambda_q1, lambda_k1, lambda_q2, lambda_k2, attn_norm_g, w_attn_out, w_out, ln1_g, ln1_b,
           w_router, b_router, w_expert_in, b_expert_in, w_expert_out, b_expert_out, ln2_g, ln2_b):
    depth = w_in.shape[0]
    assert depth == 1, "single-layer step"
    dn_alpha = (2.0 * depth) ** 0.25
    lam_init = 0.8 - 0.6 * math.exp(-0.3 * 0)
    bp, sp, _ = x_prompt.shape
    bs, ss, _ = x_sample.shape
    past = cache_k.shape[2]
    np_, ns = bp * sp, bs * ss

    c_q, c_k, c_v = POOL_DIM, POOL_DIM + QK_DIM, POOL_DIM + 2 * QK_DIM
    c_gate = c_v + ATTN_V_WIDTH
    w0 = w_in[0]
    w_pqv = jnp.concatenate([w0[:, :c_k], w0[:, c_v:c_gate]], axis=1).astype(BF16)
    w_k = w0[:, c_k:c_v].astype(BF16)
    w_gate = w0[:, c_gate:].astype(BF16)
    wmix = w_pool_mix[0].astype(BF16)
    pscale = pool_scale[0].reshape(1, POOL_DIM)
    wpo = w_pool_out[0].astype(BF16)
    wao = w_attn_out[0].astype(BF16)
    wout = w_out[0].astype(BF16)
    wr = w_router[0].astype(BF16)
    br = b_router[0].reshape(1, N_EXPERTS)
    lam_vecs = jnp.stack([lambda_q1[0], lambda_k1[0], lambda_q2[0], lambda_k2[0]])
    norm_g = attn_norm_g[0].reshape(1, V_DIM)
    g1, b1 = ln1_g[0].reshape(1, D_MODEL), ln1_b[0].reshape(1, D_MODEL)
    g2, b2 = ln2_g[0].reshape(1, D_MODEL), ln2_b[0].reshape(1, D_MODEL)
    b_glu = b_expert_in[0][:, 0::2].reshape(N_EXPERTS, 1, D_EXPERT)
    b_lin = b_expert_in[0][:, 1::2].reshape(N_EXPERTS, 1, D_EXPERT)
    b_eo = b_expert_out[0].reshape(N_EXPERTS, 1, D_MODEL)

    xp = x_prompt.reshape(np_, D_MODEL)
    cos_p, sin_p, cost_p, sint_p, icnt_p = _position_tables(0, sp)
    hist_p = jnp.zeros((bp, HIST_ROWS, POOL_DIM), F32)
    q_p, kt_p, ktb_p, v_p, vb_p, py_p, pnew_p = _inproj_prompt(
        xp, w_pqv, w_k.T, cos_p, sin_p, cost_p, sint_p, icnt_p, hist_p, wmix, pscale,
        n_streams=bp, seq=sp, tm=_tile(sp, DENSE_TILE))
    ay_p = _attn_prompt(lam_vecs, norm_g, q_p, ktb_p, vb_p, n_streams=bp, seq=sp,
                        tq=_tile(sp, ATTN_Q_TILE), lam_init=lam_init)
    tm_p, tm_s = _tile(np_, ROUTE_TILE), _tile(ns, ROUTE_TILE)
    x1_p, gate_p, lpos_p, cnt_p = _postmix(
        xp, py_p, ay_p, w_gate, wpo, wao, wout, g1, b1, wr, br,
        tm=max(tm_p, _tile(np_, DENSE_TILE)), rt=tm_p, dn_alpha=dn_alpha)

    xs_ = x_sample.reshape(ns, D_MODEL)
    cos_s, sin_s, _, _, icnt_s = _position_tables(past, ss)
    hist_s = jnp.concatenate([jnp.zeros((bs, 1, POOL_DIM), F32), state_pool[0]], axis=1)
    q_s, k_s, v_s, py_s, pnew_s = _inproj_sample(
        xs_, w_pqv, w_k, cos_s, sin_s, icnt_s, hist_s, wmix, pscale, n_streams=bs, seq=ss)
    kct = jnp.transpose(cache_k[0], (0, 2, 3, 4, 1)).reshape(bs, QK_DIM, past)
    vc = cache_v[0].reshape(bs, past * N_HEADS, V_DIM)
    ay_s = _attn_sample(lam_vecs, norm_g, q_s, kct, vc, k_s, v_s, n_streams=bs, tn=ss, past=past,
                        tk=_tile(past, 512), lam_init=lam_init)
    x1_s, gate_s, lpos_s, cnt_s = _postmix(
        xs_, py_s, ay_s, w_gate, wpo, wao, wout, g1, b1, wr, br, tm=tm_s, rt=tm_s, dn_alpha=dn_alpha)

    ntp = np_ // tm_p
    cnt = jnp.concatenate([cnt_p[:, 0, :], cnt_s[:, 0, :]], axis=0).astype(jnp.int32)
    n_tiles = cnt.shape[0]
    group = jnp.sum(cnt, axis=0)
    padded = (group + EXPERT_BLOCK - 1) // EXPERT_BLOCK * EXPERT_BLOCK
    pad_end = jnp.cumsum(padded).astype(jnp.int32)
    run_start = (pad_end - padded)[None, :] + jnp.cumsum(cnt, axis=0) - cnt
    run_off = jnp.cumsum(cnt, axis=1) - cnt
    tail = jnp.where(padded > 0, pad_end - EXPERT_BLOCK, -1).astype(jnp.int32)
    max_rows = (np_ + ns) * TOP_K + n_tiles * N_EXPERTS * (RUN_ALIGN - 1) + N_EXPERTS * (EXPERT_BLOCK - 1)
    n_blocks = -(-max_rows // EXPERT_BLOCK)
    n_used = pad_end[-1:] // EXPERT_BLOCK
    blk_start = jnp.arange(n_blocks, dtype=jnp.int32) * EXPERT_BLOCK
    blk_expert = jnp.minimum(jnp.sum((blk_start[:, None] >= pad_end[None, :]).astype(jnp.int32), axis=1),
                             N_EXPERTS - 1)
    tabs_p = tuple(a[:ntp].reshape(-1).astype(jnp.int32) for a in (run_start, run_off, cnt))
    tabs_s = tuple(a[ntp:].reshape(-1).astype(jnp.int32) for a in (run_start, run_off, cnt))

    xsorted = _dispatch_first(tail, n_used, tabs_p, lpos_p, x1_p, tm=tm_p, n_blocks=n_blocks)
    xsorted = _dispatch_next(tabs_s, lpos_s, x1_s, xsorted, tm=tm_s)
    blk = jnp.arange(n_blocks, dtype=jnp.int32)
    later = (blk[None, :] > blk[:, None]) & (blk_expert[None, :] != blk_expert[:, None]) & (blk[None, :] < n_used)
    next_expert = jnp.where(jnp.any(later, axis=1), blk_expert[jnp.argmax(later, axis=1)], -1).astype(jnp.int32)
    yb = _experts(blk_expert, next_expert, n_used, xsorted, w_expert_in[0], b_glu, b_lin, w_expert_out[0], b_eo)
    y_p = _combine(tabs_p, gate_p, lpos_p, x1_p, g2, b2, yb, tm=tm_p, dn_alpha=dn_alpha)
    y_s = _combine(tabs_s, gate_s, lpos_s, x1_s, g2, b2, yb, tm=tm_s, dn_alpha=dn_alpha)

    k_prompt = jnp.transpose(kt_p.reshape(bp, N_HEADS, 2, HEAD_DIM, sp), (0, 4, 1, 2, 3))
    return (
        y_p.reshape(bp, sp, D_MODEL),
        y_s.reshape(bs, ss, D_MODEL),
        k_prompt[None],
        v_p.reshape(1, bp, sp, N_HEADS, V_DIM),
        pnew_p[:, 1:].reshape(1, bp, POOL_HIST, POOL_DIM),
        k_s.reshape(1, bs, ss, N_HEADS, 2, HEAD_DIM),
        v_s.reshape(1, bs, ss, N_HEADS, V_DIM),
        pnew_s[:, 1:].reshape(1, bs, POOL_HIST, POOL_DIM),
    )
```

```python
import functools
import math

import jax
import jax.numpy as jnp
from jax import lax
from jax.experimental import pallas as pl
from jax.experimental.pallas import tpu as pltpu

D_MODEL = 1024
CHUNK = 64
POOL_WINDOWS = (2, 4, 8, 16)
POOL_GROUP_DIM = 128
POOL_DIM = len(POOL_WINDOWS) * POOL_GROUP_DIM
POOL_HIST = max(POOL_WINDOWS) - 1
HIST_ROWS = POOL_HIST + 1
N_HEADS = 8
HEAD_DIM = 64
HALF_DIM = HEAD_DIM // 2
V_DIM = 2 * HEAD_DIM
QK_DIM = N_HEADS * 2 * HEAD_DIM
ATTN_V_WIDTH = N_HEADS * V_DIM
ATTN_SCALE = HEAD_DIM ** -0.5
LOG2_E = math.log2(math.e)
ROPE_THETA = 10000.0
SUBLN_EPS = 1e-5
N_EXPERTS = 32
TOP_K = 4
D_EXPERT = 1024
SWIGLU_LIMIT = 7.0
SWIGLU_ALPHA = 1.702
LN_EPS = 1e-5
NEG_INF = -1e30
LANES = 128
MXU_DIM = 256

F32 = jnp.float32
BF16 = jnp.bfloat16

VMEM_LIMIT = 56 * 1024 * 1024
EXPERT_BLOCK = 256
RUN_ALIGN = 8
ROUTE_TILE = 256
DENSE_TILE = 512
ATTN_Q_TILE = 256
RUN_BITS = tuple(range(3, 9))


def _dot(a, b):
    return jnp.dot(a, b, preferred_element_type=F32)


def _dot_nt(a, b):
    return lax.dot_general(a, b, (((1,), (1,)), ((), ())), preferred_element_type=F32)


def _params(semantics):
    return pltpu.CompilerParams(dimension_semantics=semantics, vmem_limit_bytes=VMEM_LIMIT)


def _pool_branch(x, w_ref, icnt_ref, hist_ref, wmix_ref, pscale_ref, py_ref, pnew_ref, ext_ref, *, bb, tm):
    u = _dot(x, w_ref[:, 0:POOL_DIM])

    @pl.when(pl.program_id(1) == 0)
    def _():
        ext_ref[:, 0:HIST_ROWS, :] = hist_ref[...]

    for b in range(bb):
        ext_ref[b, HIST_ROWS:HIST_ROWS + tm, :] = u[b * tm:(b + 1) * tm]
    for b in range(bb):
        for g, w in enumerate(POOL_WINDOWS):
            cols = slice(g * POOL_GROUP_DIM, (g + 1) * POOL_GROUP_DIM)
            cur = ext_ref[b, HIST_ROWS:HIST_ROWS + tm, cols]
            acc = cur
            for j in range(1, w):
                acc = acc + ext_ref[b, HIST_ROWS - j:HIST_ROWS - j + tm, cols]
            d = acc * icnt_ref[:, cols] - cur
            y = _dot(d.astype(BF16), wmix_ref[g]) * pscale_ref[:, cols]
            py_ref[b * tm:(b + 1) * tm, cols] = y.astype(BF16)
    tail = ext_ref[:, tm:tm + HIST_ROWS, :]
    pnew_ref[...] = tail
    ext_ref[:, 0:HIST_ROWS, :] = tail


def _rope_rows(z, cos, sin):
    lane = lax.broadcasted_iota(jnp.int32, z.shape, 1)
    first_half = (lane % HEAD_DIM) < HALF_DIM
    partner = jnp.where(first_half, pltpu.roll(z, LANES - HALF_DIM, 1), pltpu.roll(z, HALF_DIM, 1))
    return z * cos + partner * sin


def _inproj_prompt_kernel(x_ref, w_ref, wkt_ref, cos_ref, sin_ref, cost_ref, sint_ref, icnt_ref, hist_ref,
                          wmix_ref, pscale_ref, q_ref, kt_ref, ktb_ref, v_ref, vb_ref, py_ref, pnew_ref,
                          ext_ref, *, tm):
    x = x_ref[...].astype(BF16)
    _pool_branch(x, w_ref, icnt_ref, hist_ref, wmix_ref, pscale_ref, py_ref, pnew_ref, ext_ref, bb=1, tm=tm)

    cos, sin = cos_ref[...], sin_ref[...]
    hq = _dot(x, w_ref[:, POOL_DIM:POOL_DIM + QK_DIM])
    for h in range(N_HEADS):
        sl = slice(h * V_DIM, (h + 1) * V_DIM)
        q_ref[:, sl] = (_rope_rows(hq[:, sl], cos, sin) * (ATTN_SCALE * LOG2_E)).astype(BF16)

    hkt = _dot_nt(wkt_ref[...], x)
    cost, sint = cost_ref[...], sint_ref[...]
    for hc in range(2 * N_HEADS):
        r0 = hc * HEAD_DIM
        x1 = hkt[r0:r0 + HALF_DIM]
        x2 = hkt[r0 + HALF_DIM:r0 + HEAD_DIM]
        o1 = x1 * cost - x2 * sint
        o2 = x2 * cost + x1 * sint
        kt_ref[r0:r0 + HALF_DIM, :] = o1
        kt_ref[r0 + HALF_DIM:r0 + HEAD_DIM, :] = o2
        ktb_ref[r0:r0 + HALF_DIM, :] = o1.astype(BF16)
        ktb_ref[r0 + HALF_DIM:r0 + HEAD_DIM, :] = o2.astype(BF16)

    hv = _dot(x, w_ref[:, POOL_DIM + QK_DIM:POOL_DIM + QK_DIM + ATTN_V_WIDTH])
    vb_ref[...] = hv.astype(BF16)
    for h in range(N_HEADS):
        v_ref[pl.ds(h, tm, stride=N_HEADS), :] = hv[:, h * V_DIM:(h + 1) * V_DIM]


def _inproj_prompt(x2d, w_pqv, wkt, cos, sin, cost, sint, icnt, hist, wmix, pscale, *, n_streams, seq, tm):
    n = n_streams * seq
    nt = seq // tm
    row_map = lambda b, t: (b * nt + t, 0)
    const2 = lambda b, t: (0, 0)
    out_shape = (
        jax.ShapeDtypeStruct((n, QK_DIM), BF16),
        jax.ShapeDtypeStruct((n_streams, QK_DIM, seq), F32),
        jax.ShapeDtypeStruct((n_streams, QK_DIM, seq), BF16),
        jax.ShapeDtypeStruct((n_streams, seq * N_HEADS, V_DIM), F32),
        jax.ShapeDtypeStruct((n, ATTN_V_WIDTH), BF16),
        jax.ShapeDtypeStruct((n, POOL_DIM), BF16),
        jax.ShapeDtypeStruct((n_streams, HIST_ROWS, POOL_DIM), F32),
    )
    return pl.pallas_call(
        functools.partial(_inproj_prompt_kernel, tm=tm),
        out_shape=out_shape,
        grid=(n_streams, nt),
        in_specs=[
            pl.BlockSpec((tm, D_MODEL), row_map),
            pl.BlockSpec(w_pqv.shape, const2),
            pl.BlockSpec(wkt.shape, const2),
            pl.BlockSpec((tm, LANES), lambda b, t: (t, 0)),
            pl.BlockSpec((tm, LANES), lambda b, t: (t, 0)),
            pl.BlockSpec((HALF_DIM, tm), lambda b, t: (0, t)),
            pl.BlockSpec((HALF_DIM, tm), lambda b, t: (0, t)),
            pl.BlockSpec((tm, POOL_DIM), lambda b, t: (t, 0)),
            pl.BlockSpec((1, HIST_ROWS, POOL_DIM), lambda b, t: (b, 0, 0)),
            pl.BlockSpec((len(POOL_WINDOWS), POOL_GROUP_DIM, POOL_GROUP_DIM), lambda b, t: (0, 0, 0)),
            pl.BlockSpec((1, POOL_DIM), const2),
        ],
        out_specs=(
            pl.BlockSpec((tm, QK_DIM), row_map),
            pl.BlockSpec((None, QK_DIM, tm), lambda b, t: (b, 0, t)),
            pl.BlockSpec((None, QK_DIM, tm), lambda b, t: (b, 0, t)),
            pl.BlockSpec((None, tm * N_HEADS, V_DIM), lambda b, t: (b, t, 0)),
            pl.BlockSpec((tm, ATTN_V_WIDTH), row_map),
            pl.BlockSpec((tm, POOL_DIM), row_map),
            pl.BlockSpec((1, HIST_ROWS, POOL_DIM), lambda b, t: (b, 0, 0)),
        ),
        scratch_shapes=[pltpu.VMEM((1, HIST_ROWS + tm, POOL_DIM), F32)],
        compiler_params=_params(("arbitrary", "arbitrary")),
        name="inproj_prompt",
    )(x2d, w_pqv, wkt, cos, sin, cost, sint, icnt, hist, wmix, pscale)


def _inproj_sample_kernel(x_ref, w_ref, wk_ref, cos_ref, sin_ref, icnt_ref, hist_ref, wmix_ref, pscale_ref,
                          q_ref, k_ref, v_ref, py_ref, pnew_ref, ext_ref, *, bb, tm):
    x = x_ref[...].astype(BF16)
    _pool_branch(x, w_ref, icnt_ref, hist_ref, wmix_ref, pscale_ref, py_ref, pnew_ref, ext_ref, bb=bb, tm=tm)
    cos = jnp.concatenate([cos_ref[...]] * bb, axis=0)
    sin = jnp.concatenate([sin_ref[...]] * bb, axis=0)
    hq = _dot(x, w_ref[:, POOL_DIM:POOL_DIM + QK_DIM])
    hk = _dot(x, wk_ref[...])
    for h in range(N_HEADS):
        sl = slice(h * V_DIM, (h + 1) * V_DIM)
        q_ref[:, sl] = (_rope_rows(hq[:, sl], cos, sin) * ATTN_SCALE).astype(BF16)
        k_ref[:, sl] = _rope_rows(hk[:, sl], cos, sin)
    v_ref[...] = _dot(x, w_ref[:, POOL_DIM + QK_DIM:POOL_DIM + QK_DIM + ATTN_V_WIDTH])


def _inproj_sample(x2d, w_pqv, wk, cos, sin, icnt, hist, wmix, pscale, *, n_streams, seq):
    n = n_streams * seq
    const2 = lambda i, t: (0, 0)
    const3 = lambda i, t: (0, 0, 0)
    out_shape = (
        jax.ShapeDtypeStruct((n, QK_DIM), BF16),
        jax.ShapeDtypeStruct((n, QK_DIM), F32),
        jax.ShapeDtypeStruct((n, ATTN_V_WIDTH), F32),
        jax.ShapeDtypeStruct((n, POOL_DIM), BF16),
        jax.ShapeDtypeStruct((n_streams, HIST_ROWS, POOL_DIM), F32),
    )
    return pl.pallas_call(
        functools.partial(_inproj_sample_kernel, bb=n_streams, tm=seq),
        out_shape=out_shape,
        grid=(1, 1),
        in_specs=[
            pl.BlockSpec((n, D_MODEL), const2),
            pl.BlockSpec(w_pqv.shape, const2),
            pl.BlockSpec(wk.shape, const2),
            pl.BlockSpec((seq, LANES), const2),
            pl.BlockSpec((seq, LANES), const2),
            pl.BlockSpec((seq, POOL_DIM), const2),
            pl.BlockSpec((n_streams, HIST_ROWS, POOL_DIM), const3),
            pl.BlockSpec((len(POOL_WINDOWS), POOL_GROUP_DIM, POOL_GROUP_DIM), const3),
            pl.BlockSpec((1, POOL_DIM), const2),
        ],
        out_specs=(
            pl.BlockSpec((n, QK_DIM), const2),
            pl.BlockSpec((n, QK_DIM), const2),
            pl.BlockSpec((n, ATTN_V_WIDTH), const2),
            pl.BlockSpec((n, POOL_DIM), const2),
            pl.BlockSpec((n_streams, HIST_ROWS, POOL_DIM), const3),
        ),
        scratch_shapes=[pltpu.VMEM((n_streams, HIST_ROWS + seq, POOL_DIM), F32)],
        compiler_params=_params(("arbitrary", "arbitrary")),
        name="inproj_sample",
    )(x2d, w_pqv, wk, cos, sin, icnt, hist, wmix, pscale)


def _lambda_value(lam_ref, lam_init):
    lv = lam_ref[...]
    s1 = jnp.sum(lv[0:1] * lv[1:2], axis=1, keepdims=True)
    s2 = jnp.sum(lv[2:3] * lv[3:4], axis=1, keepdims=True)
    return jnp.exp(s1) - jnp.exp(s2) + lam_init


def _head_norm(o, g, lam_init):
    ms = jnp.mean(o * o, axis=-1, keepdims=True)
    return o * lax.rsqrt(ms + SUBLN_EPS) * g * (1.0 - lam_init)


def _attn_prompt_kernel(lam_ref, g_ref, q_ref, kt_ref, v_ref, o_ref, vext, *, seq, tq, lam_init):
    lam = _lambda_value(lam_ref, lam_init)
    g = g_ref[...]
    r = lax.broadcasted_iota(jnp.int32, (tq, tq), 0)
    c = lax.broadcasted_iota(jnp.int32, (tq, tq), 1)
    diag_visible = (c // CHUNK) <= (r // CHUNK)
    lane = lax.broadcasted_iota(jnp.int32, (tq, V_DIM), 1)
    vext[:, 0:V_DIM] = v_ref[...]
    vlane = lax.broadcasted_iota(jnp.int32, (seq, V_DIM), 1)
    vext[:, V_DIM:] = jnp.where(vlane == 0, 1.0, 0.0).astype(vext.dtype)

    for i in range(seq // tq):
        lo = i * tq
        q = q_ref[lo:lo + tq, :]
        zero = jnp.zeros_like(q)
        qc = (jnp.where(lane < HEAD_DIM, q, zero), jnp.where(lane >= HEAD_DIM, q, zero))
        normed = []
        for k in range(2):
            sd = jnp.where(diag_visible, _dot(qc[k], kt_ref[:, lo:lo + tq]), NEG_INF)
            m = jnp.max(sd, axis=1, keepdims=True)
            if i > 0:
                sp = _dot(qc[k], kt_ref[:, 0:lo])
                m = jnp.maximum(m, jnp.max(sp, axis=1, keepdims=True))
            acc = _dot(jnp.exp2(sd - m).astype(BF16), vext[lo:lo + tq, :])
            if i > 0:
                acc = acc + _dot(jnp.exp2(sp - m).astype(BF16), vext[0:lo, :])
            normed.append(acc[:, 0:V_DIM] / acc[:, V_DIM:V_DIM + 1])
        o = normed[0] - lam * normed[1]
        o_ref[lo:lo + tq, :] = _head_norm(o, g, lam_init).astype(o_ref.dtype)


def _attn_prompt(lam_vecs, norm_g, q, ktb, vb, *, n_streams, seq, tq, lam_init):
    return pl.pallas_call(
        functools.partial(_attn_prompt_kernel, seq=seq, tq=tq, lam_init=lam_init),
        out_shape=jax.ShapeDtypeStruct((n_streams * seq, ATTN_V_WIDTH), BF16),
        grid=(n_streams, N_HEADS),
        in_specs=[
            pl.BlockSpec((4, HEAD_DIM), lambda b, h: (0, 0)),
            pl.BlockSpec((1, V_DIM), lambda b, h: (0, 0)),
            pl.BlockSpec((seq, V_DIM), lambda b, h: (b, h)),
            pl.BlockSpec((None, V_DIM, seq), lambda b, h: (b, h, 0)),
            pl.BlockSpec((seq, V_DIM), lambda b, h: (b, h)),
        ],
        out_specs=pl.BlockSpec((seq, V_DIM), lambda b, h: (b, h)),
        scratch_shapes=[pltpu.VMEM((seq, MXU_DIM), BF16)],
        compiler_params=_params(("arbitrary", "arbitrary")),
        name="attn_prompt",
    )(lam_vecs, norm_g, q, ktb, vb)


def _attn_sample_kernel(lam_ref, g_ref, q_ref, kc_ref, vc_ref, kn_ref, vn_ref, o_ref,
                        s_scr, w_scr, wn_scr, m_scr, acc_scr, qbd_scr, *, nk, tn, past, lam_init):
    j = pl.program_id(1)
    half = N_HEADS * tn

    @pl.when(j == 0)
    def _():
        q = q_ref[...]
        qt = jnp.concatenate([q] * (2 * N_HEADS), axis=0)
        r = lax.broadcasted_iota(jnp.int32, qt.shape, 0)
        l = lax.broadcasted_iota(jnp.int32, qt.shape, 1)
        keep = ((r // half) == ((l % V_DIM) // HEAD_DIM)) & (((r % half) // tn) == (l // V_DIM))
        qbd_scr[...] = jnp.where(keep, qt, jnp.zeros_like(qt))
        m_scr[...] = jnp.full(m_scr.shape, NEG_INF, F32)
        acc_scr[...] = jnp.zeros(acc_scr.shape, F32)

    @pl.when(j < nk)
    def _():
        s = _dot(qbd_scr[...], kc_ref[...].astype(BF16))
        s_scr[j] = s
        m_scr[...] = jnp.maximum(m_scr[...], jnp.max(s, axis=1, keepdims=True))

    @pl.when(j == nk - 1)
    def _():
        lam = _lambda_value(lam_ref, lam_init)
        sn = _dot_nt(qbd_scr[...], kn_ref[...].astype(BF16))
        qpos = past + (lax.broadcasted_iota(jnp.int32, sn.shape, 0) % tn)
        kpos = past + lax.broadcasted_iota(jnp.int32, sn.shape, 1)
        sn = jnp.where((kpos // CHUNK) <= (qpos // CHUNK), sn, NEG_INF)
        m = jnp.maximum(m_scr[...], jnp.max(sn, axis=1, keepdims=True))
        pn = jnp.exp(sn - m)
        l = jnp.sum(pn, axis=1, keepdims=True)
        for c in range(nk):
            p = jnp.exp(s_scr[c] - m)
            s_scr[c] = p
            l = l + jnp.sum(p, axis=1, keepdims=True)
        r0 = 1.0 / l[:half]
        r1 = lam / l[half:]
        wn_scr[...] = pn[:half] * r0 - pn[half:] * r1
        for c in range(nk):
            p = s_scr[c]
            w_scr[c] = (p[:half] * r0 - p[half:] * r1).astype(BF16)

    def v_rows(ref):
        tk = ref.shape[0] // N_HEADS
        return jnp.concatenate([ref[pl.ds(h, tk, stride=N_HEADS), :] for h in range(N_HEADS)],
                               axis=1).astype(BF16)

    @pl.when(j >= nk)
    def _():
        acc_scr[...] += _dot(w_scr[j - nk], v_rows(vc_ref))

    @pl.when(j == 2 * nk - 1)
    def _():
        acc = acc_scr[...] + _dot(wn_scr[...].astype(BF16), vn_ref[...].astype(BF16))
        g = g_ref[...]
        for h in range(N_HEADS):
            o = acc[h * tn:(h + 1) * tn, h * V_DIM:(h + 1) * V_DIM]
            o_ref[:, h * V_DIM:(h + 1) * V_DIM] = _head_norm(o, g, lam_init).astype(o_ref.dtype)


def _attn_sample(lam_vecs, norm_g, q, kct, vc, kn, vn, *, n_streams, tn, past, tk, lam_init):
    nk = past // tk
    rows = 2 * N_HEADS * tn
    half = N_HEADS * tn
    return pl.pallas_call(
        functools.partial(_attn_sample_kernel, nk=nk, tn=tn, past=past, lam_init=lam_init),
        out_shape=jax.ShapeDtypeStruct((n_streams * tn, ATTN_V_WIDTH), BF16),
        grid=(n_streams, 2 * nk),
        in_specs=[
            pl.BlockSpec((4, HEAD_DIM), lambda b, j: (0, 0)),
            pl.BlockSpec((1, V_DIM), lambda b, j: (0, 0)),
            pl.BlockSpec((tn, QK_DIM), lambda b, j: (b, 0)),
            pl.BlockSpec((None, QK_DIM, tk), lambda b, j: (b, 0, jnp.minimum(j, nk - 1))),
            pl.BlockSpec((None, tk * N_HEADS, V_DIM), lambda b, j: (b, jnp.maximum(j - nk, 0), 0)),
            pl.BlockSpec((tn, QK_DIM), lambda b, j: (b, 0)),
            pl.BlockSpec((tn, ATTN_V_WIDTH), lambda b, j: (b, 0)),
        ],
        out_specs=pl.BlockSpec((tn, ATTN_V_WIDTH), lambda b, j: (b, 0)),
        scratch_shapes=[
            pltpu.VMEM((nk, rows, tk), F32),
            pltpu.VMEM((nk, half, tk), BF16),
            pltpu.VMEM((half, tn), F32),
            pltpu.VMEM((rows, 1), F32),
            pltpu.VMEM((half, ATTN_V_WIDTH), F32),
            pltpu.VMEM((rows, QK_DIM), BF16),
        ],
        compiler_params=_params(("arbitrary", "arbitrary")),
        name="attn_sample",
    )(lam_vecs, norm_g, q, kct, vc, kn, vn)


def _layer_norm(z, g, b):
    mu = jnp.mean(z, axis=-1, keepdims=True)
    zc = z - mu
    var = jnp.mean(zc * zc, axis=-1, keepdims=True)
    return zc * lax.rsqrt(var + LN_EPS) * g + b


def _postmix_kernel(x_ref, py_ref, ay_ref, wg_ref, wpo_ref, wao_ref, wout_ref, g1_ref, b1_ref,
                    wr_ref, br_ref, x1_ref, gate_ref, lpos_ref, cnt_ref, *, tm, rt, dn_alpha):
    x = x_ref[...]
    xb = x.astype(BF16)
    gates = jax.nn.sigmoid(_dot(xb, wg_ref[...]))
    a = _dot(py_ref[...], wpo_ref[...])
    b = _dot(ay_ref[...], wao_ref[...])
    mixed = gates[:, :D_MODEL] * a + gates[:, D_MODEL:] * b
    mo = _dot(mixed.astype(BF16), wout_ref[...])
    x1 = _layer_norm(dn_alpha * x + mo, g1_ref[...], b1_ref[...])
    x1_ref[...] = x1

    logits = _dot(x1.astype(BF16), wr_ref[...]) + br_ref[...]
    for sub in range(tm // rt):
        rows = slice(sub * rt, (sub + 1) * rt)
        gate_ref[rows, :], lpos_ref[rows, :], cnt_ref[sub] = _route_tile(logits[rows], rt)


def _route_tile(logits, tm):
    lane = lax.broadcasted_iota(jnp.int32, logits.shape, 1)
    work = logits
    vals, idxs = [], []
    for _ in range(TOP_K):
        mx = jnp.max(work, axis=1, keepdims=True)
        ix = jnp.min(jnp.where(work == mx, lane, N_EXPERTS), axis=1, keepdims=True)
        vals.append(mx)
        idxs.append(ix)
        work = jnp.where(lane == ix, -jnp.inf, work)
    exps = [jnp.exp(v - vals[0]) for v in vals]
    denom = exps[0] + exps[1] + exps[2] + exps[3]

    onehot = jnp.zeros(logits.shape, F32)
    for ix in idxs:
        onehot = onehot + (lane == ix).astype(F32)
    r = lax.broadcasted_iota(jnp.int32, (tm, tm), 0)
    c = lax.broadcasted_iota(jnp.int32, (tm, tm), 1)
    tri = jnp.where(c < r, 1.0, 0.0).astype(BF16)
    earlier = _dot(tri, onehot.astype(BF16))
    cnt = jnp.sum(onehot, axis=0, keepdims=True)
    units = jnp.floor((cnt + (RUN_ALIGN - 1.0)) * (1.0 / RUN_ALIGN))
    er = lax.broadcasted_iota(jnp.int32, (N_EXPERTS, N_EXPERTS), 0)
    ec = lax.broadcasted_iota(jnp.int32, (N_EXPERTS, N_EXPERTS), 1)
    upper = jnp.where(er < ec, 1.0, 0.0).astype(BF16)
    run_off = _dot(jnp.broadcast_to(units, (8, N_EXPERTS)).astype(BF16), upper)[0:1] * float(RUN_ALIGN)
    pos = earlier + run_off

    lane_out = lax.broadcasted_iota(jnp.int32, (tm, LANES), 1)
    lpos_out = jnp.zeros((tm, LANES), jnp.int32)
    gate_out = jnp.zeros((tm, LANES), F32)
    for k in range(TOP_K):
        lpos_k = jnp.sum(jnp.where(lane == idxs[k], pos, 0.0), axis=1, keepdims=True).astype(jnp.int32)
        lpos_out = jnp.where(lane_out == k, lpos_k, lpos_out)
        gate_out = jnp.where(lane_out == k, exps[k] / denom, gate_out)
    return gate_out, lpos_out, units * float(RUN_ALIGN)


def _postmix(x2d, py, ay, wg, wpo, wao, wout, g1, b1, wr, br, *, tm, rt, dn_alpha):
    n = x2d.shape[0]
    row = lambda i: (i, 0)
    const = lambda i: (0, 0)
    out_shape = (
        jax.ShapeDtypeStruct((n, D_MODEL), F32),
        jax.ShapeDtypeStruct((n, LANES), F32),
        jax.ShapeDtypeStruct((n, LANES), jnp.int32),
        jax.ShapeDtypeStruct((n // rt, 1, N_EXPERTS), F32),
    )
    return pl.pallas_call(
        functools.partial(_postmix_kernel, tm=tm, rt=rt, dn_alpha=dn_alpha),
        out_shape=out_shape,
        grid=(n // tm,),
        in_specs=[
            pl.BlockSpec((tm, D_MODEL), row),
            pl.BlockSpec((tm, POOL_DIM), row),
            pl.BlockSpec((tm, ATTN_V_WIDTH), row),
            pl.BlockSpec(wg.shape, const),
            pl.BlockSpec(wpo.shape, const),
            pl.BlockSpec(wao.shape, const),
            pl.BlockSpec(wout.shape, const),
            pl.BlockSpec((1, D_MODEL), const),
            pl.BlockSpec((1, D_MODEL), const),
            pl.BlockSpec(wr.shape, const),
            pl.BlockSpec((1, N_EXPERTS), const),
        ],
        out_specs=(
            pl.BlockSpec((tm, D_MODEL), row),
            pl.BlockSpec((tm, LANES), row),
            pl.BlockSpec((tm, LANES), row),
            pl.BlockSpec((tm // rt, 1, N_EXPERTS), lambda i: (i, 0, 0)),
        ),
        compiler_params=_params(("arbitrary",)),
        name="postmix",
    )(x2d, py, ay, wg, wpo, wao, wout, g1, b1, wr, br)


def _local_rows(tm):
    return TOP_K * tm + N_EXPERTS * RUN_ALIGN


def _for_each_run_chunk(tile, tab_refs, local_buf, sorted_ref, sem, to_sorted, fn):
    gs_ref, off_ref, cnt_ref = tab_refs[:3]

    def per_expert(e, carry):
        t = tile * N_EXPERTS + e
        cnt, off, gs = cnt_ref[t], off_ref[t], gs_ref[t]
        for b in RUN_BITS:
            size = 1 << b

            @pl.when((cnt & size) != 0)
            def _():
                lower = cnt & (size - 1)
                loc = local_buf.at[pl.ds(pl.multiple_of(off + lower, RUN_ALIGN), size)]
                srt = sorted_ref.at[pl.ds(pl.multiple_of(gs + lower, RUN_ALIGN), size)]
                fn(pltpu.make_async_copy(loc, srt, sem) if to_sorted else pltpu.make_async_copy(srt, loc, sem))
        return carry

    lax.fori_loop(0, N_EXPERTS, per_expert, 0)


def _wait_run_rows(total, local_buf, sorted_ref, sem, to_sorted):
    rows = local_buf.shape[0]
    for b in range(RUN_BITS[0], rows.bit_length()):
        size = 1 << b

        @pl.when((total & size) != 0)
        def _():
            loc, srt = local_buf.at[pl.ds(0, size)], sorted_ref.at[pl.ds(0, size)]
            (pltpu.make_async_copy(loc, srt, sem) if to_sorted else pltpu.make_async_copy(srt, loc, sem)).wait()


def _dispatch_tile(tab_refs, lpos_ref, x_ref, xs_ref, xloc, sem, tm):
    lpos = lpos_ref[...]
    col = lax.broadcasted_iota(jnp.int32, (tm, _local_rows(tm)), 1)
    hit = col == lpos[:, 0:1]
    for k in range(1, TOP_K):
        hit = jnp.logical_or(hit, col == lpos[:, k:k + 1])
    perm_t = jnp.where(hit, 1.0, 0.0).astype(BF16)
    tile = pl.program_id(0)
    slot = tile % 2
    xloc[slot] = lax.dot_general(perm_t, x_ref[...].astype(BF16), (((0,), (0,)), ((), ())),
                                 preferred_element_type=F32)

    def copies(t, s, fn):
        _for_each_run_chunk(t, tab_refs, xloc.at[s], xs_ref, sem.at[s], True, fn)

    copies(tile, slot, lambda cp: cp.start())
    tot_ref = tab_refs[3]

    @pl.when(tile > 0)
    def _():
        _wait_run_rows(tot_ref[tile - 1], xloc.at[1 - slot], xs_ref, sem.at[1 - slot], True)

    @pl.when(tile == pl.num_programs(0) - 1)
    def _():
        _wait_run_rows(tot_ref[tile], xloc.at[slot], xs_ref, sem.at[slot], True)


def _dispatch_first_kernel(tail_ref, nu_ref, gs_ref, off_ref, cnt_ref, tot_ref, lpos_ref, x_ref, xs_ref,
                           xloc, zbuf, sem, zsem, *, tm, n_blocks):
    @pl.when(pl.program_id(0) == 0)
    def _():
        zbuf[...] = jnp.zeros(zbuf.shape, zbuf.dtype)

        def zero_copy(row):
            row = pl.multiple_of(row, EXPERT_BLOCK)
            return pltpu.make_async_copy(zbuf, xs_ref.at[pl.ds(row, EXPERT_BLOCK)], zsem)

        def over_blocks(fn):
            for e in range(N_EXPERTS):
                @pl.when(tail_ref[e] >= 0)
                def _():
                    fn(zero_copy(tail_ref[e]))
            lax.fori_loop(nu_ref[0], n_blocks, lambda b, c: (fn(zero_copy(b * EXPERT_BLOCK)), c)[1], 0)

        over_blocks(lambda cp: cp.start())
        over_blocks(lambda cp: cp.wait())

    _dispatch_tile((gs_ref, off_ref, cnt_ref, tot_ref), lpos_ref, x_ref, xs_ref, xloc, sem, tm)


def _dispatch_next_kernel(gs_ref, off_ref, cnt_ref, tot_ref, lpos_ref, x_ref, xs_in_ref, xs_ref, xloc, sem,
                          *, tm):
    del xs_in_ref
    _dispatch_tile((gs_ref, off_ref, cnt_ref, tot_ref), lpos_ref, x_ref, xs_ref, xloc, sem, tm)


def _dispatch_first(tail, n_used, tabs, lpos, x1, *, tm, n_blocks):
    n = x1.shape[0]
    grid_spec = pltpu.PrefetchScalarGridSpec(
        num_scalar_prefetch=6,
        grid=(n // tm,),
        in_specs=[pl.BlockSpec((tm, LANES), lambda i, *_: (i, 0)),
                  pl.BlockSpec((tm, D_MODEL), lambda i, *_: (i, 0))],
        out_specs=pl.BlockSpec(memory_space=pl.ANY),
        scratch_shapes=[pltpu.VMEM((2, _local_rows(tm), D_MODEL), F32),
                        pltpu.VMEM((EXPERT_BLOCK, D_MODEL), F32),
                        pltpu.SemaphoreType.DMA((2,)), pltpu.SemaphoreType.DMA],
    )
    return pl.pallas_call(
        functools.partial(_dispatch_first_kernel, tm=tm, n_blocks=n_blocks),
        out_shape=jax.ShapeDtypeStruct((n_blocks * EXPERT_BLOCK, D_MODEL), F32),
        grid_spec=grid_spec,
        compiler_params=_params(("arbitrary",)),
        name="dispatch_first",
    )(tail, n_used, *tabs, lpos, x1)


def _dispatch_next(tabs, lpos, x1, xs, *, tm):
    n = x1.shape[0]
    grid_spec = pltpu.PrefetchScalarGridSpec(
        num_scalar_prefetch=len(tabs),
        grid=(n // tm,),
        in_specs=[pl.BlockSpec((tm, LANES), lambda i, *_: (i, 0)),
                  pl.BlockSpec((tm, D_MODEL), lambda i, *_: (i, 0)),
                  pl.BlockSpec(memory_space=pl.ANY)],
        out_specs=pl.BlockSpec(memory_space=pl.ANY),
        scratch_shapes=[pltpu.VMEM((2, _local_rows(tm), D_MODEL), F32), pltpu.SemaphoreType.DMA((2,))],
    )
    return pl.pallas_call(
        functools.partial(_dispatch_next_kernel, tm=tm),
        out_shape=jax.ShapeDtypeStruct(xs.shape, xs.dtype),
        grid_spec=grid_spec,
        input_output_aliases={6: 0},
        compiler_params=_params(("arbitrary",)),
        name="dispatch_next",
    )(*tabs, lpos, x1, xs)


def _experts_kernel(be_ref, nxt_ref, nu_ref, xs_ref, bg_ref, bl_ref, bo_ref, win_hbm, wo_hbm, y_ref,
                    win_buf, wo_buf, wg_scr, wl_scr, wo_scr, sem_in, sem_out):
    i = pl.program_id(0)
    used = i < nu_ref[0]
    first_of_expert = jnp.logical_or(i == 0, be_ref[i] != be_ref[jnp.maximum(i - 1, 0)])

    def weight_copies(e):
        return (pltpu.make_async_copy(win_hbm.at[e], win_buf, sem_in),
                pltpu.make_async_copy(wo_hbm.at[e], wo_buf, sem_out))

    @pl.when(jnp.logical_not(used))
    def _():
        y_ref[...] = jnp.zeros(y_ref.shape, y_ref.dtype)

    @pl.when(i == 0)
    def _():
        for cp in weight_copies(be_ref[0]):
            cp.start()

    @pl.when(jnp.logical_and(used, first_of_expert))
    def _():
        for cp in weight_copies(be_ref[i]):
            cp.wait()
        r = lax.broadcasted_iota(jnp.int32, (MXU_DIM, MXU_DIM), 0)
        c = lax.broadcasted_iota(jnp.int32, (MXU_DIM, MXU_DIM), 1)
        src = jnp.where(c < LANES, 2 * c, 2 * (c - LANES) + 1)
        sel = jnp.where(r == src, 1.0, 0.0).astype(BF16)
        for gq in range(2 * D_EXPERT // MXU_DIM):
            blk = win_buf[:, gq * MXU_DIM:(gq + 1) * MXU_DIM].astype(BF16)
            d = _dot(blk, sel)
            wg_scr[:, gq * LANES:(gq + 1) * LANES] = d[:, :LANES].astype(BF16)
            wl_scr[:, gq * LANES:(gq + 1) * LANES] = d[:, LANES:].astype(BF16)
        wo_scr[...] = wo_buf[...].astype(BF16)

        @pl.when(nxt_ref[i] >= 0)
        def _():
            for cp in weight_copies(nxt_ref[i]):
                cp.start(priority=1)

    @pl.when(used)
    def _():
        xb = xs_ref[...].astype(BF16)
        glu = jnp.minimum(_dot(xb, wg_scr[...]) + bg_ref[...], SWIGLU_LIMIT)
        lin = jnp.clip(_dot(xb, wl_scr[...]) + bl_ref[...], -SWIGLU_LIMIT, SWIGLU_LIMIT)
        act = glu * jax.nn.sigmoid(SWIGLU_ALPHA * glu) * (lin + 1.0)
        y_ref[...] = _dot(act.astype(BF16), wo_scr[...]) + bo_ref[...]


def _experts(blk_expert, next_expert, n_used, xs, w_in, b_glu, b_lin, w_out, b_out):
    rows = xs.shape[0]
    n_blocks = rows // EXPERT_BLOCK
    wsel = lambda i, be, nxt, nu: (be[i], 0, 0)
    grid_spec = pltpu.PrefetchScalarGridSpec(
        num_scalar_prefetch=3,
        grid=(n_blocks,),
        in_specs=[
            pl.BlockSpec((EXPERT_BLOCK, D_MODEL), lambda i, be, nxt, nu: (jnp.minimum(i, nu[0] - 1), 0)),
            pl.BlockSpec((None, 1, D_EXPERT), wsel),
            pl.BlockSpec((None, 1, D_EXPERT), wsel),
            pl.BlockSpec((None, 1, D_MODEL), wsel),
            pl.BlockSpec(memory_space=pl.ANY),
            pl.BlockSpec(memory_space=pl.ANY),
        ],
        out_specs=pl.BlockSpec((EXPERT_BLOCK, D_MODEL), lambda i, be, nxt, nu: (i, 0)),
        scratch_shapes=[pltpu.VMEM((D_MODEL, 2 * D_EXPERT), F32),
                        pltpu.VMEM((D_EXPERT, D_MODEL), F32),
                        pltpu.VMEM((D_MODEL, D_EXPERT), BF16),
                        pltpu.VMEM((D_MODEL, D_EXPERT), BF16),
                        pltpu.VMEM((D_EXPERT, D_MODEL), BF16),
                        pltpu.SemaphoreType.DMA, pltpu.SemaphoreType.DMA],
    )
    return pl.pallas_call(
        _experts_kernel,
        out_shape=jax.ShapeDtypeStruct((rows, D_MODEL), F32),
        grid_spec=grid_spec,
        compiler_params=_params(("arbitrary",)),
        name="experts",
    )(blk_expert, next_expert, n_used, xs, b_glu, b_lin, b_out, w_in, w_out)


def _split_bf16(a):
    hi = a.astype(BF16)
    return hi, (a - hi.astype(F32)).astype(BF16)


def _combine_kernel(gs_ref, off_ref, cnt_ref, tot_ref, gate_ref, lpos_ref, x1_ref, g2_ref, b2_ref, yb_ref, o_ref,
                    yloc, sem, *, tm, dn_alpha):
    tile = pl.program_id(0)
    slot = tile % 2
    tabs = (gs_ref, off_ref, cnt_ref)

    def copies(t, s, fn):
        _for_each_run_chunk(t, tabs, yloc.at[s], yb_ref, sem.at[s], False, fn)

    @pl.when(tile == 0)
    def _():
        yloc[...] = jnp.zeros(yloc.shape, yloc.dtype)
        copies(tile, slot, lambda cp: cp.start())

    @pl.when(tile + 1 < pl.num_programs(0))
    def _():
        copies(tile + 1, 1 - slot, lambda cp: cp.start())

    _wait_run_rows(tot_ref[tile], yloc.at[slot], yb_ref, sem.at[slot], False)

    gate, lpos = gate_ref[...], lpos_ref[...]
    col = lax.broadcasted_iota(jnp.int32, (tm, _local_rows(tm)), 1)
    weights = jnp.zeros(col.shape, F32)
    for k in range(TOP_K):
        weights = jnp.where(col == lpos[:, k:k + 1], gate[:, k:k + 1], weights)
    w_hi, w_lo = _split_bf16(weights)
    y_hi, y_lo = _split_bf16(yloc[slot])
    y = _dot(w_hi, y_hi) + (_dot(w_hi, y_lo) + _dot(w_lo, y_hi))
    o_ref[...] = _layer_norm(dn_alpha * x1_ref[...] + y, g2_ref[...], b2_ref[...])


def _combine(tabs, gate, lpos, x1, g2, b2, yb, *, tm, dn_alpha):
    n = x1.shape[0]
    grid_spec = pltpu.PrefetchScalarGridSpec(
        num_scalar_prefetch=len(tabs),
        grid=(n // tm,),
        in_specs=[
            pl.BlockSpec((tm, LANES), lambda i, *_: (i, 0)),
            pl.BlockSpec((tm, LANES), lambda i, *_: (i, 0)),
            pl.BlockSpec((tm, D_MODEL), lambda i, *_: (i, 0)),
            pl.BlockSpec((1, D_MODEL), lambda i, *_: (0, 0)),
            pl.BlockSpec((1, D_MODEL), lambda i, *_: (0, 0)),
            pl.BlockSpec(memory_space=pl.ANY),
        ],
        out_specs=pl.BlockSpec((tm, D_MODEL), lambda i, *_: (i, 0)),
        scratch_shapes=[pltpu.VMEM((2, _local_rows(tm), D_MODEL), F32), pltpu.SemaphoreType.DMA((2,))],
    )
    return pl.pallas_call(
        functools.partial(_combine_kernel, tm=tm, dn_alpha=dn_alpha),
        out_shape=jax.ShapeDtypeStruct((n, D_MODEL), F32),
        grid_spec=grid_spec,
        compiler_params=_params(("arbitrary",)),
        name="combine",
    )(*tabs, gate, lpos, x1, g2, b2, yb)


def _position_tables(pos0, seq):
    pos = pos0 + jnp.arange(seq, dtype=jnp.int32)
    inv = ROPE_THETA ** (-jnp.arange(HALF_DIM, dtype=F32) / HALF_DIM)
    ang = pos.astype(F32)[:, None] * inv[None, :]
    cos, sin = jnp.cos(ang), jnp.sin(ang)
    cos_rows = jnp.concatenate([cos, cos, cos, cos], axis=-1)
    sin_rows = jnp.concatenate([-sin, sin, -sin, sin], axis=-1)
    icnt = jnp.concatenate(
        [jnp.broadcast_to((1.0 / jnp.minimum(pos + 1, w).astype(F32))[:, None], (seq, POOL_GROUP_DIM))
         for w in POOL_WINDOWS], axis=-1)
    return cos_rows, sin_rows, cos.T, sin.T, icnt


def _tile(n, pref):
    t = min(n, pref)
    while n % t:
        t //= 2
    return t


def kernel(x_prompt, x_sample, cache_k, cache_v, state_pool, w_in, w_pool_mix, pool_scale, w_pool_out,
           lambda_q1, lambda_k1, lambda_q2, lambda_k2, attn_norm_g, w_attn_out, w_out, ln1_g, ln1_b,
           w_router, b_router, w_expert_in, b_expert_in, w_expert_out, b_expert_out, ln2_g, ln2_b):
    depth = w_in.shape[0]
    assert depth == 1, "single-layer step"
    dn_alpha = (2.0 * depth) ** 0.25
    lam_init = 0.8 - 0.6 * math.exp(-0.3 * 0)
    bp, sp, _ = x_prompt.shape
    bs, ss, _ = x_sample.shape
    past = cache_k.shape[2]
    np_, ns = bp * sp, bs * ss

    c_q, c_k, c_v = POOL_DIM, POOL_DIM + QK_DIM, POOL_DIM + 2 * QK_DIM
    c_gate = c_v + ATTN_V_WIDTH
    w0 = w_in[0]
    w_pqv = jnp.concatenate([w0[:, :c_k], w0[:, c_v:c_gate]], axis=1).astype(BF16)
    w_k = w0[:, c_k:c_v].astype(BF16)
    w_gate = w0[:, c_gate:].astype(BF16)
    wmix = w_pool_mix[0].astype(BF16)
    pscale = pool_scale[0].reshape(1, POOL_DIM)
    wpo = w_pool_out[0].astype(BF16)
    wao = w_attn_out[0].astype(BF16)
    wout = w_out[0].astype(BF16)
    wr = w_router[0].astype(BF16)
    br = b_router[0].reshape(1, N_EXPERTS)
    lam_vecs = jnp.stack([lambda_q1[0], lambda_k1[0], lambda_q2[0], lambda_k2[0]])
    norm_g = attn_norm_g[0].reshape(1, V_DIM)
    g1, b1 = ln1_g[0].reshape(1, D_MODEL), ln1_b[0].reshape(1, D_MODEL)
    g2, b2 = ln2_g[0].reshape(1, D_MODEL), ln2_b[0].reshape(1, D_MODEL)
    b_glu = b_expert_in[0][:, 0::2].reshape(N_EXPERTS, 1, D_EXPERT)
    b_lin = b_expert_in[0][:, 1::2].reshape(N_EXPERTS, 1, D_EXPERT)
    b_eo = b_expert_out[0].reshape(N_EXPERTS, 1, D_MODEL)

    xp = x_prompt.reshape(np_, D_MODEL)
    cos_p, sin_p, cost_p, sint_p, icnt_p = _position_tables(0, sp)
    hist_p = jnp.zeros((bp, HIST_ROWS, POOL_DIM), F32)
    q_p, kt_p, ktb_p, v_p, vb_p, py_p, pnew_p = _inproj_prompt(
        xp, w_pqv, w_k.T, cos_p, sin_p, cost_p, sint_p, icnt_p, hist_p, wmix, pscale,
        n_streams=bp, seq=sp, tm=_tile(sp, DENSE_TILE))
    ay_p = _attn_prompt(lam_vecs, norm_g, q_p, ktb_p, vb_p, n_streams=bp, seq=sp,
                        tq=_tile(sp, ATTN_Q_TILE), lam_init=lam_init)
    tm_p, tm_s = _tile(np_, ROUTE_TILE), _tile(ns, ROUTE_TILE)
    x1_p, gate_p, lpos_p, cnt_p = _postmix(
        xp, py_p, ay_p, w_gate, wpo, wao, wout, g1, b1, wr, br,
        tm=max(tm_p, _tile(np_, DENSE_TILE)), rt=tm_p, dn_alpha=dn_alpha)

    xs_ = x_sample.reshape(ns, D_MODEL)
    cos_s, sin_s, _, _, icnt_s = _position_tables(past, ss)
    hist_s = jnp.concatenate([jnp.zeros((bs, 1, POOL_DIM), F32), state_pool[0]], axis=1)
    q_s, k_s, v_s, py_s, pnew_s = _inproj_sample(
        xs_, w_pqv, w_k, cos_s, sin_s, icnt_s, hist_s, wmix, pscale, n_streams=bs, seq=ss)
    kct = jnp.transpose(cache_k[0], (0, 2, 3, 4, 1)).reshape(bs, QK_DIM, past)
    vc = cache_v[0].reshape(bs, past * N_HEADS, V_DIM)
    ay_s = _attn_sample(lam_vecs, norm_g, q_s, kct, vc, k_s, v_s, n_streams=bs, tn=ss, past=past,
                        tk=_tile(past, 1024), lam_init=lam_init)
    x1_s, gate_s, lpos_s, cnt_s = _postmix(
        xs_, py_s, ay_s, w_gate, wpo, wao, wout, g1, b1, wr, br, tm=tm_s, rt=tm_s, dn_alpha=dn_alpha)

    ntp = np_ // tm_p
    cnt = jnp.concatenate([cnt_p[:, 0, :], cnt_s[:, 0, :]], axis=0).astype(jnp.int32)
    n_tiles = cnt.shape[0]
    group = jnp.sum(cnt, axis=0)
    padded = (group + EXPERT_BLOCK - 1) // EXPERT_BLOCK * EXPERT_BLOCK
    pad_end = jnp.cumsum(padded).astype(jnp.int32)
    run_start = (pad_end - padded)[None, :] + jnp.cumsum(cnt, axis=0) - cnt
    run_off = jnp.cumsum(cnt, axis=1) - cnt
    tail = jnp.where(padded > 0, pad_end - EXPERT_BLOCK, -1).astype(jnp.int32)
    max_rows = (np_ + ns) * TOP_K + n_tiles * N_EXPERTS * (RUN_ALIGN - 1) + N_EXPERTS * (EXPERT_BLOCK - 1)
    n_blocks = -(-max_rows // EXPERT_BLOCK)
    n_used = pad_end[-1:] // EXPERT_BLOCK
    blk_start = jnp.arange(n_blocks, dtype=jnp.int32) * EXPERT_BLOCK
    blk_expert = jnp.minimum(jnp.sum((blk_start[:, None] >= pad_end[None, :]).astype(jnp.int32), axis=1),
                             N_EXPERTS - 1)
    tables = (run_start, run_off, cnt, jnp.sum(cnt, axis=1))
    tabs_p = tuple(a[:ntp].reshape(-1).astype(jnp.int32) for a in tables)
    tabs_s = tuple(a[ntp:].reshape(-1).astype(jnp.int32) for a in tables)

    xsorted = _dispatch_first(tail, n_used, tabs_p, lpos_p, x1_p, tm=tm_p, n_blocks=n_blocks)
    xsorted = _dispatch_next(tabs_s, lpos_s, x1_s, xsorted, tm=tm_s)
    blk = jnp.arange(n_blocks, dtype=jnp.int32)
    later = (blk[None, :] > blk[:, None]) & (blk_expert[None, :] != blk_expert[:, None]) & (blk[None, :] < n_used)
    next_expert = jnp.where(jnp.any(later, axis=1), blk_expert[jnp.argmax(later, axis=1)], -1).astype(jnp.int32)
    yb = _experts(blk_expert, next_expert, n_used, xsorted, w_expert_in[0], b_glu, b_lin, w_expert_out[0], b_eo)
    y_p = _combine(tabs_p, gate_p, lpos_p, x1_p, g2, b2, yb, tm=tm_p, dn_alpha=dn_alpha)
    y_s = _combine(tabs_s, gate_s, lpos_s, x1_s, g2, b2, yb, tm=tm_s, dn_alpha=dn_alpha)

    k_prompt = jnp.transpose(kt_p.reshape(bp, N_HEADS, 2, HEAD_DIM, sp), (0, 4, 1, 2, 3))
    return (
        y_p.reshape(bp, sp, D_MODEL),
        y_s.reshape(bs, ss, D_MODEL),
        k_prompt[None],
        v_p.reshape(1, bp, sp, N_HEADS, V_DIM),
        pnew_p[:, 1:].reshape(1, bp, POOL_HIST, POOL_DIM),
        k_s.reshape(1, bs, ss, N_HEADS, 2, HEAD_DIM),
        v_s.reshape(1, bs, ss, N_HEADS, V_DIM),
        pnew_s[:, 1:].reshape(1, bs, POOL_HIST, POOL_DIM),
    )
```

```python
import functools
import math

import jax
import jax.numpy as jnp
from jax import lax
from jax.experimental import pallas as pl
from jax.experimental.pallas import tpu as pltpu

D_MODEL = 1024
CHUNK = 64
POOL_WINDOWS = (2, 4, 8, 16)
POOL_GROUP_DIM = 128
POOL_DIM = len(POOL_WINDOWS) * POOL_GROUP_DIM
POOL_HIST = max(POOL_WINDOWS) - 1
HIST_ROWS = POOL_HIST + 1
N_HEADS = 8
HEAD_DIM = 64
HALF_DIM = HEAD_DIM // 2
V_DIM = 2 * HEAD_DIM
QK_DIM = N_HEADS * 2 * HEAD_DIM
ATTN_V_WIDTH = N_HEADS * V_DIM
ATTN_SCALE = HEAD_DIM ** -0.5
LOG2_E = math.log2(math.e)
ROPE_THETA = 10000.0
SUBLN_EPS = 1e-5
N_EXPERTS = 32
TOP_K = 4
D_EXPERT = 1024
SWIGLU_LIMIT = 7.0
SWIGLU_ALPHA = 1.702
LN_EPS = 1e-5
NEG_INF = -1e30
LANES = 128
MXU_DIM = 256

F32 = jnp.float32
BF16 = jnp.bfloat16

VMEM_LIMIT = 56 * 1024 * 1024
EXPERT_BLOCK = 512
EXPERT_SUB = 256
RUN_ALIGN = 8
ROUTE_TILE = 256
DENSE_TILE = 512
ATTN_Q_TILE = 256
RUN_BITS = tuple(range(3, 9))


def _dot(a, b):
    return jnp.dot(a, b, preferred_element_type=F32)


def _dot_nt(a, b):
    return lax.dot_general(a, b, (((1,), (1,)), ((), ())), preferred_element_type=F32)


def _params(semantics):
    return pltpu.CompilerParams(dimension_semantics=semantics, vmem_limit_bytes=VMEM_LIMIT)


def _pool_branch(x, w_ref, icnt_ref, hist_ref, wmix_ref, pscale_ref, py_ref, pnew_ref, ext_ref, *, bb, tm):
    u = _dot(x, w_ref[:, 0:POOL_DIM])

    @pl.when(pl.program_id(1) == 0)
    def _():
        ext_ref[:, 0:HIST_ROWS, :] = hist_ref[...]

    for b in range(bb):
        ext_ref[b, HIST_ROWS:HIST_ROWS + tm, :] = u[b * tm:(b + 1) * tm]
    for b in range(bb):
        for g, w in enumerate(POOL_WINDOWS):
            cols = slice(g * POOL_GROUP_DIM, (g + 1) * POOL_GROUP_DIM)
            cur = ext_ref[b, HIST_ROWS:HIST_ROWS + tm, cols]
            acc = cur
            for j in range(1, w):
                acc = acc + ext_ref[b, HIST_ROWS - j:HIST_ROWS - j + tm, cols]
            d = acc * icnt_ref[:, cols] - cur
            y = _dot(d.astype(BF16), wmix_ref[g]) * pscale_ref[:, cols]
            py_ref[b * tm:(b + 1) * tm, cols] = y.astype(BF16)
    tail = ext_ref[:, tm:tm + HIST_ROWS, :]
    pnew_ref[...] = tail
    ext_ref[:, 0:HIST_ROWS, :] = tail


def _rope_rows(z, cos, sin):
    lane = lax.broadcasted_iota(jnp.int32, z.shape, 1)
    first_half = (lane % HEAD_DIM) < HALF_DIM
    partner = jnp.where(first_half, pltpu.roll(z, LANES - HALF_DIM, 1), pltpu.roll(z, HALF_DIM, 1))
    return z * cos + partner * sin


def _inproj_prompt_kernel(x_ref, w_ref, wkt_ref, cos_ref, sin_ref, cost_ref, sint_ref, icnt_ref, hist_ref,
                          wmix_ref, pscale_ref, q_ref, kt_ref, ktb_ref, v_ref, vb_ref, py_ref, pnew_ref,
                          ext_ref, *, tm):
    x = x_ref[...].astype(BF16)
    _pool_branch(x, w_ref, icnt_ref, hist_ref, wmix_ref, pscale_ref, py_ref, pnew_ref, ext_ref, bb=1, tm=tm)

    cos, sin = cos_ref[...], sin_ref[...]
    hq = _dot(x, w_ref[:, POOL_DIM:POOL_DIM + QK_DIM])
    for h in range(N_HEADS):
        sl = slice(h * V_DIM, (h + 1) * V_DIM)
        q_ref[:, sl] = (_rope_rows(hq[:, sl], cos, sin) * (ATTN_SCALE * LOG2_E)).astype(BF16)

    hkt = _dot_nt(wkt_ref[...], x)
    cost, sint = cost_ref[...], sint_ref[...]
    for hc in range(2 * N_HEADS):
        r0 = hc * HEAD_DIM
        x1 = hkt[r0:r0 + HALF_DIM]
        x2 = hkt[r0 + HALF_DIM:r0 + HEAD_DIM]
        o1 = x1 * cost - x2 * sint
        o2 = x2 * cost + x1 * sint
        kt_ref[r0:r0 + HALF_DIM, :] = o1
        kt_ref[r0 + HALF_DIM:r0 + HEAD_DIM, :] = o2
        ktb_ref[r0:r0 + HALF_DIM, :] = o1.astype(BF16)
        ktb_ref[r0 + HALF_DIM:r0 + HEAD_DIM, :] = o2.astype(BF16)

    hv = _dot(x, w_ref[:, POOL_DIM + QK_DIM:POOL_DIM + QK_DIM + ATTN_V_WIDTH])
    vb_ref[...] = hv.astype(BF16)
    for h in range(N_HEADS):
        v_ref[pl.ds(h, tm, stride=N_HEADS), :] = hv[:, h * V_DIM:(h + 1) * V_DIM]


def _inproj_prompt(x2d, w_pqv, wkt, cos, sin, cost, sint, icnt, hist, wmix, pscale, *, n_streams, seq, tm):
    n = n_streams * seq
    nt = seq // tm
    row_map = lambda b, t: (b * nt + t, 0)
    const2 = lambda b, t: (0, 0)
    out_shape = (
        jax.ShapeDtypeStruct((n, QK_DIM), BF16),
        jax.ShapeDtypeStruct((n_streams, QK_DIM, seq), F32),
        jax.ShapeDtypeStruct((n_streams, QK_DIM, seq), BF16),
        jax.ShapeDtypeStruct((n_streams, seq * N_HEADS, V_DIM), F32),
        jax.ShapeDtypeStruct((n, ATTN_V_WIDTH), BF16),
        jax.ShapeDtypeStruct((n, POOL_DIM), BF16),
        jax.ShapeDtypeStruct((n_streams, HIST_ROWS, POOL_DIM), F32),
    )
    return pl.pallas_call(
        functools.partial(_inproj_prompt_kernel, tm=tm),
        out_shape=out_shape,
        grid=(n_streams, nt),
        in_specs=[
            pl.BlockSpec((tm, D_MODEL), row_map),
            pl.BlockSpec(w_pqv.shape, const2),
            pl.BlockSpec(wkt.shape, const2),
            pl.BlockSpec((tm, LANES), lambda b, t: (t, 0)),
            pl.BlockSpec((tm, LANES), lambda b, t: (t, 0)),
            pl.BlockSpec((HALF_DIM, tm), lambda b, t: (0, t)),
            pl.BlockSpec((HALF_DIM, tm), lambda b, t: (0, t)),
            pl.BlockSpec((tm, POOL_DIM), lambda b, t: (t, 0)),
            pl.BlockSpec((1, HIST_ROWS, POOL_DIM), lambda b, t: (b, 0, 0)),
            pl.BlockSpec((len(POOL_WINDOWS), POOL_GROUP_DIM, POOL_GROUP_DIM), lambda b, t: (0, 0, 0)),
            pl.BlockSpec((1, POOL_DIM), const2),
        ],
        out_specs=(
            pl.BlockSpec((tm, QK_DIM), row_map),
            pl.BlockSpec((None, QK_DIM, tm), lambda b, t: (b, 0, t)),
            pl.BlockSpec((None, QK_DIM, tm), lambda b, t: (b, 0, t)),
            pl.BlockSpec((None, tm * N_HEADS, V_DIM), lambda b, t: (b, t, 0)),
            pl.BlockSpec((tm, ATTN_V_WIDTH), row_map),
            pl.BlockSpec((tm, POOL_DIM), row_map),
            pl.BlockSpec((1, HIST_ROWS, POOL_DIM), lambda b, t: (b, 0, 0)),
        ),
        scratch_shapes=[pltpu.VMEM((1, HIST_ROWS + tm, POOL_DIM), F32)],
        compiler_params=_params(("arbitrary", "arbitrary")),
        name="inproj_prompt",
    )(x2d, w_pqv, wkt, cos, sin, cost, sint, icnt, hist, wmix, pscale)


def _inproj_sample_kernel(x_ref, w_ref, wk_ref, cos_ref, sin_ref, icnt_ref, hist_ref, wmix_ref, pscale_ref,
                          q_ref, k_ref, v_ref, py_ref, pnew_ref, ext_ref, *, bb, tm):
    x = x_ref[...].astype(BF16)
    _pool_branch(x, w_ref, icnt_ref, hist_ref, wmix_ref, pscale_ref, py_ref, pnew_ref, ext_ref, bb=bb, tm=tm)
    cos = jnp.concatenate([cos_ref[...]] * bb, axis=0)
    sin = jnp.concatenate([sin_ref[...]] * bb, axis=0)
    hq = _dot(x, w_ref[:, POOL_DIM:POOL_DIM + QK_DIM])
    hk = _dot(x, wk_ref[...])
    for h in range(N_HEADS):
        sl = slice(h * V_DIM, (h + 1) * V_DIM)
        q_ref[:, sl] = (_rope_rows(hq[:, sl], cos, sin) * ATTN_SCALE).astype(BF16)
        k_ref[:, sl] = _rope_rows(hk[:, sl], cos, sin)
    v_ref[...] = _dot(x, w_ref[:, POOL_DIM + QK_DIM:POOL_DIM + QK_DIM + ATTN_V_WIDTH])


def _inproj_sample(x2d, w_pqv, wk, cos, sin, icnt, hist, wmix, pscale, *, n_streams, seq):
    n = n_streams * seq
    const2 = lambda i, t: (0, 0)
    const3 = lambda i, t: (0, 0, 0)
    out_shape = (
        jax.ShapeDtypeStruct((n, QK_DIM), BF16),
        jax.ShapeDtypeStruct((n, QK_DIM), F32),
        jax.ShapeDtypeStruct((n, ATTN_V_WIDTH), F32),
        jax.ShapeDtypeStruct((n, POOL_DIM), BF16),
        jax.ShapeDtypeStruct((n_streams, HIST_ROWS, POOL_DIM), F32),
    )
    return pl.pallas_call(
        functools.partial(_inproj_sample_kernel, bb=n_streams, tm=seq),
        out_shape=out_shape,
        grid=(1, 1),
        in_specs=[
            pl.BlockSpec((n, D_MODEL), const2),
            pl.BlockSpec(w_pqv.shape, const2),
            pl.BlockSpec(wk.shape, const2),
            pl.BlockSpec((seq, LANES), const2),
            pl.BlockSpec((seq, LANES), const2),
            pl.BlockSpec((seq, POOL_DIM), const2),
            pl.BlockSpec((n_streams, HIST_ROWS, POOL_DIM), const3),
            pl.BlockSpec((len(POOL_WINDOWS), POOL_GROUP_DIM, POOL_GROUP_DIM), const3),
            pl.BlockSpec((1, POOL_DIM), const2),
        ],
        out_specs=(
            pl.BlockSpec((n, QK_DIM), const2),
            pl.BlockSpec((n, QK_DIM), const2),
            pl.BlockSpec((n, ATTN_V_WIDTH), const2),
            pl.BlockSpec((n, POOL_DIM), const2),
            pl.BlockSpec((n_streams, HIST_ROWS, POOL_DIM), const3),
        ),
        scratch_shapes=[pltpu.VMEM((n_streams, HIST_ROWS + seq, POOL_DIM), F32)],
        compiler_params=_params(("arbitrary", "arbitrary")),
        name="inproj_sample",
    )(x2d, w_pqv, wk, cos, sin, icnt, hist, wmix, pscale)


def _lambda_value(lam_ref, lam_init):
    lv = lam_ref[...]
    s1 = jnp.sum(lv[0:1] * lv[1:2], axis=1, keepdims=True)
    s2 = jnp.sum(lv[2:3] * lv[3:4], axis=1, keepdims=True)
    return jnp.exp(s1) - jnp.exp(s2) + lam_init


def _head_norm(o, g, lam_init):
    ms = jnp.mean(o * o, axis=-1, keepdims=True)
    return o * lax.rsqrt(ms + SUBLN_EPS) * g * (1.0 - lam_init)


def _attn_prompt_kernel(lam_ref, g_ref, q_ref, kt_ref, v_ref, o_ref, vext, *, seq, tq, lam_init):
    lam = _lambda_value(lam_ref, lam_init)
    g = g_ref[...]
    r = lax.broadcasted_iota(jnp.int32, (tq, tq), 0)
    c = lax.broadcasted_iota(jnp.int32, (tq, tq), 1)
    diag_visible = (c // CHUNK) <= (r // CHUNK)
    lane = lax.broadcasted_iota(jnp.int32, (tq, V_DIM), 1)
    vext[:, 0:V_DIM] = v_ref[...]
    vlane = lax.broadcasted_iota(jnp.int32, (seq, V_DIM), 1)
    vext[:, V_DIM:] = jnp.where(vlane == 0, 1.0, 0.0).astype(vext.dtype)

    for i in range(seq // tq):
        lo = i * tq
        q = q_ref[lo:lo + tq, :]
        zero = jnp.zeros_like(q)
        qc = (jnp.where(lane < HEAD_DIM, q, zero), jnp.where(lane >= HEAD_DIM, q, zero))
        normed = []
        for k in range(2):
            sd = jnp.where(diag_visible, _dot(qc[k], kt_ref[:, lo:lo + tq]), NEG_INF)
            m = jnp.max(sd, axis=1, keepdims=True)
            if i > 0:
                sp = _dot(qc[k], kt_ref[:, 0:lo])
                m = jnp.maximum(m, jnp.max(sp, axis=1, keepdims=True))
            acc = _dot(jnp.exp2(sd - m).astype(BF16), vext[lo:lo + tq, :])
            if i > 0:
                acc = acc + _dot(jnp.exp2(sp - m).astype(BF16), vext[0:lo, :])
            normed.append(acc[:, 0:V_DIM] / acc[:, V_DIM:V_DIM + 1])
        o = normed[0] - lam * normed[1]
        o_ref[lo:lo + tq, :] = _head_norm(o, g, lam_init).astype(o_ref.dtype)


def _attn_prompt(lam_vecs, norm_g, q, ktb, vb, *, n_streams, seq, tq, lam_init):
    return pl.pallas_call(
        functools.partial(_attn_prompt_kernel, seq=seq, tq=tq, lam_init=lam_init),
        out_shape=jax.ShapeDtypeStruct((n_streams * seq, ATTN_V_WIDTH), BF16),
        grid=(n_streams, N_HEADS),
        in_specs=[
            pl.BlockSpec((4, HEAD_DIM), lambda b, h: (0, 0)),
            pl.BlockSpec((1, V_DIM), lambda b, h: (0, 0)),
            pl.BlockSpec((seq, V_DIM), lambda b, h: (b, h)),
            pl.BlockSpec((None, V_DIM, seq), lambda b, h: (b, h, 0)),
            pl.BlockSpec((seq, V_DIM), lambda b, h: (b, h)),
        ],
        out_specs=pl.BlockSpec((seq, V_DIM), lambda b, h: (b, h)),
        scratch_shapes=[pltpu.VMEM((seq, MXU_DIM), BF16)],
        compiler_params=_params(("arbitrary", "arbitrary")),
        name="attn_prompt",
    )(lam_vecs, norm_g, q, ktb, vb)


def _attn_sample_kernel(lam_ref, g_ref, q_ref, kc_ref, vc_ref, kn_ref, vn_ref, o_ref,
                        s_scr, w_scr, wn_scr, m_scr, acc_scr, qbd_scr, *, nk, tn, past, lam_init):
    j = pl.program_id(1)
    half = N_HEADS * tn

    @pl.when(j == 0)
    def _():
        q = q_ref[...]
        qt = jnp.concatenate([q] * (2 * N_HEADS), axis=0)
        r = lax.broadcasted_iota(jnp.int32, qt.shape, 0)
        l = lax.broadcasted_iota(jnp.int32, qt.shape, 1)
        keep = ((r // half) == ((l % V_DIM) // HEAD_DIM)) & (((r % half) // tn) == (l // V_DIM))
        qbd_scr[...] = jnp.where(keep, qt, jnp.zeros_like(qt))
        m_scr[...] = jnp.full(m_scr.shape, NEG_INF, F32)
        acc_scr[...] = jnp.zeros(acc_scr.shape, F32)

    @pl.when(j < nk)
    def _():
        s = _dot(qbd_scr[...], kc_ref[...].astype(BF16))
        s_scr[j] = s
        m_scr[...] = jnp.maximum(m_scr[...], jnp.max(s, axis=1, keepdims=True))

    @pl.when(j == nk - 1)
    def _():
        lam = _lambda_value(lam_ref, lam_init)
        sn = _dot_nt(qbd_scr[...], kn_ref[...].astype(BF16))
        qpos = past + (lax.broadcasted_iota(jnp.int32, sn.shape, 0) % tn)
        kpos = past + lax.broadcasted_iota(jnp.int32, sn.shape, 1)
        sn = jnp.where((kpos // CHUNK) <= (qpos // CHUNK), sn, NEG_INF)
        m = jnp.maximum(m_scr[...], jnp.max(sn, axis=1, keepdims=True))
        pn = jnp.exp(sn - m)
        l = jnp.sum(pn, axis=1, keepdims=True)
        for c in range(nk):
            p = jnp.exp(s_scr[c] - m)
            s_scr[c] = p
            l = l + jnp.sum(p, axis=1, keepdims=True)
        r0 = 1.0 / l[:half]
        r1 = lam / l[half:]
        wn_scr[...] = pn[:half] * r0 - pn[half:] * r1
        for c in range(nk):
            p = s_scr[c]
            w_scr[c] = (p[:half] * r0 - p[half:] * r1).astype(BF16)

    def v_rows(ref):
        tk = ref.shape[0] // N_HEADS
        return jnp.concatenate([ref[pl.ds(h, tk, stride=N_HEADS), :] for h in range(N_HEADS)],
                               axis=1).astype(BF16)

    @pl.when(j >= nk)
    def _():
        acc_scr[...] += _dot(w_scr[j - nk], v_rows(vc_ref))

    @pl.when(j == 2 * nk - 1)
    def _():
        acc = acc_scr[...] + _dot(wn_scr[...].astype(BF16), vn_ref[...].astype(BF16))
        g = g_ref[...]
        for h in range(N_HEADS):
            o = acc[h * tn:(h + 1) * tn, h * V_DIM:(h + 1) * V_DIM]
            o_ref[:, h * V_DIM:(h + 1) * V_DIM] = _head_norm(o, g, lam_init).astype(o_ref.dtype)


def _attn_sample(lam_vecs, norm_g, q, kct, vc, kn, vn, *, n_streams, tn, past, tk, lam_init):
    nk = past // tk
    rows = 2 * N_HEADS * tn
    half = N_HEADS * tn
    return pl.pallas_call(
        functools.partial(_attn_sample_kernel, nk=nk, tn=tn, past=past, lam_init=lam_init),
        out_shape=jax.ShapeDtypeStruct((n_streams * tn, ATTN_V_WIDTH), BF16),
        grid=(n_streams, 2 * nk),
        in_specs=[
            pl.BlockSpec((4, HEAD_DIM), lambda b, j: (0, 0)),
            pl.BlockSpec((1, V_DIM), lambda b, j: (0, 0)),
            pl.BlockSpec((tn, QK_DIM), lambda b, j: (b, 0)),
            pl.BlockSpec((None, QK_DIM, tk), lambda b, j: (b, 0, jnp.minimum(j, nk - 1))),
            pl.BlockSpec((None, tk * N_HEADS, V_DIM), lambda b, j: (b, jnp.maximum(j - nk, 0), 0)),
            pl.BlockSpec((tn, QK_DIM), lambda b, j: (b, 0)),
            pl.BlockSpec((tn, ATTN_V_WIDTH), lambda b, j: (b, 0)),
        ],
        out_specs=pl.BlockSpec((tn, ATTN_V_WIDTH), lambda b, j: (b, 0)),
        scratch_shapes=[
            pltpu.VMEM((nk, rows, tk), F32),
            pltpu.VMEM((nk, half, tk), BF16),
            pltpu.VMEM((half, tn), F32),
            pltpu.VMEM((rows, 1), F32),
            pltpu.VMEM((half, ATTN_V_WIDTH), F32),
            pltpu.VMEM((rows, QK_DIM), BF16),
        ],
        compiler_params=_params(("arbitrary", "arbitrary")),
        name="attn_sample",
    )(lam_vecs, norm_g, q, kct, vc, kn, vn)


def _layer_norm(z, g, b):
    mu = jnp.mean(z, axis=-1, keepdims=True)
    zc = z - mu
    var = jnp.mean(zc * zc, axis=-1, keepdims=True)
    return zc * lax.rsqrt(var + LN_EPS) * g + b


def _postmix_kernel(x_ref, py_ref, ay_ref, wg_ref, wpo_ref, wao_ref, wout_ref, g1_ref, b1_ref,
                    wr_ref, br_ref, x1_ref, gate_ref, lpos_ref, cnt_ref, *, tm, rt, dn_alpha):
    x = x_ref[...]
    xb = x.astype(BF16)
    gates = jax.nn.sigmoid(_dot(xb, wg_ref[...]))
    a = _dot(py_ref[...], wpo_ref[...])
    b = _dot(ay_ref[...], wao_ref[...])
    mixed = gates[:, :D_MODEL] * a + gates[:, D_MODEL:] * b
    mo = _dot(mixed.astype(BF16), wout_ref[...])
    x1 = _layer_norm(dn_alpha * x + mo, g1_ref[...], b1_ref[...])
    x1_ref[...] = x1

    logits = _dot(x1.astype(BF16), wr_ref[...]) + br_ref[...]
    for sub in range(tm // rt):
        rows = slice(sub * rt, (sub + 1) * rt)
        gate_ref[rows, :], lpos_ref[rows, :], cnt_ref[sub] = _route_tile(logits[rows], rt)


def _route_tile(logits, tm):
    lane = lax.broadcasted_iota(jnp.int32, logits.shape, 1)
    work = logits
    vals, idxs = [], []
    for _ in range(TOP_K):
        mx = jnp.max(work, axis=1, keepdims=True)
        ix = jnp.min(jnp.where(work == mx, lane, N_EXPERTS), axis=1, keepdims=True)
        vals.append(mx)
        idxs.append(ix)
        work = jnp.where(lane == ix, -jnp.inf, work)
    exps = [jnp.exp(v - vals[0]) for v in vals]
    denom = exps[0] + exps[1] + exps[2] + exps[3]

    onehot = jnp.zeros(logits.shape, F32)
    for ix in idxs:
        onehot = onehot + (lane == ix).astype(F32)
    r = lax.broadcasted_iota(jnp.int32, (tm, tm), 0)
    c = lax.broadcasted_iota(jnp.int32, (tm, tm), 1)
    tri = jnp.where(c < r, 1.0, 0.0).astype(BF16)
    earlier = _dot(tri, onehot.astype(BF16))
    cnt = jnp.sum(onehot, axis=0, keepdims=True)
    units = jnp.floor((cnt + (RUN_ALIGN - 1.0)) * (1.0 / RUN_ALIGN))
    er = lax.broadcasted_iota(jnp.int32, (N_EXPERTS, N_EXPERTS), 0)
    ec = lax.broadcasted_iota(jnp.int32, (N_EXPERTS, N_EXPERTS), 1)
    upper = jnp.where(er < ec, 1.0, 0.0).astype(BF16)
    run_off = _dot(jnp.broadcast_to(units, (8, N_EXPERTS)).astype(BF16), upper)[0:1] * float(RUN_ALIGN)
    pos = earlier + run_off

    lane_out = lax.broadcasted_iota(jnp.int32, (tm, LANES), 1)
    lpos_out = jnp.zeros((tm, LANES), jnp.int32)
    gate_out = jnp.zeros((tm, LANES), F32)
    for k in range(TOP_K):
        lpos_k = jnp.sum(jnp.where(lane == idxs[k], pos, 0.0), axis=1, keepdims=True).astype(jnp.int32)
        lpos_out = jnp.where(lane_out == k, lpos_k, lpos_out)
        gate_out = jnp.where(lane_out == k, exps[k] / denom, gate_out)
    return gate_out, lpos_out, units * float(RUN_ALIGN)


def _postmix(x2d, py, ay, wg, wpo, wao, wout, g1, b1, wr, br, *, tm, rt, dn_alpha):
    n = x2d.shape[0]
    row = lambda i: (i, 0)
    const = lambda i: (0, 0)
    out_shape = (
        jax.ShapeDtypeStruct((n, D_MODEL), F32),
        jax.ShapeDtypeStruct((n, LANES), F32),
        jax.ShapeDtypeStruct((n, LANES), jnp.int32),
        jax.ShapeDtypeStruct((n // rt, 1, N_EXPERTS), F32),
    )
    return pl.pallas_call(
        functools.partial(_postmix_kernel, tm=tm, rt=rt, dn_alpha=dn_alpha),
        out_shape=out_shape,
        grid=(n // tm,),
        in_specs=[
            pl.BlockSpec((tm, D_MODEL), row),
            pl.BlockSpec((tm, POOL_DIM), row),
            pl.BlockSpec((tm, ATTN_V_WIDTH), row),
            pl.BlockSpec(wg.shape, const),
            pl.BlockSpec(wpo.shape, const),
            pl.BlockSpec(wao.shape, const),
            pl.BlockSpec(wout.shape, const),
            pl.BlockSpec((1, D_MODEL), const),
            pl.BlockSpec((1, D_MODEL), const),
            pl.BlockSpec(wr.shape, const),
            pl.BlockSpec((1, N_EXPERTS), const),
        ],
        out_specs=(
            pl.BlockSpec((tm, D_MODEL), row),
            pl.BlockSpec((tm, LANES), row),
            pl.BlockSpec((tm, LANES), row),
            pl.BlockSpec((tm // rt, 1, N_EXPERTS), lambda i: (i, 0, 0)),
        ),
        compiler_params=_params(("arbitrary",)),
        name="postmix",
    )(x2d, py, ay, wg, wpo, wao, wout, g1, b1, wr, br)


def _local_rows(tm):
    return TOP_K * tm + N_EXPERTS * RUN_ALIGN


def _for_each_run_chunk(tile, tab_refs, local_buf, sorted_ref, sem, to_sorted, fn):
    gs_ref, off_ref, cnt_ref = tab_refs[:3]

    def per_expert(e, carry):
        t = tile * N_EXPERTS + e
        cnt, off, gs = cnt_ref[t], off_ref[t], gs_ref[t]
        for b in RUN_BITS:
            size = 1 << b

            @pl.when((cnt & size) != 0)
            def _():
                lower = cnt & (size - 1)
                loc = local_buf.at[pl.ds(pl.multiple_of(off + lower, RUN_ALIGN), size)]
                srt = sorted_ref.at[pl.ds(pl.multiple_of(gs + lower, RUN_ALIGN), size)]
                fn(pltpu.make_async_copy(loc, srt, sem) if to_sorted else pltpu.make_async_copy(srt, loc, sem))
        return carry

    lax.fori_loop(0, N_EXPERTS, per_expert, 0)


def _wait_run_rows(total, local_buf, sorted_ref, sem, to_sorted):
    rows = local_buf.shape[0]
    for b in range(RUN_BITS[0], rows.bit_length()):
        size = 1 << b

        @pl.when((total & size) != 0)
        def _():
            loc, srt = local_buf.at[pl.ds(0, size)], sorted_ref.at[pl.ds(0, size)]
            (pltpu.make_async_copy(loc, srt, sem) if to_sorted else pltpu.make_async_copy(srt, loc, sem)).wait()


def _dispatch_tile(tab_refs, lpos_ref, x_ref, xs_ref, xloc, sem, tm):
    lpos = lpos_ref[...]
    col = lax.broadcasted_iota(jnp.int32, (tm, _local_rows(tm)), 1)
    hit = col == lpos[:, 0:1]
    for k in range(1, TOP_K):
        hit = jnp.logical_or(hit, col == lpos[:, k:k + 1])
    perm_t = jnp.where(hit, 1.0, 0.0).astype(BF16)
    tile = pl.program_id(0)
    slot = tile % 2
    xloc[slot] = lax.dot_general(perm_t, x_ref[...].astype(BF16), (((0,), (0,)), ((), ())),
                                 preferred_element_type=F32)

    def copies(t, s, fn):
        _for_each_run_chunk(t, tab_refs, xloc.at[s], xs_ref, sem.at[s], True, fn)

    copies(tile, slot, lambda cp: cp.start())
    tot_ref = tab_refs[3]

    @pl.when(tile > 0)
    def _():
        _wait_run_rows(tot_ref[tile - 1], xloc.at[1 - slot], xs_ref, sem.at[1 - slot], True)

    @pl.when(tile == pl.num_programs(0) - 1)
    def _():
        _wait_run_rows(tot_ref[tile], xloc.at[slot], xs_ref, sem.at[slot], True)


def _dispatch_first_kernel(tail_ref, nu_ref, gs_ref, off_ref, cnt_ref, tot_ref, lpos_ref, x_ref, xs_ref,
                           xloc, zbuf, sem, zsem, *, tm, n_blocks):
    @pl.when(pl.program_id(0) == 0)
    def _():
        zbuf[...] = jnp.zeros(zbuf.shape, zbuf.dtype)

        def zero_copy(row):
            row = pl.multiple_of(row, EXPERT_BLOCK)
            return pltpu.make_async_copy(zbuf, xs_ref.at[pl.ds(row, EXPERT_BLOCK)], zsem)

        def over_blocks(fn):
            for e in range(N_EXPERTS):
                @pl.when(tail_ref[e] >= 0)
                def _():
                    fn(zero_copy(tail_ref[e]))
            lax.fori_loop(nu_ref[0], n_blocks, lambda b, c: (fn(zero_copy(b * EXPERT_BLOCK)), c)[1], 0)

        over_blocks(lambda cp: cp.start())
        over_blocks(lambda cp: cp.wait())

    _dispatch_tile((gs_ref, off_ref, cnt_ref, tot_ref), lpos_ref, x_ref, xs_ref, xloc, sem, tm)


def _dispatch_next_kernel(gs_ref, off_ref, cnt_ref, tot_ref, lpos_ref, x_ref, xs_in_ref, xs_ref, xloc, sem,
                          *, tm):
    del xs_in_ref
    _dispatch_tile((gs_ref, off_ref, cnt_ref, tot_ref), lpos_ref, x_ref, xs_ref, xloc, sem, tm)


def _dispatch_first(tail, n_used, tabs, lpos, x1, *, tm, n_blocks):
    n = x1.shape[0]
    grid_spec = pltpu.PrefetchScalarGridSpec(
        num_scalar_prefetch=6,
        grid=(n // tm,),
        in_specs=[pl.BlockSpec((tm, LANES), lambda i, *_: (i, 0)),
                  pl.BlockSpec((tm, D_MODEL), lambda i, *_: (i, 0))],
        out_specs=pl.BlockSpec(memory_space=pl.ANY),
        scratch_shapes=[pltpu.VMEM((2, _local_rows(tm), D_MODEL), F32),
                        pltpu.VMEM((EXPERT_BLOCK, D_MODEL), F32),
                        pltpu.SemaphoreType.DMA((2,)), pltpu.SemaphoreType.DMA],
    )
    return pl.pallas_call(
        functools.partial(_dispatch_first_kernel, tm=tm, n_blocks=n_blocks),
        out_shape=jax.ShapeDtypeStruct((n_blocks * EXPERT_BLOCK, D_MODEL), F32),
        grid_spec=grid_spec,
        compiler_params=_params(("arbitrary",)),
        name="dispatch_first",
    )(tail, n_used, *tabs, lpos, x1)


def _dispatch_next(tabs, lpos, x1, xs, *, tm):
    n = x1.shape[0]
    grid_spec = pltpu.PrefetchScalarGridSpec(
        num_scalar_prefetch=len(tabs),
        grid=(n // tm,),
        in_specs=[pl.BlockSpec((tm, LANES), lambda i, *_: (i, 0)),
                  pl.BlockSpec((tm, D_MODEL), lambda i, *_: (i, 0)),
                  pl.BlockSpec(memory_space=pl.ANY)],
        out_specs=pl.BlockSpec(memory_space=pl.ANY),
        scratch_shapes=[pltpu.VMEM((2, _local_rows(tm), D_MODEL), F32), pltpu.SemaphoreType.DMA((2,))],
    )
    return pl.pallas_call(
        functools.partial(_dispatch_next_kernel, tm=tm),
        out_shape=jax.ShapeDtypeStruct(xs.shape, xs.dtype),
        grid_spec=grid_spec,
        input_output_aliases={6: 0},
        compiler_params=_params(("arbitrary",)),
        name="dispatch_next",
    )(*tabs, lpos, x1, xs)


def _experts_kernel(be_ref, nxt_ref, rows_ref, nu_ref, xs_ref, bg_ref, bl_ref, bo_ref, win_hbm, wo_hbm, y_ref,
                    win_buf, wo_buf, wg_scr, wl_scr, wo_scr, sem_in, sem_out):
    i = pl.program_id(0)
    used = i < nu_ref[0]
    first_of_expert = jnp.logical_or(i == 0, be_ref[i] != be_ref[jnp.maximum(i - 1, 0)])

    def weight_copies(e):
        return (pltpu.make_async_copy(win_hbm.at[e], win_buf, sem_in),
                pltpu.make_async_copy(wo_hbm.at[e], wo_buf, sem_out))

    @pl.when(jnp.logical_not(used))
    def _():
        y_ref[...] = jnp.zeros(y_ref.shape, y_ref.dtype)

    @pl.when(i == 0)
    def _():
        for cp in weight_copies(be_ref[0]):
            cp.start()

    @pl.when(jnp.logical_and(used, first_of_expert))
    def _():
        for cp in weight_copies(be_ref[i]):
            cp.wait()
        r = lax.broadcasted_iota(jnp.int32, (MXU_DIM, MXU_DIM), 0)
        c = lax.broadcasted_iota(jnp.int32, (MXU_DIM, MXU_DIM), 1)
        src = jnp.where(c < LANES, 2 * c, 2 * (c - LANES) + 1)
        sel = jnp.where(r == src, 1.0, 0.0).astype(BF16)
        for gq in range(2 * D_EXPERT // MXU_DIM):
            blk = win_buf[:, gq * MXU_DIM:(gq + 1) * MXU_DIM].astype(BF16)
            d = _dot(blk, sel)
            wg_scr[:, gq * LANES:(gq + 1) * LANES] = d[:, :LANES].astype(BF16)
            wl_scr[:, gq * LANES:(gq + 1) * LANES] = d[:, LANES:].astype(BF16)
        wo_scr[...] = wo_buf[...].astype(BF16)

        @pl.when(nxt_ref[i] >= 0)
        def _():
            for cp in weight_copies(nxt_ref[i]):
                cp.start(priority=1)

    @pl.when(used)
    def _():
        for s in range(EXPERT_BLOCK // EXPERT_SUB):
            rows = slice(s * EXPERT_SUB, (s + 1) * EXPERT_SUB)
            has_rows = rows_ref[i] > s * EXPERT_SUB

            @pl.when(has_rows)
            def _():
                xb = xs_ref[rows, :].astype(BF16)
                glu = jnp.minimum(_dot(xb, wg_scr[...]) + bg_ref[...], SWIGLU_LIMIT)
                lin = jnp.clip(_dot(xb, wl_scr[...]) + bl_ref[...], -SWIGLU_LIMIT, SWIGLU_LIMIT)
                act = glu * jax.nn.sigmoid(SWIGLU_ALPHA * glu) * (lin + 1.0)
                y_ref[rows, :] = _dot(act.astype(BF16), wo_scr[...]) + bo_ref[...]

            @pl.when(jnp.logical_not(has_rows))
            def _():
                y_ref[rows, :] = jnp.zeros((EXPERT_SUB, D_MODEL), y_ref.dtype)


def _experts(blk_expert, next_expert, blk_rows, n_used, xs, w_in, b_glu, b_lin, w_out, b_out):
    rows = xs.shape[0]
    n_blocks = rows // EXPERT_BLOCK
    wsel = lambda i, be, nxt, rows, nu: (be[i], 0, 0)
    grid_spec = pltpu.PrefetchScalarGridSpec(
        num_scalar_prefetch=4,
        grid=(n_blocks,),
        in_specs=[
            pl.BlockSpec((EXPERT_BLOCK, D_MODEL), lambda i, be, nxt, rows, nu: (jnp.minimum(i, nu[0] - 1), 0)),
            pl.BlockSpec((None, 1, D_EXPERT), wsel),
            pl.BlockSpec((None, 1, D_EXPERT), wsel),
            pl.BlockSpec((None, 1, D_MODEL), wsel),
            pl.BlockSpec(memory_space=pl.ANY),
            pl.BlockSpec(memory_space=pl.ANY),
        ],
        out_specs=pl.BlockSpec((EXPERT_BLOCK, D_MODEL), lambda i, be, nxt, rows, nu: (i, 0)),
        scratch_shapes=[pltpu.VMEM((D_MODEL, 2 * D_EXPERT), F32),
                        pltpu.VMEM((D_EXPERT, D_MODEL), F32),
                        pltpu.VMEM((D_MODEL, D_EXPERT), BF16),
                        pltpu.VMEM((D_MODEL, D_EXPERT), BF16),
                        pltpu.VMEM((D_EXPERT, D_MODEL), BF16),
                        pltpu.SemaphoreType.DMA, pltpu.SemaphoreType.DMA],
    )
    return pl.pallas_call(
        _experts_kernel,
        out_shape=jax.ShapeDtypeStruct((rows, D_MODEL), F32),
        grid_spec=grid_spec,
        compiler_params=_params(("arbitrary",)),
        name="experts",
    )(blk_expert, next_expert, blk_rows, n_used, xs, b_glu, b_lin, b_out, w_in, w_out)


def _split_bf16(a):
    hi = a.astype(BF16)
    return hi, (a - hi.astype(F32)).astype(BF16)


def _combine_kernel(gs_ref, off_ref, cnt_ref, tot_ref, gate_ref, lpos_ref, x1_ref, g2_ref, b2_ref, yb_ref, o_ref,
                    yloc, sem, *, tm, dn_alpha):
    tile = pl.program_id(0)
    slot = tile % 2
    tabs = (gs_ref, off_ref, cnt_ref)

    def copies(t, s, fn):
        _for_each_run_chunk(t, tabs, yloc.at[s], yb_ref, sem.at[s], False, fn)

    @pl.when(tile == 0)
    def _():
        yloc[...] = jnp.zeros(yloc.shape, yloc.dtype)
        copies(tile, slot, lambda cp: cp.start())

    @pl.when(tile + 1 < pl.num_programs(0))
    def _():
        copies(tile + 1, 1 - slot, lambda cp: cp.start())

    _wait_run_rows(tot_ref[tile], yloc.at[slot], yb_ref, sem.at[slot], False)

    gate, lpos = gate_ref[...], lpos_ref[...]
    col = lax.broadcasted_iota(jnp.int32, (tm, _local_rows(tm)), 1)
    weights = jnp.zeros(col.shape, F32)
    for k in range(TOP_K):
        weights = jnp.where(col == lpos[:, k:k + 1], gate[:, k:k + 1], weights)
    w_hi, w_lo = _split_bf16(weights)
    y_hi, y_lo = _split_bf16(yloc[slot])
    y = _dot(w_hi, y_hi) + (_dot(w_hi, y_lo) + _dot(w_lo, y_hi))
    o_ref[...] = _layer_norm(dn_alpha * x1_ref[...] + y, g2_ref[...], b2_ref[...])


def _combine(tabs, gate, lpos, x1, g2, b2, yb, *, tm, dn_alpha):
    n = x1.shape[0]
    grid_spec = pltpu.PrefetchScalarGridSpec(
        num_scalar_prefetch=len(tabs),
        grid=(n // tm,),
        in_specs=[
            pl.BlockSpec((tm, LANES), lambda i, *_: (i, 0)),
            pl.BlockSpec((tm, LANES), lambda i, *_: (i, 0)),
            pl.BlockSpec((tm, D_MODEL), lambda i, *_: (i, 0)),
            pl.BlockSpec((1, D_MODEL), lambda i, *_: (0, 0)),
            pl.BlockSpec((1, D_MODEL), lambda i, *_: (0, 0)),
            pl.BlockSpec(memory_space=pl.ANY),
        ],
        out_specs=pl.BlockSpec((tm, D_MODEL), lambda i, *_: (i, 0)),
        scratch_shapes=[pltpu.VMEM((2, _local_rows(tm), D_MODEL), F32), pltpu.SemaphoreType.DMA((2,))],
    )
    return pl.pallas_call(
        functools.partial(_combine_kernel, tm=tm, dn_alpha=dn_alpha),
        out_shape=jax.ShapeDtypeStruct((n, D_MODEL), F32),
        grid_spec=grid_spec,
        compiler_params=_params(("arbitrary",)),
        name="combine",
    )(*tabs, gate, lpos, x1, g2, b2, yb)


def _position_tables(pos0, seq):
    pos = pos0 + jnp.arange(seq, dtype=jnp.int32)
    inv = ROPE_THETA ** (-jnp.arange(HALF_DIM, dtype=F32) / HALF_DIM)
    ang = pos.astype(F32)[:, None] * inv[None, :]
    cos, sin = jnp.cos(ang), jnp.sin(ang)
    cos_rows = jnp.concatenate([cos, cos, cos, cos], axis=-1)
    sin_rows = jnp.concatenate([-sin, sin, -sin, sin], axis=-1)
    icnt = jnp.concatenate(
        [jnp.broadcast_to((1.0 / jnp.minimum(pos + 1, w).astype(F32))[:, None], (seq, POOL_GROUP_DIM))
         for w in POOL_WINDOWS], axis=-1)
    return cos_rows, sin_rows, cos.T, sin.T, icnt


def _tile(n, pref):
    t = min(n, pref)
    while n % t:
        t //= 2
    return t


def kernel(x_prompt, x_sample, cache_k, cache_v, state_pool, w_in, w_pool_mix, pool_scale, w_pool_out,
           lambda_q1, lambda_k1, lambda_q2, lambda_k2, attn_norm_g, w_attn_out, w_out, ln1_g, ln1_b,
           w_router, b_router, w_expert_in, b_expert_in, w_expert_out, b_expert_out, ln2_g, ln2_b):
    depth = w_in.shape[0]
    assert depth == 1, "single-layer step"
    dn_alpha = (2.0 * depth) ** 0.25
    lam_init = 0.8 - 0.6 * math.exp(-0.3 * 0)
    bp, sp, _ = x_prompt.shape
    bs, ss, _ = x_sample.shape
    past = cache_k.shape[2]
    np_, ns = bp * sp, bs * ss

    c_q, c_k, c_v = POOL_DIM, POOL_DIM + QK_DIM, POOL_DIM + 2 * QK_DIM
    c_gate = c_v + ATTN_V_WIDTH
    w0 = w_in[0]
    w_pqv = jnp.concatenate([w0[:, :c_k], w0[:, c_v:c_gate]], axis=1).astype(BF16)
    w_k = w0[:, c_k:c_v].astype(BF16)
    w_gate = w0[:, c_gate:].astype(BF16)
    wmix = w_pool_mix[0].astype(BF16)
    pscale = pool_scale[0].reshape(1, POOL_DIM)
    wpo = w_pool_out[0].astype(BF16)
    wao = w_attn_out[0].astype(BF16)
    wout = w_out[0].astype(BF16)
    wr = w_router[0].astype(BF16)
    br = b_router[0].reshape(1, N_EXPERTS)
    lam_vecs = jnp.stack([lambda_q1[0], lambda_k1[0], lambda_q2[0], lambda_k2[0]])
    norm_g = attn_norm_g[0].reshape(1, V_DIM)
    g1, b1 = ln1_g[0].reshape(1, D_MODEL), ln1_b[0].reshape(1, D_MODEL)
    g2, b2 = ln2_g[0].reshape(1, D_MODEL), ln2_b[0].reshape(1, D_MODEL)
    b_glu = b_expert_in[0][:, 0::2].reshape(N_EXPERTS, 1, D_EXPERT)
    b_lin = b_expert_in[0][:, 1::2].reshape(N_EXPERTS, 1, D_EXPERT)
    b_eo = b_expert_out[0].reshape(N_EXPERTS, 1, D_MODEL)

    xp = x_prompt.reshape(np_, D_MODEL)
    cos_p, sin_p, cost_p, sint_p, icnt_p = _position_tables(0, sp)
    hist_p = jnp.zeros((bp, HIST_ROWS, POOL_DIM), F32)
    q_p, kt_p, ktb_p, v_p, vb_p, py_p, pnew_p = _inproj_prompt(
        xp, w_pqv, w_k.T, cos_p, sin_p, cost_p, sint_p, icnt_p, hist_p, wmix, pscale,
        n_streams=bp, seq=sp, tm=_tile(sp, DENSE_TILE))
    ay_p = _attn_prompt(lam_vecs, norm_g, q_p, ktb_p, vb_p, n_streams=bp, seq=sp,
                        tq=_tile(sp, ATTN_Q_TILE), lam_init=lam_init)
    tm_p, tm_s = _tile(np_, ROUTE_TILE), _tile(ns, ROUTE_TILE)
    x1_p, gate_p, lpos_p, cnt_p = _postmix(
        xp, py_p, ay_p, w_gate, wpo, wao, wout, g1, b1, wr, br,
        tm=max(tm_p, _tile(np_, DENSE_TILE)), rt=tm_p, dn_alpha=dn_alpha)

    xs_ = x_sample.reshape(ns, D_MODEL)
    cos_s, sin_s, _, _, icnt_s = _position_tables(past, ss)
    hist_s = jnp.concatenate([jnp.zeros((bs, 1, POOL_DIM), F32), state_pool[0]], axis=1)
    q_s, k_s, v_s, py_s, pnew_s = _inproj_sample(
        xs_, w_pqv, w_k, cos_s, sin_s, icnt_s, hist_s, wmix, pscale, n_streams=bs, seq=ss)
    kct = jnp.transpose(cache_k[0], (0, 2, 3, 4, 1)).reshape(bs, QK_DIM, past)
    vc = cache_v[0].reshape(bs, past * N_HEADS, V_DIM)
    ay_s = _attn_sample(lam_vecs, norm_g, q_s, kct, vc, k_s, v_s, n_streams=bs, tn=ss, past=past,
                        tk=_tile(past, 1024), lam_init=lam_init)
    x1_s, gate_s, lpos_s, cnt_s = _postmix(
        xs_, py_s, ay_s, w_gate, wpo, wao, wout, g1, b1, wr, br, tm=tm_s, rt=tm_s, dn_alpha=dn_alpha)

    ntp = np_ // tm_p
    cnt = jnp.concatenate([cnt_p[:, 0, :], cnt_s[:, 0, :]], axis=0).astype(jnp.int32)
    n_tiles = cnt.shape[0]
    group = jnp.sum(cnt, axis=0)
    padded = (group + EXPERT_BLOCK - 1) // EXPERT_BLOCK * EXPERT_BLOCK
    pad_end = jnp.cumsum(padded).astype(jnp.int32)
    run_start = (pad_end - padded)[None, :] + jnp.cumsum(cnt, axis=0) - cnt
    run_off = jnp.cumsum(cnt, axis=1) - cnt
    tail = jnp.where(padded > 0, pad_end - EXPERT_BLOCK, -1).astype(jnp.int32)
    max_rows = (np_ + ns) * TOP_K + n_tiles * N_EXPERTS * (RUN_ALIGN - 1) + N_EXPERTS * (EXPERT_BLOCK - 1)
    n_blocks = -(-max_rows // EXPERT_BLOCK)
    n_used = pad_end[-1:] // EXPERT_BLOCK
    blk_start = jnp.arange(n_blocks, dtype=jnp.int32) * EXPERT_BLOCK
    blk_expert = jnp.minimum(jnp.sum((blk_start[:, None] >= pad_end[None, :]).astype(jnp.int32), axis=1),
                             N_EXPERTS - 1)
    tables = (run_start, run_off, cnt, jnp.sum(cnt, axis=1))
    tabs_p = tuple(a[:ntp].reshape(-1).astype(jnp.int32) for a in tables)
    tabs_s = tuple(a[ntp:].reshape(-1).astype(jnp.int32) for a in tables)

    xsorted = _dispatch_first(tail, n_used, tabs_p, lpos_p, x1_p, tm=tm_p, n_blocks=n_blocks)
    xsorted = _dispatch_next(tabs_s, lpos_s, x1_s, xsorted, tm=tm_s)
    blk = jnp.arange(n_blocks, dtype=jnp.int32)
    later = (blk[None, :] > blk[:, None]) & (blk_expert[None, :] != blk_expert[:, None]) & (blk[None, :] < n_used)
    next_expert = jnp.where(jnp.any(later, axis=1), blk_expert[jnp.argmax(later, axis=1)], -1).astype(jnp.int32)
    group_end = (pad_end - padded + group)[blk_expert]
    blk_rows = jnp.clip(group_end - blk_start, 0, EXPERT_BLOCK).astype(jnp.int32)
    yb = _experts(blk_expert, next_expert, blk_rows, n_used, xsorted, w_expert_in[0], b_glu, b_lin,
                  w_expert_out[0], b_eo)
    y_p = _combine(tabs_p, gate_p, lpos_p, x1_p, g2, b2, yb, tm=tm_p, dn_alpha=dn_alpha)
    y_s = _combine(tabs_s, gate_s, lpos_s, x1_s, g2, b2, yb, tm=tm_s, dn_alpha=dn_alpha)

    k_prompt = jnp.transpose(kt_p.reshape(bp, N_HEADS, 2, HEAD_DIM, sp), (0, 4, 1, 2, 3))
    return (
        y_p.reshape(bp, sp, D_MODEL),
        y_s.reshape(bs, ss, D_MODEL),
        k_prompt[None],
        v_p.reshape(1, bp, sp, N_HEADS, V_DIM),
        pnew_p[:, 1:].reshape(1, bp, POOL_HIST, POOL_DIM),
        k_s.reshape(1, bs, ss, N_HEADS, 2, HEAD_DIM),
        v_s.reshape(1, bs, ss, N_HEADS, V_DIM),
        pnew_s[:, 1:].reshape(1, bs, POOL_HIST, POOL_DIM),
    )
```

```python
import functools
import math

import jax
import jax.numpy as jnp
from jax import lax
from jax.experimental import pallas as pl
from jax.experimental.pallas import tpu as pltpu

D_MODEL = 1024
CHUNK = 64
POOL_WINDOWS = (2, 4, 8, 16)
POOL_GROUP_DIM = 128
POOL_DIM = len(POOL_WINDOWS) * POOL_GROUP_DIM
POOL_HIST = max(POOL_WINDOWS) - 1
HIST_ROWS = POOL_HIST + 1
N_HEADS = 8
HEAD_DIM = 64
HALF_DIM = HEAD_DIM // 2
V_DIM = 2 * HEAD_DIM
QK_DIM = N_HEADS * 2 * HEAD_DIM
ATTN_V_WIDTH = N_HEADS * V_DIM
ATTN_SCALE = HEAD_DIM ** -0.5
LOG2_E = math.log2(math.e)
ROPE_THETA = 10000.0
SUBLN_EPS = 1e-5
N_EXPERTS = 32
TOP_K = 4
D_EXPERT = 1024
SWIGLU_LIMIT = 7.0
SWIGLU_ALPHA = 1.702
LN_EPS = 1e-5
NEG_INF = -1e30
LANES = 128
MXU_DIM = 256

F32 = jnp.float32
BF16 = jnp.bfloat16

VMEM_LIMIT = 56 * 1024 * 1024
EXPERT_BLOCK = 256
SUBLANES = 8
RUN_ALIGN = SUBLANES
ROUTE_TILE = 256
DENSE_TILE = 512
ATTN_Q_TILE = 256
RUN_BITS = tuple(range(3, 9))


def _dot(a, b):
    return jnp.dot(a, b, preferred_element_type=F32)


def _dot_nt(a, b):
    return lax.dot_general(a, b, (((1,), (1,)), ((), ())), preferred_element_type=F32)


def _params(semantics):
    return pltpu.CompilerParams(dimension_semantics=semantics, vmem_limit_bytes=VMEM_LIMIT)


def _pool_branch(u, icnt_ref, hist_ref, wmix_ref, pscale_ref, py_ref, pnew_ref, ext_ref, *, bb, tm):
    @pl.when(pl.program_id(1) == 0)
    def _():
        ext_ref[:, 0:HIST_ROWS, :] = hist_ref[...]

    for b in range(bb):
        ext_ref[b, HIST_ROWS:HIST_ROWS + tm, :] = u[b * tm:(b + 1) * tm]
    for b in range(bb):
        for g, w in enumerate(POOL_WINDOWS):
            cols = slice(g * POOL_GROUP_DIM, (g + 1) * POOL_GROUP_DIM)
            cur = ext_ref[b, HIST_ROWS:HIST_ROWS + tm, cols]
            acc = cur
            for j in range(1, w):
                acc = acc + ext_ref[b, HIST_ROWS - j:HIST_ROWS - j + tm, cols]
            d = acc * icnt_ref[:, cols] - cur
            y = _dot(d.astype(BF16), wmix_ref[g]) * pscale_ref[:, cols]
            py_ref[b * tm:(b + 1) * tm, cols] = y.astype(BF16)
    tail = ext_ref[:, tm:tm + HIST_ROWS, :]
    pnew_ref[...] = tail
    ext_ref[:, 0:HIST_ROWS, :] = tail


def _rope_rows(z, cos, sin):
    lane = lax.broadcasted_iota(jnp.int32, z.shape, 1)
    first_half = (lane % HEAD_DIM) < HALF_DIM
    partner = jnp.where(first_half, pltpu.roll(z, LANES - HALF_DIM, 1), pltpu.roll(z, HALF_DIM, 1))
    return z * cos + partner * sin


def _inproj_prompt_kernel(x_ref, w_ref, wkt_ref, cos_ref, sin_ref, cost_ref, sint_ref, icnt_ref, hist_ref,
                          wmix_ref, pscale_ref, q_ref, kt_ref, ktb_ref, v_ref, vb_ref, py_ref, pnew_ref,
                          ext_ref, *, tm):
    x = x_ref[...].astype(BF16)
    _pool_branch(_dot(x, w_ref[:, 0:POOL_DIM]), icnt_ref, hist_ref, wmix_ref, pscale_ref, py_ref, pnew_ref,
                 ext_ref, bb=1, tm=tm)

    cos, sin = cos_ref[...], sin_ref[...]
    hq = _dot(x, w_ref[:, POOL_DIM:POOL_DIM + QK_DIM])
    for h in range(N_HEADS):
        sl = slice(h * V_DIM, (h + 1) * V_DIM)
        q_ref[:, sl] = (_rope_rows(hq[:, sl], cos, sin) * (ATTN_SCALE * LOG2_E)).astype(BF16)

    hkt = _dot_nt(wkt_ref[...], x)
    cost, sint = cost_ref[...], sint_ref[...]
    for hc in range(2 * N_HEADS):
        r0 = hc * HEAD_DIM
        x1 = hkt[r0:r0 + HALF_DIM]
        x2 = hkt[r0 + HALF_DIM:r0 + HEAD_DIM]
        o1 = x1 * cost - x2 * sint
        o2 = x2 * cost + x1 * sint
        kt_ref[r0:r0 + HALF_DIM, :] = o1
        kt_ref[r0 + HALF_DIM:r0 + HEAD_DIM, :] = o2
        ktb_ref[r0:r0 + HALF_DIM, :] = o1.astype(BF16)
        ktb_ref[r0 + HALF_DIM:r0 + HEAD_DIM, :] = o2.astype(BF16)

    hv = _dot(x, w_ref[:, POOL_DIM + QK_DIM:POOL_DIM + QK_DIM + ATTN_V_WIDTH])
    vb_ref[...] = hv.astype(BF16)
    for h in range(N_HEADS):
        v_ref[pl.ds(h, tm, stride=N_HEADS), :] = hv[:, h * V_DIM:(h + 1) * V_DIM]


def _inproj_prompt(x2d, w_pqv, wkt, cos, sin, cost, sint, icnt, hist, wmix, pscale, *, n_streams, seq, tm):
    n = n_streams * seq
    nt = seq // tm
    row_map = lambda b, t: (b * nt + t, 0)
    const2 = lambda b, t: (0, 0)
    out_shape = (
        jax.ShapeDtypeStruct((n, QK_DIM), BF16),
        jax.ShapeDtypeStruct((n_streams, QK_DIM, seq), F32),
        jax.ShapeDtypeStruct((n_streams, QK_DIM, seq), BF16),
        jax.ShapeDtypeStruct((n_streams, seq * N_HEADS, V_DIM), F32),
        jax.ShapeDtypeStruct((n, ATTN_V_WIDTH), BF16),
        jax.ShapeDtypeStruct((n, POOL_DIM), BF16),
        jax.ShapeDtypeStruct((n_streams, HIST_ROWS, POOL_DIM), F32),
    )
    return pl.pallas_call(
        functools.partial(_inproj_prompt_kernel, tm=tm),
        out_shape=out_shape,
        grid=(n_streams, nt),
        in_specs=[
            pl.BlockSpec((tm, D_MODEL), row_map),
            pl.BlockSpec(w_pqv.shape, const2),
            pl.BlockSpec(wkt.shape, const2),
            pl.BlockSpec((tm, LANES), lambda b, t: (t, 0)),
            pl.BlockSpec((tm, LANES), lambda b, t: (t, 0)),
            pl.BlockSpec((HALF_DIM, tm), lambda b, t: (0, t)),
            pl.BlockSpec((HALF_DIM, tm), lambda b, t: (0, t)),
            pl.BlockSpec((tm, POOL_DIM), lambda b, t: (t, 0)),
            pl.BlockSpec((1, HIST_ROWS, POOL_DIM), lambda b, t: (b, 0, 0)),
            pl.BlockSpec((len(POOL_WINDOWS), POOL_GROUP_DIM, POOL_GROUP_DIM), lambda b, t: (0, 0, 0)),
            pl.BlockSpec((1, POOL_DIM), const2),
        ],
        out_specs=(
            pl.BlockSpec((tm, QK_DIM), row_map),
            pl.BlockSpec((None, QK_DIM, tm), lambda b, t: (b, 0, t)),
            pl.BlockSpec((None, QK_DIM, tm), lambda b, t: (b, 0, t)),
            pl.BlockSpec((None, tm * N_HEADS, V_DIM), lambda b, t: (b, t, 0)),
            pl.BlockSpec((tm, ATTN_V_WIDTH), row_map),
            pl.BlockSpec((tm, POOL_DIM), row_map),
            pl.BlockSpec((1, HIST_ROWS, POOL_DIM), lambda b, t: (b, 0, 0)),
        ),
        scratch_shapes=[pltpu.VMEM((1, HIST_ROWS + tm, POOL_DIM), F32)],
        compiler_params=_params(("arbitrary", "arbitrary")),
        name="inproj_prompt",
    )(x2d, w_pqv, wkt, cos, sin, cost, sint, icnt, hist, wmix, pscale)


def _inproj_sample_kernel(x_ref, w_ref, wk_ref, cos_ref, sin_ref, icnt_ref, hist_ref, wmix_ref, pscale_ref,
                          q_ref, k_ref, v_ref, py_ref, pnew_ref, ext_ref, *, bb, tm):
    x = x_ref[...].astype(BF16)
    _pool_branch(_dot(x, w_ref[:, 0:POOL_DIM]), icnt_ref, hist_ref, wmix_ref, pscale_ref, py_ref, pnew_ref,
                 ext_ref, bb=bb, tm=tm)
    cos = jnp.concatenate([cos_ref[...]] * bb, axis=0)
    sin = jnp.concatenate([sin_ref[...]] * bb, axis=0)
    hq = _dot(x, w_ref[:, POOL_DIM:POOL_DIM + QK_DIM])
    hk = _dot(x, wk_ref[...])
    for h in range(N_HEADS):
        sl = slice(h * V_DIM, (h + 1) * V_DIM)
        q_ref[:, sl] = (_rope_rows(hq[:, sl], cos, sin) * ATTN_SCALE).astype(BF16)
        k_ref[:, sl] = _rope_rows(hk[:, sl], cos, sin)
    v_ref[...] = _dot(x, w_ref[:, POOL_DIM + QK_DIM:POOL_DIM + QK_DIM + ATTN_V_WIDTH])


def _inproj_sample(x2d, w_pqv, wk, cos, sin, icnt, hist, wmix, pscale, *, n_streams, seq):
    n = n_streams * seq
    const2 = lambda i, t: (0, 0)
    const3 = lambda i, t: (0, 0, 0)
    out_shape = (
        jax.ShapeDtypeStruct((n, QK_DIM), BF16),
        jax.ShapeDtypeStruct((n, QK_DIM), F32),
        jax.ShapeDtypeStruct((n, ATTN_V_WIDTH), F32),
        jax.ShapeDtypeStruct((n, POOL_DIM), BF16),
        jax.ShapeDtypeStruct((n_streams, HIST_ROWS, POOL_DIM), F32),
    )
    return pl.pallas_call(
        functools.partial(_inproj_sample_kernel, bb=n_streams, tm=seq),
        out_shape=out_shape,
        grid=(1, 1),
        in_specs=[
            pl.BlockSpec((n, D_MODEL), const2),
            pl.BlockSpec(w_pqv.shape, const2),
            pl.BlockSpec(wk.shape, const2),
            pl.BlockSpec((seq, LANES), const2),
            pl.BlockSpec((seq, LANES), const2),
            pl.BlockSpec((seq, POOL_DIM), const2),
            pl.BlockSpec((n_streams, HIST_ROWS, POOL_DIM), const3),
            pl.BlockSpec((len(POOL_WINDOWS), POOL_GROUP_DIM, POOL_GROUP_DIM), const3),
            pl.BlockSpec((1, POOL_DIM), const2),
        ],
        out_specs=(
            pl.BlockSpec((n, QK_DIM), const2),
            pl.BlockSpec((n, QK_DIM), const2),
            pl.BlockSpec((n, ATTN_V_WIDTH), const2),
            pl.BlockSpec((n, POOL_DIM), const2),
            pl.BlockSpec((n_streams, HIST_ROWS, POOL_DIM), const3),
        ),
        scratch_shapes=[pltpu.VMEM((n_streams, HIST_ROWS + seq, POOL_DIM), F32)],
        compiler_params=_params(("arbitrary", "arbitrary")),
        name="inproj_sample",
    )(x2d, w_pqv, wk, cos, sin, icnt, hist, wmix, pscale)


def _lambda_value(lam_ref, lam_init):
    lv = lam_ref[...]
    s1 = jnp.sum(lv[0:1] * lv[1:2], axis=1, keepdims=True)
    s2 = jnp.sum(lv[2:3] * lv[3:4], axis=1, keepdims=True)
    return jnp.exp(s1) - jnp.exp(s2) + lam_init


def _head_norm(o, g, lam_init):
    ms = jnp.mean(o * o, axis=-1, keepdims=True)
    return o * lax.rsqrt(ms + SUBLN_EPS) * g * (1.0 - lam_init)


def _attn_prompt_kernel(lam_ref, g_ref, q_ref, kt_ref, v_ref, o_ref, vext, *, seq, tq, lam_init):
    lam = _lambda_value(lam_ref, lam_init)
    g = g_ref[...]
    r = lax.broadcasted_iota(jnp.int32, (tq, tq), 0)
    c = lax.broadcasted_iota(jnp.int32, (tq, tq), 1)
    diag_visible = (c // CHUNK) <= (r // CHUNK)
    lane = lax.broadcasted_iota(jnp.int32, (tq, V_DIM), 1)
    vext[:, 0:V_DIM] = v_ref[...]
    vlane = lax.broadcasted_iota(jnp.int32, (seq, V_DIM), 1)
    vext[:, V_DIM:] = jnp.where(vlane == 0, 1.0, 0.0).astype(vext.dtype)

    def scores(i):
        lo = i * tq
        q = q_ref[lo:lo + tq, :]
        zero = jnp.zeros_like(q)
        qc = (jnp.where(lane < HEAD_DIM, q, zero), jnp.where(lane >= HEAD_DIM, q, zero))
        out = []
        for k in range(2):
            sd = jnp.where(diag_visible, _dot(qc[k], kt_ref[:, lo:lo + tq]), NEG_INF)
            m = jnp.max(sd, axis=1, keepdims=True)
            sp = None
            if i > 0:
                sp = _dot(qc[k], kt_ref[:, 0:lo])
                m = jnp.maximum(m, jnp.max(sp, axis=1, keepdims=True))
            out.append((sd, sp, m))
        return out

    def finish(i, parts):
        lo = i * tq
        normed = []
        for sd, sp, m in parts:
            acc = _dot(jnp.exp2(sd - m).astype(BF16), vext[lo:lo + tq, :])
            if sp is not None:
                acc = acc + _dot(jnp.exp2(sp - m).astype(BF16), vext[0:lo, :])
            normed.append(acc[:, 0:V_DIM] / acc[:, V_DIM:V_DIM + 1])
        o = normed[0] - lam * normed[1]
        o_ref[lo:lo + tq, :] = _head_norm(o, g, lam_init).astype(o_ref.dtype)

    nq = seq // tq
    pending = scores(0)
    for i in range(nq):
        upcoming = scores(i + 1) if i + 1 < nq else None
        finish(i, pending)
        pending = upcoming


def _attn_prompt(lam_vecs, norm_g, q, ktb, vb, *, n_streams, seq, tq, lam_init):
    return pl.pallas_call(
        functools.partial(_attn_prompt_kernel, seq=seq, tq=tq, lam_init=lam_init),
        out_shape=jax.ShapeDtypeStruct((n_streams * seq, ATTN_V_WIDTH), BF16),
        grid=(n_streams, N_HEADS),
        in_specs=[
            pl.BlockSpec((4, HEAD_DIM), lambda b, h: (0, 0)),
            pl.BlockSpec((1, V_DIM), lambda b, h: (0, 0)),
            pl.BlockSpec((seq, V_DIM), lambda b, h: (b, h)),
            pl.BlockSpec((None, V_DIM, seq), lambda b, h: (b, h, 0)),
            pl.BlockSpec((seq, V_DIM), lambda b, h: (b, h)),
        ],
        out_specs=pl.BlockSpec((seq, V_DIM), lambda b, h: (b, h)),
        scratch_shapes=[pltpu.VMEM((seq, MXU_DIM), BF16)],
        compiler_params=_params(("arbitrary", "arbitrary")),
        name="attn_prompt",
    )(lam_vecs, norm_g, q, ktb, vb)


def _attn_sample_kernel(lam_ref, g_ref, q_ref, kc_ref, vc_ref, kn_ref, vn_ref, o_ref,
                        s_scr, w_scr, wn_scr, m_scr, acc_scr, qbd_scr, *, nk, tn, past, lam_init):
    j = pl.program_id(1)
    half = N_HEADS * tn

    @pl.when(j == 0)
    def _():
        q = q_ref[...]
        qt = jnp.concatenate([q] * (2 * N_HEADS), axis=0)
        r = lax.broadcasted_iota(jnp.int32, qt.shape, 0)
        l = lax.broadcasted_iota(jnp.int32, qt.shape, 1)
        keep = ((r // half) == ((l % V_DIM) // HEAD_DIM)) & (((r % half) // tn) == (l // V_DIM))
        qbd_scr[...] = jnp.where(keep, qt, jnp.zeros_like(qt))
        m_scr[...] = jnp.full(m_scr.shape, NEG_INF, F32)
        acc_scr[...] = jnp.zeros(acc_scr.shape, F32)

    @pl.when(j < nk)
    def _():
        s = _dot(qbd_scr[...], kc_ref[...].astype(BF16))
        s_scr[j] = s
        m_scr[...] = jnp.maximum(m_scr[...], jnp.max(s, axis=1, keepdims=True))

    @pl.when(j == nk - 1)
    def _():
        lam = _lambda_value(lam_ref, lam_init)
        sn = _dot_nt(qbd_scr[...], kn_ref[...].astype(BF16))
        qpos = past + (lax.broadcasted_iota(jnp.int32, sn.shape, 0) % tn)
        kpos = past + lax.broadcasted_iota(jnp.int32, sn.shape, 1)
        sn = jnp.where((kpos // CHUNK) <= (qpos // CHUNK), sn, NEG_INF)
        m = jnp.maximum(m_scr[...], jnp.max(sn, axis=1, keepdims=True))
        pn = jnp.exp(sn - m)
        l = jnp.sum(pn, axis=1, keepdims=True)
        for c in range(nk):
            p = jnp.exp(s_scr[c] - m)
            s_scr[c] = p
            l = l + jnp.sum(p, axis=1, keepdims=True)
        r0 = 1.0 / l[:half]
        r1 = lam / l[half:]
        wn_scr[...] = pn[:half] * r0 - pn[half:] * r1
        for c in range(nk):
            p = s_scr[c]
            w_scr[c] = (p[:half] * r0 - p[half:] * r1).astype(BF16)

    def v_rows(ref):
        tk = ref.shape[0] // N_HEADS
        return jnp.concatenate([ref[pl.ds(h, tk, stride=N_HEADS), :] for h in range(N_HEADS)],
                               axis=1).astype(BF16)

    @pl.when(j >= nk)
    def _():
        acc_scr[...] += _dot(w_scr[j - nk], v_rows(vc_ref))

    @pl.when(j == 2 * nk - 1)
    def _():
        acc = acc_scr[...] + _dot(wn_scr[...].astype(BF16), vn_ref[...].astype(BF16))
        g = g_ref[...]
        for h in range(N_HEADS):
            o = acc[h * tn:(h + 1) * tn, h * V_DIM:(h + 1) * V_DIM]
            o_ref[:, h * V_DIM:(h + 1) * V_DIM] = _head_norm(o, g, lam_init).astype(o_ref.dtype)


def _attn_sample(lam_vecs, norm_g, q, kct, vc, kn, vn, *, n_streams, tn, past, tk, lam_init):
    nk = past // tk
    rows = 2 * N_HEADS * tn
    half = N_HEADS * tn
    return pl.pallas_call(
        functools.partial(_attn_sample_kernel, nk=nk, tn=tn, past=past, lam_init=lam_init),
        out_shape=jax.ShapeDtypeStruct((n_streams * tn, ATTN_V_WIDTH), BF16),
        grid=(n_streams, 2 * nk),
        in_specs=[
            pl.BlockSpec((4, HEAD_DIM), lambda b, j: (0, 0)),
            pl.BlockSpec((1, V_DIM), lambda b, j: (0, 0)),
            pl.BlockSpec((tn, QK_DIM), lambda b, j: (b, 0)),
            pl.BlockSpec((None, QK_DIM, tk), lambda b, j: (b, 0, jnp.minimum(j, nk - 1))),
            pl.BlockSpec((None, tk * N_HEADS, V_DIM), lambda b, j: (b, jnp.maximum(j - nk, 0), 0)),
            pl.BlockSpec((tn, QK_DIM), lambda b, j: (b, 0)),
            pl.BlockSpec((tn, ATTN_V_WIDTH), lambda b, j: (b, 0)),
        ],
        out_specs=pl.BlockSpec((tn, ATTN_V_WIDTH), lambda b, j: (b, 0)),
        scratch_shapes=[
            pltpu.VMEM((nk, rows, tk), F32),
            pltpu.VMEM((nk, half, tk), BF16),
            pltpu.VMEM((half, tn), F32),
            pltpu.VMEM((rows, 1), F32),
            pltpu.VMEM((half, ATTN_V_WIDTH), F32),
            pltpu.VMEM((rows, QK_DIM), BF16),
        ],
        compiler_params=_params(("arbitrary", "arbitrary")),
        name="attn_sample",
    )(lam_vecs, norm_g, q, kct, vc, kn, vn)


def _layer_norm(z, g, b):
    mu = jnp.mean(z, axis=-1, keepdims=True)
    zc = z - mu
    var = jnp.mean(zc * zc, axis=-1, keepdims=True)
    return zc * lax.rsqrt(var + LN_EPS) * g + b


def _postmix_kernel(x_ref, py_ref, ay_ref, wg_ref, wpo_ref, wao_ref, wout_ref, g1_ref, b1_ref,
                    wr_ref, br_ref, x1_ref, gate_ref, lpos_ref, cnt_ref, *, tm, rt, dn_alpha):
    x = x_ref[...]
    xb = x.astype(BF16)
    gates = jax.nn.sigmoid(_dot(xb, wg_ref[...]))
    a = _dot(py_ref[...], wpo_ref[...])
    b = _dot(ay_ref[...], wao_ref[...])
    mixed = gates[:, :D_MODEL] * a + gates[:, D_MODEL:] * b
    mo = _dot(mixed.astype(BF16), wout_ref[...])
    x1 = _layer_norm(dn_alpha * x + mo, g1_ref[...], b1_ref[...])
    x1_ref[...] = x1

    logits = _dot(x1.astype(BF16), wr_ref[...]) + br_ref[...]
    for sub in range(tm // rt):
        rows = slice(sub * rt, (sub + 1) * rt)
        gate_ref[rows, :], lpos_ref[rows, :], cnt_ref[sub] = _route_tile(logits[rows], rt)


def _route_tile(logits, tm):
    lane = lax.broadcasted_iota(jnp.int32, logits.shape, 1)
    work = logits
    vals, idxs = [], []
    for _ in range(TOP_K):
        mx = jnp.max(work, axis=1, keepdims=True)
        ix = jnp.min(jnp.where(work == mx, lane, N_EXPERTS), axis=1, keepdims=True)
        vals.append(mx)
        idxs.append(ix)
        work = jnp.where(lane == ix, -jnp.inf, work)
    exps = [jnp.exp(v - vals[0]) for v in vals]
    denom = exps[0] + exps[1] + exps[2] + exps[3]

    onehot = jnp.zeros(logits.shape, F32)
    for ix in idxs:
        onehot = onehot + (lane == ix).astype(F32)
    r = lax.broadcasted_iota(jnp.int32, (tm, tm), 0)
    c = lax.broadcasted_iota(jnp.int32, (tm, tm), 1)
    tri = jnp.where(c < r, 1.0, 0.0).astype(BF16)
    earlier = _dot(tri, onehot.astype(BF16))
    cnt = jnp.sum(onehot, axis=0, keepdims=True)
    units = jnp.floor((cnt + (RUN_ALIGN - 1.0)) * (1.0 / RUN_ALIGN))
    er = lax.broadcasted_iota(jnp.int32, (N_EXPERTS, N_EXPERTS), 0)
    ec = lax.broadcasted_iota(jnp.int32, (N_EXPERTS, N_EXPERTS), 1)
    upper = jnp.where(er < ec, 1.0, 0.0).astype(BF16)
    run_off = _dot(jnp.broadcast_to(units, (SUBLANES, N_EXPERTS)).astype(BF16), upper)[0:1] * float(RUN_ALIGN)
    pos = earlier + run_off

    lane_out = lax.broadcasted_iota(jnp.int32, (tm, LANES), 1)
    lpos_out = jnp.zeros((tm, LANES), jnp.int32)
    gate_out = jnp.zeros((tm, LANES), F32)
    for k in range(TOP_K):
        lpos_k = jnp.sum(jnp.where(lane == idxs[k], pos, 0.0), axis=1, keepdims=True).astype(jnp.int32)
        lpos_out = jnp.where(lane_out == k, lpos_k, lpos_out)
        gate_out = jnp.where(lane_out == k, exps[k] / denom, gate_out)
    return gate_out, lpos_out, units * float(RUN_ALIGN)


def _postmix(x2d, py, ay, wg, wpo, wao, wout, g1, b1, wr, br, *, tm, rt, dn_alpha):
    n = x2d.shape[0]
    row = lambda i: (i, 0)
    const = lambda i: (0, 0)
    out_shape = (
        jax.ShapeDtypeStruct((n, D_MODEL), F32),
        jax.ShapeDtypeStruct((n, LANES), F32),
        jax.ShapeDtypeStruct((n, LANES), jnp.int32),
        jax.ShapeDtypeStruct((n // rt, 1, N_EXPERTS), F32),
    )
    return pl.pallas_call(
        functools.partial(_postmix_kernel, tm=tm, rt=rt, dn_alpha=dn_alpha),
        out_shape=out_shape,
        grid=(n // tm,),
        in_specs=[
            pl.BlockSpec((tm, D_MODEL), row),
            pl.BlockSpec((tm, POOL_DIM), row),
            pl.BlockSpec((tm, ATTN_V_WIDTH), row),
            pl.BlockSpec(wg.shape, const),
            pl.BlockSpec(wpo.shape, const),
            pl.BlockSpec(wao.shape, const),
            pl.BlockSpec(wout.shape, const),
            pl.BlockSpec((1, D_MODEL), const),
            pl.BlockSpec((1, D_MODEL), const),
            pl.BlockSpec(wr.shape, const),
            pl.BlockSpec((1, N_EXPERTS), const),
        ],
        out_specs=(
            pl.BlockSpec((tm, D_MODEL), row),
            pl.BlockSpec((tm, LANES), row),
            pl.BlockSpec((tm, LANES), row),
            pl.BlockSpec((tm // rt, 1, N_EXPERTS), lambda i: (i, 0, 0)),
        ),
        compiler_params=_params(("arbitrary",)),
        name="postmix",
    )(x2d, py, ay, wg, wpo, wao, wout, g1, b1, wr, br)


def _local_rows(tm):
    return TOP_K * tm + N_EXPERTS * RUN_ALIGN


def _for_each_run_chunk(tile, tab_refs, local_buf, sorted_ref, sem, to_sorted, fn):
    gs_ref, off_ref, cnt_ref = tab_refs[:3]

    def per_expert(e, carry):
        t = tile * N_EXPERTS + e
        cnt, off, gs = cnt_ref[t], off_ref[t], gs_ref[t]
        for b in RUN_BITS:
            size = 1 << b

            @pl.when((cnt & size) != 0)
            def _():
                lower = cnt & (size - 1)
                loc = local_buf.at[pl.ds(pl.multiple_of(off + lower, RUN_ALIGN), size)]
                srt = sorted_ref.at[pl.ds(pl.multiple_of(gs + lower, RUN_ALIGN), size)]
                fn(pltpu.make_async_copy(loc, srt, sem) if to_sorted else pltpu.make_async_copy(srt, loc, sem))
        return carry

    lax.fori_loop(0, N_EXPERTS, per_expert, 0)


def _wait_run_rows(total, local_buf, sorted_ref, sem, to_sorted):
    rows = local_buf.shape[0]
    for b in range(RUN_BITS[0], rows.bit_length()):
        size = 1 << b

        @pl.when((total & size) != 0)
        def _():
            loc, srt = local_buf.at[pl.ds(0, size)], sorted_ref.at[pl.ds(0, size)]
            (pltpu.make_async_copy(loc, srt, sem) if to_sorted else pltpu.make_async_copy(srt, loc, sem)).wait()


def _dispatch_tile(tab_refs, lpos_ref, x_ref, xs_ref, xloc, sem, tm):
    lpos = lpos_ref[...]
    col = lax.broadcasted_iota(jnp.int32, (tm, _local_rows(tm)), 1)
    hit = col == lpos[:, 0:1]
    for k in range(1, TOP_K):
        hit = jnp.logical_or(hit, col == lpos[:, k:k + 1])
    perm_t = jnp.where(hit, 1.0, 0.0).astype(BF16)
    tile = pl.program_id(0)
    slot = tile % 2
    xloc[slot] = lax.dot_general(perm_t, x_ref[...].astype(BF16), (((0,), (0,)), ((), ())),
                                 preferred_element_type=F32)

    def copies(t, s, fn):
        _for_each_run_chunk(t, tab_refs, xloc.at[s], xs_ref, sem.at[s], True, fn)

    copies(tile, slot, lambda cp: cp.start())
    tot_ref = tab_refs[3]

    @pl.when(tile > 0)
    def _():
        _wait_run_rows(tot_ref[tile - 1], xloc.at[1 - slot], xs_ref, sem.at[1 - slot], True)

    @pl.when(tile == pl.num_programs(0) - 1)
    def _():
        _wait_run_rows(tot_ref[tile], xloc.at[slot], xs_ref, sem.at[slot], True)


def _dispatch_first_kernel(tail_ref, nu_ref, gs_ref, off_ref, cnt_ref, tot_ref, lpos_ref, x_ref, xs_ref,
                           xloc, zbuf, sem, zsem, *, tm, n_blocks):
    @pl.when(pl.program_id(0) == 0)
    def _():
        zbuf[...] = jnp.zeros(zbuf.shape, zbuf.dtype)

        def zero_copy(row):
            row = pl.multiple_of(row, EXPERT_BLOCK)
            return pltpu.make_async_copy(zbuf, xs_ref.at[pl.ds(row, EXPERT_BLOCK)], zsem)

        def over_blocks(fn):
            for e in range(N_EXPERTS):
                @pl.when(tail_ref[e] >= 0)
                def _():
                    fn(zero_copy(tail_ref[e]))
            lax.fori_loop(nu_ref[0], n_blocks, lambda b, c: (fn(zero_copy(b * EXPERT_BLOCK)), c)[1], 0)

        over_blocks(lambda cp: cp.start())
        over_blocks(lambda cp: cp.wait())

    _dispatch_tile((gs_ref, off_ref, cnt_ref, tot_ref), lpos_ref, x_ref, xs_ref, xloc, sem, tm)


def _dispatch_next_kernel(gs_ref, off_ref, cnt_ref, tot_ref, lpos_ref, x_ref, xs_in_ref, xs_ref, xloc, sem,
                          *, tm):
    del xs_in_ref
    _dispatch_tile((gs_ref, off_ref, cnt_ref, tot_ref), lpos_ref, x_ref, xs_ref, xloc, sem, tm)


def _dispatch_first(tail, n_used, tabs, lpos, x1, *, tm, n_blocks):
    n = x1.shape[0]
    grid_spec = pltpu.PrefetchScalarGridSpec(
        num_scalar_prefetch=6,
        grid=(n // tm,),
        in_specs=[pl.BlockSpec((tm, LANES), lambda i, *_: (i, 0)),
                  pl.BlockSpec((tm, D_MODEL), lambda i, *_: (i, 0))],
        out_specs=pl.BlockSpec(memory_space=pl.ANY),
        scratch_shapes=[pltpu.VMEM((2, _local_rows(tm), D_MODEL), F32),
                        pltpu.VMEM((EXPERT_BLOCK, D_MODEL), F32),
                        pltpu.SemaphoreType.DMA((2,)), pltpu.SemaphoreType.DMA],
    )
    return pl.pallas_call(
        functools.partial(_dispatch_first_kernel, tm=tm, n_blocks=n_blocks),
        out_shape=jax.ShapeDtypeStruct((n_blocks * EXPERT_BLOCK, D_MODEL), F32),
        grid_spec=grid_spec,
        compiler_params=_params(("arbitrary",)),
        name="dispatch_first",
    )(tail, n_used, *tabs, lpos, x1)


def _dispatch_next(tabs, lpos, x1, xs, *, tm):
    n = x1.shape[0]
    grid_spec = pltpu.PrefetchScalarGridSpec(
        num_scalar_prefetch=len(tabs),
        grid=(n // tm,),
        in_specs=[pl.BlockSpec((tm, LANES), lambda i, *_: (i, 0)),
                  pl.BlockSpec((tm, D_MODEL), lambda i, *_: (i, 0)),
                  pl.BlockSpec(memory_space=pl.ANY)],
        out_specs=pl.BlockSpec(memory_space=pl.ANY),
        scratch_shapes=[pltpu.VMEM((2, _local_rows(tm), D_MODEL), F32), pltpu.SemaphoreType.DMA((2,))],
    )
    return pl.pallas_call(
        functools.partial(_dispatch_next_kernel, tm=tm),
        out_shape=jax.ShapeDtypeStruct(xs.shape, xs.dtype),
        grid_spec=grid_spec,
        input_output_aliases={6: 0},
        compiler_params=_params(("arbitrary",)),
        name="dispatch_next",
    )(*tabs, lpos, x1, xs)


def _experts_kernel(be_ref, nxt_ref, nu_ref, xs_ref, bg_ref, bl_ref, bo_ref, win_hbm, wo_hbm, y_ref,
                    win_buf, wo_buf, wg_scr, wl_scr, wo_scr, sem_in, sem_out):
    i = pl.program_id(0)
    used = i < nu_ref[0]
    first_of_expert = jnp.logical_or(i == 0, be_ref[i] != be_ref[jnp.maximum(i - 1, 0)])

    def weight_copies(e):
        return (pltpu.make_async_copy(win_hbm.at[e], win_buf, sem_in),
                pltpu.make_async_copy(wo_hbm.at[e], wo_buf, sem_out))

    @pl.when(jnp.logical_not(used))
    def _():
        y_ref[...] = jnp.zeros(y_ref.shape, y_ref.dtype)

    @pl.when(i == 0)
    def _():
        for cp in weight_copies(be_ref[0]):
            cp.start()

    @pl.when(jnp.logical_and(used, first_of_expert))
    def _():
        for cp in weight_copies(be_ref[i]):
            cp.wait()
        r = lax.broadcasted_iota(jnp.int32, (MXU_DIM, MXU_DIM), 0)
        c = lax.broadcasted_iota(jnp.int32, (MXU_DIM, MXU_DIM), 1)
        src = jnp.where(c < LANES, 2 * c, 2 * (c - LANES) + 1)
        sel = jnp.where(r == src, 1.0, 0.0).astype(BF16)
        for gq in range(2 * D_EXPERT // MXU_DIM):
            blk = win_buf[:, gq * MXU_DIM:(gq + 1) * MXU_DIM].astype(BF16)
            d = _dot(blk, sel)
            wg_scr[:, gq * LANES:(gq + 1) * LANES] = d[:, :LANES].astype(BF16)
            wl_scr[:, gq * LANES:(gq + 1) * LANES] = d[:, LANES:].astype(BF16)
        wo_scr[...] = wo_buf[...].astype(BF16)

        @pl.when(nxt_ref[i] >= 0)
        def _():
            for cp in weight_copies(nxt_ref[i]):
                cp.start(priority=1)

    @pl.when(used)
    def _():
        xb = xs_ref[...].astype(BF16)
        glu = jnp.minimum(_dot(xb, wg_scr[...]) + bg_ref[...], SWIGLU_LIMIT)
        lin = jnp.clip(_dot(xb, wl_scr[...]) + bl_ref[...], -SWIGLU_LIMIT, SWIGLU_LIMIT)
        act = glu * jax.nn.sigmoid(SWIGLU_ALPHA * glu) * (lin + 1.0)
        y_ref[...] = _dot(act.astype(BF16), wo_scr[...]) + bo_ref[...]


def _experts(blk_expert, next_expert, n_used, xs, w_in, b_glu, b_lin, w_out, b_out):
    rows = xs.shape[0]
    n_blocks = rows // EXPERT_BLOCK
    wsel = lambda i, be, nxt, nu: (be[i], 0, 0)
    grid_spec = pltpu.PrefetchScalarGridSpec(
        num_scalar_prefetch=3,
        grid=(n_blocks,),
        in_specs=[
            pl.BlockSpec((EXPERT_BLOCK, D_MODEL), lambda i, be, nxt, nu: (jnp.minimum(i, nu[0] - 1), 0)),
            pl.BlockSpec((None, 1, D_EXPERT), wsel),
            pl.BlockSpec((None, 1, D_EXPERT), wsel),
            pl.BlockSpec((None, 1, D_MODEL), wsel),
            pl.BlockSpec(memory_space=pl.ANY),
            pl.BlockSpec(memory_space=pl.ANY),
        ],
        out_specs=pl.BlockSpec((EXPERT_BLOCK, D_MODEL), lambda i, be, nxt, nu: (i, 0)),
        scratch_shapes=[pltpu.VMEM((D_MODEL, 2 * D_EXPERT), F32),
                        pltpu.VMEM((D_EXPERT, D_MODEL), F32),
                        pltpu.VMEM((D_MODEL, D_EXPERT), BF16),
                        pltpu.VMEM((D_MODEL, D_EXPERT), BF16),
                        pltpu.VMEM((D_EXPERT, D_MODEL), BF16),
                        pltpu.SemaphoreType.DMA, pltpu.SemaphoreType.DMA],
    )
    return pl.pallas_call(
        _experts_kernel,
        out_shape=jax.ShapeDtypeStruct((rows, D_MODEL), F32),
        grid_spec=grid_spec,
        compiler_params=_params(("arbitrary",)),
        name="experts",
    )(blk_expert, next_expert, n_used, xs, b_glu, b_lin, b_out, w_in, w_out)


def _split_bf16(a):
    hi = a.astype(BF16)
    return hi, (a - hi.astype(F32)).astype(BF16)


def _combine_kernel(gs_ref, off_ref, cnt_ref, tot_ref, gate_ref, lpos_ref, x1_ref, g2_ref, b2_ref, yb_ref, o_ref,
                    yloc, sem, *, tm, dn_alpha):
    tile = pl.program_id(0)
    slot = tile % 2
    tabs = (gs_ref, off_ref, cnt_ref)

    def copies(t, s, fn):
        _for_each_run_chunk(t, tabs, yloc.at[s], yb_ref, sem.at[s], False, fn)

    @pl.when(tile == 0)
    def _():
        yloc[...] = jnp.zeros(yloc.shape, yloc.dtype)
        copies(tile, slot, lambda cp: cp.start())

    @pl.when(tile + 1 < pl.num_programs(0))
    def _():
        copies(tile + 1, 1 - slot, lambda cp: cp.start())

    _wait_run_rows(tot_ref[tile], yloc.at[slot], yb_ref, sem.at[slot], False)

    gate, lpos = gate_ref[...], lpos_ref[...]
    col = lax.broadcasted_iota(jnp.int32, (tm, _local_rows(tm)), 1)
    weights = jnp.zeros(col.shape, F32)
    for k in range(TOP_K):
        weights = jnp.where(col == lpos[:, k:k + 1], gate[:, k:k + 1], weights)
    w_hi, w_lo = _split_bf16(weights)
    y_hi, y_lo = _split_bf16(yloc[slot])
    y = _dot(w_hi, y_hi) + (_dot(w_hi, y_lo) + _dot(w_lo, y_hi))
    o_ref[...] = _layer_norm(dn_alpha * x1_ref[...] + y, g2_ref[...], b2_ref[...])


def _combine(tabs, gate, lpos, x1, g2, b2, yb, *, tm, dn_alpha):
    n = x1.shape[0]
    grid_spec = pltpu.PrefetchScalarGridSpec(
        num_scalar_prefetch=len(tabs),
        grid=(n // tm,),
        in_specs=[
            pl.BlockSpec((tm, LANES), lambda i, *_: (i, 0)),
            pl.BlockSpec((tm, LANES), lambda i, *_: (i, 0)),
            pl.BlockSpec((tm, D_MODEL), lambda i, *_: (i, 0)),
            pl.BlockSpec((1, D_MODEL), lambda i, *_: (0, 0)),
            pl.BlockSpec((1, D_MODEL), lambda i, *_: (0, 0)),
            pl.BlockSpec(memory_space=pl.ANY),
        ],
        out_specs=pl.BlockSpec((tm, D_MODEL), lambda i, *_: (i, 0)),
        scratch_shapes=[pltpu.VMEM((2, _local_rows(tm), D_MODEL), F32), pltpu.SemaphoreType.DMA((2,))],
    )
    return pl.pallas_call(
        functools.partial(_combine_kernel, tm=tm, dn_alpha=dn_alpha),
        out_shape=jax.ShapeDtypeStruct((n, D_MODEL), F32),
        grid_spec=grid_spec,
        compiler_params=_params(("arbitrary",)),
        name="combine",
    )(*tabs, gate, lpos, x1, g2, b2, yb)


def _position_tables(pos0, seq):
    pos = pos0 + jnp.arange(seq, dtype=jnp.int32)
    inv = ROPE_THETA ** (-jnp.arange(HALF_DIM, dtype=F32) / HALF_DIM)
    ang = pos.astype(F32)[:, None] * inv[None, :]
    cos, sin = jnp.cos(ang), jnp.sin(ang)
    cos_rows = jnp.concatenate([cos, cos, cos, cos], axis=-1)
    sin_rows = jnp.concatenate([-sin, sin, -sin, sin], axis=-1)
    icnt = jnp.concatenate(
        [jnp.broadcast_to((1.0 / jnp.minimum(pos + 1, w).astype(F32))[:, None], (seq, POOL_GROUP_DIM))
         for w in POOL_WINDOWS], axis=-1)
    return cos_rows, sin_rows, cos.T, sin.T, icnt


def _tile(n, pref):
    t = min(n, pref)
    while n % t:
        t //= 2
    return t


def kernel(x_prompt, x_sample, cache_k, cache_v, state_pool, w_in, w_pool_mix, pool_scale, w_pool_out,
           lambda_q1, lambda_k1, lambda_q2, lambda_k2, attn_norm_g, w_attn_out, w_out, ln1_g, ln1_b,
           w_router, b_router, w_expert_in, b_expert_in, w_expert_out, b_expert_out, ln2_g, ln2_b):
    depth = w_in.shape[0]
    assert depth == 1, "single-layer step"
    dn_alpha = (2.0 * depth) ** 0.25
    lam_init = 0.8 - 0.6 * math.exp(-0.3 * 0)
    bp, sp, _ = x_prompt.shape
    bs, ss, _ = x_sample.shape
    past = cache_k.shape[2]
    np_, ns = bp * sp, bs * ss

    c_q, c_k, c_v = POOL_DIM, POOL_DIM + QK_DIM, POOL_DIM + 2 * QK_DIM
    c_gate = c_v + ATTN_V_WIDTH
    w0 = w_in[0]
    w_pqv = jnp.concatenate([w0[:, :c_k], w0[:, c_v:c_gate]], axis=1).astype(BF16)
    w_k = w0[:, c_k:c_v].astype(BF16)
    w_gate = w0[:, c_gate:].astype(BF16)
    wmix = w_pool_mix[0].astype(BF16)
    pscale = pool_scale[0].reshape(1, POOL_DIM)
    wpo = w_pool_out[0].astype(BF16)
    wao = w_attn_out[0].astype(BF16)
    wout = w_out[0].astype(BF16)
    wr = w_router[0].astype(BF16)
    br = b_router[0].reshape(1, N_EXPERTS)
    lam_vecs = jnp.stack([lambda_q1[0], lambda_k1[0], lambda_q2[0], lambda_k2[0]])
    norm_g = attn_norm_g[0].reshape(1, V_DIM)
    g1, b1 = ln1_g[0].reshape(1, D_MODEL), ln1_b[0].reshape(1, D_MODEL)
    g2, b2 = ln2_g[0].reshape(1, D_MODEL), ln2_b[0].reshape(1, D_MODEL)
    b_glu = b_expert_in[0][:, 0::2].reshape(N_EXPERTS, 1, D_EXPERT)
    b_lin = b_expert_in[0][:, 1::2].reshape(N_EXPERTS, 1, D_EXPERT)
    b_eo = b_expert_out[0].reshape(N_EXPERTS, 1, D_MODEL)

    xp = x_prompt.reshape(np_, D_MODEL)
    cos_p, sin_p, cost_p, sint_p, icnt_p = _position_tables(0, sp)
    hist_p = jnp.zeros((bp, HIST_ROWS, POOL_DIM), F32)
    q_p, kt_p, ktb_p, v_p, vb_p, py_p, pnew_p = _inproj_prompt(
        xp, w_pqv, w_k.T, cos_p, sin_p, cost_p, sint_p, icnt_p, hist_p, wmix, pscale,
        n_streams=bp, seq=sp, tm=_tile(sp, DENSE_TILE))
    ay_p = _attn_prompt(lam_vecs, norm_g, q_p, ktb_p, vb_p, n_streams=bp, seq=sp,
                        tq=_tile(sp, ATTN_Q_TILE), lam_init=lam_init)
    tm_p, tm_s = _tile(np_, ROUTE_TILE), _tile(ns, ROUTE_TILE)
    x1_p, gate_p, lpos_p, cnt_p = _postmix(
        xp, py_p, ay_p, w_gate, wpo, wao, wout, g1, b1, wr, br,
        tm=max(tm_p, _tile(np_, DENSE_TILE)), rt=tm_p, dn_alpha=dn_alpha)

    xs_ = x_sample.reshape(ns, D_MODEL)
    cos_s, sin_s, _, _, icnt_s = _position_tables(past, ss)
    hist_s = jnp.concatenate([jnp.zeros((bs, 1, POOL_DIM), F32), state_pool[0]], axis=1)
    q_s, k_s, v_s, py_s, pnew_s = _inproj_sample(
        xs_, w_pqv, w_k, cos_s, sin_s, icnt_s, hist_s, wmix, pscale, n_streams=bs, seq=ss)
    kct = jnp.transpose(cache_k[0], (0, 2, 3, 4, 1)).reshape(bs, QK_DIM, past)
    vc = cache_v[0].reshape(bs, past * N_HEADS, V_DIM)
    ay_s = _attn_sample(lam_vecs, norm_g, q_s, kct, vc, k_s, v_s, n_streams=bs, tn=ss, past=past,
                        tk=_tile(past, 1024), lam_init=lam_init)
    x1_s, gate_s, lpos_s, cnt_s = _postmix(
        xs_, py_s, ay_s, w_gate, wpo, wao, wout, g1, b1, wr, br, tm=tm_s, rt=tm_s, dn_alpha=dn_alpha)

    ntp = np_ // tm_p
    cnt = jnp.concatenate([cnt_p[:, 0, :], cnt_s[:, 0, :]], axis=0).astype(jnp.int32)
    n_tiles = cnt.shape[0]
    group = jnp.sum(cnt, axis=0)
    padded = (group + EXPERT_BLOCK - 1) // EXPERT_BLOCK * EXPERT_BLOCK
    pad_end = jnp.cumsum(padded).astype(jnp.int32)
    run_start = (pad_end - padded)[None, :] + jnp.cumsum(cnt, axis=0) - cnt
    run_off = jnp.cumsum(cnt, axis=1) - cnt
    tail = jnp.where(padded > 0, pad_end - EXPERT_BLOCK, -1).astype(jnp.int32)
    max_rows = (np_ + ns) * TOP_K + n_tiles * N_EXPERTS * (RUN_ALIGN - 1) + N_EXPERTS * (EXPERT_BLOCK - 1)
    n_blocks = -(-max_rows // EXPERT_BLOCK)
    n_used = pad_end[-1:] // EXPERT_BLOCK
    blk_start = jnp.arange(n_blocks, dtype=jnp.int32) * EXPERT_BLOCK
    blk_expert = jnp.minimum(jnp.sum((blk_start[:, None] >= pad_end[None, :]).astype(jnp.int32), axis=1),
                             N_EXPERTS - 1)
    tables = (run_start, run_off, cnt, jnp.sum(cnt, axis=1))
    tabs_p = tuple(a[:ntp].reshape(-1).astype(jnp.int32) for a in tables)
    tabs_s = tuple(a[ntp:].reshape(-1).astype(jnp.int32) for a in tables)

    xsorted = _dispatch_first(tail, n_used, tabs_p, lpos_p, x1_p, tm=tm_p, n_blocks=n_blocks)
    xsorted = _dispatch_next(tabs_s, lpos_s, x1_s, xsorted, tm=tm_s)
    blk = jnp.arange(n_blocks, dtype=jnp.int32)
    later = (blk[None, :] > blk[:, None]) & (blk_expert[None, :] != blk_expert[:, None]) & (blk[None, :] < n_used)
    next_expert = jnp.where(jnp.any(later, axis=1), blk_expert[jnp.argmax(later, axis=1)], -1).astype(jnp.int32)
    yb = _experts(blk_expert, next_expert, n_used, xsorted, w_expert_in[0], b_glu, b_lin, w_expert_out[0], b_eo)
    y_p = _combine(tabs_p, gate_p, lpos_p, x1_p, g2, b2, yb, tm=tm_p, dn_alpha=dn_alpha)
    y_s = _combine(tabs_s, gate_s, lpos_s, x1_s, g2, b2, yb, tm=tm_s, dn_alpha=dn_alpha)

    k_prompt = jnp.transpose(kt_p.reshape(bp, N_HEADS, 2, HEAD_DIM, sp), (0, 4, 1, 2, 3))
    return (
        y_p.reshape(bp, sp, D_MODEL),
        y_s.reshape(bs, ss, D_MODEL),
        k_prompt[None],
        v_p.reshape(1, bp, sp, N_HEADS, V_DIM),
        pnew_p[:, 1:].reshape(1, bp, POOL_HIST, POOL_DIM),
        k_s.reshape(1, bs, ss, N_HEADS, 2, HEAD_DIM),
        v_s.reshape(1, bs, ss, N_HEADS, V_DIM),
        pnew_s[:, 1:].reshape(1, bs, POOL_HIST, POOL_DIM),
    )
```

```python
import functools
import math

import jax
import jax.numpy as jnp
from jax import lax
from jax.experimental import pallas as pl
from jax.experimental.pallas import tpu as pltpu

D_MODEL = 1024
CHUNK = 64
POOL_WINDOWS = (2, 4, 8, 16)
POOL_GROUP_DIM = 128
POOL_DIM = len(POOL_WINDOWS) * POOL_GROUP_DIM
POOL_HIST = max(POOL_WINDOWS) - 1
HIST_ROWS = POOL_HIST + 1
N_HEADS = 8
HEAD_DIM = 64
HALF_DIM = HEAD_DIM // 2
V_DIM = 2 * HEAD_DIM
QK_DIM = N_HEADS * 2 * HEAD_DIM
ATTN_V_WIDTH = N_HEADS * V_DIM
ATTN_SCALE = HEAD_DIM ** -0.5
LOG2_E = math.log2(math.e)
ROPE_THETA = 10000.0
SUBLN_EPS = 1e-5
N_EXPERTS = 32
TOP_K = 4
D_EXPERT = 1024
SWIGLU_LIMIT = 7.0
SWIGLU_ALPHA = 1.702
LN_EPS = 1e-5
NEG_INF = -1e30
LANES = 128
MXU_DIM = 256

F32 = jnp.float32
BF16 = jnp.bfloat16

VMEM_LIMIT = 56 * 1024 * 1024
EXPERT_BLOCK = 256
SUBLANES = 8
RUN_ALIGN = SUBLANES
ROUTE_TILE = 256
DENSE_TILE = 512
ATTN_Q_TILE = 256
RUN_BITS = tuple(range(3, 9))


def _dot(a, b):
    return jnp.dot(a, b, preferred_element_type=F32)


def _dot_nt(a, b):
    return lax.dot_general(a, b, (((1,), (1,)), ((), ())), preferred_element_type=F32)


def _params(semantics):
    return pltpu.CompilerParams(dimension_semantics=semantics, vmem_limit_bytes=VMEM_LIMIT)


def _pool_branch(u, icnt_ref, hist_ref, wmix_ref, pscale_ref, py_ref, pnew_ref, ext_ref, *, bb, tm):
    @pl.when(pl.program_id(1) == 0)
    def _():
        ext_ref[:, 0:HIST_ROWS, :] = hist_ref[...]

    for b in range(bb):
        ext_ref[b, HIST_ROWS:HIST_ROWS + tm, :] = u[b * tm:(b + 1) * tm]
    for b in range(bb):
        for g, w in enumerate(POOL_WINDOWS):
            cols = slice(g * POOL_GROUP_DIM, (g + 1) * POOL_GROUP_DIM)
            cur = ext_ref[b, HIST_ROWS:HIST_ROWS + tm, cols]
            acc = cur
            for j in range(1, w):
                acc = acc + ext_ref[b, HIST_ROWS - j:HIST_ROWS - j + tm, cols]
            d = acc * icnt_ref[:, cols] - cur
            y = _dot(d.astype(BF16), wmix_ref[g]) * pscale_ref[:, cols]
            py_ref[b * tm:(b + 1) * tm, cols] = y.astype(BF16)
    tail = ext_ref[:, tm:tm + HIST_ROWS, :]
    pnew_ref[...] = tail
    ext_ref[:, 0:HIST_ROWS, :] = tail


def _rope_rows(z, cos, sin):
    lane = lax.broadcasted_iota(jnp.int32, z.shape, 1)
    first_half = (lane % HEAD_DIM) < HALF_DIM
    partner = jnp.where(first_half, pltpu.roll(z, LANES - HALF_DIM, 1), pltpu.roll(z, HALF_DIM, 1))
    return z * cos + partner * sin


def _inproj_prompt_kernel(x_ref, w_ref, wkt_ref, cos_ref, sin_ref, cost_ref, sint_ref, icnt_ref, hist_ref,
                          wmix_ref, pscale_ref, q_ref, kt_ref, ktb_ref, v_ref, vb_ref, py_ref, pnew_ref,
                          ext_ref, *, tm):
    x = x_ref[...].astype(BF16)
    _pool_branch(_dot(x, w_ref[:, 0:POOL_DIM]), icnt_ref, hist_ref, wmix_ref, pscale_ref, py_ref, pnew_ref,
                 ext_ref, bb=1, tm=tm)

    cos, sin = cos_ref[...], sin_ref[...]
    hq = _dot(x, w_ref[:, POOL_DIM:POOL_DIM + QK_DIM])
    for h in range(N_HEADS):
        sl = slice(h * V_DIM, (h + 1) * V_DIM)
        q_ref[:, sl] = (_rope_rows(hq[:, sl], cos, sin) * (ATTN_SCALE * LOG2_E)).astype(BF16)

    hkt = _dot_nt(wkt_ref[...], x)
    cost, sint = cost_ref[...], sint_ref[...]
    for hc in range(2 * N_HEADS):
        r0 = hc * HEAD_DIM
        x1 = hkt[r0:r0 + HALF_DIM]
        x2 = hkt[r0 + HALF_DIM:r0 + HEAD_DIM]
        o1 = x1 * cost - x2 * sint
        o2 = x2 * cost + x1 * sint
        kt_ref[r0:r0 + HALF_DIM, :] = o1
        kt_ref[r0 + HALF_DIM:r0 + HEAD_DIM, :] = o2
        ktb_ref[r0:r0 + HALF_DIM, :] = o1.astype(BF16)
        ktb_ref[r0 + HALF_DIM:r0 + HEAD_DIM, :] = o2.astype(BF16)

    hv = _dot(x, w_ref[:, POOL_DIM + QK_DIM:POOL_DIM + QK_DIM + ATTN_V_WIDTH])
    vb_ref[...] = hv.astype(BF16)
    for h in range(N_HEADS):
        v_ref[pl.ds(h, tm, stride=N_HEADS), :] = hv[:, h * V_DIM:(h + 1) * V_DIM]


def _inproj_prompt(x2d, w_pqv, wkt, cos, sin, cost, sint, icnt, hist, wmix, pscale, *, n_streams, seq, tm):
    n = n_streams * seq
    nt = seq // tm
    row_map = lambda b, t: (b * nt + t, 0)
    const2 = lambda b, t: (0, 0)
    out_shape = (
        jax.ShapeDtypeStruct((n, QK_DIM), BF16),
        jax.ShapeDtypeStruct((n_streams, QK_DIM, seq), F32),
        jax.ShapeDtypeStruct((n_streams, QK_DIM, seq), BF16),
        jax.ShapeDtypeStruct((n_streams, seq * N_HEADS, V_DIM), F32),
        jax.ShapeDtypeStruct((n, ATTN_V_WIDTH), BF16),
        jax.ShapeDtypeStruct((n, POOL_DIM), BF16),
        jax.ShapeDtypeStruct((n_streams, HIST_ROWS, POOL_DIM), F32),
    )
    return pl.pallas_call(
        functools.partial(_inproj_prompt_kernel, tm=tm),
        out_shape=out_shape,
        grid=(n_streams, nt),
        in_specs=[
            pl.BlockSpec((tm, D_MODEL), row_map),
            pl.BlockSpec(w_pqv.shape, const2),
            pl.BlockSpec(wkt.shape, const2),
            pl.BlockSpec((tm, LANES), lambda b, t: (t, 0)),
            pl.BlockSpec((tm, LANES), lambda b, t: (t, 0)),
            pl.BlockSpec((HALF_DIM, tm), lambda b, t: (0, t)),
            pl.BlockSpec((HALF_DIM, tm), lambda b, t: (0, t)),
            pl.BlockSpec((tm, POOL_DIM), lambda b, t: (t, 0)),
            pl.BlockSpec((1, HIST_ROWS, POOL_DIM), lambda b, t: (b, 0, 0)),
            pl.BlockSpec((len(POOL_WINDOWS), POOL_GROUP_DIM, POOL_GROUP_DIM), lambda b, t: (0, 0, 0)),
            pl.BlockSpec((1, POOL_DIM), const2),
        ],
        out_specs=(
            pl.BlockSpec((tm, QK_DIM), row_map),
            pl.BlockSpec((None, QK_DIM, tm), lambda b, t: (b, 0, t)),
            pl.BlockSpec((None, QK_DIM, tm), lambda b, t: (b, 0, t)),
            pl.BlockSpec((None, tm * N_HEADS, V_DIM), lambda b, t: (b, t, 0)),
            pl.BlockSpec((tm, ATTN_V_WIDTH), row_map),
            pl.BlockSpec((tm, POOL_DIM), row_map),
            pl.BlockSpec((1, HIST_ROWS, POOL_DIM), lambda b, t: (b, 0, 0)),
        ),
        scratch_shapes=[pltpu.VMEM((1, HIST_ROWS + tm, POOL_DIM), F32)],
        compiler_params=_params(("arbitrary", "arbitrary")),
        name="inproj_prompt",
    )(x2d, w_pqv, wkt, cos, sin, cost, sint, icnt, hist, wmix, pscale)


def _inproj_sample_kernel(x_ref, w_ref, wk_ref, cos_ref, sin_ref, icnt_ref, hist_ref, wmix_ref, pscale_ref,
                          q_ref, k_ref, v_ref, py_ref, pnew_ref, ext_ref, *, bb, tm):
    x = x_ref[...].astype(BF16)
    _pool_branch(_dot(x, w_ref[:, 0:POOL_DIM]), icnt_ref, hist_ref, wmix_ref, pscale_ref, py_ref, pnew_ref,
                 ext_ref, bb=bb, tm=tm)
    cos = jnp.concatenate([cos_ref[...]] * bb, axis=0)
    sin = jnp.concatenate([sin_ref[...]] * bb, axis=0)
    hq = _dot(x, w_ref[:, POOL_DIM:POOL_DIM + QK_DIM])
    hk = _dot(x, wk_ref[...])
    for h in range(N_HEADS):
        sl = slice(h * V_DIM, (h + 1) * V_DIM)
        q_ref[:, sl] = (_rope_rows(hq[:, sl], cos, sin) * ATTN_SCALE).astype(BF16)
        k_ref[:, sl] = _rope_rows(hk[:, sl], cos, sin)
    v_ref[...] = _dot(x, w_ref[:, POOL_DIM + QK_DIM:POOL_DIM + QK_DIM + ATTN_V_WIDTH])


def _inproj_sample(x2d, w_pqv, wk, cos, sin, icnt, hist, wmix, pscale, *, n_streams, seq):
    n = n_streams * seq
    const2 = lambda i, t: (0, 0)
    const3 = lambda i, t: (0, 0, 0)
    out_shape = (
        jax.ShapeDtypeStruct((n, QK_DIM), BF16),
        jax.ShapeDtypeStruct((n, QK_DIM), F32),
        jax.ShapeDtypeStruct((n, ATTN_V_WIDTH), F32),
        jax.ShapeDtypeStruct((n, POOL_DIM), BF16),
        jax.ShapeDtypeStruct((n_streams, HIST_ROWS, POOL_DIM), F32),
    )
    return pl.pallas_call(
        functools.partial(_inproj_sample_kernel, bb=n_streams, tm=seq),
        out_shape=out_shape,
        grid=(1, 1),
        in_specs=[
            pl.BlockSpec((n, D_MODEL), const2),
            pl.BlockSpec(w_pqv.shape, const2),
            pl.BlockSpec(wk.shape, const2),
            pl.BlockSpec((seq, LANES), const2),
            pl.BlockSpec((seq, LANES), const2),
            pl.BlockSpec((seq, POOL_DIM), const2),
            pl.BlockSpec((n_streams, HIST_ROWS, POOL_DIM), const3),
            pl.BlockSpec((len(POOL_WINDOWS), POOL_GROUP_DIM, POOL_GROUP_DIM), const3),
            pl.BlockSpec((1, POOL_DIM), const2),
        ],
        out_specs=(
            pl.BlockSpec((n, QK_DIM), const2),
            pl.BlockSpec((n, QK_DIM), const2),
            pl.BlockSpec((n, ATTN_V_WIDTH), const2),
            pl.BlockSpec((n, POOL_DIM), const2),
            pl.BlockSpec((n_streams, HIST_ROWS, POOL_DIM), const3),
        ),
        scratch_shapes=[pltpu.VMEM((n_streams, HIST_ROWS + seq, POOL_DIM), F32)],
        compiler_params=_params(("arbitrary", "arbitrary")),
        name="inproj_sample",
    )(x2d, w_pqv, wk, cos, sin, icnt, hist, wmix, pscale)


def _lambda_value(lam_ref, lam_init):
    lv = lam_ref[...]
    s1 = jnp.sum(lv[0:1] * lv[1:2], axis=1, keepdims=True)
    s2 = jnp.sum(lv[2:3] * lv[3:4], axis=1, keepdims=True)
    return jnp.exp(s1) - jnp.exp(s2) + lam_init


def _head_norm(o, g, lam_init):
    ms = jnp.mean(o * o, axis=-1, keepdims=True)
    return o * lax.rsqrt(ms + SUBLN_EPS) * g * (1.0 - lam_init)


def _attn_prompt_kernel(lam_ref, g_ref, q_ref, kt_ref, v_ref, o_ref, vext, *, seq, tq, lam_init):
    lam = _lambda_value(lam_ref, lam_init)
    g = g_ref[...]
    r = lax.broadcasted_iota(jnp.int32, (tq, tq), 0)
    c = lax.broadcasted_iota(jnp.int32, (tq, tq), 1)
    diag_visible = (c // CHUNK) <= (r // CHUNK)
    lane = lax.broadcasted_iota(jnp.int32, (tq, V_DIM), 1)
    vext[:, 0:V_DIM] = v_ref[...]
    vlane = lax.broadcasted_iota(jnp.int32, (seq, V_DIM), 1)
    vext[:, V_DIM:] = jnp.where(vlane == 0, 1.0, 0.0).astype(vext.dtype)

    def scores(i):
        lo = i * tq
        q = q_ref[lo:lo + tq, :]
        zero = jnp.zeros_like(q)
        qc = (jnp.where(lane < HEAD_DIM, q, zero), jnp.where(lane >= HEAD_DIM, q, zero))
        out = []
        for k in range(2):
            sd = jnp.where(diag_visible, _dot(qc[k], kt_ref[:, lo:lo + tq]), NEG_INF)
            m = jnp.max(sd, axis=1, keepdims=True)
            sp = None
            if i > 0:
                sp = _dot(qc[k], kt_ref[:, 0:lo])
                m = jnp.maximum(m, jnp.max(sp, axis=1, keepdims=True))
            out.append((sd, sp, m))
        return out

    def finish(i, parts):
        lo = i * tq
        normed = []
        for sd, sp, m in parts:
            acc = _dot(jnp.exp2(sd - m).astype(BF16), vext[lo:lo + tq, :])
            if sp is not None:
                acc = acc + _dot(jnp.exp2(sp - m).astype(BF16), vext[0:lo, :])
            normed.append(acc[:, 0:V_DIM] / acc[:, V_DIM:V_DIM + 1])
        o = normed[0] - lam * normed[1]
        o_ref[lo:lo + tq, :] = _head_norm(o, g, lam_init).astype(o_ref.dtype)

    nq = seq // tq
    pending = scores(0)
    for i in range(nq):
        upcoming = scores(i + 1) if i + 1 < nq else None
        finish(i, pending)
        pending = upcoming


def _attn_prompt(lam_vecs, norm_g, q, ktb, vb, *, n_streams, seq, tq, lam_init):
    return pl.pallas_call(
        functools.partial(_attn_prompt_kernel, seq=seq, tq=tq, lam_init=lam_init),
        out_shape=jax.ShapeDtypeStruct((n_streams * seq, ATTN_V_WIDTH), BF16),
        grid=(n_streams, N_HEADS),
        in_specs=[
            pl.BlockSpec((4, HEAD_DIM), lambda b, h: (0, 0)),
            pl.BlockSpec((1, V_DIM), lambda b, h: (0, 0)),
            pl.BlockSpec((seq, V_DIM), lambda b, h: (b, h)),
            pl.BlockSpec((None, V_DIM, seq), lambda b, h: (b, h, 0)),
            pl.BlockSpec((seq, V_DIM), lambda b, h: (b, h)),
        ],
        out_specs=pl.BlockSpec((seq, V_DIM), lambda b, h: (b, h)),
        scratch_shapes=[pltpu.VMEM((seq, MXU_DIM), BF16)],
        compiler_params=_params(("arbitrary", "arbitrary")),
        name="attn_prompt",
    )(lam_vecs, norm_g, q, ktb, vb)


def _attn_sample_kernel(lam_ref, g_ref, q_ref, kc_ref, vc_ref, kn_ref, vn_ref, o_ref,
                        s_scr, w_scr, wn_scr, m_scr, acc_scr, qbd_scr, *, nk, tn, past, lam_init):
    j = pl.program_id(1)
    half = N_HEADS * tn

    @pl.when(j == 0)
    def _():
        q = q_ref[...]
        qt = jnp.concatenate([q] * (2 * N_HEADS), axis=0)
        r = lax.broadcasted_iota(jnp.int32, qt.shape, 0)
        l = lax.broadcasted_iota(jnp.int32, qt.shape, 1)
        keep = ((r // half) == ((l % V_DIM) // HEAD_DIM)) & (((r % half) // tn) == (l // V_DIM))
        qbd_scr[...] = jnp.where(keep, qt, jnp.zeros_like(qt))
        m_scr[...] = jnp.full(m_scr.shape, NEG_INF, F32)
        acc_scr[...] = jnp.zeros(acc_scr.shape, F32)

    @pl.when(j < nk)
    def _():
        s = _dot(qbd_scr[...], kc_ref[...].astype(BF16))
        s_scr[j] = s
        m_scr[...] = jnp.maximum(m_scr[...], jnp.max(s, axis=1, keepdims=True))

    @pl.when(j == nk - 1)
    def _():
        lam = _lambda_value(lam_ref, lam_init)
        sn = _dot_nt(qbd_scr[...], kn_ref[...].astype(BF16))
        qpos = past + (lax.broadcasted_iota(jnp.int32, sn.shape, 0) % tn)
        kpos = past + lax.broadcasted_iota(jnp.int32, sn.shape, 1)
        sn = jnp.where((kpos // CHUNK) <= (qpos // CHUNK), sn, NEG_INF)
        m = jnp.maximum(m_scr[...], jnp.max(sn, axis=1, keepdims=True))
        pn = jnp.exp(sn - m)
        l = jnp.sum(pn, axis=1, keepdims=True)
        for c in range(nk):
            p = jnp.exp(s_scr[c] - m)
            s_scr[c] = p
            l = l + jnp.sum(p, axis=1, keepdims=True)
        r0 = 1.0 / l[:half]
        r1 = lam / l[half:]
        wn_scr[...] = pn[:half] * r0 - pn[half:] * r1
        for c in range(nk):
            p = s_scr[c]
            w_scr[c] = (p[:half] * r0 - p[half:] * r1).astype(BF16)

    def v_rows(ref):
        tk = ref.shape[0] // N_HEADS
        return jnp.concatenate([ref[pl.ds(h, tk, stride=N_HEADS), :] for h in range(N_HEADS)],
                               axis=1).astype(BF16)

    @pl.when(j >= nk)
    def _():
        acc_scr[...] += _dot(w_scr[j - nk], v_rows(vc_ref))

    @pl.when(j == 2 * nk - 1)
    def _():
        acc = acc_scr[...] + _dot(wn_scr[...].astype(BF16), vn_ref[...].astype(BF16))
        g = g_ref[...]
        for h in range(N_HEADS):
            o = acc[h * tn:(h + 1) * tn, h * V_DIM:(h + 1) * V_DIM]
            o_ref[:, h * V_DIM:(h + 1) * V_DIM] = _head_norm(o, g, lam_init).astype(o_ref.dtype)


def _attn_sample(lam_vecs, norm_g, q, kct, vc, kn, vn, *, n_streams, tn, past, tk, lam_init):
    nk = past // tk
    rows = 2 * N_HEADS * tn
    half = N_HEADS * tn
    return pl.pallas_call(
        functools.partial(_attn_sample_kernel, nk=nk, tn=tn, past=past, lam_init=lam_init),
        out_shape=jax.ShapeDtypeStruct((n_streams * tn, ATTN_V_WIDTH), BF16),
        grid=(n_streams, 2 * nk),
        in_specs=[
            pl.BlockSpec((4, HEAD_DIM), lambda b, j: (0, 0)),
            pl.BlockSpec((1, V_DIM), lambda b, j: (0, 0)),
            pl.BlockSpec((tn, QK_DIM), lambda b, j: (b, 0)),
            pl.BlockSpec((None, QK_DIM, tk), lambda b, j: (b, 0, jnp.minimum(j, nk - 1))),
            pl.BlockSpec((None, tk * N_HEADS, V_DIM), lambda b, j: (b, jnp.maximum(j - nk, 0), 0)),
            pl.BlockSpec((tn, QK_DIM), lambda b, j: (b, 0)),
            pl.BlockSpec((tn, ATTN_V_WIDTH), lambda b, j: (b, 0)),
        ],
        out_specs=pl.BlockSpec((tn, ATTN_V_WIDTH), lambda b, j: (b, 0)),
        scratch_shapes=[
            pltpu.VMEM((nk, rows, tk), F32),
            pltpu.VMEM((nk, half, tk), BF16),
            pltpu.VMEM((half, tn), F32),
            pltpu.VMEM((rows, 1), F32),
            pltpu.VMEM((half, ATTN_V_WIDTH), F32),
            pltpu.VMEM((rows, QK_DIM), BF16),
        ],
        compiler_params=_params(("arbitrary", "arbitrary")),
        name="attn_sample",
    )(lam_vecs, norm_g, q, kct, vc, kn, vn)


def _layer_norm(z, g, b):
    mu = jnp.mean(z, axis=-1, keepdims=True)
    zc = z - mu
    var = jnp.mean(zc * zc, axis=-1, keepdims=True)
    return zc * lax.rsqrt(var + LN_EPS) * g + b


def _postmix_kernel(x_ref, xprev_ref, py_ref, ay_ref, wg_ref, wpo_ref, wao_ref, wout_ref, g1_ref, b1_ref,
                    wr_ref, br_ref, x1_ref, gate_ref, lpos_ref, cnt_ref, mix_scr, *, tm, rt, dn_alpha):
    @pl.when(pl.program_id(0) == 0)
    def _():
        mix_scr[...] = jnp.zeros(mix_scr.shape, mix_scr.dtype)

    mixed_prev = mix_scr[...]

    mo = _dot(mixed_prev, wout_ref[...])
    x1 = _layer_norm(dn_alpha * xprev_ref[...] + mo, g1_ref[...], b1_ref[...])
    x1_ref[...] = x1
    logits = _dot(x1.astype(BF16), wr_ref[...]) + br_ref[...]

    gates = jax.nn.sigmoid(_dot(x_ref[...].astype(BF16), wg_ref[...]))
    a = _dot(py_ref[...], wpo_ref[...])
    b = _dot(ay_ref[...], wao_ref[...])
    mix_scr[...] = (gates[:, :D_MODEL] * a + gates[:, D_MODEL:] * b).astype(BF16)

    for sub in range(tm // rt):
        rows = slice(sub * rt, (sub + 1) * rt)
        gate_ref[rows, :], lpos_ref[rows, :], cnt_ref[sub] = _route_tile(logits[rows], rt)


def _route_tile(logits, tm):
    lane = lax.broadcasted_iota(jnp.int32, logits.shape, 1)
    work = logits
    vals, idxs = [], []
    for _ in range(TOP_K):
        mx = jnp.max(work, axis=1, keepdims=True)
        ix = jnp.min(jnp.where(work == mx, lane, N_EXPERTS), axis=1, keepdims=True)
        vals.append(mx)
        idxs.append(ix)
        work = jnp.where(lane == ix, -jnp.inf, work)
    exps = [jnp.exp(v - vals[0]) for v in vals]
    denom = exps[0] + exps[1] + exps[2] + exps[3]

    onehot = jnp.zeros(logits.shape, F32)
    for ix in idxs:
        onehot = onehot + (lane == ix).astype(F32)
    r = lax.broadcasted_iota(jnp.int32, (tm, tm), 0)
    c = lax.broadcasted_iota(jnp.int32, (tm, tm), 1)
    tri = jnp.where(c < r, 1.0, 0.0).astype(BF16)
    earlier = _dot(tri, onehot.astype(BF16))
    cnt = jnp.sum(onehot, axis=0, keepdims=True)
    units = jnp.floor((cnt + (RUN_ALIGN - 1.0)) * (1.0 / RUN_ALIGN))
    er = lax.broadcasted_iota(jnp.int32, (N_EXPERTS, N_EXPERTS), 0)
    ec = lax.broadcasted_iota(jnp.int32, (N_EXPERTS, N_EXPERTS), 1)
    upper = jnp.where(er < ec, 1.0, 0.0).astype(BF16)
    run_off = _dot(jnp.broadcast_to(units, (SUBLANES, N_EXPERTS)).astype(BF16), upper)[0:1] * float(RUN_ALIGN)
    pos = earlier + run_off

    lane_out = lax.broadcasted_iota(jnp.int32, (tm, LANES), 1)
    lpos_out = jnp.zeros((tm, LANES), jnp.int32)
    gate_out = jnp.zeros((tm, LANES), F32)
    for k in range(TOP_K):
        lpos_k = jnp.sum(jnp.where(lane == idxs[k], pos, 0.0), axis=1, keepdims=True).astype(jnp.int32)
        lpos_out = jnp.where(lane_out == k, lpos_k, lpos_out)
        gate_out = jnp.where(lane_out == k, exps[k] / denom, gate_out)
    return gate_out, lpos_out, units * float(RUN_ALIGN)


def _postmix(x2d, py, ay, wg, wpo, wao, wout, g1, b1, wr, br, *, tm, rt, dn_alpha):
    n = x2d.shape[0]
    nt = n // tm
    row = lambda i: (jnp.minimum(i, nt - 1), 0)
    prev = lambda i: (jnp.maximum(i - 1, 0), 0)
    const = lambda i: (0, 0)
    out_shape = (
        jax.ShapeDtypeStruct((n, D_MODEL), F32),
        jax.ShapeDtypeStruct((n, LANES), F32),
        jax.ShapeDtypeStruct((n, LANES), jnp.int32),
        jax.ShapeDtypeStruct((n // rt, 1, N_EXPERTS), F32),
    )
    return pl.pallas_call(
        functools.partial(_postmix_kernel, tm=tm, rt=rt, dn_alpha=dn_alpha),
        out_shape=out_shape,
        grid=(nt + 1,),
        in_specs=[
            pl.BlockSpec((tm, D_MODEL), row),
            pl.BlockSpec((tm, D_MODEL), prev),
            pl.BlockSpec((tm, POOL_DIM), row),
            pl.BlockSpec((tm, ATTN_V_WIDTH), row),
            pl.BlockSpec(wg.shape, const),
            pl.BlockSpec(wpo.shape, const),
            pl.BlockSpec(wao.shape, const),
            pl.BlockSpec(wout.shape, const),
            pl.BlockSpec((1, D_MODEL), const),
            pl.BlockSpec((1, D_MODEL), const),
            pl.BlockSpec(wr.shape, const),
            pl.BlockSpec((1, N_EXPERTS), const),
        ],
        out_specs=(
            pl.BlockSpec((tm, D_MODEL), prev),
            pl.BlockSpec((tm, LANES), prev),
            pl.BlockSpec((tm, LANES), prev),
            pl.BlockSpec((tm // rt, 1, N_EXPERTS), lambda i: (jnp.maximum(i - 1, 0), 0, 0)),
        ),
        scratch_shapes=[pltpu.VMEM((tm, D_MODEL), BF16)],
        compiler_params=_params(("arbitrary",)),
        name="postmix",
    )(x2d, x2d, py, ay, wg, wpo, wao, wout, g1, b1, wr, br)


def _local_rows(tm):
    return TOP_K * tm + N_EXPERTS * RUN_ALIGN


def _for_each_run_chunk(tile, tab_refs, local_buf, sorted_ref, sem, to_sorted, fn):
    gs_ref, off_ref, cnt_ref = tab_refs[:3]

    def per_expert(e, carry):
        t = tile * N_EXPERTS + e
        cnt, off, gs = cnt_ref[t], off_ref[t], gs_ref[t]
        for b in RUN_BITS:
            size = 1 << b

            @pl.when((cnt & size) != 0)
            def _():
                lower = cnt & (size - 1)
                loc = local_buf.at[pl.ds(pl.multiple_of(off + lower, RUN_ALIGN), size)]
                srt = sorted_ref.at[pl.ds(pl.multiple_of(gs + lower, RUN_ALIGN), size)]
                fn(pltpu.make_async_copy(loc, srt, sem) if to_sorted else pltpu.make_async_copy(srt, loc, sem))
        return carry

    lax.fori_loop(0, N_EXPERTS, per_expert, 0)


def _wait_run_rows(total, local_buf, sorted_ref, sem, to_sorted):
    rows = local_buf.shape[0]
    for b in range(RUN_BITS[0], rows.bit_length()):
        size = 1 << b

        @pl.when((total & size) != 0)
        def _():
            loc, srt = local_buf.at[pl.ds(0, size)], sorted_ref.at[pl.ds(0, size)]
            (pltpu.make_async_copy(loc, srt, sem) if to_sorted else pltpu.make_async_copy(srt, loc, sem)).wait()


def _dispatch_tile(tab_refs, lpos_ref, x_ref, xs_ref, xloc, sem, tm):
    lpos = lpos_ref[...]
    col = lax.broadcasted_iota(jnp.int32, (tm, _local_rows(tm)), 1)
    hit = col == lpos[:, 0:1]
    for k in range(1, TOP_K):
        hit = jnp.logical_or(hit, col == lpos[:, k:k + 1])
    perm_t = jnp.where(hit, 1.0, 0.0).astype(BF16)
    tile = pl.program_id(0)
    slot = tile % 2
    xloc[slot] = lax.dot_general(perm_t, x_ref[...].astype(BF16), (((0,), (0,)), ((), ())),
                                 preferred_element_type=F32)

    def copies(t, s, fn):
        _for_each_run_chunk(t, tab_refs, xloc.at[s], xs_ref, sem.at[s], True, fn)

    copies(tile, slot, lambda cp: cp.start())
    tot_ref = tab_refs[3]

    @pl.when(tile > 0)
    def _():
        _wait_run_rows(tot_ref[tile - 1], xloc.at[1 - slot], xs_ref, sem.at[1 - slot], True)

    @pl.when(tile == pl.num_programs(0) - 1)
    def _():
        _wait_run_rows(tot_ref[tile], xloc.at[slot], xs_ref, sem.at[slot], True)


def _dispatch_first_kernel(tail_ref, nu_ref, gs_ref, off_ref, cnt_ref, tot_ref, lpos_ref, x_ref, xs_ref,
                           xloc, zbuf, sem, zsem, *, tm, n_blocks):
    @pl.when(pl.program_id(0) == 0)
    def _():
        zbuf[...] = jnp.zeros(zbuf.shape, zbuf.dtype)

        def zero_copy(row):
            row = pl.multiple_of(row, EXPERT_BLOCK)
            return pltpu.make_async_copy(zbuf, xs_ref.at[pl.ds(row, EXPERT_BLOCK)], zsem)

        def over_blocks(fn):
            for e in range(N_EXPERTS):
                @pl.when(tail_ref[e] >= 0)
                def _():
                    fn(zero_copy(tail_ref[e]))
            lax.fori_loop(nu_ref[0], n_blocks, lambda b, c: (fn(zero_copy(b * EXPERT_BLOCK)), c)[1], 0)

        over_blocks(lambda cp: cp.start())
        over_blocks(lambda cp: cp.wait())

    _dispatch_tile((gs_ref, off_ref, cnt_ref, tot_ref), lpos_ref, x_ref, xs_ref, xloc, sem, tm)


def _dispatch_next_kernel(gs_ref, off_ref, cnt_ref, tot_ref, lpos_ref, x_ref, xs_in_ref, xs_ref, xloc, sem,
                          *, tm):
    del xs_in_ref
    _dispatch_tile((gs_ref, off_ref, cnt_ref, tot_ref), lpos_ref, x_ref, xs_ref, xloc, sem, tm)


def _dispatch_first(tail, n_used, tabs, lpos, x1, *, tm, n_blocks):
    n = x1.shape[0]
    grid_spec = pltpu.PrefetchScalarGridSpec(
        num_scalar_prefetch=6,
        grid=(n // tm,),
        in_specs=[pl.BlockSpec((tm, LANES), lambda i, *_: (i, 0)),
                  pl.BlockSpec((tm, D_MODEL), lambda i, *_: (i, 0))],
        out_specs=pl.BlockSpec(memory_space=pl.ANY),
        scratch_shapes=[pltpu.VMEM((2, _local_rows(tm), D_MODEL), F32),
                        pltpu.VMEM((EXPERT_BLOCK, D_MODEL), F32),
                        pltpu.SemaphoreType.DMA((2,)), pltpu.SemaphoreType.DMA],
    )
    return pl.pallas_call(
        functools.partial(_dispatch_first_kernel, tm=tm, n_blocks=n_blocks),
        out_shape=jax.ShapeDtypeStruct((n_blocks * EXPERT_BLOCK, D_MODEL), F32),
        grid_spec=grid_spec,
        compiler_params=_params(("arbitrary",)),
        name="dispatch_first",
    )(tail, n_used, *tabs, lpos, x1)


def _dispatch_next(tabs, lpos, x1, xs, *, tm):
    n = x1.shape[0]
    grid_spec = pltpu.PrefetchScalarGridSpec(
        num_scalar_prefetch=len(tabs),
        grid=(n // tm,),
        in_specs=[pl.BlockSpec((tm, LANES), lambda i, *_: (i, 0)),
                  pl.BlockSpec((tm, D_MODEL), lambda i, *_: (i, 0)),
                  pl.BlockSpec(memory_space=pl.ANY)],
        out_specs=pl.BlockSpec(memory_space=pl.ANY),
        scratch_shapes=[pltpu.VMEM((2, _local_rows(tm), D_MODEL), F32), pltpu.SemaphoreType.DMA((2,))],
    )
    return pl.pallas_call(
        functools.partial(_dispatch_next_kernel, tm=tm),
        out_shape=jax.ShapeDtypeStruct(xs.shape, xs.dtype),
        grid_spec=grid_spec,
        input_output_aliases={6: 0},
        compiler_params=_params(("arbitrary",)),
        name="dispatch_next",
    )(*tabs, lpos, x1, xs)


def _experts_kernel(be_ref, nxt_ref, nu_ref, xs_ref, bg_ref, bl_ref, bo_ref, win_hbm, wo_hbm, y_ref,
                    win_buf, wo_buf, wg_scr, wl_scr, wo_scr, sem_in, sem_out):
    i = pl.program_id(0)
    used = i < nu_ref[0]
    first_of_expert = jnp.logical_or(i == 0, be_ref[i] != be_ref[jnp.maximum(i - 1, 0)])

    def weight_copies(e):
        return (pltpu.make_async_copy(win_hbm.at[e], win_buf, sem_in),
                pltpu.make_async_copy(wo_hbm.at[e], wo_buf, sem_out))

    @pl.when(jnp.logical_not(used))
    def _():
        y_ref[...] = jnp.zeros(y_ref.shape, y_ref.dtype)

    @pl.when(i == 0)
    def _():
        for cp in weight_copies(be_ref[0]):
            cp.start()

    @pl.when(jnp.logical_and(used, first_of_expert))
    def _():
        for cp in weight_copies(be_ref[i]):
            cp.wait()
        r = lax.broadcasted_iota(jnp.int32, (MXU_DIM, MXU_DIM), 0)
        c = lax.broadcasted_iota(jnp.int32, (MXU_DIM, MXU_DIM), 1)
        src = jnp.where(c < LANES, 2 * c, 2 * (c - LANES) + 1)
        sel = jnp.where(r == src, 1.0, 0.0).astype(BF16)
        for gq in range(2 * D_EXPERT // MXU_DIM):
            blk = win_buf[:, gq * MXU_DIM:(gq + 1) * MXU_DIM].astype(BF16)
            d = _dot(blk, sel)
            wg_scr[:, gq * LANES:(gq + 1) * LANES] = d[:, :LANES].astype(BF16)
            wl_scr[:, gq * LANES:(gq + 1) * LANES] = d[:, LANES:].astype(BF16)
        wo_scr[...] = wo_buf[...].astype(BF16)

        @pl.when(nxt_ref[i] >= 0)
        def _():
            for cp in weight_copies(nxt_ref[i]):
                cp.start(priority=1)

    @pl.when(used)
    def _():
        xb = xs_ref[...].astype(BF16)
        glu = jnp.minimum(_dot(xb, wg_scr[...]) + bg_ref[...], SWIGLU_LIMIT)
        lin = jnp.clip(_dot(xb, wl_scr[...]) + bl_ref[...], -SWIGLU_LIMIT, SWIGLU_LIMIT)
        act = glu * jax.nn.sigmoid(SWIGLU_ALPHA * glu) * (lin + 1.0)
        y_ref[...] = _dot(act.astype(BF16), wo_scr[...]) + bo_ref[...]


def _experts(blk_expert, next_expert, n_used, xs, w_in, b_glu, b_lin, w_out, b_out):
    rows = xs.shape[0]
    n_blocks = rows // EXPERT_BLOCK
    wsel = lambda i, be, nxt, nu: (be[i], 0, 0)
    grid_spec = pltpu.PrefetchScalarGridSpec(
        num_scalar_prefetch=3,
        grid=(n_blocks,),
        in_specs=[
            pl.BlockSpec((EXPERT_BLOCK, D_MODEL), lambda i, be, nxt, nu: (jnp.minimum(i, nu[0] - 1), 0)),
            pl.BlockSpec((None, 1, D_EXPERT), wsel),
            pl.BlockSpec((None, 1, D_EXPERT), wsel),
            pl.BlockSpec((None, 1, D_MODEL), wsel),
            pl.BlockSpec(memory_space=pl.ANY),
            pl.BlockSpec(memory_space=pl.ANY),
        ],
        out_specs=pl.BlockSpec((EXPERT_BLOCK, D_MODEL), lambda i, be, nxt, nu: (i, 0)),
        scratch_shapes=[pltpu.VMEM((D_MODEL, 2 * D_EXPERT), F32),
                        pltpu.VMEM((D_EXPERT, D_MODEL), F32),
                        pltpu.VMEM((D_MODEL, D_EXPERT), BF16),
                        pltpu.VMEM((D_MODEL, D_EXPERT), BF16),
                        pltpu.VMEM((D_EXPERT, D_MODEL), BF16),
                        pltpu.SemaphoreType.DMA, pltpu.SemaphoreType.DMA],
    )
    return pl.pallas_call(
        _experts_kernel,
        out_shape=jax.ShapeDtypeStruct((rows, D_MODEL), F32),
        grid_spec=grid_spec,
        compiler_params=_params(("arbitrary",)),
        name="experts",
    )(blk_expert, next_expert, n_used, xs, b_glu, b_lin, b_out, w_in, w_out)


def _split_bf16(a):
    hi = a.astype(BF16)
    return hi, (a - hi.astype(F32)).astype(BF16)


def _combine_kernel(gs_ref, off_ref, cnt_ref, tot_ref, gate_ref, lpos_ref, x1_ref, g2_ref, b2_ref, yb_ref, o_ref,
                    yloc, sem, *, tm, dn_alpha):
    tile = pl.program_id(0)
    slot = tile % 2
    tabs = (gs_ref, off_ref, cnt_ref)

    def copies(t, s, fn):
        _for_each_run_chunk(t, tabs, yloc.at[s], yb_ref, sem.at[s], False, fn)

    @pl.when(tile == 0)
    def _():
        yloc[...] = jnp.zeros(yloc.shape, yloc.dtype)
        copies(tile, slot, lambda cp: cp.start())

    @pl.when(tile + 1 < pl.num_programs(0))
    def _():
        copies(tile + 1, 1 - slot, lambda cp: cp.start())

    _wait_run_rows(tot_ref[tile], yloc.at[slot], yb_ref, sem.at[slot], False)

    gate, lpos = gate_ref[...], lpos_ref[...]
    col = lax.broadcasted_iota(jnp.int32, (tm, _local_rows(tm)), 1)
    weights = jnp.zeros(col.shape, F32)
    for k in range(TOP_K):
        weights = jnp.where(col == lpos[:, k:k + 1], gate[:, k:k + 1], weights)
    w_hi, w_lo = _split_bf16(weights)
    y_hi, y_lo = _split_bf16(yloc[slot])
    y = _dot(w_hi, y_hi) + (_dot(w_hi, y_lo) + _dot(w_lo, y_hi))
    o_ref[...] = _layer_norm(dn_alpha * x1_ref[...] + y, g2_ref[...], b2_ref[...])


def _combine(tabs, gate, lpos, x1, g2, b2, yb, *, tm, dn_alpha):
    n = x1.shape[0]
    grid_spec = pltpu.PrefetchScalarGridSpec(
        num_scalar_prefetch=len(tabs),
        grid=(n // tm,),
        in_specs=[
            pl.BlockSpec((tm, LANES), lambda i, *_: (i, 0)),
            pl.BlockSpec((tm, LANES), lambda i, *_: (i, 0)),
            pl.BlockSpec((tm, D_MODEL), lambda i, *_: (i, 0)),
            pl.BlockSpec((1, D_MODEL), lambda i, *_: (0, 0)),
            pl.BlockSpec((1, D_MODEL), lambda i, *_: (0, 0)),
            pl.BlockSpec(memory_space=pl.ANY),
        ],
        out_specs=pl.BlockSpec((tm, D_MODEL), lambda i, *_: (i, 0)),
        scratch_shapes=[pltpu.VMEM((2, _local_rows(tm), D_MODEL), F32), pltpu.SemaphoreType.DMA((2,))],
    )
    return pl.pallas_call(
        functools.partial(_combine_kernel, tm=tm, dn_alpha=dn_alpha),
        out_shape=jax.ShapeDtypeStruct((n, D_MODEL), F32),
        grid_spec=grid_spec,
        compiler_params=_params(("arbitrary",)),
        name="combine",
    )(*tabs, gate, lpos, x1, g2, b2, yb)


def _position_tables(pos0, seq):
    pos = pos0 + jnp.arange(seq, dtype=jnp.int32)
    inv = ROPE_THETA ** (-jnp.arange(HALF_DIM, dtype=F32) / HALF_DIM)
    ang = pos.astype(F32)[:, None] * inv[None, :]
    cos, sin = jnp.cos(ang), jnp.sin(ang)
    cos_rows = jnp.concatenate([cos, cos, cos, cos], axis=-1)
    sin_rows = jnp.concatenate([-sin, sin, -sin, sin], axis=-1)
    icnt = jnp.concatenate(
        [jnp.broadcast_to((1.0 / jnp.minimum(pos + 1, w).astype(F32))[:, None], (seq, POOL_GROUP_DIM))
         for w in POOL_WINDOWS], axis=-1)
    return cos_rows, sin_rows, cos.T, sin.T, icnt


def _tile(n, pref):
    t = min(n, pref)
    while n % t:
        t //= 2
    return t


def kernel(x_prompt, x_sample, cache_k, cache_v, state_pool, w_in, w_pool_mix, pool_scale, w_pool_out,
           lambda_q1, lambda_k1, lambda_q2, lambda_k2, attn_norm_g, w_attn_out, w_out, ln1_g, ln1_b,
           w_router, b_router, w_expert_in, b_expert_in, w_expert_out, b_expert_out, ln2_g, ln2_b):
    depth = w_in.shape[0]
    assert depth == 1, "single-layer step"
    dn_alpha = (2.0 * depth) ** 0.25
    lam_init = 0.8 - 0.6 * math.exp(-0.3 * 0)
    bp, sp, _ = x_prompt.shape
    bs, ss, _ = x_sample.shape
    past = cache_k.shape[2]
    np_, ns = bp * sp, bs * ss

    c_q, c_k, c_v = POOL_DIM, POOL_DIM + QK_DIM, POOL_DIM + 2 * QK_DIM
    c_gate = c_v + ATTN_V_WIDTH
    w0 = w_in[0]
    w_pqv = jnp.concatenate([w0[:, :c_k], w0[:, c_v:c_gate]], axis=1).astype(BF16)
    w_k = w0[:, c_k:c_v].astype(BF16)
    w_gate = w0[:, c_gate:].astype(BF16)
    wmix = w_pool_mix[0].astype(BF16)
    pscale = pool_scale[0].reshape(1, POOL_DIM)
    wpo = w_pool_out[0].astype(BF16)
    wao = w_attn_out[0].astype(BF16)
    wout = w_out[0].astype(BF16)
    wr = w_router[0].astype(BF16)
    br = b_router[0].reshape(1, N_EXPERTS)
    lam_vecs = jnp.stack([lambda_q1[0], lambda_k1[0], lambda_q2[0], lambda_k2[0]])
    norm_g = attn_norm_g[0].reshape(1, V_DIM)
    g1, b1 = ln1_g[0].reshape(1, D_MODEL), ln1_b[0].reshape(1, D_MODEL)
    g2, b2 = ln2_g[0].reshape(1, D_MODEL), ln2_b[0].reshape(1, D_MODEL)
    b_glu = b_expert_in[0][:, 0::2].reshape(N_EXPERTS, 1, D_EXPERT)
    b_lin = b_expert_in[0][:, 1::2].reshape(N_EXPERTS, 1, D_EXPERT)
    b_eo = b_expert_out[0].reshape(N_EXPERTS, 1, D_MODEL)

    xp = x_prompt.reshape(np_, D_MODEL)
    cos_p, sin_p, cost_p, sint_p, icnt_p = _position_tables(0, sp)
    hist_p = jnp.zeros((bp, HIST_ROWS, POOL_DIM), F32)
    q_p, kt_p, ktb_p, v_p, vb_p, py_p, pnew_p = _inproj_prompt(
        xp, w_pqv, w_k.T, cos_p, sin_p, cost_p, sint_p, icnt_p, hist_p, wmix, pscale,
        n_streams=bp, seq=sp, tm=_tile(sp, DENSE_TILE))
    ay_p = _attn_prompt(lam_vecs, norm_g, q_p, ktb_p, vb_p, n_streams=bp, seq=sp,
                        tq=_tile(sp, ATTN_Q_TILE), lam_init=lam_init)
    tm_p, tm_s = _tile(np_, ROUTE_TILE), _tile(ns, ROUTE_TILE)
    x1_p, gate_p, lpos_p, cnt_p = _postmix(
        xp, py_p, ay_p, w_gate, wpo, wao, wout, g1, b1, wr, br,
        tm=max(tm_p, _tile(np_, DENSE_TILE)), rt=tm_p, dn_alpha=dn_alpha)

    xs_ = x_sample.reshape(ns, D_MODEL)
    cos_s, sin_s, _, _, icnt_s = _position_tables(past, ss)
    hist_s = jnp.concatenate([jnp.zeros((bs, 1, POOL_DIM), F32), state_pool[0]], axis=1)
    q_s, k_s, v_s, py_s, pnew_s = _inproj_sample(
        xs_, w_pqv, w_k, cos_s, sin_s, icnt_s, hist_s, wmix, pscale, n_streams=bs, seq=ss)
    kct = jnp.transpose(cache_k[0], (0, 2, 3, 4, 1)).reshape(bs, QK_DIM, past)
    vc = cache_v[0].reshape(bs, past * N_HEADS, V_DIM)
    ay_s = _attn_sample(lam_vecs, norm_g, q_s, kct, vc, k_s, v_s, n_streams=bs, tn=ss, past=past,
                        tk=_tile(past, 1024), lam_init=lam_init)
    x1_s, gate_s, lpos_s, cnt_s = _postmix(
        xs_, py_s, ay_s, w_gate, wpo, wao, wout, g1, b1, wr, br, tm=tm_s, rt=tm_s, dn_alpha=dn_alpha)

    ntp = np_ // tm_p
    cnt = jnp.concatenate([cnt_p[:, 0, :], cnt_s[:, 0, :]], axis=0).astype(jnp.int32)
    n_tiles = cnt.shape[0]
    group = jnp.sum(cnt, axis=0)
    padded = (group + EXPERT_BLOCK - 1) // EXPERT_BLOCK * EXPERT_BLOCK
    pad_end = jnp.cumsum(padded).astype(jnp.int32)
    run_start = (pad_end - padded)[None, :] + jnp.cumsum(cnt, axis=0) - cnt
    run_off = jnp.cumsum(cnt, axis=1) - cnt
    tail = jnp.where(padded > 0, pad_end - EXPERT_BLOCK, -1).astype(jnp.int32)
    max_rows = (np_ + ns) * TOP_K + n_tiles * N_EXPERTS * (RUN_ALIGN - 1) + N_EXPERTS * (EXPERT_BLOCK - 1)
    n_blocks = -(-max_rows // EXPERT_BLOCK)
    n_used = pad_end[-1:] // EXPERT_BLOCK
    blk_start = jnp.arange(n_blocks, dtype=jnp.int32) * EXPERT_BLOCK
    blk_expert = jnp.minimum(jnp.sum((blk_start[:, None] >= pad_end[None, :]).astype(jnp.int32), axis=1),
                             N_EXPERTS - 1)
    tables = (run_start, run_off, cnt, jnp.sum(cnt, axis=1))
    tabs_p = tuple(a[:ntp].reshape(-1).astype(jnp.int32) for a in tables)
    tabs_s = tuple(a[ntp:].reshape(-1).astype(jnp.int32) for a in tables)

    xsorted = _dispatch_first(tail, n_used, tabs_p, lpos_p, x1_p, tm=tm_p, n_blocks=n_blocks)
    xsorted = _dispatch_next(tabs_s, lpos_s, x1_s, xsorted, tm=tm_s)
    blk = jnp.arange(n_blocks, dtype=jnp.int32)
    later = (blk[None, :] > blk[:, None]) & (blk_expert[None, :] != blk_expert[:, None]) & (blk[None, :] < n_used)
    next_expert = jnp.where(jnp.any(later, axis=1), blk_expert[jnp.argmax(later, axis=1)], -1).astype(jnp.int32)
    yb = _experts(blk_expert, next_expert, n_used, xsorted, w_expert_in[0], b_glu, b_lin, w_expert_out[0], b_eo)
    y_p = _combine(tabs_p, gate_p, lpos_p, x1_p, g2, b2, yb, tm=tm_p, dn_alpha=dn_alpha)
    y_s = _combine(tabs_s, gate_s, lpos_s, x1_s, g2, b2, yb, tm=tm_s, dn_alpha=dn_alpha)

    k_prompt = jnp.transpose(kt_p.reshape(bp, N_HEADS, 2, HEAD_DIM, sp), (0, 4, 1, 2, 3))
    return (
        y_p.reshape(bp, sp, D_MODEL),
        y_s.reshape(bs, ss, D_MODEL),
        k_prompt[None],
        v_p.reshape(1, bp, sp, N_HEADS, V_DIM),
        pnew_p[:, 1:].reshape(1, bp, POOL_HIST, POOL_DIM),
        k_s.reshape(1, bs, ss, N_HEADS, 2, HEAD_DIM),
        v_s.reshape(1, bs, ss, N_HEADS, V_DIM),
        pnew_s[:, 1:].reshape(1, bs, POOL_HIST, POOL_DIM),
    )
```

```python
import functools
import math

import jax
import jax.numpy as jnp
from jax import lax
from jax.experimental import pallas as pl
from jax.experimental.pallas import tpu as pltpu

D_MODEL = 1024
CHUNK = 64
POOL_WINDOWS = (2, 4, 8, 16)
POOL_GROUP_DIM = 128
POOL_DIM = len(POOL_WINDOWS) * POOL_GROUP_DIM
POOL_HIST = max(POOL_WINDOWS) - 1
HIST_ROWS = POOL_HIST + 1
N_HEADS = 8
HEAD_DIM = 64
HALF_DIM = HEAD_DIM // 2
V_DIM = 2 * HEAD_DIM
QK_DIM = N_HEADS * 2 * HEAD_DIM
ATTN_V_WIDTH = N_HEADS * V_DIM
ATTN_SCALE = HEAD_DIM ** -0.5
LOG2_E = math.log2(math.e)
ROPE_THETA = 10000.0
SUBLN_EPS = 1e-5
N_EXPERTS = 32
TOP_K = 4
D_EXPERT = 1024
SWIGLU_LIMIT = 7.0
SWIGLU_ALPHA = 1.702
LN_EPS = 1e-5
NEG_INF = -1e30
LANES = 128
MXU_DIM = 256

F32 = jnp.float32
BF16 = jnp.bfloat16

VMEM_LIMIT = 56 * 1024 * 1024
EXPERT_BLOCK = 256
SUBLANES = 8
RUN_ALIGN = SUBLANES
ROUTE_TILE = 256
DENSE_TILE = 512
ATTN_Q_TILE = 256
RUN_BITS = tuple(range(3, 9))
RUN_BITS_SMALL = 3


def _dot(a, b):
    return jnp.dot(a, b, preferred_element_type=F32)


def _dot_nt(a, b):
    return lax.dot_general(a, b, (((1,), (1,)), ((), ())), preferred_element_type=F32)


def _params(semantics):
    return pltpu.CompilerParams(dimension_semantics=semantics, vmem_limit_bytes=VMEM_LIMIT)


def _pool_branch(u, icnt_ref, hist_ref, wmix_ref, pscale_ref, py_ref, pnew_ref, ext_ref, *, bb, tm):
    @pl.when(pl.program_id(1) == 0)
    def _():
        ext_ref[:, 0:HIST_ROWS, :] = hist_ref[...]

    for b in range(bb):
        ext_ref[b, HIST_ROWS:HIST_ROWS + tm, :] = u[b * tm:(b + 1) * tm]
    for b in range(bb):
        for g, w in enumerate(POOL_WINDOWS):
            cols = slice(g * POOL_GROUP_DIM, (g + 1) * POOL_GROUP_DIM)
            cur = ext_ref[b, HIST_ROWS:HIST_ROWS + tm, cols]
            acc = cur
            for j in range(1, w):
                acc = acc + ext_ref[b, HIST_ROWS - j:HIST_ROWS - j + tm, cols]
            d = acc * icnt_ref[:, cols] - cur
            y = _dot(d.astype(BF16), wmix_ref[g]) * pscale_ref[:, cols]
            py_ref[b * tm:(b + 1) * tm, cols] = y.astype(BF16)
    tail = ext_ref[:, tm:tm + HIST_ROWS, :]
    pnew_ref[...] = tail
    ext_ref[:, 0:HIST_ROWS, :] = tail


def _rope_rows(z, cos, sin):
    lane = lax.broadcasted_iota(jnp.int32, z.shape, 1)
    first_half = (lane % HEAD_DIM) < HALF_DIM
    partner = jnp.where(first_half, pltpu.roll(z, LANES - HALF_DIM, 1), pltpu.roll(z, HALF_DIM, 1))
    return z * cos + partner * sin


def _inproj_prompt_kernel(x_ref, w_ref, wkt_ref, cos_ref, sin_ref, cost_ref, sint_ref, icnt_ref, hist_ref,
                          wmix_ref, pscale_ref, q_ref, kt_ref, ktb_ref, v_ref, vb_ref, py_ref, pnew_ref,
                          ext_ref, *, tm):
    x = x_ref[...].astype(BF16)
    _pool_branch(_dot(x, w_ref[:, 0:POOL_DIM]), icnt_ref, hist_ref, wmix_ref, pscale_ref, py_ref, pnew_ref,
                 ext_ref, bb=1, tm=tm)

    cos, sin = cos_ref[...], sin_ref[...]
    hq = _dot(x, w_ref[:, POOL_DIM:POOL_DIM + QK_DIM])
    for h in range(N_HEADS):
        sl = slice(h * V_DIM, (h + 1) * V_DIM)
        q_ref[:, sl] = (_rope_rows(hq[:, sl], cos, sin) * (ATTN_SCALE * LOG2_E)).astype(BF16)

    hkt = _dot_nt(wkt_ref[...], x)
    cost, sint = cost_ref[...], sint_ref[...]
    for hc in range(2 * N_HEADS):
        r0 = hc * HEAD_DIM
        x1 = hkt[r0:r0 + HALF_DIM]
        x2 = hkt[r0 + HALF_DIM:r0 + HEAD_DIM]
        o1 = x1 * cost - x2 * sint
        o2 = x2 * cost + x1 * sint
        kt_ref[r0:r0 + HALF_DIM, :] = o1
        kt_ref[r0 + HALF_DIM:r0 + HEAD_DIM, :] = o2
        ktb_ref[r0:r0 + HALF_DIM, :] = o1.astype(BF16)
        ktb_ref[r0 + HALF_DIM:r0 + HEAD_DIM, :] = o2.astype(BF16)

    hv = _dot(x, w_ref[:, POOL_DIM + QK_DIM:POOL_DIM + QK_DIM + ATTN_V_WIDTH])
    vb_ref[...] = hv.astype(BF16)
    for h in range(N_HEADS):
        v_ref[pl.ds(h, tm, stride=N_HEADS), :] = hv[:, h * V_DIM:(h + 1) * V_DIM]


def _inproj_prompt(x2d, w_pqv, wkt, cos, sin, cost, sint, icnt, hist, wmix, pscale, *, n_streams, seq, tm):
    n = n_streams * seq
    nt = seq // tm
    row_map = lambda b, t: (b * nt + t, 0)
    const2 = lambda b, t: (0, 0)
    out_shape = (
        jax.ShapeDtypeStruct((n, QK_DIM), BF16),
        jax.ShapeDtypeStruct((n_streams, QK_DIM, seq), F32),
        jax.ShapeDtypeStruct((n_streams, QK_DIM, seq), BF16),
        jax.ShapeDtypeStruct((n_streams, seq * N_HEADS, V_DIM), F32),
        jax.ShapeDtypeStruct((n, ATTN_V_WIDTH), BF16),
        jax.ShapeDtypeStruct((n, POOL_DIM), BF16),
        jax.ShapeDtypeStruct((n_streams, HIST_ROWS, POOL_DIM), F32),
    )
    return pl.pallas_call(
        functools.partial(_inproj_prompt_kernel, tm=tm),
        out_shape=out_shape,
        grid=(n_streams, nt),
        in_specs=[
            pl.BlockSpec((tm, D_MODEL), row_map),
            pl.BlockSpec(w_pqv.shape, const2),
            pl.BlockSpec(wkt.shape, const2),
            pl.BlockSpec((tm, LANES), lambda b, t: (t, 0)),
            pl.BlockSpec((tm, LANES), lambda b, t: (t, 0)),
            pl.BlockSpec((HALF_DIM, tm), lambda b, t: (0, t)),
            pl.BlockSpec((HALF_DIM, tm), lambda b, t: (0, t)),
            pl.BlockSpec((tm, POOL_DIM), lambda b, t: (t, 0)),
            pl.BlockSpec((1, HIST_ROWS, POOL_DIM), lambda b, t: (b, 0, 0)),
            pl.BlockSpec((len(POOL_WINDOWS), POOL_GROUP_DIM, POOL_GROUP_DIM), lambda b, t: (0, 0, 0)),
            pl.BlockSpec((1, POOL_DIM), const2),
        ],
        out_specs=(
            pl.BlockSpec((tm, QK_DIM), row_map),
            pl.BlockSpec((None, QK_DIM, tm), lambda b, t: (b, 0, t)),
            pl.BlockSpec((None, QK_DIM, tm), lambda b, t: (b, 0, t)),
            pl.BlockSpec((None, tm * N_HEADS, V_DIM), lambda b, t: (b, t, 0)),
            pl.BlockSpec((tm, ATTN_V_WIDTH), row_map),
            pl.BlockSpec((tm, POOL_DIM), row_map),
            pl.BlockSpec((1, HIST_ROWS, POOL_DIM), lambda b, t: (b, 0, 0)),
        ),
        scratch_shapes=[pltpu.VMEM((1, HIST_ROWS + tm, POOL_DIM), F32)],
        compiler_params=_params(("arbitrary", "arbitrary")),
        name="inproj_prompt",
    )(x2d, w_pqv, wkt, cos, sin, cost, sint, icnt, hist, wmix, pscale)


def _inproj_sample_kernel(x_ref, w_ref, wk_ref, cos_ref, sin_ref, icnt_ref, hist_ref, wmix_ref, pscale_ref,
                          q_ref, k_ref, v_ref, py_ref, pnew_ref, ext_ref, *, bb, tm):
    x = x_ref[...].astype(BF16)
    _pool_branch(_dot(x, w_ref[:, 0:POOL_DIM]), icnt_ref, hist_ref, wmix_ref, pscale_ref, py_ref, pnew_ref,
                 ext_ref, bb=bb, tm=tm)
    cos = jnp.concatenate([cos_ref[...]] * bb, axis=0)
    sin = jnp.concatenate([sin_ref[...]] * bb, axis=0)
    hq = _dot(x, w_ref[:, POOL_DIM:POOL_DIM + QK_DIM])
    hk = _dot(x, wk_ref[...])
    for h in range(N_HEADS):
        sl = slice(h * V_DIM, (h + 1) * V_DIM)
        q_ref[:, sl] = (_rope_rows(hq[:, sl], cos, sin) * ATTN_SCALE).astype(BF16)
        k_ref[:, sl] = _rope_rows(hk[:, sl], cos, sin)
    v_ref[...] = _dot(x, w_ref[:, POOL_DIM + QK_DIM:POOL_DIM + QK_DIM + ATTN_V_WIDTH])


def _inproj_sample(x2d, w_pqv, wk, cos, sin, icnt, hist, wmix, pscale, *, n_streams, seq):
    n = n_streams * seq
    const2 = lambda i, t: (0, 0)
    const3 = lambda i, t: (0, 0, 0)
    out_shape = (
        jax.ShapeDtypeStruct((n, QK_DIM), BF16),
        jax.ShapeDtypeStruct((n, QK_DIM), F32),
        jax.ShapeDtypeStruct((n, ATTN_V_WIDTH), F32),
        jax.ShapeDtypeStruct((n, POOL_DIM), BF16),
        jax.ShapeDtypeStruct((n_streams, HIST_ROWS, POOL_DIM), F32),
    )
    return pl.pallas_call(
        functools.partial(_inproj_sample_kernel, bb=n_streams, tm=seq),
        out_shape=out_shape,
        grid=(1, 1),
        in_specs=[
            pl.BlockSpec((n, D_MODEL), const2),
            pl.BlockSpec(w_pqv.shape, const2),
            pl.BlockSpec(wk.shape, const2),
            pl.BlockSpec((seq, LANES), const2),
            pl.BlockSpec((seq, LANES), const2),
            pl.BlockSpec((seq, POOL_DIM), const2),
            pl.BlockSpec((n_streams, HIST_ROWS, POOL_DIM), const3),
            pl.BlockSpec((len(POOL_WINDOWS), POOL_GROUP_DIM, POOL_GROUP_DIM), const3),
            pl.BlockSpec((1, POOL_DIM), const2),
        ],
        out_specs=(
            pl.BlockSpec((n, QK_DIM), const2),
            pl.BlockSpec((n, QK_DIM), const2),
            pl.BlockSpec((n, ATTN_V_WIDTH), const2),
            pl.BlockSpec((n, POOL_DIM), const2),
            pl.BlockSpec((n_streams, HIST_ROWS, POOL_DIM), const3),
        ),
        scratch_shapes=[pltpu.VMEM((n_streams, HIST_ROWS + seq, POOL_DIM), F32)],
        compiler_params=_params(("arbitrary", "arbitrary")),
        name="inproj_sample",
    )(x2d, w_pqv, wk, cos, sin, icnt, hist, wmix, pscale)


def _lambda_value(lam_ref, lam_init):
    lv = lam_ref[...]
    s1 = jnp.sum(lv[0:1] * lv[1:2], axis=1, keepdims=True)
    s2 = jnp.sum(lv[2:3] * lv[3:4], axis=1, keepdims=True)
    return jnp.exp(s1) - jnp.exp(s2) + lam_init


def _head_norm(o, g, lam_init):
    ms = jnp.mean(o * o, axis=-1, keepdims=True)
    return o * lax.rsqrt(ms + SUBLN_EPS) * g * (1.0 - lam_init)


def _attn_prompt_kernel(lam_ref, g_ref, q_ref, kt_ref, v_ref, o_ref, vext, *, seq, tq, lam_init):
    lam = _lambda_value(lam_ref, lam_init)
    g = g_ref[...]
    r = lax.broadcasted_iota(jnp.int32, (tq, tq), 0)
    c = lax.broadcasted_iota(jnp.int32, (tq, tq), 1)
    diag_visible = (c // CHUNK) <= (r // CHUNK)
    lane = lax.broadcasted_iota(jnp.int32, (tq, V_DIM), 1)
    vext[:, 0:V_DIM] = v_ref[...]
    vlane = lax.broadcasted_iota(jnp.int32, (seq, V_DIM), 1)
    vext[:, V_DIM:] = jnp.where(vlane == 0, 1.0, 0.0).astype(vext.dtype)

    def scores(i):
        lo = i * tq
        q = q_ref[lo:lo + tq, :]
        zero = jnp.zeros_like(q)
        qc = (jnp.where(lane < HEAD_DIM, q, zero), jnp.where(lane >= HEAD_DIM, q, zero))
        out = []
        for k in range(2):
            sd = jnp.where(diag_visible, _dot(qc[k], kt_ref[:, lo:lo + tq]), NEG_INF)
            m = jnp.max(sd, axis=1, keepdims=True)
            sp = None
            if i > 0:
                sp = _dot(qc[k], kt_ref[:, 0:lo])
                m = jnp.maximum(m, jnp.max(sp, axis=1, keepdims=True))
            out.append((sd, sp, m))
        return out

    def finish(i, parts):
        lo = i * tq
        normed = []
        for sd, sp, m in parts:
            acc = _dot(jnp.exp2(sd - m).astype(BF16), vext[lo:lo + tq, :])
            if sp is not None:
                acc = acc + _dot(jnp.exp2(sp - m).astype(BF16), vext[0:lo, :])
            normed.append(acc[:, 0:V_DIM] / acc[:, V_DIM:V_DIM + 1])
        o = normed[0] - lam * normed[1]
        o_ref[lo:lo + tq, :] = _head_norm(o, g, lam_init).astype(o_ref.dtype)

    nq = seq // tq
    pending = scores(0)
    for i in range(nq):
        upcoming = scores(i + 1) if i + 1 < nq else None
        finish(i, pending)
        pending = upcoming


def _attn_prompt(lam_vecs, norm_g, q, ktb, vb, *, n_streams, seq, tq, lam_init):
    return pl.pallas_call(
        functools.partial(_attn_prompt_kernel, seq=seq, tq=tq, lam_init=lam_init),
        out_shape=jax.ShapeDtypeStruct((n_streams * seq, ATTN_V_WIDTH), BF16),
        grid=(n_streams, N_HEADS),
        in_specs=[
            pl.BlockSpec((4, HEAD_DIM), lambda b, h: (0, 0)),
            pl.BlockSpec((1, V_DIM), lambda b, h: (0, 0)),
            pl.BlockSpec((seq, V_DIM), lambda b, h: (b, h)),
            pl.BlockSpec((None, V_DIM, seq), lambda b, h: (b, h, 0)),
            pl.BlockSpec((seq, V_DIM), lambda b, h: (b, h)),
        ],
        out_specs=pl.BlockSpec((seq, V_DIM), lambda b, h: (b, h)),
        scratch_shapes=[pltpu.VMEM((seq, MXU_DIM), BF16)],
        compiler_params=_params(("arbitrary", "arbitrary")),
        name="attn_prompt",
    )(lam_vecs, norm_g, q, ktb, vb)


def _attn_sample_kernel(lam_ref, g_ref, q_ref, kc_ref, vc_ref, kn_ref, vn_ref, o_ref,
                        s_scr, w_scr, wn_scr, m_scr, acc_scr, qbd_scr, *, nk, tn, past, lam_init):
    j = pl.program_id(1)
    half = N_HEADS * tn

    @pl.when(j == 0)
    def _():
        q = q_ref[...]
        qt = jnp.concatenate([q] * (2 * N_HEADS), axis=0)
        r = lax.broadcasted_iota(jnp.int32, qt.shape, 0)
        l = lax.broadcasted_iota(jnp.int32, qt.shape, 1)
        keep = ((r // half) == ((l % V_DIM) // HEAD_DIM)) & (((r % half) // tn) == (l // V_DIM))
        qbd_scr[...] = jnp.where(keep, qt, jnp.zeros_like(qt))
        m_scr[...] = jnp.full(m_scr.shape, NEG_INF, F32)
        acc_scr[...] = jnp.zeros(acc_scr.shape, F32)

    @pl.when(j < nk)
    def _():
        s = _dot(qbd_scr[...], kc_ref[...].astype(BF16))
        s_scr[j] = s
        m_scr[...] = jnp.maximum(m_scr[...], jnp.max(s, axis=1, keepdims=True))

    @pl.when(j == nk - 1)
    def _():
        lam = _lambda_value(lam_ref, lam_init)
        sn = _dot_nt(qbd_scr[...], kn_ref[...].astype(BF16))
        qpos = past + (lax.broadcasted_iota(jnp.int32, sn.shape, 0) % tn)
        kpos = past + lax.broadcasted_iota(jnp.int32, sn.shape, 1)
        sn = jnp.where((kpos // CHUNK) <= (qpos // CHUNK), sn, NEG_INF)
        m = jnp.maximum(m_scr[...], jnp.max(sn, axis=1, keepdims=True))
        pn = jnp.exp(sn - m)
        l = jnp.sum(pn, axis=1, keepdims=True)
        for c in range(nk):
            p = jnp.exp(s_scr[c] - m)
            s_scr[c] = p
            l = l + jnp.sum(p, axis=1, keepdims=True)
        r0 = 1.0 / l[:half]
        r1 = lam / l[half:]
        wn_scr[...] = pn[:half] * r0 - pn[half:] * r1
        for c in range(nk):
            p = s_scr[c]
            w_scr[c] = (p[:half] * r0 - p[half:] * r1).astype(BF16)

    def v_rows(ref):
        tk = ref.shape[0] // N_HEADS
        return jnp.concatenate([ref[pl.ds(h, tk, stride=N_HEADS), :] for h in range(N_HEADS)],
                               axis=1).astype(BF16)

    @pl.when(j >= nk)
    def _():
        acc_scr[...] += _dot(w_scr[j - nk], v_rows(vc_ref))

    @pl.when(j == 2 * nk - 1)
    def _():
        acc = acc_scr[...] + _dot(wn_scr[...].astype(BF16), vn_ref[...].astype(BF16))
        g = g_ref[...]
        for h in range(N_HEADS):
            o = acc[h * tn:(h + 1) * tn, h * V_DIM:(h + 1) * V_DIM]
            o_ref[:, h * V_DIM:(h + 1) * V_DIM] = _head_norm(o, g, lam_init).astype(o_ref.dtype)


def _attn_sample(lam_vecs, norm_g, q, kct, vc, kn, vn, *, n_streams, tn, past, tk, lam_init):
    nk = past // tk
    rows = 2 * N_HEADS * tn
    half = N_HEADS * tn
    return pl.pallas_call(
        functools.partial(_attn_sample_kernel, nk=nk, tn=tn, past=past, lam_init=lam_init),
        out_shape=jax.ShapeDtypeStruct((n_streams * tn, ATTN_V_WIDTH), BF16),
        grid=(n_streams, 2 * nk),
        in_specs=[
            pl.BlockSpec((4, HEAD_DIM), lambda b, j: (0, 0)),
            pl.BlockSpec((1, V_DIM), lambda b, j: (0, 0)),
            pl.BlockSpec((tn, QK_DIM), lambda b, j: (b, 0)),
            pl.BlockSpec((None, QK_DIM, tk), lambda b, j: (b, 0, jnp.minimum(j, nk - 1))),
            pl.BlockSpec((None, tk * N_HEADS, V_DIM), lambda b, j: (b, jnp.maximum(j - nk, 0), 0)),
            pl.BlockSpec((tn, QK_DIM), lambda b, j: (b, 0)),
            pl.BlockSpec((tn, ATTN_V_WIDTH), lambda b, j: (b, 0)),
        ],
        out_specs=pl.BlockSpec((tn, ATTN_V_WIDTH), lambda b, j: (b, 0)),
        scratch_shapes=[
            pltpu.VMEM((nk, rows, tk), F32),
            pltpu.VMEM((nk, half, tk), BF16),
            pltpu.VMEM((half, tn), F32),
            pltpu.VMEM((rows, 1), F32),
            pltpu.VMEM((half, ATTN_V_WIDTH), F32),
            pltpu.VMEM((rows, QK_DIM), BF16),
        ],
        compiler_params=_params(("arbitrary", "arbitrary")),
        name="attn_sample",
    )(lam_vecs, norm_g, q, kct, vc, kn, vn)


def _layer_norm(z, g, b):
    mu = jnp.mean(z, axis=-1, keepdims=True)
    zc = z - mu
    var = jnp.mean(zc * zc, axis=-1, keepdims=True)
    return zc * lax.rsqrt(var + LN_EPS) * g + b


def _postmix_kernel(x_ref, xprev_ref, py_ref, ay_ref, wg_ref, wpo_ref, wao_ref, wout_ref, g1_ref, b1_ref,
                    wr_ref, br_ref, x1_ref, gate_ref, lpos_ref, cnt_ref, mix_scr, *, tm, rt, dn_alpha):
    @pl.when(pl.program_id(0) == 0)
    def _():
        mix_scr[...] = jnp.zeros(mix_scr.shape, mix_scr.dtype)

    mixed_prev = mix_scr[...]

    mo = _dot(mixed_prev, wout_ref[...])
    x1 = _layer_norm(dn_alpha * xprev_ref[...] + mo, g1_ref[...], b1_ref[...])
    x1_ref[...] = x1
    logits = _dot(x1.astype(BF16), wr_ref[...]) + br_ref[...]

    gates = jax.nn.sigmoid(_dot(x_ref[...].astype(BF16), wg_ref[...]))
    a = _dot(py_ref[...], wpo_ref[...])
    b = _dot(ay_ref[...], wao_ref[...])
    mix_scr[...] = (gates[:, :D_MODEL] * a + gates[:, D_MODEL:] * b).astype(BF16)

    for sub in range(tm // rt):
        rows = slice(sub * rt, (sub + 1) * rt)
        gate_ref[rows, :], lpos_ref[rows, :], cnt_ref[sub] = _route_tile(logits[rows], rt)


def _route_tile(logits, tm):
    lane = lax.broadcasted_iota(jnp.int32, logits.shape, 1)
    work = logits
    vals, idxs = [], []
    for _ in range(TOP_K):
        mx = jnp.max(work, axis=1, keepdims=True)
        ix = jnp.min(jnp.where(work == mx, lane, N_EXPERTS), axis=1, keepdims=True)
        vals.append(mx)
        idxs.append(ix)
        work = jnp.where(lane == ix, -jnp.inf, work)
    exps = [jnp.exp(v - vals[0]) for v in vals]
    denom = exps[0] + exps[1] + exps[2] + exps[3]

    onehot = jnp.zeros(logits.shape, F32)
    for ix in idxs:
        onehot = onehot + (lane == ix).astype(F32)
    r = lax.broadcasted_iota(jnp.int32, (tm, tm), 0)
    c = lax.broadcasted_iota(jnp.int32, (tm, tm), 1)
    tri = jnp.where(c < r, 1.0, 0.0).astype(BF16)
    earlier = _dot(tri, onehot.astype(BF16))
    cnt = jnp.sum(onehot, axis=0, keepdims=True)
    units = jnp.floor((cnt + (RUN_ALIGN - 1.0)) * (1.0 / RUN_ALIGN))
    er = lax.broadcasted_iota(jnp.int32, (N_EXPERTS, N_EXPERTS), 0)
    ec = lax.broadcasted_iota(jnp.int32, (N_EXPERTS, N_EXPERTS), 1)
    upper = jnp.where(er < ec, 1.0, 0.0).astype(BF16)
    run_off = _dot(jnp.broadcast_to(units, (SUBLANES, N_EXPERTS)).astype(BF16), upper)[0:1] * float(RUN_ALIGN)
    pos = earlier + run_off

    lane_out = lax.broadcasted_iota(jnp.int32, (tm, LANES), 1)
    lpos_out = jnp.zeros((tm, LANES), jnp.int32)
    gate_out = jnp.zeros((tm, LANES), F32)
    for k in range(TOP_K):
        lpos_k = jnp.sum(jnp.where(lane == idxs[k], pos, 0.0), axis=1, keepdims=True).astype(jnp.int32)
        lpos_out = jnp.where(lane_out == k, lpos_k, lpos_out)
        gate_out = jnp.where(lane_out == k, exps[k] / denom, gate_out)
    return gate_out, lpos_out, units * float(RUN_ALIGN)


def _postmix(x2d, py, ay, wg, wpo, wao, wout, g1, b1, wr, br, *, tm, rt, dn_alpha):
    n = x2d.shape[0]
    nt = n // tm
    row = lambda i: (jnp.minimum(i, nt - 1), 0)
    prev = lambda i: (jnp.maximum(i - 1, 0), 0)
    const = lambda i: (0, 0)
    out_shape = (
        jax.ShapeDtypeStruct((n, D_MODEL), F32),
        jax.ShapeDtypeStruct((n, LANES), F32),
        jax.ShapeDtypeStruct((n, LANES), jnp.int32),
        jax.ShapeDtypeStruct((n // rt, 1, N_EXPERTS), F32),
    )
    return pl.pallas_call(
        functools.partial(_postmix_kernel, tm=tm, rt=rt, dn_alpha=dn_alpha),
        out_shape=out_shape,
        grid=(nt + 1,),
        in_specs=[
            pl.BlockSpec((tm, D_MODEL), row),
            pl.BlockSpec((tm, D_MODEL), prev),
            pl.BlockSpec((tm, POOL_DIM), row),
            pl.BlockSpec((tm, ATTN_V_WIDTH), row),
            pl.BlockSpec(wg.shape, const),
            pl.BlockSpec(wpo.shape, const),
            pl.BlockSpec(wao.shape, const),
            pl.BlockSpec(wout.shape, const),
            pl.BlockSpec((1, D_MODEL), const),
            pl.BlockSpec((1, D_MODEL), const),
            pl.BlockSpec(wr.shape, const),
            pl.BlockSpec((1, N_EXPERTS), const),
        ],
        out_specs=(
            pl.BlockSpec((tm, D_MODEL), prev),
            pl.BlockSpec((tm, LANES), prev),
            pl.BlockSpec((tm, LANES), prev),
            pl.BlockSpec((tm // rt, 1, N_EXPERTS), lambda i: (jnp.maximum(i - 1, 0), 0, 0)),
        ),
        scratch_shapes=[pltpu.VMEM((tm, D_MODEL), BF16)],
        compiler_params=_params(("arbitrary",)),
        name="postmix",
    )(x2d, x2d, py, ay, wg, wpo, wao, wout, g1, b1, wr, br)


def _local_rows(tm):
    return TOP_K * tm + N_EXPERTS * RUN_ALIGN


def _for_each_run_chunk(tile, tab_refs, local_buf, sorted_ref, sem, to_sorted, fn):
    gs_ref, off_ref, cnt_ref = tab_refs[:3]

    def per_expert(e, carry):
        t = tile * N_EXPERTS + e
        cnt, off, gs = cnt_ref[t], off_ref[t], gs_ref[t]

        def chunks(bits):
            for b in bits:
                size = 1 << b

                @pl.when((cnt & size) != 0)
                def _():
                    lower = cnt & (size - 1)
                    loc = local_buf.at[pl.ds(pl.multiple_of(off + lower, RUN_ALIGN), size)]
                    srt = sorted_ref.at[pl.ds(pl.multiple_of(gs + lower, RUN_ALIGN), size)]
                    fn(pltpu.make_async_copy(loc, srt, sem) if to_sorted
                       else pltpu.make_async_copy(srt, loc, sem))

        chunks(RUN_BITS[:RUN_BITS_SMALL])

        @pl.when(cnt >= (1 << RUN_BITS[RUN_BITS_SMALL]))
        def _():
            chunks(RUN_BITS[RUN_BITS_SMALL:])
        return carry

    lax.fori_loop(0, N_EXPERTS, per_expert, 0)


def _wait_run_rows(total, local_buf, sorted_ref, sem, to_sorted):
    rows = local_buf.shape[0]
    for b in range(RUN_BITS[0], rows.bit_length()):
        size = 1 << b

        @pl.when((total & size) != 0)
        def _():
            loc, srt = local_buf.at[pl.ds(0, size)], sorted_ref.at[pl.ds(0, size)]
            (pltpu.make_async_copy(loc, srt, sem) if to_sorted else pltpu.make_async_copy(srt, loc, sem)).wait()


def _dispatch_tile(tab_refs, lpos_ref, x_ref, xs_ref, xloc, sem, tm):
    lpos = lpos_ref[...]
    col = lax.broadcasted_iota(jnp.int32, (tm, _local_rows(tm)), 1)
    hit = col == lpos[:, 0:1]
    for k in range(1, TOP_K):
        hit = jnp.logical_or(hit, col == lpos[:, k:k + 1])
    perm_t = jnp.where(hit, 1.0, 0.0).astype(BF16)
    tile = pl.program_id(0)
    slot = tile % 2
    xloc[slot] = lax.dot_general(perm_t, x_ref[...].astype(BF16), (((0,), (0,)), ((), ())),
                                 preferred_element_type=F32)

    def copies(t, s, fn):
        _for_each_run_chunk(t, tab_refs, xloc.at[s], xs_ref, sem.at[s], True, fn)

    copies(tile, slot, lambda cp: cp.start())
    tot_ref = tab_refs[3]

    @pl.when(tile > 0)
    def _():
        _wait_run_rows(tot_ref[tile - 1], xloc.at[1 - slot], xs_ref, sem.at[1 - slot], True)

    @pl.when(tile == pl.num_programs(0) - 1)
    def _():
        _wait_run_rows(tot_ref[tile], xloc.at[slot], xs_ref, sem.at[slot], True)


def _dispatch_first_kernel(tail_ref, nu_ref, gs_ref, off_ref, cnt_ref, tot_ref, lpos_ref, x_ref, xs_ref,
                           xloc, zbuf, sem, zsem, *, tm, n_blocks):
    @pl.when(pl.program_id(0) == 0)
    def _():
        zbuf[...] = jnp.zeros(zbuf.shape, zbuf.dtype)

        def zero_copy(row):
            row = pl.multiple_of(row, EXPERT_BLOCK)
            return pltpu.make_async_copy(zbuf, xs_ref.at[pl.ds(row, EXPERT_BLOCK)], zsem)

        def over_blocks(fn):
            for e in range(N_EXPERTS):
                @pl.when(tail_ref[e] >= 0)
                def _():
                    fn(zero_copy(tail_ref[e]))
            lax.fori_loop(nu_ref[0], n_blocks, lambda b, c: (fn(zero_copy(b * EXPERT_BLOCK)), c)[1], 0)

        over_blocks(lambda cp: cp.start())
        over_blocks(lambda cp: cp.wait())

    _dispatch_tile((gs_ref, off_ref, cnt_ref, tot_ref), lpos_ref, x_ref, xs_ref, xloc, sem, tm)


def _dispatch_next_kernel(gs_ref, off_ref, cnt_ref, tot_ref, lpos_ref, x_ref, xs_in_ref, xs_ref, xloc, sem,
                          *, tm):
    del xs_in_ref
    _dispatch_tile((gs_ref, off_ref, cnt_ref, tot_ref), lpos_ref, x_ref, xs_ref, xloc, sem, tm)


def _dispatch_first(tail, n_used, tabs, lpos, x1, *, tm, n_blocks):
    n = x1.shape[0]
    grid_spec = pltpu.PrefetchScalarGridSpec(
        num_scalar_prefetch=6,
        grid=(n // tm,),
        in_specs=[pl.BlockSpec((tm, LANES), lambda i, *_: (i, 0)),
                  pl.BlockSpec((tm, D_MODEL), lambda i, *_: (i, 0))],
        out_specs=pl.BlockSpec(memory_space=pl.ANY),
        scratch_shapes=[pltpu.VMEM((2, _local_rows(tm), D_MODEL), F32),
                        pltpu.VMEM((EXPERT_BLOCK, D_MODEL), F32),
                        pltpu.SemaphoreType.DMA((2,)), pltpu.SemaphoreType.DMA],
    )
    return pl.pallas_call(
        functools.partial(_dispatch_first_kernel, tm=tm, n_blocks=n_blocks),
        out_shape=jax.ShapeDtypeStruct((n_blocks * EXPERT_BLOCK, D_MODEL), F32),
        grid_spec=grid_spec,
        compiler_params=_params(("arbitrary",)),
        name="dispatch_first",
    )(tail, n_used, *tabs, lpos, x1)


def _dispatch_next(tabs, lpos, x1, xs, *, tm):
    n = x1.shape[0]
    grid_spec = pltpu.PrefetchScalarGridSpec(
        num_scalar_prefetch=len(tabs),
        grid=(n // tm,),
        in_specs=[pl.BlockSpec((tm, LANES), lambda i, *_: (i, 0)),
                  pl.BlockSpec((tm, D_MODEL), lambda i, *_: (i, 0)),
                  pl.BlockSpec(memory_space=pl.ANY)],
        out_specs=pl.BlockSpec(memory_space=pl.ANY),
        scratch_shapes=[pltpu.VMEM((2, _local_rows(tm), D_MODEL), F32), pltpu.SemaphoreType.DMA((2,))],
    )
    return pl.pallas_call(
        functools.partial(_dispatch_next_kernel, tm=tm),
        out_shape=jax.ShapeDtypeStruct(xs.shape, xs.dtype),
        grid_spec=grid_spec,
        input_output_aliases={6: 0},
        compiler_params=_params(("arbitrary",)),
        name="dispatch_next",
    )(*tabs, lpos, x1, xs)


def _experts_kernel(be_ref, nxt_ref, nu_ref, xs_ref, bg_ref, bl_ref, bo_ref, win_hbm, wo_hbm, y_ref,
                    win_buf, wo_buf, wg_scr, wl_scr, wo_scr, sem_in, sem_out):
    i = pl.program_id(0)
    used = i < nu_ref[0]
    first_of_expert = jnp.logical_or(i == 0, be_ref[i] != be_ref[jnp.maximum(i - 1, 0)])

    def weight_copies(e):
        return (pltpu.make_async_copy(win_hbm.at[e], win_buf, sem_in),
                pltpu.make_async_copy(wo_hbm.at[e], wo_buf, sem_out))

    @pl.when(jnp.logical_not(used))
    def _():
        y_ref[...] = jnp.zeros(y_ref.shape, y_ref.dtype)

    @pl.when(i == 0)
    def _():
        for cp in weight_copies(be_ref[0]):
            cp.start()

    @pl.when(jnp.logical_and(used, first_of_expert))
    def _():
        for cp in weight_copies(be_ref[i]):
            cp.wait()
        r = lax.broadcasted_iota(jnp.int32, (MXU_DIM, MXU_DIM), 0)
        c = lax.broadcasted_iota(jnp.int32, (MXU_DIM, MXU_DIM), 1)
        src = jnp.where(c < LANES, 2 * c, 2 * (c - LANES) + 1)
        sel = jnp.where(r == src, 1.0, 0.0).astype(BF16)
        for gq in range(2 * D_EXPERT // MXU_DIM):
            blk = win_buf[:, gq * MXU_DIM:(gq + 1) * MXU_DIM].astype(BF16)
            d = _dot(blk, sel)
            wg_scr[:, gq * LANES:(gq + 1) * LANES] = d[:, :LANES].astype(BF16)
            wl_scr[:, gq * LANES:(gq + 1) * LANES] = d[:, LANES:].astype(BF16)
        wo_scr[...] = wo_buf[...].astype(BF16)

        @pl.when(nxt_ref[i] >= 0)
        def _():
            for cp in weight_copies(nxt_ref[i]):
                cp.start(priority=1)

    @pl.when(used)
    def _():
        xb = xs_ref[...].astype(BF16)
        glu = jnp.minimum(_dot(xb, wg_scr[...]) + bg_ref[...], SWIGLU_LIMIT)
        lin = jnp.clip(_dot(xb, wl_scr[...]) + bl_ref[...], -SWIGLU_LIMIT, SWIGLU_LIMIT)
        act = glu * jax.nn.sigmoid(SWIGLU_ALPHA * glu) * (lin + 1.0)
        y_ref[...] = _dot(act.astype(BF16), wo_scr[...]) + bo_ref[...]


def _experts(blk_expert, next_expert, n_used, xs, w_in, b_glu, b_lin, w_out, b_out):
    rows = xs.shape[0]
    n_blocks = rows // EXPERT_BLOCK
    wsel = lambda i, be, nxt, nu: (be[i], 0, 0)
    grid_spec = pltpu.PrefetchScalarGridSpec(
        num_scalar_prefetch=3,
        grid=(n_blocks,),
        in_specs=[
            pl.BlockSpec((EXPERT_BLOCK, D_MODEL), lambda i, be, nxt, nu: (jnp.minimum(i, nu[0] - 1), 0)),
            pl.BlockSpec((None, 1, D_EXPERT), wsel),
            pl.BlockSpec((None, 1, D_EXPERT), wsel),
            pl.BlockSpec((None, 1, D_MODEL), wsel),
            pl.BlockSpec(memory_space=pl.ANY),
            pl.BlockSpec(memory_space=pl.ANY),
        ],
        out_specs=pl.BlockSpec((EXPERT_BLOCK, D_MODEL), lambda i, be, nxt, nu: (i, 0)),
        scratch_shapes=[pltpu.VMEM((D_MODEL, 2 * D_EXPERT), F32),
                        pltpu.VMEM((D_EXPERT, D_MODEL), F32),
                        pltpu.VMEM((D_MODEL, D_EXPERT), BF16),
                        pltpu.VMEM((D_MODEL, D_EXPERT), BF16),
                        pltpu.VMEM((D_EXPERT, D_MODEL), BF16),
                        pltpu.SemaphoreType.DMA, pltpu.SemaphoreType.DMA],
    )
    return pl.pallas_call(
        _experts_kernel,
        out_shape=jax.ShapeDtypeStruct((rows, D_MODEL), F32),
        grid_spec=grid_spec,
        compiler_params=_params(("arbitrary",)),
        name="experts",
    )(blk_expert, next_expert, n_used, xs, b_glu, b_lin, b_out, w_in, w_out)


def _split_bf16(a):
    hi = a.astype(BF16)
    return hi, (a - hi.astype(F32)).astype(BF16)


def _combine_kernel(gs_ref, off_ref, cnt_ref, tot_ref, gate_ref, lpos_ref, x1_ref, g2_ref, b2_ref, yb_ref, o_ref,
                    yloc, sem, *, tm, dn_alpha):
    tile = pl.program_id(0)
    slot = tile % 2
    tabs = (gs_ref, off_ref, cnt_ref)

    def copies(t, s, fn):
        _for_each_run_chunk(t, tabs, yloc.at[s], yb_ref, sem.at[s], False, fn)

    @pl.when(tile == 0)
    def _():
        yloc[...] = jnp.zeros(yloc.shape, yloc.dtype)
        copies(tile, slot, lambda cp: cp.start())

    @pl.when(tile + 1 < pl.num_programs(0))
    def _():
        copies(tile + 1, 1 - slot, lambda cp: cp.start())

    _wait_run_rows(tot_ref[tile], yloc.at[slot], yb_ref, sem.at[slot], False)

    gate, lpos = gate_ref[...], lpos_ref[...]
    col = lax.broadcasted_iota(jnp.int32, (tm, _local_rows(tm)), 1)
    weights = jnp.zeros(col.shape, F32)
    for k in range(TOP_K):
        weights = jnp.where(col == lpos[:, k:k + 1], gate[:, k:k + 1], weights)
    w_hi, w_lo = _split_bf16(weights)
    y_hi, y_lo = _split_bf16(yloc[slot])
    y = _dot(w_hi, y_hi) + (_dot(w_hi, y_lo) + _dot(w_lo, y_hi))
    o_ref[...] = _layer_norm(dn_alpha * x1_ref[...] + y, g2_ref[...], b2_ref[...])


def _combine(tabs, gate, lpos, x1, g2, b2, yb, *, tm, dn_alpha):
    n = x1.shape[0]
    grid_spec = pltpu.PrefetchScalarGridSpec(
        num_scalar_prefetch=len(tabs),
        grid=(n // tm,),
        in_specs=[
            pl.BlockSpec((tm, LANES), lambda i, *_: (i, 0)),
            pl.BlockSpec((tm, LANES), lambda i, *_: (i, 0)),
            pl.BlockSpec((tm, D_MODEL), lambda i, *_: (i, 0)),
            pl.BlockSpec((1, D_MODEL), lambda i, *_: (0, 0)),
            pl.BlockSpec((1, D_MODEL), lambda i, *_: (0, 0)),
            pl.BlockSpec(memory_space=pl.ANY),
        ],
        out_specs=pl.BlockSpec((tm, D_MODEL), lambda i, *_: (i, 0)),
        scratch_shapes=[pltpu.VMEM((2, _local_rows(tm), D_MODEL), F32), pltpu.SemaphoreType.DMA((2,))],
    )
    return pl.pallas_call(
        functools.partial(_combine_kernel, tm=tm, dn_alpha=dn_alpha),
        out_shape=jax.ShapeDtypeStruct((n, D_MODEL), F32),
        grid_spec=grid_spec,
        compiler_params=_params(("arbitrary",)),
        name="combine",
    )(*tabs, gate, lpos, x1, g2, b2, yb)


def _position_tables(pos0, seq):
    pos = pos0 + jnp.arange(seq, dtype=jnp.int32)
    inv = ROPE_THETA ** (-jnp.arange(HALF_DIM, dtype=F32) / HALF_DIM)
    ang = pos.astype(F32)[:, None] * inv[None, :]
    cos, sin = jnp.cos(ang), jnp.sin(ang)
    cos_rows = jnp.concatenate([cos, cos, cos, cos], axis=-1)
    sin_rows = jnp.concatenate([-sin, sin, -sin, sin], axis=-1)
    icnt = jnp.concatenate(
        [jnp.broadcast_to((1.0 / jnp.minimum(pos + 1, w).astype(F32))[:, None], (seq, POOL_GROUP_DIM))
         for w in POOL_WINDOWS], axis=-1)
    return cos_rows, sin_rows, cos.T, sin.T, icnt


def _tile(n, pref):
    t = min(n, pref)
    while n % t:
        t //= 2
    return t


def kernel(x_prompt, x_sample, cache_k, cache_v, state_pool, w_in, w_pool_mix, pool_scale, w_pool_out,
           lambda_q1, lambda_k1, lambda_q2, lambda_k2, attn_norm_g, w_attn_out, w_out, ln1_g, ln1_b,
           w_router, b_router, w_expert_in, b_expert_in, w_expert_out, b_expert_out, ln2_g, ln2_b):
    depth = w_in.shape[0]
    assert depth == 1, "single-layer step"
    dn_alpha = (2.0 * depth) ** 0.25
    lam_init = 0.8 - 0.6 * math.exp(-0.3 * 0)
    bp, sp, _ = x_prompt.shape
    bs, ss, _ = x_sample.shape
    past = cache_k.shape[2]
    np_, ns = bp * sp, bs * ss

    c_q, c_k, c_v = POOL_DIM, POOL_DIM + QK_DIM, POOL_DIM + 2 * QK_DIM
    c_gate = c_v + ATTN_V_WIDTH
    w0 = w_in[0]
    w_pqv = jnp.concatenate([w0[:, :c_k], w0[:, c_v:c_gate]], axis=1).astype(BF16)
    w_k = w0[:, c_k:c_v].astype(BF16)
    w_gate = w0[:, c_gate:].astype(BF16)
    wmix = w_pool_mix[0].astype(BF16)
    pscale = pool_scale[0].reshape(1, POOL_DIM)
    wpo = w_pool_out[0].astype(BF16)
    wao = w_attn_out[0].astype(BF16)
    wout = w_out[0].astype(BF16)
    wr = w_router[0].astype(BF16)
    br = b_router[0].reshape(1, N_EXPERTS)
    lam_vecs = jnp.stack([lambda_q1[0], lambda_k1[0], lambda_q2[0], lambda_k2[0]])
    norm_g = attn_norm_g[0].reshape(1, V_DIM)
    g1, b1 = ln1_g[0].reshape(1, D_MODEL), ln1_b[0].reshape(1, D_MODEL)
    g2, b2 = ln2_g[0].reshape(1, D_MODEL), ln2_b[0].reshape(1, D_MODEL)
    b_glu = b_expert_in[0][:, 0::2].reshape(N_EXPERTS, 1, D_EXPERT)
    b_lin = b_expert_in[0][:, 1::2].reshape(N_EXPERTS, 1, D_EXPERT)
    b_eo = b_expert_out[0].reshape(N_EXPERTS, 1, D_MODEL)

    xp = x_prompt.reshape(np_, D_MODEL)
    cos_p, sin_p, cost_p, sint_p, icnt_p = _position_tables(0, sp)
    hist_p = jnp.zeros((bp, HIST_ROWS, POOL_DIM), F32)
    q_p, kt_p, ktb_p, v_p, vb_p, py_p, pnew_p = _inproj_prompt(
        xp, w_pqv, w_k.T, cos_p, sin_p, cost_p, sint_p, icnt_p, hist_p, wmix, pscale,
        n_streams=bp, seq=sp, tm=_tile(sp, DENSE_TILE))
    ay_p = _attn_prompt(lam_vecs, norm_g, q_p, ktb_p, vb_p, n_streams=bp, seq=sp,
                        tq=_tile(sp, ATTN_Q_TILE), lam_init=lam_init)
    tm_p, tm_s = _tile(np_, ROUTE_TILE), _tile(ns, ROUTE_TILE)
    x1_p, gate_p, lpos_p, cnt_p = _postmix(
        xp, py_p, ay_p, w_gate, wpo, wao, wout, g1, b1, wr, br,
        tm=max(tm_p, _tile(np_, DENSE_TILE)), rt=tm_p, dn_alpha=dn_alpha)

    xs_ = x_sample.reshape(ns, D_MODEL)
    cos_s, sin_s, _, _, icnt_s = _position_tables(past, ss)
    hist_s = jnp.concatenate([jnp.zeros((bs, 1, POOL_DIM), F32), state_pool[0]], axis=1)
    q_s, k_s, v_s, py_s, pnew_s = _inproj_sample(
        xs_, w_pqv, w_k, cos_s, sin_s, icnt_s, hist_s, wmix, pscale, n_streams=bs, seq=ss)
    kct = jnp.transpose(cache_k[0], (0, 2, 3, 4, 1)).reshape(bs, QK_DIM, past)
    vc = cache_v[0].reshape(bs, past * N_HEADS, V_DIM)
    ay_s = _attn_sample(lam_vecs, norm_g, q_s, kct, vc, k_s, v_s, n_streams=bs, tn=ss, past=past,
                        tk=_tile(past, 1024), lam_init=lam_init)
    x1_s, gate_s, lpos_s, cnt_s = _postmix(
        xs_, py_s, ay_s, w_gate, wpo, wao, wout, g1, b1, wr, br, tm=tm_s, rt=tm_s, dn_alpha=dn_alpha)

    ntp = np_ // tm_p
    cnt = jnp.concatenate([cnt_p[:, 0, :], cnt_s[:, 0, :]], axis=0).astype(jnp.int32)
    n_tiles = cnt.shape[0]
    group = jnp.sum(cnt, axis=0)
    padded = (group + EXPERT_BLOCK - 1) // EXPERT_BLOCK * EXPERT_BLOCK
    pad_end = jnp.cumsum(padded).astype(jnp.int32)
    run_start = (pad_end - padded)[None, :] + jnp.cumsum(cnt, axis=0) - cnt
    run_off = jnp.cumsum(cnt, axis=1) - cnt
    tail = jnp.where(padded > 0, pad_end - EXPERT_BLOCK, -1).astype(jnp.int32)
    max_rows = (np_ + ns) * TOP_K + n_tiles * N_EXPERTS * (RUN_ALIGN - 1) + N_EXPERTS * (EXPERT_BLOCK - 1)
    n_blocks = -(-max_rows // EXPERT_BLOCK)
    n_used = pad_end[-1:] // EXPERT_BLOCK
    blk_start = jnp.arange(n_blocks, dtype=jnp.int32) * EXPERT_BLOCK
    blk_expert = jnp.minimum(jnp.sum((blk_start[:, None] >= pad_end[None, :]).astype(jnp.int32), axis=1),
                             N_EXPERTS - 1)
    tables = (run_start, run_off, cnt, jnp.sum(cnt, axis=1))
    tabs_p = tuple(a[:ntp].reshape(-1).astype(jnp.int32) for a in tables)
    tabs_s = tuple(a[ntp:].reshape(-1).astype(jnp.int32) for a in tables)

    xsorted = _dispatch_first(tail, n_used, tabs_p, lpos_p, x1_p, tm=tm_p, n_blocks=n_blocks)
    xsorted = _dispatch_next(tabs_s, lpos_s, x1_s, xsorted, tm=tm_s)
    blk = jnp.arange(n_blocks, dtype=jnp.int32)
    later = (blk[None, :] > blk[:, None]) & (blk_expert[None, :] != blk_expert[:, None]) & (blk[None, :] < n_used)
    next_expert = jnp.where(jnp.any(later, axis=1), blk_expert[jnp.argmax(later, axis=1)], -1).astype(jnp.int32)
    yb = _experts(blk_expert, next_expert, n_used, xsorted, w_expert_in[0], b_glu, b_lin, w_expert_out[0], b_eo)
    y_p = _combine(tabs_p, gate_p, lpos_p, x1_p, g2, b2, yb, tm=tm_p, dn_alpha=dn_alpha)
    y_s = _combine(tabs_s, gate_s, lpos_s, x1_s, g2, b2, yb, tm=tm_s, dn_alpha=dn_alpha)

    k_prompt = jnp.transpose(kt_p.reshape(bp, N_HEADS, 2, HEAD_DIM, sp), (0, 4, 1, 2, 3))
    return (
        y_p.reshape(bp, sp, D_MODEL),
        y_s.reshape(bs, ss, D_MODEL),
        k_prompt[None],
        v_p.reshape(1, bp, sp, N_HEADS, V_DIM),
        pnew_p[:, 1:].reshape(1, bp, POOL_HIST, POOL_DIM),
        k_s.reshape(1, bs, ss, N_HEADS, 2, HEAD_DIM),
        v_s.reshape(1, bs, ss, N_HEADS, V_DIM),
        pnew_s[:, 1:].reshape(1, bs, POOL_HIST, POOL_DIM),
    )
```

```python
import functools
import math

import jax
import jax.numpy as jnp
from jax import lax
from jax.experimental import pallas as pl
from jax.experimental.pallas import tpu as pltpu

D_MODEL = 1024
CHUNK = 64
POOL_WINDOWS = (2, 4, 8, 16)
POOL_GROUP_DIM = 128
POOL_DIM = len(POOL_WINDOWS) * POOL_GROUP_DIM
POOL_HIST = max(POOL_WINDOWS) - 1
HIST_ROWS = POOL_HIST + 1
N_HEADS = 8
HEAD_DIM = 64
HALF_DIM = HEAD_DIM // 2
V_DIM = 2 * HEAD_DIM
QK_DIM = N_HEADS * 2 * HEAD_DIM
ATTN_V_WIDTH = N_HEADS * V_DIM
ATTN_SCALE = HEAD_DIM ** -0.5
LOG2_E = math.log2(math.e)
ROPE_THETA = 10000.0
SUBLN_EPS = 1e-5
N_EXPERTS = 32
TOP_K = 4
D_EXPERT = 1024
SWIGLU_LIMIT = 7.0
SWIGLU_ALPHA = 1.702
LN_EPS = 1e-5
NEG_INF = -1e30
LANES = 128
MXU_DIM = 256

F32 = jnp.float32
BF16 = jnp.bfloat16

VMEM_LIMIT = 56 * 1024 * 1024
EXPERT_BLOCK = 256
SUBLANES = 8
RUN_ALIGN = SUBLANES
ROUTE_TILE = 256
DENSE_TILE = 512
ATTN_Q_TILE = 256
RUN_BITS = tuple(range(3, 9))


def _dot(a, b):
    return jnp.dot(a, b, preferred_element_type=F32)


def _dot_nt(a, b):
    return lax.dot_general(a, b, (((1,), (1,)), ((), ())), preferred_element_type=F32)


def _params(semantics):
    return pltpu.CompilerParams(dimension_semantics=semantics, vmem_limit_bytes=VMEM_LIMIT)


def _pool_branch(u, icnt_ref, hist_ref, wmix_ref, pscale_ref, py_ref, pnew_ref, ext_ref, *, bb, tm):
    @pl.when(pl.program_id(1) == 0)
    def _():
        ext_ref[:, 0:HIST_ROWS, :] = hist_ref[...]

    for b in range(bb):
        ext_ref[b, HIST_ROWS:HIST_ROWS + tm, :] = u[b * tm:(b + 1) * tm]
    for b in range(bb):
        for g, w in enumerate(POOL_WINDOWS):
            cols = slice(g * POOL_GROUP_DIM, (g + 1) * POOL_GROUP_DIM)
            cur = ext_ref[b, HIST_ROWS:HIST_ROWS + tm, cols]
            acc = cur
            for j in range(1, w):
                acc = acc + ext_ref[b, HIST_ROWS - j:HIST_ROWS - j + tm, cols]
            d = acc * icnt_ref[:, cols] - cur
            y = _dot(d.astype(BF16), wmix_ref[g]) * pscale_ref[:, cols]
            py_ref[b * tm:(b + 1) * tm, cols] = y.astype(BF16)
    tail = ext_ref[:, tm:tm + HIST_ROWS, :]
    pnew_ref[...] = tail
    ext_ref[:, 0:HIST_ROWS, :] = tail


def _rope_rows(z, cos, sin):
    lane = lax.broadcasted_iota(jnp.int32, z.shape, 1)
    first_half = (lane % HEAD_DIM) < HALF_DIM
    partner = jnp.where(first_half, pltpu.roll(z, LANES - HALF_DIM, 1), pltpu.roll(z, HALF_DIM, 1))
    return z * cos + partner * sin


def _inproj_prompt_kernel(x_ref, w_ref, wkt_ref, cos_ref, sin_ref, cost_ref, sint_ref, icnt_ref, hist_ref,
                          wmix_ref, pscale_ref, q_ref, kt_ref, ktb_ref, v_ref, vb_ref, py_ref, pnew_ref,
                          ext_ref, *, tm):
    x = x_ref[...].astype(BF16)
    _pool_branch(_dot(x, w_ref[:, 0:POOL_DIM]), icnt_ref, hist_ref, wmix_ref, pscale_ref, py_ref, pnew_ref,
                 ext_ref, bb=1, tm=tm)

    cos, sin = cos_ref[...], sin_ref[...]
    hq = _dot(x, w_ref[:, POOL_DIM:POOL_DIM + QK_DIM])
    for h in range(N_HEADS):
        sl = slice(h * V_DIM, (h + 1) * V_DIM)
        q_ref[:, sl] = (_rope_rows(hq[:, sl], cos, sin) * (ATTN_SCALE * LOG2_E)).astype(BF16)

    hkt = _dot_nt(wkt_ref[...], x)
    cost, sint = cost_ref[...], sint_ref[...]
    for hc in range(2 * N_HEADS):
        r0 = hc * HEAD_DIM
        x1 = hkt[r0:r0 + HALF_DIM]
        x2 = hkt[r0 + HALF_DIM:r0 + HEAD_DIM]
        o1 = x1 * cost - x2 * sint
        o2 = x2 * cost + x1 * sint
        kt_ref[r0:r0 + HALF_DIM, :] = o1
        kt_ref[r0 + HALF_DIM:r0 + HEAD_DIM, :] = o2
        ktb_ref[r0:r0 + HALF_DIM, :] = o1.astype(BF16)
        ktb_ref[r0 + HALF_DIM:r0 + HEAD_DIM, :] = o2.astype(BF16)

    hv = _dot(x, w_ref[:, POOL_DIM + QK_DIM:POOL_DIM + QK_DIM + ATTN_V_WIDTH])
    vb_ref[...] = hv.astype(BF16)
    for h in range(N_HEADS):
        v_ref[pl.ds(h, tm, stride=N_HEADS), :] = hv[:, h * V_DIM:(h + 1) * V_DIM]


def _inproj_prompt(x2d, w_pqv, wkt, cos, sin, cost, sint, icnt, hist, wmix, pscale, *, n_streams, seq, tm):
    n = n_streams * seq
    nt = seq // tm
    row_map = lambda b, t: (b * nt + t, 0)
    const2 = lambda b, t: (0, 0)
    out_shape = (
        jax.ShapeDtypeStruct((n, QK_DIM), BF16),
        jax.ShapeDtypeStruct((n_streams, QK_DIM, seq), F32),
        jax.ShapeDtypeStruct((n_streams, QK_DIM, seq), BF16),
        jax.ShapeDtypeStruct((n_streams, seq * N_HEADS, V_DIM), F32),
        jax.ShapeDtypeStruct((n, ATTN_V_WIDTH), BF16),
        jax.ShapeDtypeStruct((n, POOL_DIM), BF16),
        jax.ShapeDtypeStruct((n_streams, HIST_ROWS, POOL_DIM), F32),
    )
    return pl.pallas_call(
        functools.partial(_inproj_prompt_kernel, tm=tm),
        out_shape=out_shape,
        grid=(n_streams, nt),
        in_specs=[
            pl.BlockSpec((tm, D_MODEL), row_map),
            pl.BlockSpec(w_pqv.shape, const2),
            pl.BlockSpec(wkt.shape, const2),
            pl.BlockSpec((tm, LANES), lambda b, t: (t, 0)),
            pl.BlockSpec((tm, LANES), lambda b, t: (t, 0)),
            pl.BlockSpec((HALF_DIM, tm), lambda b, t: (0, t)),
            pl.BlockSpec((HALF_DIM, tm), lambda b, t: (0, t)),
            pl.BlockSpec((tm, POOL_DIM), lambda b, t: (t, 0)),
            pl.BlockSpec((1, HIST_ROWS, POOL_DIM), lambda b, t: (b, 0, 0)),
            pl.BlockSpec((len(POOL_WINDOWS), POOL_GROUP_DIM, POOL_GROUP_DIM), lambda b, t: (0, 0, 0)),
            pl.BlockSpec((1, POOL_DIM), const2),
        ],
        out_specs=(
            pl.BlockSpec((tm, QK_DIM), row_map),
            pl.BlockSpec((None, QK_DIM, tm), lambda b, t: (b, 0, t)),
            pl.BlockSpec((None, QK_DIM, tm), lambda b, t: (b, 0, t)),
            pl.BlockSpec((None, tm * N_HEADS, V_DIM), lambda b, t: (b, t, 0)),
            pl.BlockSpec((tm, ATTN_V_WIDTH), row_map),
            pl.BlockSpec((tm, POOL_DIM), row_map),
            pl.BlockSpec((1, HIST_ROWS, POOL_DIM), lambda b, t: (b, 0, 0)),
        ),
        scratch_shapes=[pltpu.VMEM((1, HIST_ROWS + tm, POOL_DIM), F32)],
        compiler_params=_params(("arbitrary", "arbitrary")),
        name="inproj_prompt",
    )(x2d, w_pqv, wkt, cos, sin, cost, sint, icnt, hist, wmix, pscale)


def _inproj_sample_kernel(x_ref, w_ref, wk_ref, cos_ref, sin_ref, icnt_ref, hist_ref, wmix_ref, pscale_ref,
                          q_ref, k_ref, v_ref, py_ref, pnew_ref, ext_ref, *, bb, tm):
    x = x_ref[...].astype(BF16)
    _pool_branch(_dot(x, w_ref[:, 0:POOL_DIM]), icnt_ref, hist_ref, wmix_ref, pscale_ref, py_ref, pnew_ref,
                 ext_ref, bb=bb, tm=tm)
    cos = jnp.concatenate([cos_ref[...]] * bb, axis=0)
    sin = jnp.concatenate([sin_ref[...]] * bb, axis=0)
    hq = _dot(x, w_ref[:, POOL_DIM:POOL_DIM + QK_DIM])
    hk = _dot(x, wk_ref[...])
    for h in range(N_HEADS):
        sl = slice(h * V_DIM, (h + 1) * V_DIM)
        q_ref[:, sl] = (_rope_rows(hq[:, sl], cos, sin) * ATTN_SCALE).astype(BF16)
        k_ref[:, sl] = _rope_rows(hk[:, sl], cos, sin)
    v_ref[...] = _dot(x, w_ref[:, POOL_DIM + QK_DIM:POOL_DIM + QK_DIM + ATTN_V_WIDTH])


def _inproj_sample(x2d, w_pqv, wk, cos, sin, icnt, hist, wmix, pscale, *, n_streams, seq):
    n = n_streams * seq
    const2 = lambda i, t: (0, 0)
    const3 = lambda i, t: (0, 0, 0)
    out_shape = (
        jax.ShapeDtypeStruct((n, QK_DIM), BF16),
        jax.ShapeDtypeStruct((n, QK_DIM), F32),
        jax.ShapeDtypeStruct((n, ATTN_V_WIDTH), F32),
        jax.ShapeDtypeStruct((n, POOL_DIM), BF16),
        jax.ShapeDtypeStruct((n_streams, HIST_ROWS, POOL_DIM), F32),
    )
    return pl.pallas_call(
        functools.partial(_inproj_sample_kernel, bb=n_streams, tm=seq),
        out_shape=out_shape,
        grid=(1, 1),
        in_specs=[
            pl.BlockSpec((n, D_MODEL), const2),
            pl.BlockSpec(w_pqv.shape, const2),
            pl.BlockSpec(wk.shape, const2),
            pl.BlockSpec((seq, LANES), const2),
            pl.BlockSpec((seq, LANES), const2),
            pl.BlockSpec((seq, POOL_DIM), const2),
            pl.BlockSpec((n_streams, HIST_ROWS, POOL_DIM), const3),
            pl.BlockSpec((len(POOL_WINDOWS), POOL_GROUP_DIM, POOL_GROUP_DIM), const3),
            pl.BlockSpec((1, POOL_DIM), const2),
        ],
        out_specs=(
            pl.BlockSpec((n, QK_DIM), const2),
            pl.BlockSpec((n, QK_DIM), const2),
            pl.BlockSpec((n, ATTN_V_WIDTH), const2),
            pl.BlockSpec((n, POOL_DIM), const2),
            pl.BlockSpec((n_streams, HIST_ROWS, POOL_DIM), const3),
        ),
        scratch_shapes=[pltpu.VMEM((n_streams, HIST_ROWS + seq, POOL_DIM), F32)],
        compiler_params=_params(("arbitrary", "arbitrary")),
        name="inproj_sample",
    )(x2d, w_pqv, wk, cos, sin, icnt, hist, wmix, pscale)


def _lambda_value(lam_ref, lam_init):
    lv = lam_ref[...]
    s1 = jnp.sum(lv[0:1] * lv[1:2], axis=1, keepdims=True)
    s2 = jnp.sum(lv[2:3] * lv[3:4], axis=1, keepdims=True)
    return jnp.exp(s1) - jnp.exp(s2) + lam_init


def _head_norm(o, g, lam_init):
    ms = jnp.mean(o * o, axis=-1, keepdims=True)
    return o * lax.rsqrt(ms + SUBLN_EPS) * g * (1.0 - lam_init)


def _attn_prompt_kernel(lam_ref, g_ref, q_ref, kt_ref, v_ref, o_ref, vext, *, seq, tq, lam_init):
    lam = _lambda_value(lam_ref, lam_init)
    g = g_ref[...]
    r = lax.broadcasted_iota(jnp.int32, (tq, tq), 0)
    c = lax.broadcasted_iota(jnp.int32, (tq, tq), 1)
    diag_visible = (c // CHUNK) <= (r // CHUNK)
    lane = lax.broadcasted_iota(jnp.int32, (tq, V_DIM), 1)
    vext[:, 0:V_DIM] = v_ref[...]
    vlane = lax.broadcasted_iota(jnp.int32, (seq, V_DIM), 1)
    vext[:, V_DIM:] = jnp.where(vlane == 0, 1.0, 0.0).astype(vext.dtype)

    def scores(i):
        lo = i * tq
        q = q_ref[lo:lo + tq, :]
        zero = jnp.zeros_like(q)
        qc = (jnp.where(lane < HEAD_DIM, q, zero), jnp.where(lane >= HEAD_DIM, q, zero))
        out = []
        for k in range(2):
            sd = jnp.where(diag_visible, _dot(qc[k], kt_ref[:, lo:lo + tq]), NEG_INF)
            m = jnp.max(sd, axis=1, keepdims=True)
            sp = None
            if i > 0:
                sp = _dot(qc[k], kt_ref[:, 0:lo])
                m = jnp.maximum(m, jnp.max(sp, axis=1, keepdims=True))
            out.append((sd, sp, m))
        return out

    def finish(i, parts):
        lo = i * tq
        normed = []
        for sd, sp, m in parts:
            acc = _dot(jnp.exp2(sd - m).astype(BF16), vext[lo:lo + tq, :])
            if sp is not None:
                acc = acc + _dot(jnp.exp2(sp - m).astype(BF16), vext[0:lo, :])
            normed.append(acc[:, 0:V_DIM] / acc[:, V_DIM:V_DIM + 1])
        o = normed[0] - lam * normed[1]
        o_ref[lo:lo + tq, :] = _head_norm(o, g, lam_init).astype(o_ref.dtype)

    nq = seq // tq
    pending = scores(0)
    for i in range(nq):
        upcoming = scores(i + 1) if i + 1 < nq else None
        finish(i, pending)
        pending = upcoming


def _attn_prompt(lam_vecs, norm_g, q, ktb, vb, *, n_streams, seq, tq, lam_init):
    return pl.pallas_call(
        functools.partial(_attn_prompt_kernel, seq=seq, tq=tq, lam_init=lam_init),
        out_shape=jax.ShapeDtypeStruct((n_streams * seq, ATTN_V_WIDTH), BF16),
        grid=(n_streams, N_HEADS),
        in_specs=[
            pl.BlockSpec((4, HEAD_DIM), lambda b, h: (0, 0)),
            pl.BlockSpec((1, V_DIM), lambda b, h: (0, 0)),
            pl.BlockSpec((seq, V_DIM), lambda b, h: (b, h)),
            pl.BlockSpec((None, V_DIM, seq), lambda b, h: (b, h, 0)),
            pl.BlockSpec((seq, V_DIM), lambda b, h: (b, h)),
        ],
        out_specs=pl.BlockSpec((seq, V_DIM), lambda b, h: (b, h)),
        scratch_shapes=[pltpu.VMEM((seq, MXU_DIM), BF16)],
        compiler_params=_params(("arbitrary", "arbitrary")),
        name="attn_prompt",
    )(lam_vecs, norm_g, q, ktb, vb)


def _attn_sample_kernel(lam_ref, g_ref, q_ref, kc_ref, vc_ref, kn_ref, vn_ref, o_ref,
                        s_scr, w_scr, wn_scr, m_scr, acc_scr, qbd_scr, *, nk, tn, past, lam_init):
    j = pl.program_id(1)
    half = N_HEADS * tn

    @pl.when(j == 0)
    def _():
        q = q_ref[...]
        qt = jnp.concatenate([q] * (2 * N_HEADS), axis=0)
        r = lax.broadcasted_iota(jnp.int32, qt.shape, 0)
        l = lax.broadcasted_iota(jnp.int32, qt.shape, 1)
        keep = ((r // half) == ((l % V_DIM) // HEAD_DIM)) & (((r % half) // tn) == (l // V_DIM))
        qbd_scr[...] = jnp.where(keep, qt, jnp.zeros_like(qt))
        m_scr[...] = jnp.full(m_scr.shape, NEG_INF, F32)
        acc_scr[...] = jnp.zeros(acc_scr.shape, F32)

    @pl.when(j < nk)
    def _():
        s = _dot(qbd_scr[...], kc_ref[...].astype(BF16))
        s_scr[j] = s
        m_scr[...] = jnp.maximum(m_scr[...], jnp.max(s, axis=1, keepdims=True))

    @pl.when(j == nk - 1)
    def _():
        lam = _lambda_value(lam_ref, lam_init)
        sn = _dot_nt(qbd_scr[...], kn_ref[...].astype(BF16))
        qpos = past + (lax.broadcasted_iota(jnp.int32, sn.shape, 0) % tn)
        kpos = past + lax.broadcasted_iota(jnp.int32, sn.shape, 1)
        sn = jnp.where((kpos // CHUNK) <= (qpos // CHUNK), sn, NEG_INF)
        m = jnp.maximum(m_scr[...], jnp.max(sn, axis=1, keepdims=True))
        pn = jnp.exp(sn - m)
        l = jnp.sum(pn, axis=1, keepdims=True)
        for c in range(nk):
            p = jnp.exp(s_scr[c] - m)
            s_scr[c] = p
            l = l + jnp.sum(p, axis=1, keepdims=True)
        r0 = 1.0 / l[:half]
        r1 = lam / l[half:]
        wn_scr[...] = pn[:half] * r0 - pn[half:] * r1
        for c in range(nk):
            p = s_scr[c]
            w_scr[c] = (p[:half] * r0 - p[half:] * r1).astype(BF16)

    def v_rows(ref):
        tk = ref.shape[0] // N_HEADS
        return jnp.concatenate([ref[pl.ds(h, tk, stride=N_HEADS), :] for h in range(N_HEADS)],
                               axis=1).astype(BF16)

    @pl.when(j >= nk)
    def _():
        acc_scr[...] += _dot(w_scr[j - nk], v_rows(vc_ref))

    @pl.when(j == 2 * nk - 1)
    def _():
        acc = acc_scr[...] + _dot(wn_scr[...].astype(BF16), vn_ref[...].astype(BF16))
        g = g_ref[...]
        for h in range(N_HEADS):
            o = acc[h * tn:(h + 1) * tn, h * V_DIM:(h + 1) * V_DIM]
            o_ref[:, h * V_DIM:(h + 1) * V_DIM] = _head_norm(o, g, lam_init).astype(o_ref.dtype)


def _attn_sample(lam_vecs, norm_g, q, kct, vc, kn, vn, *, n_streams, tn, past, tk, lam_init):
    nk = past // tk
    rows = 2 * N_HEADS * tn
    half = N_HEADS * tn
    return pl.pallas_call(
        functools.partial(_attn_sample_kernel, nk=nk, tn=tn, past=past, lam_init=lam_init),
        out_shape=jax.ShapeDtypeStruct((n_streams * tn, ATTN_V_WIDTH), BF16),
        grid=(n_streams, 2 * nk),
        in_specs=[
            pl.BlockSpec((4, HEAD_DIM), lambda b, j: (0, 0)),
            pl.BlockSpec((1, V_DIM), lambda b, j: (0, 0)),
            pl.BlockSpec((tn, QK_DIM), lambda b, j: (b, 0)),
            pl.BlockSpec((None, QK_DIM, tk), lambda b, j: (b, 0, jnp.minimum(j, nk - 1))),
            pl.BlockSpec((None, tk * N_HEADS, V_DIM), lambda b, j: (b, jnp.maximum(j - nk, 0), 0)),
            pl.BlockSpec((tn, QK_DIM), lambda b, j: (b, 0)),
            pl.BlockSpec((tn, ATTN_V_WIDTH), lambda b, j: (b, 0)),
        ],
        out_specs=pl.BlockSpec((tn, ATTN_V_WIDTH), lambda b, j: (b, 0)),
        scratch_shapes=[
            pltpu.VMEM((nk, rows, tk), F32),
            pltpu.VMEM((nk, half, tk), BF16),
            pltpu.VMEM((half, tn), F32),
            pltpu.VMEM((rows, 1), F32),
            pltpu.VMEM((half, ATTN_V_WIDTH), F32),
            pltpu.VMEM((rows, QK_DIM), BF16),
        ],
        compiler_params=_params(("arbitrary", "arbitrary")),
        name="attn_sample",
    )(lam_vecs, norm_g, q, kct, vc, kn, vn)


def _layer_norm(z, g, b):
    mu = jnp.mean(z, axis=-1, keepdims=True)
    zc = z - mu
    var = jnp.mean(zc * zc, axis=-1, keepdims=True)
    return zc * lax.rsqrt(var + LN_EPS) * g + b


def _postmix_kernel(x_ref, xprev_ref, py_ref, ay_ref, wg_ref, wpo_ref, wao_ref, wout_ref, g1_ref, b1_ref,
                    wr_ref, br_ref, x1_ref, gate_ref, lpos_ref, cnt_ref, mix_scr, *, tm, rt, dn_alpha):
    @pl.when(pl.program_id(0) == 0)
    def _():
        mix_scr[...] = jnp.zeros(mix_scr.shape, mix_scr.dtype)

    mixed_prev = mix_scr[...]

    mo = _dot(mixed_prev, wout_ref[...])
    x1 = _layer_norm(dn_alpha * xprev_ref[...] + mo, g1_ref[...], b1_ref[...])
    x1_ref[...] = x1
    logits = _dot(x1.astype(BF16), wr_ref[...]) + br_ref[...]

    a = _dot(py_ref[...], wpo_ref[...])
    b = _dot(ay_ref[...], wao_ref[...])
    gates = jax.nn.sigmoid(_dot(x_ref[...].astype(BF16), wg_ref[...]))
    mix_scr[...] = (gates[:, :D_MODEL] * a + gates[:, D_MODEL:] * b).astype(BF16)

    for sub in range(tm // rt):
        rows = slice(sub * rt, (sub + 1) * rt)
        gate_ref[rows, :], lpos_ref[rows, :], cnt_ref[sub] = _route_tile(logits[rows], rt)


def _route_tile(logits, tm):
    lane = lax.broadcasted_iota(jnp.int32, logits.shape, 1)
    work = logits
    vals, idxs = [], []
    for _ in range(TOP_K):
        mx = jnp.max(work, axis=1, keepdims=True)
        ix = jnp.min(jnp.where(work == mx, lane, N_EXPERTS), axis=1, keepdims=True)
        vals.append(mx)
        idxs.append(ix)
        work = jnp.where(lane == ix, -jnp.inf, work)
    exps = [jnp.exp(v - vals[0]) for v in vals]
    denom = exps[0] + exps[1] + exps[2] + exps[3]

    onehot = jnp.zeros(logits.shape, F32)
    for ix in idxs:
        onehot = onehot + (lane == ix).astype(F32)
    r = lax.broadcasted_iota(jnp.int32, (tm, tm), 0)
    c = lax.broadcasted_iota(jnp.int32, (tm, tm), 1)
    tri = jnp.where(c < r, 1.0, 0.0).astype(BF16)
    earlier = _dot(tri, onehot.astype(BF16))
    cnt = jnp.sum(onehot, axis=0, keepdims=True)
    units = jnp.floor((cnt + (RUN_ALIGN - 1.0)) * (1.0 / RUN_ALIGN))
    er = lax.broadcasted_iota(jnp.int32, (N_EXPERTS, N_EXPERTS), 0)
    ec = lax.broadcasted_iota(jnp.int32, (N_EXPERTS, N_EXPERTS), 1)
    upper = jnp.where(er < ec, 1.0, 0.0).astype(BF16)
    run_off = _dot(jnp.broadcast_to(units, (SUBLANES, N_EXPERTS)).astype(BF16), upper)[0:1] * float(RUN_ALIGN)
    pos = earlier + run_off

    lane_out = lax.broadcasted_iota(jnp.int32, (tm, LANES), 1)
    lpos_out = jnp.zeros((tm, LANES), jnp.int32)
    gate_out = jnp.zeros((tm, LANES), F32)
    for k in range(TOP_K):
        lpos_k = jnp.sum(jnp.where(lane == idxs[k], pos, 0.0), axis=1, keepdims=True).astype(jnp.int32)
        lpos_out = jnp.where(lane_out == k, lpos_k, lpos_out)
        gate_out = jnp.where(lane_out == k, exps[k] / denom, gate_out)
    return gate_out, lpos_out, units * float(RUN_ALIGN)


def _postmix(x2d, py, ay, wg, wpo, wao, wout, g1, b1, wr, br, *, tm, rt, dn_alpha):
    n = x2d.shape[0]
    nt = n // tm
    row = lambda i: (jnp.minimum(i, nt - 1), 0)
    prev = lambda i: (jnp.maximum(i - 1, 0), 0)
    const = lambda i: (0, 0)
    out_shape = (
        jax.ShapeDtypeStruct((n, D_MODEL), F32),
        jax.ShapeDtypeStruct((n, LANES), F32),
        jax.ShapeDtypeStruct((n, LANES), jnp.int32),
        jax.ShapeDtypeStruct((n // rt, 1, N_EXPERTS), F32),
    )
    return pl.pallas_call(
        functools.partial(_postmix_kernel, tm=tm, rt=rt, dn_alpha=dn_alpha),
        out_shape=out_shape,
        grid=(nt + 1,),
        in_specs=[
            pl.BlockSpec((tm, D_MODEL), row),
            pl.BlockSpec((tm, D_MODEL), prev),
            pl.BlockSpec((tm, POOL_DIM), row),
            pl.BlockSpec((tm, ATTN_V_WIDTH), row),
            pl.BlockSpec(wg.shape, const),
            pl.BlockSpec(wpo.shape, const),
            pl.BlockSpec(wao.shape, const),
            pl.BlockSpec(wout.shape, const),
            pl.BlockSpec((1, D_MODEL), const),
            pl.BlockSpec((1, D_MODEL), const),
            pl.BlockSpec(wr.shape, const),
            pl.BlockSpec((1, N_EXPERTS), const),
        ],
        out_specs=(
            pl.BlockSpec((tm, D_MODEL), prev),
            pl.BlockSpec((tm, LANES), prev),
            pl.BlockSpec((tm, LANES), prev),
            pl.BlockSpec((tm // rt, 1, N_EXPERTS), lambda i: (jnp.maximum(i - 1, 0), 0, 0)),
        ),
        scratch_shapes=[pltpu.VMEM((tm, D_MODEL), BF16)],
        compiler_params=_params(("arbitrary",)),
        name="postmix",
    )(x2d, x2d, py, ay, wg, wpo, wao, wout, g1, b1, wr, br)


def _local_rows(tm):
    return TOP_K * tm + N_EXPERTS * RUN_ALIGN


def _for_each_run_chunk(tile, tab_refs, local_buf, sorted_ref, sem, to_sorted, fn):
    gs_ref, off_ref, cnt_ref = tab_refs[:3]

    def per_expert_pair(pair, carry):
        for de in range(2):
            t = tile * N_EXPERTS + 2 * pair + de
            cnt, off, gs = cnt_ref[t], off_ref[t], gs_ref[t]
            for b in RUN_BITS:
                size = 1 << b

                @pl.when((cnt & size) != 0)
                def _():
                    lower = cnt & (size - 1)
                    loc = local_buf.at[pl.ds(pl.multiple_of(off + lower, RUN_ALIGN), size)]
                    srt = sorted_ref.at[pl.ds(pl.multiple_of(gs + lower, RUN_ALIGN), size)]
                    fn(pltpu.make_async_copy(loc, srt, sem) if to_sorted
                       else pltpu.make_async_copy(srt, loc, sem), b + de)
        return carry

    lax.fori_loop(0, N_EXPERTS // 2, per_expert_pair, 0)


def _start_alternating(copy, k):
    copy.start(priority=k % 2)


def _wait_run_rows(total, local_buf, sorted_ref, sem, to_sorted):
    rows = local_buf.shape[0]
    for b in range(RUN_BITS[0], rows.bit_length()):
        size = 1 << b

        @pl.when((total & size) != 0)
        def _():
            loc, srt = local_buf.at[pl.ds(0, size)], sorted_ref.at[pl.ds(0, size)]
            (pltpu.make_async_copy(loc, srt, sem) if to_sorted else pltpu.make_async_copy(srt, loc, sem)).wait()


def _pack_bf16_pairs(a):
    n = a.shape[1] // 2
    bits = lax.bitcast_convert_type(a, jnp.int32)
    return bits[:, :n] | lax.shift_right_logical(bits[:, n:], 16)


def _unpack_bf16_pairs(w):
    hi = lax.bitcast_convert_type(w & jnp.int32(-65536), F32)
    lo = lax.bitcast_convert_type(lax.shift_left(w, 16), F32)
    return jnp.concatenate([hi, lo], axis=1).astype(BF16)


def _dispatch_tile(tab_refs, lpos_ref, x_ref, xs_ref, xloc, sem, tm):
    lpos = lpos_ref[...]
    col = lax.broadcasted_iota(jnp.int32, (tm, _local_rows(tm)), 1)
    hit = col == lpos[:, 0:1]
    for k in range(1, TOP_K):
        hit = jnp.logical_or(hit, col == lpos[:, k:k + 1])
    perm_t = jnp.where(hit, 1.0, 0.0).astype(BF16)
    tile = pl.program_id(0)
    slot = tile % 2
    sorted_rows = lax.dot_general(perm_t, x_ref[...].astype(BF16), (((0,), (0,)), ((), ())),
                                  preferred_element_type=F32)
    xloc[slot] = _pack_bf16_pairs(sorted_rows)

    def copies(t, s, fn):
        _for_each_run_chunk(t, tab_refs, xloc.at[s], xs_ref, sem.at[s], True, fn)

    copies(tile, slot, _start_alternating)
    tot_ref = tab_refs[3]

    @pl.when(tile > 0)
    def _():
        _wait_run_rows(tot_ref[tile - 1], xloc.at[1 - slot], xs_ref, sem.at[1 - slot], True)

    @pl.when(tile == pl.num_programs(0) - 1)
    def _():
        _wait_run_rows(tot_ref[tile], xloc.at[slot], xs_ref, sem.at[slot], True)


def _dispatch_first_kernel(tail_ref, nu_ref, gs_ref, off_ref, cnt_ref, tot_ref, lpos_ref, x_ref, xs_ref,
                           xloc, zbuf, sem, zsem, *, tm, n_blocks):
    @pl.when(pl.program_id(0) == 0)
    def _():
        zbuf[...] = jnp.zeros(zbuf.shape, zbuf.dtype)

        def zero_copy(row):
            row = pl.multiple_of(row, EXPERT_BLOCK)
            return pltpu.make_async_copy(zbuf, xs_ref.at[pl.ds(row, EXPERT_BLOCK)], zsem)

        def over_blocks(fn):
            for e in range(N_EXPERTS):
                @pl.when(tail_ref[e] >= 0)
                def _():
                    fn(zero_copy(tail_ref[e]))
            lax.fori_loop(nu_ref[0], n_blocks, lambda b, c: (fn(zero_copy(b * EXPERT_BLOCK)), c)[1], 0)

        over_blocks(lambda cp: cp.start())
        over_blocks(lambda cp: cp.wait())

    _dispatch_tile((gs_ref, off_ref, cnt_ref, tot_ref), lpos_ref, x_ref, xs_ref, xloc, sem, tm)


def _dispatch_next_kernel(gs_ref, off_ref, cnt_ref, tot_ref, lpos_ref, x_ref, xs_in_ref, xs_ref, xloc, sem,
                          *, tm):
    del xs_in_ref
    _dispatch_tile((gs_ref, off_ref, cnt_ref, tot_ref), lpos_ref, x_ref, xs_ref, xloc, sem, tm)


def _dispatch_first(tail, n_used, tabs, lpos, x1, *, tm, n_blocks):
    n = x1.shape[0]
    grid_spec = pltpu.PrefetchScalarGridSpec(
        num_scalar_prefetch=6,
        grid=(n // tm,),
        in_specs=[pl.BlockSpec((tm, LANES), lambda i, *_: (i, 0)),
                  pl.BlockSpec((tm, D_MODEL), lambda i, *_: (i, 0))],
        out_specs=pl.BlockSpec(memory_space=pl.ANY),
        scratch_shapes=[pltpu.VMEM((2, _local_rows(tm), D_MODEL // 2), jnp.int32),
                        pltpu.VMEM((EXPERT_BLOCK, D_MODEL // 2), jnp.int32),
                        pltpu.SemaphoreType.DMA((2,)), pltpu.SemaphoreType.DMA],
    )
    return pl.pallas_call(
        functools.partial(_dispatch_first_kernel, tm=tm, n_blocks=n_blocks),
        out_shape=jax.ShapeDtypeStruct((n_blocks * EXPERT_BLOCK, D_MODEL // 2), jnp.int32),
        grid_spec=grid_spec,
        compiler_params=_params(("arbitrary",)),
        name="dispatch_first",
    )(tail, n_used, *tabs, lpos, x1)


def _dispatch_next(tabs, lpos, x1, xs, *, tm):
    n = x1.shape[0]
    grid_spec = pltpu.PrefetchScalarGridSpec(
        num_scalar_prefetch=len(tabs),
        grid=(n // tm,),
        in_specs=[pl.BlockSpec((tm, LANES), lambda i, *_: (i, 0)),
                  pl.BlockSpec((tm, D_MODEL), lambda i, *_: (i, 0)),
                  pl.BlockSpec(memory_space=pl.ANY)],
        out_specs=pl.BlockSpec(memory_space=pl.ANY),
        scratch_shapes=[pltpu.VMEM((2, _local_rows(tm), D_MODEL // 2), jnp.int32), pltpu.SemaphoreType.DMA((2,))],
    )
    return pl.pallas_call(
        functools.partial(_dispatch_next_kernel, tm=tm),
        out_shape=jax.ShapeDtypeStruct(xs.shape, xs.dtype),
        grid_spec=grid_spec,
        input_output_aliases={6: 0},
        compiler_params=_params(("arbitrary",)),
        name="dispatch_next",
    )(*tabs, lpos, x1, xs)


def _experts_kernel(be_ref, nxt_ref, nu_ref, xs_ref, bg_ref, bl_ref, bo_ref, win_hbm, wo_hbm, y_ref,
                    win_buf, wo_buf, wg_scr, wl_scr, wo_scr, sem_in, sem_out):
    i = pl.program_id(0)
    used = i < nu_ref[0]
    first_of_expert = jnp.logical_or(i == 0, be_ref[i] != be_ref[jnp.maximum(i - 1, 0)])

    def weight_copies(e):
        return (pltpu.make_async_copy(win_hbm.at[e], win_buf, sem_in),
                pltpu.make_async_copy(wo_hbm.at[e], wo_buf, sem_out))

    @pl.when(jnp.logical_not(used))
    def _():
        y_ref[...] = jnp.zeros(y_ref.shape, y_ref.dtype)

    @pl.when(i == 0)
    def _():
        for cp in weight_copies(be_ref[0]):
            cp.start()

    @pl.when(jnp.logical_and(used, first_of_expert))
    def _():
        for cp in weight_copies(be_ref[i]):
            cp.wait()
        r = lax.broadcasted_iota(jnp.int32, (MXU_DIM, MXU_DIM), 0)
        c = lax.broadcasted_iota(jnp.int32, (MXU_DIM, MXU_DIM), 1)
        src = jnp.where(c < LANES, 2 * c, 2 * (c - LANES) + 1)
        sel = jnp.where(r == src, 1.0, 0.0).astype(BF16)
        for gq in range(2 * D_EXPERT // MXU_DIM):
            blk = win_buf[:, gq * MXU_DIM:(gq + 1) * MXU_DIM].astype(BF16)
            d = _dot(blk, sel)
            wg_scr[:, gq * LANES:(gq + 1) * LANES] = d[:, :LANES].astype(BF16)
            wl_scr[:, gq * LANES:(gq + 1) * LANES] = d[:, LANES:].astype(BF16)
        wo_scr[...] = wo_buf[...].astype(BF16)

        @pl.when(nxt_ref[i] >= 0)
        def _():
            for cp in weight_copies(nxt_ref[i]):
                cp.start(priority=1)

    @pl.when(used)
    def _():
        xb = _unpack_bf16_pairs(xs_ref[...])
        glu = jnp.minimum(_dot(xb, wg_scr[...]) + bg_ref[...], SWIGLU_LIMIT)
        lin = jnp.clip(_dot(xb, wl_scr[...]) + bl_ref[...], -SWIGLU_LIMIT, SWIGLU_LIMIT)
        act = glu * jax.nn.sigmoid(SWIGLU_ALPHA * glu) * (lin + 1.0)
        y_ref[...] = _dot(act.astype(BF16), wo_scr[...]) + bo_ref[...]


def _experts(blk_expert, next_expert, n_used, xs, w_in, b_glu, b_lin, w_out, b_out):
    rows = xs.shape[0]
    n_blocks = rows // EXPERT_BLOCK
    wsel = lambda i, be, nxt, nu: (be[i], 0, 0)
    grid_spec = pltpu.PrefetchScalarGridSpec(
        num_scalar_prefetch=3,
        grid=(n_blocks,),
        in_specs=[
            pl.BlockSpec((EXPERT_BLOCK, D_MODEL // 2), lambda i, be, nxt, nu: (jnp.minimum(i, nu[0] - 1), 0)),
            pl.BlockSpec((None, 1, D_EXPERT), wsel),
            pl.BlockSpec((None, 1, D_EXPERT), wsel),
            pl.BlockSpec((None, 1, D_MODEL), wsel),
            pl.BlockSpec(memory_space=pl.ANY),
            pl.BlockSpec(memory_space=pl.ANY),
        ],
        out_specs=pl.BlockSpec((EXPERT_BLOCK, D_MODEL), lambda i, be, nxt, nu: (i, 0)),
        scratch_shapes=[pltpu.VMEM((D_MODEL, 2 * D_EXPERT), F32),
                        pltpu.VMEM((D_EXPERT, D_MODEL), F32),
                        pltpu.VMEM((D_MODEL, D_EXPERT), BF16),
                        pltpu.VMEM((D_MODEL, D_EXPERT), BF16),
                        pltpu.VMEM((D_EXPERT, D_MODEL), BF16),
                        pltpu.SemaphoreType.DMA, pltpu.SemaphoreType.DMA],
    )
    return pl.pallas_call(
        _experts_kernel,
        out_shape=jax.ShapeDtypeStruct((rows, D_MODEL), F32),
        grid_spec=grid_spec,
        compiler_params=_params(("arbitrary",)),
        name="experts",
    )(blk_expert, next_expert, n_used, xs, b_glu, b_lin, b_out, w_in, w_out)


def _split_bf16(a):
    hi = a.astype(BF16)
    return hi, (a - hi.astype(F32)).astype(BF16)


def _combine_kernel(gs_ref, off_ref, cnt_ref, tot_ref, gate_ref, lpos_ref, x1_ref, g2_ref, b2_ref, yb_ref, o_ref,
                    yloc, sem, *, tm, dn_alpha):
    tile = pl.program_id(0)
    slot = tile % 2
    tabs = (gs_ref, off_ref, cnt_ref)

    def copies(t, s, fn):
        _for_each_run_chunk(t, tabs, yloc.at[s], yb_ref, sem.at[s], False, fn)

    @pl.when(tile == 0)
    def _():
        yloc[...] = jnp.zeros(yloc.shape, yloc.dtype)
        copies(tile, slot, _start_alternating)

    @pl.when(tile + 1 < pl.num_programs(0))
    def _():
        copies(tile + 1, 1 - slot, _start_alternating)

    _wait_run_rows(tot_ref[tile], yloc.at[slot], yb_ref, sem.at[slot], False)

    gate, lpos = gate_ref[...], lpos_ref[...]
    col = lax.broadcasted_iota(jnp.int32, (tm, _local_rows(tm)), 1)
    weights = jnp.zeros(col.shape, F32)
    for k in range(TOP_K):
        weights = jnp.where(col == lpos[:, k:k + 1], gate[:, k:k + 1], weights)
    w_hi, w_lo = _split_bf16(weights)
    y_hi, y_lo = _split_bf16(yloc[slot])
    y = _dot(w_hi, y_hi) + (_dot(w_hi, y_lo) + _dot(w_lo, y_hi))
    o_ref[...] = _layer_norm(dn_alpha * x1_ref[...] + y, g2_ref[...], b2_ref[...])


def _combine(tabs, gate, lpos, x1, g2, b2, yb, *, tm, dn_alpha):
    n = x1.shape[0]
    grid_spec = pltpu.PrefetchScalarGridSpec(
        num_scalar_prefetch=len(tabs),
        grid=(n // tm,),
        in_specs=[
            pl.BlockSpec((tm, LANES), lambda i, *_: (i, 0)),
            pl.BlockSpec((tm, LANES), lambda i, *_: (i, 0)),
            pl.BlockSpec((tm, D_MODEL), lambda i, *_: (i, 0)),
            pl.BlockSpec((1, D_MODEL), lambda i, *_: (0, 0)),
            pl.BlockSpec((1, D_MODEL), lambda i, *_: (0, 0)),
            pl.BlockSpec(memory_space=pl.ANY),
        ],
        out_specs=pl.BlockSpec((tm, D_MODEL), lambda i, *_: (i, 0)),
        scratch_shapes=[pltpu.VMEM((2, _local_rows(tm), D_MODEL), F32), pltpu.SemaphoreType.DMA((2,))],
    )
    return pl.pallas_call(
        functools.partial(_combine_kernel, tm=tm, dn_alpha=dn_alpha),
        out_shape=jax.ShapeDtypeStruct((n, D_MODEL), F32),
        grid_spec=grid_spec,
        compiler_params=_params(("arbitrary",)),
        name="combine",
    )(*tabs, gate, lpos, x1, g2, b2, yb)


def _position_tables(pos0, seq):
    pos = pos0 + jnp.arange(seq, dtype=jnp.int32)
    inv = ROPE_THETA ** (-jnp.arange(HALF_DIM, dtype=F32) / HALF_DIM)
    ang = pos.astype(F32)[:, None] * inv[None, :]
    cos, sin = jnp.cos(ang), jnp.sin(ang)
    cos_rows = jnp.concatenate([cos, cos, cos, cos], axis=-1)
    sin_rows = jnp.concatenate([-sin, sin, -sin, sin], axis=-1)
    icnt = jnp.concatenate(
        [jnp.broadcast_to((1.0 / jnp.minimum(pos + 1, w).astype(F32))[:, None], (seq, POOL_GROUP_DIM))
         for w in POOL_WINDOWS], axis=-1)
    return cos_rows, sin_rows, cos.T, sin.T, icnt


def _tile(n, pref):
    t = min(n, pref)
    while n % t:
        t //= 2
    return t


def kernel(x_prompt, x_sample, cache_k, cache_v, state_pool, w_in, w_pool_mix, pool_scale, w_pool_out,
           lambda_q1, lambda_k1, lambda_q2, lambda_k2, attn_norm_g, w_attn_out, w_out, ln1_g, ln1_b,
           w_router, b_router, w_expert_in, b_expert_in, w_expert_out, b_expert_out, ln2_g, ln2_b):
    depth = w_in.shape[0]
    assert depth == 1, "single-layer step"
    dn_alpha = (2.0 * depth) ** 0.25
    lam_init = 0.8 - 0.6 * math.exp(-0.3 * 0)
    bp, sp, _ = x_prompt.shape
    bs, ss, _ = x_sample.shape
    past = cache_k.shape[2]
    np_, ns = bp * sp, bs * ss

    c_q, c_k, c_v = POOL_DIM, POOL_DIM + QK_DIM, POOL_DIM + 2 * QK_DIM
    c_gate = c_v + ATTN_V_WIDTH
    w0 = w_in[0]
    w_pqv = jnp.concatenate([w0[:, :c_k], w0[:, c_v:c_gate]], axis=1).astype(BF16)
    w_k = w0[:, c_k:c_v].astype(BF16)
    w_gate = w0[:, c_gate:].astype(BF16)
    wmix = w_pool_mix[0].astype(BF16)
    pscale = pool_scale[0].reshape(1, POOL_DIM)
    wpo = w_pool_out[0].astype(BF16)
    wao = w_attn_out[0].astype(BF16)
    wout = w_out[0].astype(BF16)
    wr = w_router[0].astype(BF16)
    br = b_router[0].reshape(1, N_EXPERTS)
    lam_vecs = jnp.stack([lambda_q1[0], lambda_k1[0], lambda_q2[0], lambda_k2[0]])
    norm_g = attn_norm_g[0].reshape(1, V_DIM)
    g1, b1 = ln1_g[0].reshape(1, D_MODEL), ln1_b[0].reshape(1, D_MODEL)
    g2, b2 = ln2_g[0].reshape(1, D_MODEL), ln2_b[0].reshape(1, D_MODEL)
    b_glu = b_expert_in[0][:, 0::2].reshape(N_EXPERTS, 1, D_EXPERT)
    b_lin = b_expert_in[0][:, 1::2].reshape(N_EXPERTS, 1, D_EXPERT)
    b_eo = b_expert_out[0].reshape(N_EXPERTS, 1, D_MODEL)

    xp = x_prompt.reshape(np_, D_MODEL)
    cos_p, sin_p, cost_p, sint_p, icnt_p = _position_tables(0, sp)
    hist_p = jnp.zeros((bp, HIST_ROWS, POOL_DIM), F32)
    q_p, kt_p, ktb_p, v_p, vb_p, py_p, pnew_p = _inproj_prompt(
        xp, w_pqv, w_k.T, cos_p, sin_p, cost_p, sint_p, icnt_p, hist_p, wmix, pscale,
        n_streams=bp, seq=sp, tm=_tile(sp, DENSE_TILE))
    ay_p = _attn_prompt(lam_vecs, norm_g, q_p, ktb_p, vb_p, n_streams=bp, seq=sp,
                        tq=_tile(sp, ATTN_Q_TILE), lam_init=lam_init)
    tm_p, tm_s = _tile(np_, ROUTE_TILE), _tile(ns, ROUTE_TILE)
    x1_p, gate_p, lpos_p, cnt_p = _postmix(
        xp, py_p, ay_p, w_gate, wpo, wao, wout, g1, b1, wr, br,
        tm=max(tm_p, _tile(np_, DENSE_TILE)), rt=tm_p, dn_alpha=dn_alpha)

    xs_ = x_sample.reshape(ns, D_MODEL)
    cos_s, sin_s, _, _, icnt_s = _position_tables(past, ss)
    hist_s = jnp.concatenate([jnp.zeros((bs, 1, POOL_DIM), F32), state_pool[0]], axis=1)
    q_s, k_s, v_s, py_s, pnew_s = _inproj_sample(
        xs_, w_pqv, w_k, cos_s, sin_s, icnt_s, hist_s, wmix, pscale, n_streams=bs, seq=ss)
    kct = jnp.transpose(cache_k[0], (0, 2, 3, 4, 1)).reshape(bs, QK_DIM, past)
    vc = cache_v[0].reshape(bs, past * N_HEADS, V_DIM)
    ay_s = _attn_sample(lam_vecs, norm_g, q_s, kct, vc, k_s, v_s, n_streams=bs, tn=ss, past=past,
                        tk=_tile(past, 1024), lam_init=lam_init)
    x1_s, gate_s, lpos_s, cnt_s = _postmix(
        xs_, py_s, ay_s, w_gate, wpo, wao, wout, g1, b1, wr, br, tm=tm_s, rt=tm_s, dn_alpha=dn_alpha)

    ntp = np_ // tm_p
    cnt = jnp.concatenate([cnt_p[:, 0, :], cnt_s[:, 0, :]], axis=0).astype(jnp.int32)
    n_tiles = cnt.shape[0]
    group = jnp.sum(cnt, axis=0)
    padded = (group + EXPERT_BLOCK - 1) // EXPERT_BLOCK * EXPERT_BLOCK
    pad_end = jnp.cumsum(padded).astype(jnp.int32)
    run_start = (pad_end - padded)[None, :] + jnp.cumsum(cnt, axis=0) - cnt
    run_off = jnp.cumsum(cnt, axis=1) - cnt
    tail = jnp.where(padded > 0, pad_end - EXPERT_BLOCK, -1).astype(jnp.int32)
    max_rows = (np_ + ns) * TOP_K + n_tiles * N_EXPERTS * (RUN_ALIGN - 1) + N_EXPERTS * (EXPERT_BLOCK - 1)
    n_blocks = -(-max_rows // EXPERT_BLOCK)
    n_used = pad_end[-1:] // EXPERT_BLOCK
    blk_start = jnp.arange(n_blocks, dtype=jnp.int32) * EXPERT_BLOCK
    blk_expert = jnp.minimum(jnp.sum((blk_start[:, None] >= pad_end[None, :]).astype(jnp.int32), axis=1),
                             N_EXPERTS - 1)
    tables = (run_start, run_off, cnt, jnp.sum(cnt, axis=1))
    tabs_p = tuple(a[:ntp].reshape(-1).astype(jnp.int32) for a in tables)
    tabs_s = tuple(a[ntp:].reshape(-1).astype(jnp.int32) for a in tables)

    xsorted = _dispatch_first(tail, n_used, tabs_p, lpos_p, x1_p, tm=tm_p, n_blocks=n_blocks)
    xsorted = _dispatch_next(tabs_s, lpos_s, x1_s, xsorted, tm=tm_s)
    blk = jnp.arange(n_blocks, dtype=jnp.int32)
    later = (blk[None, :] > blk[:, None]) & (blk_expert[None, :] != blk_expert[:, None]) & (blk[None, :] < n_used)
    next_expert = jnp.where(jnp.any(later, axis=1), blk_expert[jnp.argmax(later, axis=1)], -1).astype(jnp.int32)
    yb = _experts(blk_expert, next_expert, n_used, xsorted, w_expert_in[0], b_glu, b_lin, w_expert_out[0], b_eo)
    y_p = _combine(tabs_p, gate_p, lpos_p, x1_p, g2, b2, yb, tm=tm_p, dn_alpha=dn_alpha)
    y_s = _combine(tabs_s, gate_s, lpos_s, x1_s, g2, b2, yb, tm=tm_s, dn_alpha=dn_alpha)

    k_prompt = jnp.transpose(kt_p.reshape(bp, N_HEADS, 2, HEAD_DIM, sp), (0, 4, 1, 2, 3))
    return (
        y_p.reshape(bp, sp, D_MODEL),
        y_s.reshape(bs, ss, D_MODEL),
        k_prompt[None],
        v_p.reshape(1, bp, sp, N_HEADS, V_DIM),
        pnew_p[:, 1:].reshape(1, bp, POOL_HIST, POOL_DIM),
        k_s.reshape(1, bs, ss, N_HEADS, 2, HEAD_DIM),
        v_s.reshape(1, bs, ss, N_HEADS, V_DIM),
        pnew_s[:, 1:].reshape(1, bs, POOL_HIST, POOL_DIM),
    )
```

```python
import functools
import math

import jax
import jax.numpy as jnp
import numpy as np
from jax import lax
from jax.experimental import pallas as pl
from jax.experimental.pallas import tpu as pltpu

D_MODEL = 1024
CHUNK = 64
POOL_WINDOWS = (2, 4, 8, 16)
POOL_GROUP_DIM = 128
POOL_DIM = len(POOL_WINDOWS) * POOL_GROUP_DIM
POOL_HIST = max(POOL_WINDOWS) - 1
HIST_ROWS = POOL_HIST + 1
N_HEADS = 8
HEAD_DIM = 64
HALF_DIM = HEAD_DIM // 2
V_DIM = 2 * HEAD_DIM
QK_DIM = N_HEADS * 2 * HEAD_DIM
ATTN_V_WIDTH = N_HEADS * V_DIM
ATTN_SCALE = HEAD_DIM ** -0.5
LOG2_E = math.log2(math.e)
ROPE_THETA = 10000.0
SUBLN_EPS = 1e-5
N_EXPERTS = 32
TOP_K = 4
D_EXPERT = 1024
SWIGLU_LIMIT = 7.0
SWIGLU_ALPHA = 1.702
LN_EPS = 1e-5
NEG_INF = -1e30
LANES = 128
MXU_DIM = 256

F32 = jnp.float32
BF16 = jnp.bfloat16

VMEM_LIMIT = 56 * 1024 * 1024
EXPERT_BLOCK = 256
SUBLANES = 8
RUN_ALIGN = SUBLANES
ROUTE_TILE = 256
DENSE_TILE = 512
ATTN_Q_TILE = 256
RUN_BITS = tuple(range(3, 9))


def _dot(a, b):
    return jnp.dot(a, b, preferred_element_type=F32)


def _dot_nt(a, b):
    return lax.dot_general(a, b, (((1,), (1,)), ((), ())), preferred_element_type=F32)


def _params(semantics):
    return pltpu.CompilerParams(dimension_semantics=semantics, vmem_limit_bytes=VMEM_LIMIT)


def _pool_branch(u, icnt_ref, hist_ref, wmix_ref, pscale_ref, py_ref, pnew_ref, ext_ref, *, bb, tm):
    @pl.when(pl.program_id(1) == 0)
    def _():
        ext_ref[:, 0:HIST_ROWS, :] = hist_ref[...]

    for b in range(bb):
        ext_ref[b, HIST_ROWS:HIST_ROWS + tm, :] = u[b * tm:(b + 1) * tm]
    for b in range(bb):
        for g, w in enumerate(POOL_WINDOWS):
            cols = slice(g * POOL_GROUP_DIM, (g + 1) * POOL_GROUP_DIM)
            cur = ext_ref[b, HIST_ROWS:HIST_ROWS + tm, cols]
            acc = cur
            for j in range(1, w):
                acc = acc + ext_ref[b, HIST_ROWS - j:HIST_ROWS - j + tm, cols]
            d = acc * icnt_ref[:, cols] - cur
            y = _dot(d.astype(BF16), wmix_ref[g]) * pscale_ref[:, cols]
            py_ref[b * tm:(b + 1) * tm, cols] = y.astype(BF16)
    tail = ext_ref[:, tm:tm + HIST_ROWS, :]
    pnew_ref[...] = tail
    ext_ref[:, 0:HIST_ROWS, :] = tail


def _rope_rows(z, cos, sin):
    lane = lax.broadcasted_iota(jnp.int32, z.shape, 1)
    first_half = (lane % HEAD_DIM) < HALF_DIM
    partner = jnp.where(first_half, pltpu.roll(z, LANES - HALF_DIM, 1), pltpu.roll(z, HALF_DIM, 1))
    return z * cos + partner * sin


def _inproj_prompt_kernel(x_ref, w_ref, wkt_ref, cos_ref, sin_ref, cost_ref, sint_ref, icnt_ref, hist_ref,
                          wmix_ref, pscale_ref, q_ref, kt_ref, ktb_ref, v_ref, vb_ref, py_ref, pnew_ref,
                          ext_ref, *, tm):
    x = x_ref[...].astype(BF16)
    _pool_branch(_dot(x, w_ref[:, 0:POOL_DIM]), icnt_ref, hist_ref, wmix_ref, pscale_ref, py_ref, pnew_ref,
                 ext_ref, bb=1, tm=tm)

    cos, sin = cos_ref[...], sin_ref[...]
    hq = _dot(x, w_ref[:, POOL_DIM:POOL_DIM + QK_DIM])
    for h in range(N_HEADS):
        sl = slice(h * V_DIM, (h + 1) * V_DIM)
        q_ref[:, sl] = (_rope_rows(hq[:, sl], cos, sin) * (ATTN_SCALE * LOG2_E)).astype(BF16)

    hkt = _dot_nt(wkt_ref[...], x)
    cost, sint = cost_ref[...], sint_ref[...]
    for hc in range(2 * N_HEADS):
        r0 = hc * HEAD_DIM
        x1 = hkt[r0:r0 + HALF_DIM]
        x2 = hkt[r0 + HALF_DIM:r0 + HEAD_DIM]
        o1 = x1 * cost - x2 * sint
        o2 = x2 * cost + x1 * sint
        kt_ref[r0:r0 + HALF_DIM, :] = o1
        kt_ref[r0 + HALF_DIM:r0 + HEAD_DIM, :] = o2
        ktb_ref[r0:r0 + HALF_DIM, :] = o1.astype(BF16)
        ktb_ref[r0 + HALF_DIM:r0 + HEAD_DIM, :] = o2.astype(BF16)

    hv = _dot(x, w_ref[:, POOL_DIM + QK_DIM:POOL_DIM + QK_DIM + ATTN_V_WIDTH])
    vb_ref[...] = hv.astype(BF16)
    for h in range(N_HEADS):
        v_ref[pl.ds(h, tm, stride=N_HEADS), :] = hv[:, h * V_DIM:(h + 1) * V_DIM]


def _inproj_prompt(x2d, w_pqv, wkt, cos, sin, cost, sint, icnt, hist, wmix, pscale, *, n_streams, seq, tm):
    n = n_streams * seq
    nt = seq // tm
    row_map = lambda b, t: (b * nt + t, 0)
    const2 = lambda b, t: (0, 0)
    out_shape = (
        jax.ShapeDtypeStruct((n, QK_DIM), BF16),
        jax.ShapeDtypeStruct((n_streams, QK_DIM, seq), F32),
        jax.ShapeDtypeStruct((n_streams, QK_DIM, seq), BF16),
        jax.ShapeDtypeStruct((n_streams, seq * N_HEADS, V_DIM), F32),
        jax.ShapeDtypeStruct((n, ATTN_V_WIDTH), BF16),
        jax.ShapeDtypeStruct((n, POOL_DIM), BF16),
        jax.ShapeDtypeStruct((n_streams, HIST_ROWS, POOL_DIM), F32),
    )
    return pl.pallas_call(
        functools.partial(_inproj_prompt_kernel, tm=tm),
        out_shape=out_shape,
        grid=(n_streams, nt),
        in_specs=[
            pl.BlockSpec((tm, D_MODEL), row_map),
            pl.BlockSpec(w_pqv.shape, const2),
            pl.BlockSpec(wkt.shape, const2),
            pl.BlockSpec((tm, LANES), lambda b, t: (t, 0)),
            pl.BlockSpec((tm, LANES), lambda b, t: (t, 0)),
            pl.BlockSpec((HALF_DIM, tm), lambda b, t: (0, t)),
            pl.BlockSpec((HALF_DIM, tm), lambda b, t: (0, t)),
            pl.BlockSpec((tm, POOL_DIM), lambda b, t: (t, 0)),
            pl.BlockSpec((1, HIST_ROWS, POOL_DIM), lambda b, t: (b, 0, 0)),
            pl.BlockSpec((len(POOL_WINDOWS), POOL_GROUP_DIM, POOL_GROUP_DIM), lambda b, t: (0, 0, 0)),
            pl.BlockSpec((1, POOL_DIM), const2),
        ],
        out_specs=(
            pl.BlockSpec((tm, QK_DIM), row_map),
            pl.BlockSpec((None, QK_DIM, tm), lambda b, t: (b, 0, t)),
            pl.BlockSpec((None, QK_DIM, tm), lambda b, t: (b, 0, t)),
            pl.BlockSpec((None, tm * N_HEADS, V_DIM), lambda b, t: (b, t, 0)),
            pl.BlockSpec((tm, ATTN_V_WIDTH), row_map),
            pl.BlockSpec((tm, POOL_DIM), row_map),
            pl.BlockSpec((1, HIST_ROWS, POOL_DIM), lambda b, t: (b, 0, 0)),
        ),
        scratch_shapes=[pltpu.VMEM((1, HIST_ROWS + tm, POOL_DIM), F32)],
        compiler_params=_params(("arbitrary", "arbitrary")),
        name="inproj_prompt",
    )(x2d, w_pqv, wkt, cos, sin, cost, sint, icnt, hist, wmix, pscale)


def _inproj_sample_kernel(x_ref, w_ref, wk_ref, cos_ref, sin_ref, icnt_ref, hist_ref, wmix_ref, pscale_ref,
                          q_ref, k_ref, v_ref, py_ref, pnew_ref, ext_ref, *, bb, tm):
    x = x_ref[...].astype(BF16)
    _pool_branch(_dot(x, w_ref[:, 0:POOL_DIM]), icnt_ref, hist_ref, wmix_ref, pscale_ref, py_ref, pnew_ref,
                 ext_ref, bb=bb, tm=tm)
    cos = jnp.concatenate([cos_ref[...]] * bb, axis=0)
    sin = jnp.concatenate([sin_ref[...]] * bb, axis=0)
    hq = _dot(x, w_ref[:, POOL_DIM:POOL_DIM + QK_DIM])
    hk = _dot(x, wk_ref[...])
    for h in range(N_HEADS):
        sl = slice(h * V_DIM, (h + 1) * V_DIM)
        q_ref[:, sl] = (_rope_rows(hq[:, sl], cos, sin) * ATTN_SCALE).astype(BF16)
        k_ref[:, sl] = _rope_rows(hk[:, sl], cos, sin)
    v_ref[...] = _dot(x, w_ref[:, POOL_DIM + QK_DIM:POOL_DIM + QK_DIM + ATTN_V_WIDTH])


def _inproj_sample(x2d, w_pqv, wk, cos, sin, icnt, hist, wmix, pscale, *, n_streams, seq):
    n = n_streams * seq
    const2 = lambda i, t: (0, 0)
    const3 = lambda i, t: (0, 0, 0)
    out_shape = (
        jax.ShapeDtypeStruct((n, QK_DIM), BF16),
        jax.ShapeDtypeStruct((n, QK_DIM), F32),
        jax.ShapeDtypeStruct((n, ATTN_V_WIDTH), F32),
        jax.ShapeDtypeStruct((n, POOL_DIM), BF16),
        jax.ShapeDtypeStruct((n_streams, HIST_ROWS, POOL_DIM), F32),
    )
    return pl.pallas_call(
        functools.partial(_inproj_sample_kernel, bb=n_streams, tm=seq),
        out_shape=out_shape,
        grid=(1, 1),
        in_specs=[
            pl.BlockSpec((n, D_MODEL), const2),
            pl.BlockSpec(w_pqv.shape, const2),
            pl.BlockSpec(wk.shape, const2),
            pl.BlockSpec((seq, LANES), const2),
            pl.BlockSpec((seq, LANES), const2),
            pl.BlockSpec((seq, POOL_DIM), const2),
            pl.BlockSpec((n_streams, HIST_ROWS, POOL_DIM), const3),
            pl.BlockSpec((len(POOL_WINDOWS), POOL_GROUP_DIM, POOL_GROUP_DIM), const3),
            pl.BlockSpec((1, POOL_DIM), const2),
        ],
        out_specs=(
            pl.BlockSpec((n, QK_DIM), const2),
            pl.BlockSpec((n, QK_DIM), const2),
            pl.BlockSpec((n, ATTN_V_WIDTH), const2),
            pl.BlockSpec((n, POOL_DIM), const2),
            pl.BlockSpec((n_streams, HIST_ROWS, POOL_DIM), const3),
        ),
        scratch_shapes=[pltpu.VMEM((n_streams, HIST_ROWS + seq, POOL_DIM), F32)],
        compiler_params=_params(("arbitrary", "arbitrary")),
        name="inproj_sample",
    )(x2d, w_pqv, wk, cos, sin, icnt, hist, wmix, pscale)


def _lambda_value(lam_ref, lam_init):
    lv = lam_ref[...]
    s1 = jnp.sum(lv[0:1] * lv[1:2], axis=1, keepdims=True)
    s2 = jnp.sum(lv[2:3] * lv[3:4], axis=1, keepdims=True)
    return jnp.exp(s1) - jnp.exp(s2) + lam_init


def _head_norm(o, g, lam_init):
    ms = jnp.mean(o * o, axis=-1, keepdims=True)
    return o * lax.rsqrt(ms + SUBLN_EPS) * g * (1.0 - lam_init)


def _attn_prompt_kernel(lam_ref, g_ref, q_ref, kt_ref, v_ref, o_ref, vext, *, seq, tq, lam_init):
    lam = _lambda_value(lam_ref, lam_init)
    g = g_ref[...]
    r = lax.broadcasted_iota(jnp.int32, (tq, tq), 0)
    c = lax.broadcasted_iota(jnp.int32, (tq, tq), 1)
    diag_visible = (c // CHUNK) <= (r // CHUNK)
    lane = lax.broadcasted_iota(jnp.int32, (tq, V_DIM), 1)
    vext[:, 0:V_DIM] = v_ref[...]
    vlane = lax.broadcasted_iota(jnp.int32, (seq, V_DIM), 1)
    vext[:, V_DIM:] = jnp.where(vlane == 0, 1.0, 0.0).astype(vext.dtype)

    def scores(i):
        lo = i * tq
        q = q_ref[lo:lo + tq, :]
        zero = jnp.zeros_like(q)
        qc = (jnp.where(lane < HEAD_DIM, q, zero), jnp.where(lane >= HEAD_DIM, q, zero))
        out = []
        for k in range(2):
            sd = jnp.where(diag_visible, _dot(qc[k], kt_ref[:, lo:lo + tq]), NEG_INF)
            m = jnp.max(sd, axis=1, keepdims=True)
            sp = None
            if i > 0:
                sp = _dot(qc[k], kt_ref[:, 0:lo])
                m = jnp.maximum(m, jnp.max(sp, axis=1, keepdims=True))
            out.append((sd, sp, m))
        return out

    def finish(i, parts):
        lo = i * tq
        normed = []
        for sd, sp, m in parts:
            acc = _dot(jnp.exp2(sd - m).astype(BF16), vext[lo:lo + tq, :])
            if sp is not None:
                acc = acc + _dot(jnp.exp2(sp - m).astype(BF16), vext[0:lo, :])
            normed.append(acc[:, 0:V_DIM] / acc[:, V_DIM:V_DIM + 1])
        o = normed[0] - lam * normed[1]
        o_ref[lo:lo + tq, :] = _head_norm(o, g, lam_init).astype(o_ref.dtype)

    nq = seq // tq
    pending = scores(0)
    for i in range(nq):
        upcoming = scores(i + 1) if i + 1 < nq else None
        finish(i, pending)
        pending = upcoming


def _attn_prompt(lam_vecs, norm_g, q, ktb, vb, *, n_streams, seq, tq, lam_init):
    return pl.pallas_call(
        functools.partial(_attn_prompt_kernel, seq=seq, tq=tq, lam_init=lam_init),
        out_shape=jax.ShapeDtypeStruct((n_streams * seq, ATTN_V_WIDTH), BF16),
        grid=(n_streams, N_HEADS),
        in_specs=[
            pl.BlockSpec((4, HEAD_DIM), lambda b, h: (0, 0)),
            pl.BlockSpec((1, V_DIM), lambda b, h: (0, 0)),
            pl.BlockSpec((seq, V_DIM), lambda b, h: (b, h)),
            pl.BlockSpec((None, V_DIM, seq), lambda b, h: (b, h, 0)),
            pl.BlockSpec((seq, V_DIM), lambda b, h: (b, h)),
        ],
        out_specs=pl.BlockSpec((seq, V_DIM), lambda b, h: (b, h)),
        scratch_shapes=[pltpu.VMEM((seq, MXU_DIM), BF16)],
        compiler_params=_params(("arbitrary", "arbitrary")),
        name="attn_prompt",
    )(lam_vecs, norm_g, q, ktb, vb)


def _attn_sample_kernel(lam_ref, g_ref, q_ref, kc_ref, vc_ref, kn_ref, vn_ref, o_ref,
                        s_scr, w_scr, wn_scr, m_scr, acc_scr, qbd_scr, *, nk, tn, past, lam_init):
    j = pl.program_id(1)
    half = N_HEADS * tn

    @pl.when(j == 0)
    def _():
        q = q_ref[...]
        qt = jnp.concatenate([q] * (2 * N_HEADS), axis=0)
        r = lax.broadcasted_iota(jnp.int32, qt.shape, 0)
        l = lax.broadcasted_iota(jnp.int32, qt.shape, 1)
        keep = ((r // half) == ((l % V_DIM) // HEAD_DIM)) & (((r % half) // tn) == (l // V_DIM))
        qbd_scr[...] = jnp.where(keep, qt, jnp.zeros_like(qt))
        m_scr[...] = jnp.full(m_scr.shape, NEG_INF, F32)
        acc_scr[...] = jnp.zeros(acc_scr.shape, F32)

    @pl.when(j < nk)
    def _():
        s = _dot(qbd_scr[...], kc_ref[...].astype(BF16))
        s_scr[j] = s
        m_scr[...] = jnp.maximum(m_scr[...], jnp.max(s, axis=1, keepdims=True))

    @pl.when(j == nk - 1)
    def _():
        lam = _lambda_value(lam_ref, lam_init)
        sn = _dot_nt(qbd_scr[...], kn_ref[...].astype(BF16))
        qpos = past + (lax.broadcasted_iota(jnp.int32, sn.shape, 0) % tn)
        kpos = past + lax.broadcasted_iota(jnp.int32, sn.shape, 1)
        sn = jnp.where((kpos // CHUNK) <= (qpos // CHUNK), sn, NEG_INF)
        m = jnp.maximum(m_scr[...], jnp.max(sn, axis=1, keepdims=True))
        pn = jnp.exp(sn - m)
        l = jnp.sum(pn, axis=1, keepdims=True)
        for c in range(nk):
            p = jnp.exp(s_scr[c] - m)
            s_scr[c] = p
            l = l + jnp.sum(p, axis=1, keepdims=True)
        r0 = 1.0 / l[:half]
        r1 = lam / l[half:]
        wn_scr[...] = pn[:half] * r0 - pn[half:] * r1
        for c in range(nk):
            p = s_scr[c]
            w_scr[c] = (p[:half] * r0 - p[half:] * r1).astype(BF16)

    def v_rows(ref):
        tk = ref.shape[0] // N_HEADS
        return jnp.concatenate([ref[pl.ds(h, tk, stride=N_HEADS), :] for h in range(N_HEADS)],
                               axis=1).astype(BF16)

    @pl.when(j >= nk)
    def _():
        acc_scr[...] += _dot(w_scr[j - nk], v_rows(vc_ref))

    @pl.when(j == 2 * nk - 1)
    def _():
        acc = acc_scr[...] + _dot(wn_scr[...].astype(BF16), vn_ref[...].astype(BF16))
        g = g_ref[...]
        for h in range(N_HEADS):
            o = acc[h * tn:(h + 1) * tn, h * V_DIM:(h + 1) * V_DIM]
            o_ref[:, h * V_DIM:(h + 1) * V_DIM] = _head_norm(o, g, lam_init).astype(o_ref.dtype)


def _attn_sample(lam_vecs, norm_g, q, kct, vc, kn, vn, *, n_streams, tn, past, tk, lam_init):
    nk = past // tk
    rows = 2 * N_HEADS * tn
    half = N_HEADS * tn
    return pl.pallas_call(
        functools.partial(_attn_sample_kernel, nk=nk, tn=tn, past=past, lam_init=lam_init),
        out_shape=jax.ShapeDtypeStruct((n_streams * tn, ATTN_V_WIDTH), BF16),
        grid=(n_streams, 2 * nk),
        in_specs=[
            pl.BlockSpec((4, HEAD_DIM), lambda b, j: (0, 0)),
            pl.BlockSpec((1, V_DIM), lambda b, j: (0, 0)),
            pl.BlockSpec((tn, QK_DIM), lambda b, j: (b, 0)),
            pl.BlockSpec((None, QK_DIM, tk), lambda b, j: (b, 0, jnp.minimum(j, nk - 1))),
            pl.BlockSpec((None, tk * N_HEADS, V_DIM), lambda b, j: (b, jnp.maximum(j - nk, 0), 0)),
            pl.BlockSpec((tn, QK_DIM), lambda b, j: (b, 0)),
            pl.BlockSpec((tn, ATTN_V_WIDTH), lambda b, j: (b, 0)),
        ],
        out_specs=pl.BlockSpec((tn, ATTN_V_WIDTH), lambda b, j: (b, 0)),
        scratch_shapes=[
            pltpu.VMEM((nk, rows, tk), F32),
            pltpu.VMEM((nk, half, tk), BF16),
            pltpu.VMEM((half, tn), F32),
            pltpu.VMEM((rows, 1), F32),
            pltpu.VMEM((half, ATTN_V_WIDTH), F32),
            pltpu.VMEM((rows, QK_DIM), BF16),
        ],
        compiler_params=_params(("arbitrary", "arbitrary")),
        name="attn_sample",
    )(lam_vecs, norm_g, q, kct, vc, kn, vn)


def _layer_norm(z, g, b):
    mu = jnp.mean(z, axis=-1, keepdims=True)
    zc = z - mu
    var = jnp.mean(zc * zc, axis=-1, keepdims=True)
    return zc * lax.rsqrt(var + LN_EPS) * g + b


def _postmix_kernel(x_ref, xprev_ref, py_ref, ay_ref, wg_ref, wpo_ref, wao_ref, wout_ref, g1_ref, b1_ref,
                    wr_ref, br_ref, x1_ref, gate_ref, lpos_ref, cnt_ref, mix_scr, *, tm, rt, dn_alpha):
    @pl.when(pl.program_id(0) == 0)
    def _():
        mix_scr[...] = jnp.zeros(mix_scr.shape, mix_scr.dtype)

    mixed_prev = mix_scr[...]

    mo = _dot(mixed_prev, wout_ref[...])
    x1 = _layer_norm(dn_alpha * xprev_ref[...] + mo, g1_ref[...], b1_ref[...])
    x1_ref[...] = x1
    logits = _dot(x1.astype(BF16), wr_ref[...]) + br_ref[...]

    a = _dot(py_ref[...], wpo_ref[...])
    b = _dot(ay_ref[...], wao_ref[...])
    gates = jax.nn.sigmoid(_dot(x_ref[...].astype(BF16), wg_ref[...]))
    mix_scr[...] = (gates[:, :D_MODEL] * a + gates[:, D_MODEL:] * b).astype(BF16)

    for sub in range(tm // rt):
        rows = slice(sub * rt, (sub + 1) * rt)
        gate_ref[rows, :], lpos_ref[rows, :], cnt_ref[sub] = _route_tile(logits[rows], rt)


def _route_tile(logits, tm):
    lane = lax.broadcasted_iota(jnp.int32, logits.shape, 1)
    work = logits
    vals, idxs = [], []
    for _ in range(TOP_K):
        mx = jnp.max(work, axis=1, keepdims=True)
        ix = jnp.min(jnp.where(work == mx, lane, N_EXPERTS), axis=1, keepdims=True)
        vals.append(mx)
        idxs.append(ix)
        work = jnp.where(lane == ix, -jnp.inf, work)
    exps = [jnp.exp(v - vals[0]) for v in vals]
    denom = exps[0] + exps[1] + exps[2] + exps[3]

    onehot = jnp.zeros(logits.shape, F32)
    for ix in idxs:
        onehot = onehot + (lane == ix).astype(F32)
    r = lax.broadcasted_iota(jnp.int32, (tm, tm), 0)
    c = lax.broadcasted_iota(jnp.int32, (tm, tm), 1)
    tri = jnp.where(c < r, 1.0, 0.0).astype(BF16)
    earlier = _dot(tri, onehot.astype(BF16))
    cnt = jnp.sum(onehot, axis=0, keepdims=True)
    units = jnp.floor((cnt + (RUN_ALIGN - 1.0)) * (1.0 / RUN_ALIGN))
    er = lax.broadcasted_iota(jnp.int32, (N_EXPERTS, N_EXPERTS), 0)
    ec = lax.broadcasted_iota(jnp.int32, (N_EXPERTS, N_EXPERTS), 1)
    upper = jnp.where(er < ec, 1.0, 0.0).astype(BF16)
    run_off = _dot(jnp.broadcast_to(units, (SUBLANES, N_EXPERTS)).astype(BF16), upper)[0:1] * float(RUN_ALIGN)
    pos = earlier + run_off

    lane_out = lax.broadcasted_iota(jnp.int32, (tm, LANES), 1)
    lpos_out = jnp.zeros((tm, LANES), jnp.int32)
    gate_out = jnp.zeros((tm, LANES), F32)
    for k in range(TOP_K):
        lpos_k = jnp.sum(jnp.where(lane == idxs[k], pos, 0.0), axis=1, keepdims=True).astype(jnp.int32)
        lpos_out = jnp.where(lane_out == k, lpos_k, lpos_out)
        gate_out = jnp.where(lane_out == k, exps[k] / denom, gate_out)
    return gate_out, lpos_out, units * float(RUN_ALIGN)


def _postmix(x2d, py, ay, wg, wpo, wao, wout, g1, b1, wr, br, *, tm, rt, dn_alpha):
    n = x2d.shape[0]
    nt = n // tm
    row = lambda i: (jnp.minimum(i, nt - 1), 0)
    prev = lambda i: (jnp.maximum(i - 1, 0), 0)
    const = lambda i: (0, 0)
    out_shape = (
        jax.ShapeDtypeStruct((n, D_MODEL), F32),
        jax.ShapeDtypeStruct((n, LANES), F32),
        jax.ShapeDtypeStruct((n, LANES), jnp.int32),
        jax.ShapeDtypeStruct((n // rt, 1, N_EXPERTS), F32),
    )
    return pl.pallas_call(
        functools.partial(_postmix_kernel, tm=tm, rt=rt, dn_alpha=dn_alpha),
        out_shape=out_shape,
        grid=(nt + 1,),
        in_specs=[
            pl.BlockSpec((tm, D_MODEL), row),
            pl.BlockSpec((tm, D_MODEL), prev),
            pl.BlockSpec((tm, POOL_DIM), row),
            pl.BlockSpec((tm, ATTN_V_WIDTH), row),
            pl.BlockSpec(wg.shape, const),
            pl.BlockSpec(wpo.shape, const),
            pl.BlockSpec(wao.shape, const),
            pl.BlockSpec(wout.shape, const),
            pl.BlockSpec((1, D_MODEL), const),
            pl.BlockSpec((1, D_MODEL), const),
            pl.BlockSpec(wr.shape, const),
            pl.BlockSpec((1, N_EXPERTS), const),
        ],
        out_specs=(
            pl.BlockSpec((tm, D_MODEL), prev),
            pl.BlockSpec((tm, LANES), prev),
            pl.BlockSpec((tm, LANES), prev),
            pl.BlockSpec((tm // rt, 1, N_EXPERTS), lambda i: (jnp.maximum(i - 1, 0), 0, 0)),
        ),
        scratch_shapes=[pltpu.VMEM((tm, D_MODEL), BF16)],
        compiler_params=_params(("arbitrary",)),
        name="postmix",
    )(x2d, x2d, py, ay, wg, wpo, wao, wout, g1, b1, wr, br)


def _local_rows(tm):
    return TOP_K * tm + N_EXPERTS * RUN_ALIGN


def _for_each_run_chunk(tile, tab_refs, local_buf, sorted_ref, sem, to_sorted, fn):
    gs_ref, off_ref, cnt_ref = tab_refs[:3]

    def per_expert_pair(pair, carry):
        for de in range(2):
            t = tile * N_EXPERTS + 2 * pair + de
            cnt, off, gs = cnt_ref[t], off_ref[t], gs_ref[t]
            for b in RUN_BITS:
                size = 1 << b

                @pl.when((cnt & size) != 0)
                def _():
                    lower = cnt & (size - 1)
                    loc = local_buf.at[pl.ds(pl.multiple_of(off + lower, RUN_ALIGN), size)]
                    srt = sorted_ref.at[pl.ds(pl.multiple_of(gs + lower, RUN_ALIGN), size)]
                    fn(pltpu.make_async_copy(loc, srt, sem) if to_sorted
                       else pltpu.make_async_copy(srt, loc, sem), b + de)
        return carry

    lax.fori_loop(0, N_EXPERTS // 2, per_expert_pair, 0)


def _start_alternating(copy, k):
    copy.start(priority=k % 2)


def _wait_run_rows(total, local_buf, sorted_ref, sem, to_sorted):
    rows = local_buf.shape[0]
    for b in range(RUN_BITS[0], rows.bit_length()):
        size = 1 << b

        @pl.when((total & size) != 0)
        def _():
            loc, srt = local_buf.at[pl.ds(0, size)], sorted_ref.at[pl.ds(0, size)]
            (pltpu.make_async_copy(loc, srt, sem) if to_sorted else pltpu.make_async_copy(srt, loc, sem)).wait()


def _pack_bf16_pairs(a):
    n = a.shape[1] // 2
    bits = lax.bitcast_convert_type(a, jnp.int32)
    return bits[:, :n] | lax.shift_right_logical(bits[:, n:], 16)


def _unpack_bf16_pairs(w):
    hi = lax.bitcast_convert_type(w & jnp.int32(-65536), F32)
    lo = lax.bitcast_convert_type(lax.shift_left(w, 16), F32)
    return jnp.concatenate([hi, lo], axis=1).astype(BF16)


def _dispatch_tile(tab_refs, lpos_ref, x_ref, xs_ref, xloc, sem, tm):
    lpos = lpos_ref[...]
    col = lax.broadcasted_iota(jnp.int32, (tm, _local_rows(tm)), 1)
    hit = col == lpos[:, 0:1]
    for k in range(1, TOP_K):
        hit = jnp.logical_or(hit, col == lpos[:, k:k + 1])
    perm_t = jnp.where(hit, 1.0, 0.0).astype(BF16)
    tile = pl.program_id(0)
    slot = tile % 2
    sorted_rows = lax.dot_general(perm_t, x_ref[...].astype(BF16), (((0,), (0,)), ((), ())),
                                  preferred_element_type=F32)
    xloc[slot] = _pack_bf16_pairs(sorted_rows)

    def copies(t, s, fn):
        _for_each_run_chunk(t, tab_refs, xloc.at[s], xs_ref, sem.at[s], True, fn)

    copies(tile, slot, _start_alternating)
    tot_ref = tab_refs[3]

    @pl.when(tile > 0)
    def _():
        _wait_run_rows(tot_ref[tile - 1], xloc.at[1 - slot], xs_ref, sem.at[1 - slot], True)

    @pl.when(tile == pl.num_programs(0) - 1)
    def _():
        _wait_run_rows(tot_ref[tile], xloc.at[slot], xs_ref, sem.at[slot], True)


def _dispatch_first_kernel(tail_ref, nu_ref, gs_ref, off_ref, cnt_ref, tot_ref, lpos_ref, x_ref, xs_ref,
                           xloc, zbuf, sem, zsem, *, tm, n_blocks):
    @pl.when(pl.program_id(0) == 0)
    def _():
        zbuf[...] = jnp.zeros(zbuf.shape, zbuf.dtype)

        def zero_copy(row):
            row = pl.multiple_of(row, EXPERT_BLOCK)
            return pltpu.make_async_copy(zbuf, xs_ref.at[pl.ds(row, EXPERT_BLOCK)], zsem)

        def over_blocks(fn):
            for e in range(N_EXPERTS):
                @pl.when(tail_ref[e] >= 0)
                def _():
                    fn(zero_copy(tail_ref[e]))
            lax.fori_loop(nu_ref[0], n_blocks, lambda b, c: (fn(zero_copy(b * EXPERT_BLOCK)), c)[1], 0)

        over_blocks(lambda cp: cp.start())
        over_blocks(lambda cp: cp.wait())

    _dispatch_tile((gs_ref, off_ref, cnt_ref, tot_ref), lpos_ref, x_ref, xs_ref, xloc, sem, tm)


def _dispatch_next_kernel(gs_ref, off_ref, cnt_ref, tot_ref, lpos_ref, x_ref, xs_in_ref, xs_ref, xloc, sem,
                          *, tm):
    del xs_in_ref
    _dispatch_tile((gs_ref, off_ref, cnt_ref, tot_ref), lpos_ref, x_ref, xs_ref, xloc, sem, tm)


def _dispatch_first(tail, n_used, tabs, lpos, x1, *, tm, n_blocks):
    n = x1.shape[0]
    grid_spec = pltpu.PrefetchScalarGridSpec(
        num_scalar_prefetch=6,
        grid=(n // tm,),
        in_specs=[pl.BlockSpec((tm, LANES), lambda i, *_: (i, 0)),
                  pl.BlockSpec((tm, D_MODEL), lambda i, *_: (i, 0))],
        out_specs=pl.BlockSpec(memory_space=pl.ANY),
        scratch_shapes=[pltpu.VMEM((2, _local_rows(tm), D_MODEL // 2), jnp.int32),
                        pltpu.VMEM((EXPERT_BLOCK, D_MODEL // 2), jnp.int32),
                        pltpu.SemaphoreType.DMA((2,)), pltpu.SemaphoreType.DMA],
    )
    return pl.pallas_call(
        functools.partial(_dispatch_first_kernel, tm=tm, n_blocks=n_blocks),
        out_shape=jax.ShapeDtypeStruct((n_blocks * EXPERT_BLOCK, D_MODEL // 2), jnp.int32),
        grid_spec=grid_spec,
        compiler_params=_params(("arbitrary",)),
        name="dispatch_first",
    )(tail, n_used, *tabs, lpos, x1)


def _dispatch_next(tabs, lpos, x1, xs, *, tm):
    n = x1.shape[0]
    grid_spec = pltpu.PrefetchScalarGridSpec(
        num_scalar_prefetch=len(tabs),
        grid=(n // tm,),
        in_specs=[pl.BlockSpec((tm, LANES), lambda i, *_: (i, 0)),
                  pl.BlockSpec((tm, D_MODEL), lambda i, *_: (i, 0)),
                  pl.BlockSpec(memory_space=pl.ANY)],
        out_specs=pl.BlockSpec(memory_space=pl.ANY),
        scratch_shapes=[pltpu.VMEM((2, _local_rows(tm), D_MODEL // 2), jnp.int32), pltpu.SemaphoreType.DMA((2,))],
    )
    return pl.pallas_call(
        functools.partial(_dispatch_next_kernel, tm=tm),
        out_shape=jax.ShapeDtypeStruct(xs.shape, xs.dtype),
        grid_spec=grid_spec,
        input_output_aliases={6: 0},
        compiler_params=_params(("arbitrary",)),
        name="dispatch_next",
    )(*tabs, lpos, x1, xs)


def _experts_kernel(be_ref, nxt_ref, nu_ref, xs_ref, bg_ref, bl_ref, bo_ref, win_hbm, wo_hbm, y_ref,
                    win_buf, wo_buf, wg_scr, wl_scr, wo_scr, sem_in, sem_out):
    i = pl.program_id(0)
    used = i < nu_ref[0]
    first_of_expert = jnp.logical_or(i == 0, be_ref[i] != be_ref[jnp.maximum(i - 1, 0)])

    def weight_copies(e):
        return (pltpu.make_async_copy(win_hbm.at[e], win_buf, sem_in),
                pltpu.make_async_copy(wo_hbm.at[e], wo_buf, sem_out))

    @pl.when(jnp.logical_not(used))
    def _():
        y_ref[...] = jnp.zeros(y_ref.shape, y_ref.dtype)

    @pl.when(i == 0)
    def _():
        for cp in weight_copies(be_ref[0]):
            cp.start()

    @pl.when(jnp.logical_and(used, first_of_expert))
    def _():
        for cp in weight_copies(be_ref[i]):
            cp.wait()
        r = lax.broadcasted_iota(jnp.int32, (MXU_DIM, MXU_DIM), 0)
        c = lax.broadcasted_iota(jnp.int32, (MXU_DIM, MXU_DIM), 1)
        src = jnp.where(c < LANES, 2 * c, 2 * (c - LANES) + 1)
        sel = jnp.where(r == src, 1.0, 0.0).astype(BF16)
        for gq in range(2 * D_EXPERT // MXU_DIM):
            blk = win_buf[:, gq * MXU_DIM:(gq + 1) * MXU_DIM].astype(BF16)
            d = _dot(blk, sel)
            wg_scr[:, gq * LANES:(gq + 1) * LANES] = d[:, :LANES].astype(BF16)
            wl_scr[:, gq * LANES:(gq + 1) * LANES] = d[:, LANES:].astype(BF16)
        wo_scr[...] = wo_buf[...].astype(BF16)

        @pl.when(nxt_ref[i] >= 0)
        def _():
            for cp in weight_copies(nxt_ref[i]):
                cp.start(priority=1)

    @pl.when(used)
    def _():
        xb = _unpack_bf16_pairs(xs_ref[...])
        glu = jnp.minimum(_dot(xb, wg_scr[...]) + bg_ref[...], SWIGLU_LIMIT)
        lin = jnp.clip(_dot(xb, wl_scr[...]) + bl_ref[...], -SWIGLU_LIMIT, SWIGLU_LIMIT)
        act = glu * jax.nn.sigmoid(SWIGLU_ALPHA * glu) * (lin + 1.0)
        y_ref[...] = _dot(act.astype(BF16), wo_scr[...]) + bo_ref[...]


def _experts(blk_expert, next_expert, n_used, xs, w_in, b_glu, b_lin, w_out, b_out):
    rows = xs.shape[0]
    n_blocks = rows // EXPERT_BLOCK
    wsel = lambda i, be, nxt, nu: (be[i], 0, 0)
    grid_spec = pltpu.PrefetchScalarGridSpec(
        num_scalar_prefetch=3,
        grid=(n_blocks,),
        in_specs=[
            pl.BlockSpec((EXPERT_BLOCK, D_MODEL // 2), lambda i, be, nxt, nu: (jnp.minimum(i, nu[0] - 1), 0)),
            pl.BlockSpec((None, 1, D_EXPERT), wsel),
            pl.BlockSpec((None, 1, D_EXPERT), wsel),
            pl.BlockSpec((None, 1, D_MODEL), wsel),
            pl.BlockSpec(memory_space=pl.ANY),
            pl.BlockSpec(memory_space=pl.ANY),
        ],
        out_specs=pl.BlockSpec((EXPERT_BLOCK, D_MODEL), lambda i, be, nxt, nu: (i, 0)),
        scratch_shapes=[pltpu.VMEM((D_MODEL, 2 * D_EXPERT), F32),
                        pltpu.VMEM((D_EXPERT, D_MODEL), F32),
                        pltpu.VMEM((D_MODEL, D_EXPERT), BF16),
                        pltpu.VMEM((D_MODEL, D_EXPERT), BF16),
                        pltpu.VMEM((D_EXPERT, D_MODEL), BF16),
                        pltpu.SemaphoreType.DMA, pltpu.SemaphoreType.DMA],
    )
    return pl.pallas_call(
        _experts_kernel,
        out_shape=jax.ShapeDtypeStruct((rows, D_MODEL), F32),
        grid_spec=grid_spec,
        compiler_params=_params(("arbitrary",)),
        name="experts",
    )(blk_expert, next_expert, n_used, xs, b_glu, b_lin, b_out, w_in, w_out)


def _split_bf16(a):
    hi = a.astype(BF16)
    return hi, (a - hi.astype(F32)).astype(BF16)


def _combine_kernel(gs_ref, off_ref, cnt_ref, tot_ref, gate_ref, lpos_ref, x1_ref, g2_ref, b2_ref, yb_ref, o_ref,
                    yloc, sem, *, tm, dn_alpha):
    tile = pl.program_id(0)
    slot = tile % 2
    tabs = (gs_ref, off_ref, cnt_ref)

    def copies(t, s, fn):
        _for_each_run_chunk(t, tabs, yloc.at[s], yb_ref, sem.at[s], False, fn)

    @pl.when(tile == 0)
    def _():
        yloc[...] = jnp.zeros(yloc.shape, yloc.dtype)
        copies(tile, slot, _start_alternating)

    @pl.when(tile + 1 < pl.num_programs(0))
    def _():
        copies(tile + 1, 1 - slot, _start_alternating)

    _wait_run_rows(tot_ref[tile], yloc.at[slot], yb_ref, sem.at[slot], False)

    gate, lpos = gate_ref[...], lpos_ref[...]
    col = lax.broadcasted_iota(jnp.int32, (tm, _local_rows(tm)), 1)
    weights = jnp.zeros(col.shape, F32)
    for k in range(TOP_K):
        weights = jnp.where(col == lpos[:, k:k + 1], gate[:, k:k + 1], weights)
    w_hi, w_lo = _split_bf16(weights)
    y_hi, y_lo = _split_bf16(yloc[slot])
    y = _dot(w_hi, y_hi) + (_dot(w_hi, y_lo) + _dot(w_lo, y_hi))
    o_ref[...] = _layer_norm(dn_alpha * x1_ref[...] + y, g2_ref[...], b2_ref[...])


def _combine(tabs, gate, lpos, x1, g2, b2, yb, *, tm, dn_alpha):
    n = x1.shape[0]
    grid_spec = pltpu.PrefetchScalarGridSpec(
        num_scalar_prefetch=len(tabs),
        grid=(n // tm,),
        in_specs=[
            pl.BlockSpec((tm, LANES), lambda i, *_: (i, 0)),
            pl.BlockSpec((tm, LANES), lambda i, *_: (i, 0)),
            pl.BlockSpec((tm, D_MODEL), lambda i, *_: (i, 0)),
            pl.BlockSpec((1, D_MODEL), lambda i, *_: (0, 0)),
            pl.BlockSpec((1, D_MODEL), lambda i, *_: (0, 0)),
            pl.BlockSpec(memory_space=pl.ANY),
        ],
        out_specs=pl.BlockSpec((tm, D_MODEL), lambda i, *_: (i, 0)),
        scratch_shapes=[pltpu.VMEM((2, _local_rows(tm), D_MODEL), F32), pltpu.SemaphoreType.DMA((2,))],
    )
    return pl.pallas_call(
        functools.partial(_combine_kernel, tm=tm, dn_alpha=dn_alpha),
        out_shape=jax.ShapeDtypeStruct((n, D_MODEL), F32),
        grid_spec=grid_spec,
        compiler_params=_params(("arbitrary",)),
        name="combine",
    )(*tabs, gate, lpos, x1, g2, b2, yb)


def _position_tables(pos0, seq):
    pos = pos0 + np.arange(seq, dtype=np.int64)
    inv = ROPE_THETA ** (-np.arange(HALF_DIM, dtype=np.float64) / HALF_DIM)
    ang = pos.astype(np.float64)[:, None] * inv[None, :]
    cos, sin = np.cos(ang), np.sin(ang)
    cos_rows = np.concatenate([cos, cos, cos, cos], axis=-1)
    sin_rows = np.concatenate([-sin, sin, -sin, sin], axis=-1)
    icnt = np.concatenate(
        [np.broadcast_to((1.0 / np.minimum(pos + 1, w))[:, None], (seq, POOL_GROUP_DIM))
         for w in POOL_WINDOWS], axis=-1)
    return tuple(jnp.asarray(np.ascontiguousarray(a), dtype=F32)
                 for a in (cos_rows, sin_rows, cos.T, sin.T, icnt))


def _tile(n, pref):
    t = min(n, pref)
    while n % t:
        t //= 2
    return t


def kernel(x_prompt, x_sample, cache_k, cache_v, state_pool, w_in, w_pool_mix, pool_scale, w_pool_out,
           lambda_q1, lambda_k1, lambda_q2, lambda_k2, attn_norm_g, w_attn_out, w_out, ln1_g, ln1_b,
           w_router, b_router, w_expert_in, b_expert_in, w_expert_out, b_expert_out, ln2_g, ln2_b):
    depth = w_in.shape[0]
    assert depth == 1, "single-layer step"
    dn_alpha = (2.0 * depth) ** 0.25
    lam_init = 0.8 - 0.6 * math.exp(-0.3 * 0)
    bp, sp, _ = x_prompt.shape
    bs, ss, _ = x_sample.shape
    past = cache_k.shape[2]
    np_, ns = bp * sp, bs * ss

    c_q, c_k, c_v = POOL_DIM, POOL_DIM + QK_DIM, POOL_DIM + 2 * QK_DIM
    c_gate = c_v + ATTN_V_WIDTH
    w0 = w_in[0]
    w_pqv = jnp.concatenate([w0[:, :c_k], w0[:, c_v:c_gate]], axis=1).astype(BF16)
    w_k = w0[:, c_k:c_v].astype(BF16)
    w_gate = w0[:, c_gate:].astype(BF16)
    wmix = w_pool_mix[0].astype(BF16)
    pscale = pool_scale[0].reshape(1, POOL_DIM)
    wpo = w_pool_out[0].astype(BF16)
    wao = w_attn_out[0].astype(BF16)
    wout = w_out[0].astype(BF16)
    wr = w_router[0].astype(BF16)
    br = b_router[0].reshape(1, N_EXPERTS)
    lam_vecs = jnp.stack([lambda_q1[0], lambda_k1[0], lambda_q2[0], lambda_k2[0]])
    norm_g = attn_norm_g[0].reshape(1, V_DIM)
    g1, b1 = ln1_g[0].reshape(1, D_MODEL), ln1_b[0].reshape(1, D_MODEL)
    g2, b2 = ln2_g[0].reshape(1, D_MODEL), ln2_b[0].reshape(1, D_MODEL)
    b_glu = b_expert_in[0][:, 0::2].reshape(N_EXPERTS, 1, D_EXPERT)
    b_lin = b_expert_in[0][:, 1::2].reshape(N_EXPERTS, 1, D_EXPERT)
    b_eo = b_expert_out[0].reshape(N_EXPERTS, 1, D_MODEL)

    xp = x_prompt.reshape(np_, D_MODEL)
    cos_p, sin_p, cost_p, sint_p, icnt_p = _position_tables(0, sp)
    hist_p = jnp.zeros((bp, HIST_ROWS, POOL_DIM), F32)
    q_p, kt_p, ktb_p, v_p, vb_p, py_p, pnew_p = _inproj_prompt(
        xp, w_pqv, w_k.T, cos_p, sin_p, cost_p, sint_p, icnt_p, hist_p, wmix, pscale,
        n_streams=bp, seq=sp, tm=_tile(sp, DENSE_TILE))
    ay_p = _attn_prompt(lam_vecs, norm_g, q_p, ktb_p, vb_p, n_streams=bp, seq=sp,
                        tq=_tile(sp, ATTN_Q_TILE), lam_init=lam_init)
    tm_p, tm_s = _tile(np_, ROUTE_TILE), _tile(ns, ROUTE_TILE)
    x1_p, gate_p, lpos_p, cnt_p = _postmix(
        xp, py_p, ay_p, w_gate, wpo, wao, wout, g1, b1, wr, br,
        tm=max(tm_p, _tile(np_, DENSE_TILE)), rt=tm_p, dn_alpha=dn_alpha)

    xs_ = x_sample.reshape(ns, D_MODEL)
    cos_s, sin_s, _, _, icnt_s = _position_tables(past, ss)
    hist_s = jnp.concatenate([jnp.zeros((bs, 1, POOL_DIM), F32), state_pool[0]], axis=1)
    q_s, k_s, v_s, py_s, pnew_s = _inproj_sample(
        xs_, w_pqv, w_k, cos_s, sin_s, icnt_s, hist_s, wmix, pscale, n_streams=bs, seq=ss)
    kct = jnp.transpose(cache_k[0], (0, 2, 3, 4, 1)).reshape(bs, QK_DIM, past)
    vc = cache_v[0].reshape(bs, past * N_HEADS, V_DIM)
    ay_s = _attn_sample(lam_vecs, norm_g, q_s, kct, vc, k_s, v_s, n_streams=bs, tn=ss, past=past,
                        tk=_tile(past, 1024), lam_init=lam_init)
    x1_s, gate_s, lpos_s, cnt_s = _postmix(
        xs_, py_s, ay_s, w_gate, wpo, wao, wout, g1, b1, wr, br, tm=tm_s, rt=tm_s, dn_alpha=dn_alpha)

    ntp = np_ // tm_p
    cnt = jnp.concatenate([cnt_p[:, 0, :], cnt_s[:, 0, :]], axis=0).astype(jnp.int32)
    n_tiles = cnt.shape[0]
    group = jnp.sum(cnt, axis=0)
    padded = (group + EXPERT_BLOCK - 1) // EXPERT_BLOCK * EXPERT_BLOCK
    pad_end = jnp.cumsum(padded).astype(jnp.int32)
    run_start = (pad_end - padded)[None, :] + jnp.cumsum(cnt, axis=0) - cnt
    run_off = jnp.cumsum(cnt, axis=1) - cnt
    tail = jnp.where(padded > 0, pad_end - EXPERT_BLOCK, -1).astype(jnp.int32)
    max_rows = (np_ + ns) * TOP_K + n_tiles * N_EXPERTS * (RUN_ALIGN - 1) + N_EXPERTS * (EXPERT_BLOCK - 1)
    n_blocks = -(-max_rows // EXPERT_BLOCK)
    n_used = pad_end[-1:] // EXPERT_BLOCK
    blk_start = jnp.arange(n_blocks, dtype=jnp.int32) * EXPERT_BLOCK
    blk_expert = jnp.minimum(jnp.sum((blk_start[:, None] >= pad_end[None, :]).astype(jnp.int32), axis=1),
                             N_EXPERTS - 1)
    tables = (run_start, run_off, cnt, jnp.sum(cnt, axis=1))
    tabs_p = tuple(a[:ntp].reshape(-1).astype(jnp.int32) for a in tables)
    tabs_s = tuple(a[ntp:].reshape(-1).astype(jnp.int32) for a in tables)

    xsorted = _dispatch_first(tail, n_used, tabs_p, lpos_p, x1_p, tm=tm_p, n_blocks=n_blocks)
    xsorted = _dispatch_next(tabs_s, lpos_s, x1_s, xsorted, tm=tm_s)
    blk = jnp.arange(n_blocks, dtype=jnp.int32)
    later = (blk[None, :] > blk[:, None]) & (blk_expert[None, :] != blk_expert[:, None]) & (blk[None, :] < n_used)
    next_expert = jnp.where(jnp.any(later, axis=1), blk_expert[jnp.argmax(later, axis=1)], -1).astype(jnp.int32)
    yb = _experts(blk_expert, next_expert, n_used, xsorted, w_expert_in[0], b_glu, b_lin, w_expert_out[0], b_eo)
    y_p = _combine(tabs_p, gate_p, lpos_p, x1_p, g2, b2, yb, tm=tm_p, dn_alpha=dn_alpha)
    y_s = _combine(tabs_s, gate_s, lpos_s, x1_s, g2, b2, yb, tm=tm_s, dn_alpha=dn_alpha)

    k_prompt = jnp.transpose(kt_p.reshape(bp, N_HEADS, 2, HEAD_DIM, sp), (0, 4, 1, 2, 3))
    return (
        y_p.reshape(bp, sp, D_MODEL),
        y_s.reshape(bs, ss, D_MODEL),
        k_prompt[None],
        v_p.reshape(1, bp, sp, N_HEADS, V_DIM),
        pnew_p[:, 1:].reshape(1, bp, POOL_HIST, POOL_DIM),
        k_s.reshape(1, bs, ss, N_HEADS, 2, HEAD_DIM),
        v_s.reshape(1, bs, ss, N_HEADS, V_DIM),
        pnew_s[:, 1:].reshape(1, bs, POOL_HIST, POOL_DIM),
    )
```
